```python
import math
import jax
import jax.numpy as jnp
from jax import lax
import numpy as np

D_MODEL = 2048
BATCH = 2
SEQ = 4096
DEPTH = 2

POOL_WIDTH = D_MODEL // 2
POOL_WINDOWS = (2, 4, 8, 16)
POOL_GROUPS = len(POOL_WINDOWS)
POOL_GROUP_DIM = POOL_WIDTH // POOL_GROUPS
DIFF_HEADS = 8
DIFF_HEAD_DIM = D_MODEL // (4 * DIFF_HEADS)
DIFF_V_DIM = 2 * DIFF_HEAD_DIM
ATTN_WIDTH = DIFF_HEADS * DIFF_V_DIM
IN_WIDTH = POOL_WIDTH + 3 * ATTN_WIDTH
Q_BLOCK = 128
REL_BUCKETS = 32
REL_MAX_DISTANCE = 128
N_GROUPS = 4
EXPERTS_PER_GROUP = 8
N_EXPERTS = N_GROUPS * EXPERTS_PER_GROUP
TOP_K_IN_GROUP = 2
D_EXPERT = D_MODEL // 4
EPS = 1e-6
NEG_INF = -1e30

kernel_name = "hybrid_pool_diffattn_hmoe"


def rms_norm(x, gain, eps=EPS):
    xf = x.astype(jnp.float32)
    y = xf * lax.rsqrt(jnp.mean(xf * xf, axis=-1, keepdims=True) + eps)
    return (y * gain.astype(jnp.float32)).astype(x.dtype)


def relative_bucket(q_pos, k_pos):
    n = jnp.maximum(q_pos[:, None] - k_pos[None, :], 0)
    max_exact = REL_BUCKETS // 2
    nf = jnp.maximum(n, 1).astype(jnp.float32)
    large = max_exact + (jnp.log(nf / max_exact) / math.log(REL_MAX_DISTANCE / max_exact)
                         * (REL_BUCKETS - max_exact)).astype(jnp.int32)
    large = jnp.minimum(large, REL_BUCKETS - 1)
    return jnp.where(n < max_exact, n, large)


def pool_mixer(u, w_mix, scale):
    B, S, _ = u.shape
    ug = u.reshape(B, S, POOL_GROUPS, POOL_GROUP_DIM).astype(jnp.float32)
    csum = jnp.cumsum(ug, axis=1)
    t = jnp.arange(S)
    outs = []
    for g, w in enumerate(POOL_WINDOWS):
        c = csum[:, :, g]
        lagged = jnp.pad(c, ((0, 0), (w, 0), (0, 0)))[:, :S]
        count = jnp.minimum(t + 1, w).astype(jnp.float32)[None, :, None]
        outs.append((c - lagged) / count - ug[:, :, g])
    pooled = jnp.stack(outs, axis=2).astype(u.dtype)
    mixed = jnp.einsum('bsgc,gcd->bsgd', pooled, w_mix).reshape(B, S, POOL_WIDTH)
    return mixed * scale


def diff_attention(q, k, v, rel_table, lam, subln_gain, lambda_init):
    B, S = q.shape[:2]
    q = q * (DIFF_HEAD_DIM ** -0.5)
    pos = jnp.arange(S)
    outs = []
    for start in range(0, S, Q_BLOCK):
        end = start + Q_BLOCK
        qb = q[:, start:end]
        kb = k[:, :end]
        vb = v[:, :end]
        logits = jnp.einsum('bqhcd,bkhcd->bhcqk', qb, kb,
                            preferred_element_type=jnp.float32)
        q_pos = pos[start:end]
        k_pos = pos[:end]
        bias = rel_table[relative_bucket(q_pos, k_pos)].astype(jnp.float32)
        bias = jnp.transpose(bias, (2, 0, 1))[None, :, None]
        causal = k_pos[None, :] <= q_pos[:, None]
        logits = jnp.where(causal, logits + bias, NEG_INF)
        p = jax.nn.softmax(logits, axis=-1)
        a = (p[:, :, 0] - lam * p[:, :, 1]).astype(v.dtype)
        outs.append(jnp.einsum('bhqk,bkhe->bqhe', a, vb))
    o = jnp.concatenate(outs, axis=1)
    o = rms_norm(o, subln_gain) * (1.0 - lambda_init)
    return o.reshape(B, S, ATTN_WIDTH)


def mixer_block(xn, layer_idx, rel_table, w_in, w_merge_gate, b_merge_gate, pool_mix, pool_scale,
                w_up_pool, lambda_q1, lambda_k1, lambda_q2, lambda_k2, subln_gain, w_up_attn, w_out):
    B, S, _ = xn.shape
    proj = xn @ w_in
    u_pool, q, k, v = jnp.split(
        proj, [POOL_WIDTH, POOL_WIDTH + ATTN_WIDTH, POOL_WIDTH + 2 * ATTN_WIDTH], axis=-1)
    q = q.reshape(B, S, DIFF_HEADS, 2, DIFF_HEAD_DIM)
    k = k.reshape(B, S, DIFF_HEADS, 2, DIFF_HEAD_DIM)
    v = v.reshape(B, S, DIFF_HEADS, DIFF_V_DIM)
    y_pool = pool_mixer(u_pool, pool_mix, pool_scale) @ w_up_pool
    lambda_init = 0.8 - 0.6 * math.exp(-0.3 * layer_idx)
    lam = (jnp.exp(jnp.sum(lambda_q1.astype(jnp.float32) * lambda_k1.astype(jnp.float32)))
           - jnp.exp(jnp.sum(lambda_q2.astype(jnp.float32) * lambda_k2.astype(jnp.float32)))
           + lambda_init)
    y_attn = diff_attention(q, k, v, rel_table, lam, subln_gain, lambda_init) @ w_up_attn
    gates = jax.nn.sigmoid(xn @ w_merge_gate + b_merge_gate)
    g_pool, g_attn = jnp.split(gates, 2, axis=-1)
    return (g_pool * y_pool + g_attn * y_attn) @ w_out


def hierarchical_moe(x, w_rg, b_rg, w_re, b_re, w_gate, w_up, w_down):
    B, S, D = x.shape
    N = B * S
    xt = x.reshape(N, D)
    g_logits = (xt @ w_rg + b_rg).astype(jnp.float32)
    g_prob = jax.nn.softmax(g_logits, axis=-1)
    g_sel = jnp.argmax(g_logits, axis=-1)
    g_weight = jnp.take_along_axis(g_prob, g_sel[:, None], axis=-1)
    e_logits_all = (jnp.einsum('nd,gde->nge', xt, w_re) + b_re).astype(jnp.float32)
    e_logits = jnp.take_along_axis(e_logits_all, g_sel[:, None, None], axis=1)[:, 0]
    top_vals, top_idx = lax.top_k(e_logits, TOP_K_IN_GROUP)
    top_w = jax.nn.softmax(top_vals, axis=-1) * g_weight
    expert_id = g_sel[:, None] * EXPERTS_PER_GROUP + top_idx
    combine = jnp.sum(jax.nn.one_hot(expert_id, N_EXPERTS, dtype=jnp.float32)
                      * top_w[..., None], axis=1).astype(x.dtype)
    y = jnp.zeros((N, D), x.dtype)
    for g in range(N_GROUPS):
        sl = slice(g * EXPERTS_PER_GROUP, (g + 1) * EXPERTS_PER_GROUP)
        h = (jax.nn.silu(jnp.einsum('nd,edf->nef', xt, w_gate[sl]))
             * jnp.einsum('nd,edf->nef', xt, w_up[sl]))
        h = h * combine[:, sl, None]
        y = y + jnp.einsum('nef,efd->nd', h, w_down[sl])
    return y.reshape(B, S, D)


def setup_inputs(seed: int = 0) -> dict:
    key = jax.random.key(seed)
    ks = jax.random.split(key, 32)
    f32 = jnp.float32

    def nrm(k, shape, fan_in):
        return jax.random.normal(k, shape, f32) * fan_in ** -0.5

    def gain(k, shape):
        return 1.0 + 0.02 * jax.random.normal(k, shape, f32)

    def small(k, shape, s):
        return s * jax.random.normal(k, shape, f32)

    return {
        "x": jax.random.normal(ks[0], (BATCH, SEQ, D_MODEL), f32),
        "rel_bias_table": small(ks[1], (REL_BUCKETS, DIFF_HEADS), 0.5),
        "norm_mix_gain": gain(ks[2], (DEPTH, D_MODEL)),
        "w_in": nrm(ks[3], (DEPTH, D_MODEL, IN_WIDTH), D_MODEL),
        "w_merge_gate": nrm(ks[4], (DEPTH, D_MODEL, 2 * D_MODEL), D_MODEL),
        "b_merge_gate": small(ks[5], (DEPTH, 2 * D_MODEL), 0.02),
        "pool_mix": nrm(ks[6], (DEPTH, POOL_GROUPS, POOL_GROUP_DIM, POOL_GROUP_DIM), POOL_GROUP_DIM),
        "pool_scale": gain(ks[7], (DEPTH, POOL_WIDTH)),
        "w_up_pool": nrm(ks[8], (DEPTH, POOL_WIDTH, D_MODEL), POOL_WIDTH),
        "lambda_q1": small(ks[9], (DEPTH, DIFF_HEAD_DIM), 0.1),
        "lambda_k1": small(ks[10], (DEPTH, DIFF_HEAD_DIM), 0.1),
        "lambda_q2": small(ks[11], (DEPTH, DIFF_HEAD_DIM), 0.1),
        "lambda_k2": small(ks[12], (DEPTH, DIFF_HEAD_DIM), 0.1),
        "subln_gain": gain(ks[13], (DEPTH, DIFF_V_DIM)),
        "w_up_attn": nrm(ks[14], (DEPTH, ATTN_WIDTH, D_MODEL), ATTN_WIDTH),
        "w_out": nrm(ks[15], (DEPTH, D_MODEL, D_MODEL), D_MODEL),
        "norm_ffn_gain": gain(ks[16], (DEPTH, D_MODEL)),
        "w_router_group": nrm(ks[17], (DEPTH, D_MODEL, N_GROUPS), D_MODEL),
        "b_router_group": small(ks[18], (DEPTH, N_GROUPS), 0.01),
        "w_router_expert": nrm(ks[19], (DEPTH, N_GROUPS, D_MODEL, EXPERTS_PER_GROUP), D_MODEL),
        "b_router_expert": small(ks[20], (DEPTH, N_GROUPS, EXPERTS_PER_GROUP), 0.01),
        "w_expert_gate": nrm(ks[21], (DEPTH, N_EXPERTS, D_MODEL, D_EXPERT), D_MODEL),
        "w_expert_up": nrm(ks[22], (DEPTH, N_EXPERTS, D_MODEL, D_EXPERT), D_MODEL),
        "w_expert_down": nrm(ks[23], (DEPTH, N_EXPERTS, D_EXPERT, D_MODEL), D_EXPERT),
        "final_norm_gain": gain(ks[24], (D_MODEL,)),
    }


def reference(x, rel_bias_table, norm_mix_gain, w_in, w_merge_gate, b_merge_gate, pool_mix,
              pool_scale, w_up_pool, lambda_q1, lambda_k1, lambda_q2, lambda_k2, subln_gain,
              w_up_attn, w_out, norm_ffn_gain, w_router_group, b_router_group, w_router_expert,
              b_router_expert, w_expert_gate, w_expert_up, w_expert_down, final_norm_gain):
    h = x
    for l in range(DEPTH):
        xn = rms_norm(h, norm_mix_gain[l])
        h = h + mixer_block(xn, l, rel_bias_table, w_in[l], w_merge_gate[l], b_merge_gate[l],
                            pool_mix[l], pool_scale[l], w_up_pool[l], lambda_q1[l], lambda_k1[l],
                            lambda_q2[l], lambda_k2[l], subln_gain[l], w_up_attn[l], w_out[l])
        hn = rms_norm(h, norm_ffn_gain[l])
        h = h + hierarchical_moe(hn, w_router_group[l], b_router_group[l], w_router_expert[l],
                                 b_router_expert[l], w_expert_gate[l], w_expert_up[l],
                                 w_expert_down[l])
    return rms_norm(h, final_norm_gain)
```

```python
import functools
import math

import numpy as np
import jax
import jax.numpy as jnp
from jax import lax
from jax.experimental import pallas as pl
from jax.experimental.pallas import tpu as pltpu

F32 = jnp.float32
BF16 = jnp.bfloat16

D_MODEL = 2048
POOL_WIDTH = 1024
POOL_WINDOWS = (2, 4, 8, 16)
POOL_GROUP_DIM = 256
POOL_HALO = 16
DIFF_HEADS = 8
DIFF_HEAD_DIM = 64
DIFF_V_DIM = 128
ATTN_WIDTH = 1024
REL_BUCKETS = 32
REL_MAX_DISTANCE = 128
N_GROUPS = 4
EXPERTS_PER_GROUP = 8
N_EXPERTS = 32
D_EXPERT = 512
EPS = 1e-6
NEG_INF = -1e30
LOG2E = 1.4426950408889634

ROUTER_LANES = 128
TQ = 256
TK = 256
TM_EXPERT = 256
MIB = 1024 * 1024


def _cparams(sem, vmem_mib):
    return pltpu.CompilerParams(dimension_semantics=sem, vmem_limit_bytes=vmem_mib * MIB)


def _rms(xf, gain):
    ms = jnp.mean(xf * xf, axis=-1, keepdims=True)
    return xf * lax.rsqrt(ms + EPS) * gain


def _norm_kernel(h_ref, g_ref, o_ref):
    o_ref[...] = _rms(h_ref[...], g_ref[...]).astype(o_ref.dtype)


def rms_norm_call(h, gain, out_dtype, tm=512):
    n, d = h.shape
    return pl.pallas_call(
        _norm_kernel,
        grid=(n // tm,),
        in_specs=[pl.BlockSpec((tm, d), lambda m: (m, 0)),
                  pl.BlockSpec((1, d), lambda m: (0, 0))],
        out_specs=pl.BlockSpec((tm, d), lambda m: (m, 0)),
        out_shape=jax.ShapeDtypeStruct((n, d), out_dtype),
        compiler_params=_cparams(("parallel",), 32),
        name="rms_norm",
    )(h, gain.reshape(1, d))


def _proj_kernel(x_ref, w_ref, o_ref, *, first_block_scale):
    acc = jnp.dot(x_ref[...], w_ref[...], preferred_element_type=F32)
    if first_block_scale is not None:
        acc = acc * jnp.where(pl.program_id(0) == 0, first_block_scale, 1.0).astype(F32)
    o_ref[...] = acc.astype(o_ref.dtype)


def proj_call(x, w, col_block0, n_col_blocks, tn, out_dtype, first_block_scale=None, tm=512, name="proj"):
    n, k = x.shape
    return pl.pallas_call(
        functools.partial(_proj_kernel, first_block_scale=first_block_scale),
        grid=(n_col_blocks, n // tm),
        in_specs=[pl.BlockSpec((tm, k), lambda j, m: (m, 0)),
                  pl.BlockSpec((k, tn), lambda j, m: (0, col_block0 + j))],
        out_specs=pl.BlockSpec((tm, tn), lambda j, m: (m, j)),
        out_shape=jax.ShapeDtypeStruct((n, n_col_blocks * tn), out_dtype),
        compiler_params=_cparams(("parallel", "parallel"), 40),
        name=name,
    )(x, w)


def _pool_kernel(cur_ref, prev_ref, mix_ref, scale_ref, o_ref, *, tm, seq):
    m = pl.program_id(0)
    row0 = (m * tm) % seq
    cur = cur_ref[...]
    prev = jnp.where(row0 == 0, 0.0, prev_ref[...])
    pos = row0 + lax.broadcasted_iota(jnp.int32, (tm, 1), 0)
    outs = []
    for g, w in enumerate(POOL_WINDOWS):
        sl = slice(g * POOL_GROUP_DIM, (g + 1) * POOL_GROUP_DIM)
        x = jnp.concatenate([prev[:, sl], cur[:, sl]], axis=0)
        s, d = x, 1
        while d < w:
            s = s[:-d] + s[d:]
            d *= 2
        start = POOL_HALO - w + 1
        wsum = s[start:start + tm]
        count = jnp.minimum(pos + 1, w).astype(F32)
        pooled = wsum / count - cur[:, sl]
        mixed = jnp.dot(pooled.astype(BF16), mix_ref[g], preferred_element_type=F32)
        outs.append(mixed * scale_ref[:, sl])
    o_ref[...] = jnp.concatenate(outs, axis=1).astype(o_ref.dtype)


def pool_call(u, mix_bf16, scale, seq, tm=512):
    n, c = u.shape
    blocks_per_tile = tm // POOL_HALO
    return pl.pallas_call(
        functools.partial(_pool_kernel, tm=tm, seq=seq),
        grid=(n // tm,),
        in_specs=[pl.BlockSpec((tm, c), lambda m: (m, 0)),
                  pl.BlockSpec((POOL_HALO, c), lambda m: (jnp.maximum(m * blocks_per_tile - 1, 0), 0)),
                  pl.BlockSpec(mix_bf16.shape, lambda m: (0, 0, 0)),
                  pl.BlockSpec((1, c), lambda m: (0, 0))],
        out_specs=pl.BlockSpec((tm, c), lambda m: (m, 0)),
        out_shape=jax.ShapeDtypeStruct((n, c), BF16),
        compiler_params=_cparams(("parallel",), 32),
        name="pool_mixer",
    )(u, u, mix_bf16, scale.reshape(1, c))


def _bucket_tiles():
    kk = np.arange(TK)[:, None]
    qq = np.arange(TQ)[None, :]
    tiles = []
    for rel in range(3):
        n = rel * TK + qq - kk
        max_exact = REL_BUCKETS // 2
        nf = np.maximum(n, 1).astype(np.float64)
        large = max_exact + (np.log(nf / max_exact) / math.log(REL_MAX_DISTANCE / max_exact)
                             * (REL_BUCKETS - max_exact)).astype(np.int64)
        large = np.minimum(large, REL_BUCKETS - 1)
        bucket = np.where(n < max_exact, n, large)
        tiles.append(np.where(n < 0, -1, bucket))
    return np.stack(tiles).astype(np.int32)


def _bias_kernel(table_ref, bucket_ref, o_ref):
    h = pl.program_id(0)
    bucket = bucket_ref[...]
    acc = jnp.full(bucket.shape, NEG_INF, F32)
    for b in range(REL_BUCKETS):
        acc = jnp.where(bucket == b, table_ref[b * DIFF_HEADS + h] * LOG2E, acc)
    o_ref[...] = acc


def bias_tiles_call(rel_table):
    bucket = jnp.asarray(_bucket_tiles())
    return pl.pallas_call(
        _bias_kernel,
        grid=(DIFF_HEADS,),
        in_specs=[pl.BlockSpec(memory_space=pltpu.SMEM),
                  pl.BlockSpec(bucket.shape, lambda h: (0, 0, 0))],
        out_specs=pl.BlockSpec((None,) + bucket.shape, lambda h: (h, 0, 0, 0)),
        out_shape=jax.ShapeDtypeStruct((DIFF_HEADS,) + bucket.shape, F32),
        compiler_params=_cparams(("parallel",), 32),
        name="rel_bias_tiles",
    )(rel_table.reshape(-1), bucket)


def _attn_kernel(q_ref, k_ref, v_ref, bias_ref, lam_ref, gain_ref, o_ref, vt_ref, s_ref, acc_ref,
                 *, lambda_init, n_kv):
    qi = pl.program_id(2)

    @pl.when(qi == 0)
    def _():
        for c in range(n_kv):
            vt_ref[c] = v_ref[c * TK:(c + 1) * TK, :].astype(F32).T.astype(BF16)

    q = q_ref[...]
    lane = lax.broadcasted_iota(jnp.int32, q.shape, 1)
    zero = jnp.zeros_like(q)
    qd = jnp.concatenate([jnp.where(lane < DIFF_HEAD_DIM, q, zero),
                          jnp.where(lane >= DIFF_HEAD_DIM, q, zero)], axis=0)

    def scores(j):
        kb = k_ref[pl.ds(pl.multiple_of(j * TK, TK), TK), :]
        return lax.dot_general(kb, qd, (((1,), (1,)), ((), ())), preferred_element_type=F32)

    def softmax_step(s, bias, vt, m, l):
        s = s + jnp.concatenate([bias, bias], axis=1)
        m_new = jnp.maximum(m, jnp.max(s, axis=0, keepdims=True))
        alpha = jnp.exp2(m - m_new)
        p = jnp.exp2(s - m_new)
        l_new = l * alpha + jnp.sum(p, axis=0, keepdims=True)
        pv = jnp.dot(vt, p.astype(BF16), preferred_element_type=F32)
        acc_ref[...] = acc_ref[...] * alpha + pv
        return m_new, l_new

    s_ref[0] = scores(0)
    acc_ref[...] = jnp.zeros(acc_ref.shape, F32)
    m0 = jnp.full((1, 2 * TQ), NEG_INF, F32)
    l0 = jnp.zeros((1, 2 * TQ), F32)

    def body(j, carry):
        m, l = carry
        s = s_ref[j % 2]
        s_ref[(j + 1) % 2] = scores(j + 1)
        return softmax_step(s, bias_ref[jnp.minimum(qi - j, 2)], vt_ref[j], m, l)

    m, l = lax.fori_loop(0, qi, body, (m0, l0))
    m, l = softmax_step(s_ref[qi % 2], bias_ref[0], vt_ref[qi], m, l)

    lam_p = lam_ref[...]
    lam = (jnp.exp(jnp.sum(lam_p[0:1] * lam_p[1:2], axis=1, keepdims=True))
           - jnp.exp(jnp.sum(lam_p[2:3] * lam_p[3:4], axis=1, keepdims=True)) + lambda_init)
    acc = acc_ref[...]
    o = acc[:, :TQ] / l[:, :TQ] - lam * (acc[:, TQ:] / l[:, TQ:])
    ms = jnp.mean(o * o, axis=0, keepdims=True)
    y = o * lax.rsqrt(ms + EPS) * gain_ref[...] * (1.0 - lambda_init)
    o_ref[...] = y.T.astype(o_ref.dtype)


def attn_call(qkv, bias_tiles, lam_params, subln_gain, lambda_init, batch, seq):
    n_kv = seq // TK
    return pl.pallas_call(
        functools.partial(_attn_kernel, lambda_init=lambda_init, n_kv=n_kv),
        grid=(batch, DIFF_HEADS, seq // TQ),
        in_specs=[pl.BlockSpec((None, TQ, DIFF_V_DIM), lambda b, h, i: (b, i, h)),
                  pl.BlockSpec((None, seq, DIFF_V_DIM), lambda b, h, i: (b, 0, DIFF_HEADS + h)),
                  pl.BlockSpec((None, seq, DIFF_V_DIM), lambda b, h, i: (b, 0, 2 * DIFF_HEADS + h)),
                  pl.BlockSpec((None, 3, TK, TQ), lambda b, h, i: (h, 0, 0, 0)),
                  pl.BlockSpec((4, DIFF_HEAD_DIM), lambda b, h, i: (0, 0)),
                  pl.BlockSpec((DIFF_V_DIM, 1), lambda b, h, i: (0, 0))],
        out_specs=pl.BlockSpec((None, TQ, DIFF_V_DIM), lambda b, h, i: (b, i, h)),
        out_shape=jax.ShapeDtypeStruct((batch, seq, ATTN_WIDTH), BF16),
        scratch_shapes=[pltpu.VMEM((n_kv, DIFF_V_DIM, TK), BF16),
                        pltpu.VMEM((2, TK, 2 * TQ), F32),
                        pltpu.VMEM((DIFF_V_DIM, 2 * TQ), F32)],
        compiler_params=_cparams(("parallel", "parallel", "arbitrary"), 40),
        name="diff_attention",
    )(qkv, qkv, qkv, bias_tiles, lam_params, subln_gain.reshape(DIFF_V_DIM, 1))


def _sigmoid(x):
    return 1.0 / (1.0 + jnp.exp(-x))


def _merge_kernel(xn_ref, mixed_ref, attn_ref, wgp_ref, wga_ref, bgp_ref, bga_ref, wup_ref, wua_ref, z_ref):
    xn = xn_ref[...]
    g_pool = _sigmoid(jnp.dot(xn, wgp_ref[...], preferred_element_type=F32) + bgp_ref[...])
    g_attn = _sigmoid(jnp.dot(xn, wga_ref[...], preferred_element_type=F32) + bga_ref[...])
    y_pool = jnp.dot(mixed_ref[...], wup_ref[...], preferred_element_type=F32)
    y_attn = jnp.dot(attn_ref[...], wua_ref[...], preferred_element_type=F32)
    z_ref[...] = (g_pool * y_pool + g_attn * y_attn).astype(z_ref.dtype)


def merge_call(xn, mixed, attn, w_gate, b_gate, w_up_pool, w_up_attn, tm=512, tn=512):
    n, d = xn.shape
    nb = d // tn
    return pl.pallas_call(
        _merge_kernel,
        grid=(nb, n // tm),
        in_specs=[pl.BlockSpec((tm, d), lambda j, m: (m, 0)),
                  pl.BlockSpec((tm, POOL_WIDTH), lambda j, m: (m, 0)),
                  pl.BlockSpec((tm, ATTN_WIDTH), lambda j, m: (m, 0)),
                  pl.BlockSpec((d, tn), lambda j, m: (0, j)),
                  pl.BlockSpec((d, tn), lambda j, m: (0, nb + j)),
                  pl.BlockSpec((1, tn), lambda j, m: (0, j)),
                  pl.BlockSpec((1, tn), lambda j, m: (0, nb + j)),
                  pl.BlockSpec((POOL_WIDTH, tn), lambda j, m: (0, j)),
                  pl.BlockSpec((ATTN_WIDTH, tn), lambda j, m: (0, j))],
        out_specs=pl.BlockSpec((tm, tn), lambda j, m: (m, j)),
        out_shape=jax.ShapeDtypeStruct((n, d), BF16),
        compiler_params=_cparams(("parallel", "parallel"), 48),
        name="gated_merge",
    )(xn, mixed, attn, w_gate, w_gate, b_gate, b_gate, w_up_pool, w_up_attn)


def _outproj_kernel(z_ref, w_ref, h_ref, gain_ref, wr_ref, br_ref, h1_ref, hn_ref, logit_ref):
    h1 = h_ref[...] + jnp.dot(z_ref[...], w_ref[...], preferred_element_type=F32)
    h1_ref[...] = h1
    hn = _rms(h1, gain_ref[...])
    hn_ref[...] = hn
    logit_ref[...] = jnp.dot(hn.astype(BF16), wr_ref[...], preferred_element_type=F32) + br_ref[...]


def outproj_call(z, w_out, h, gain, w_router, b_router, tm=256):
    n, d = h.shape
    return pl.pallas_call(
        _outproj_kernel,
        grid=(n // tm,),
        in_specs=[pl.BlockSpec((tm, d), lambda m: (m, 0)),
                  pl.BlockSpec((d, d), lambda m: (0, 0)),
                  pl.BlockSpec((tm, d), lambda m: (m, 0)),
                  pl.BlockSpec((1, d), lambda m: (0, 0)),
                  pl.BlockSpec((d, ROUTER_LANES), lambda m: (0, 0)),
                  pl.BlockSpec((1, ROUTER_LANES), lambda m: (0, 0))],
        out_specs=[pl.BlockSpec((tm, d), lambda m: (m, 0)),
                   pl.BlockSpec((tm, d), lambda m: (m, 0)),
                   pl.BlockSpec((tm, ROUTER_LANES), lambda m: (m, 0))],
        out_shape=[jax.ShapeDtypeStruct((n, d), F32),
                   jax.ShapeDtypeStruct((n, d), F32),
                   jax.ShapeDtypeStruct((n, ROUTER_LANES), F32)],
        compiler_params=_cparams(("parallel",), 56),
        name="out_proj_norm_router",
    )(z, w_out, h, gain.reshape(1, d), w_router, b_router)


def _route_kernel(logit_ref, id_ref, w_ref):
    x = logit_ref[...]
    lane = lax.broadcasted_iota(jnp.int32, x.shape, 1).astype(F32)
    big = float(ROUTER_LANES)

    def first_argmax(vals):
        top = jnp.max(vals, axis=1, keepdims=True)
        idx = jnp.min(jnp.where(vals == top, lane, big), axis=1, keepdims=True)
        return top, idx

    gmask = lane < N_GROUPS
    g_top, g_sel = first_argmax(jnp.where(gmask, x, -jnp.inf))
    g_weight = 1.0 / jnp.sum(jnp.where(gmask, jnp.exp(x - g_top), 0.0), axis=1, keepdims=True)
    lo = N_GROUPS + EXPERTS_PER_GROUP * g_sel
    e_vals = jnp.where((lane >= lo) & (lane < lo + EXPERTS_PER_GROUP), x, -jnp.inf)
    v1, i1 = first_argmax(e_vals)
    v2, i2 = first_argmax(jnp.where(lane == i1, -jnp.inf, e_vals))
    t = jnp.exp(v2 - v1)
    w1 = g_weight / (1.0 + t)
    w2 = g_weight * t / (1.0 + t)
    id_ref[...] = jnp.where(lane == 0, i1 - N_GROUPS, jnp.where(lane == 1, i2 - N_GROUPS, 0.0)).astype(jnp.int32)
    w_ref[...] = jnp.where(lane == 0, w1, jnp.where(lane == 1, w2, 0.0))


def route_call(logits, tm=512):
    n = logits.shape[0]
    spec = pl.BlockSpec((tm, ROUTER_LANES), lambda m: (m, 0))
    return pl.pallas_call(
        _route_kernel,
        grid=(n // tm,),
        in_specs=[spec],
        out_specs=[spec, spec],
        out_shape=[jax.ShapeDtypeStruct((n, ROUTER_LANES), jnp.int32),
                   jax.ShapeDtypeStruct((n, ROUTER_LANES), F32)],
        compiler_params=_cparams(("parallel",), 32),
        name="route_top2",
    )(logits)


def _dispatch_kernel(pos_ref, src_ref, init_ref, dst_ref, sem, *, chunk):
    del init_ref
    base = pl.program_id(0) * chunk

    def row_copy(t, j):
        return pltpu.make_async_copy(src_ref.at[pl.ds(base + t, 1)],
                                     dst_ref.at[pl.ds(pos_ref[0, 0, 2 * t + j], 1)], sem)

    def start(t, c):
        row_copy(t, 0).start()
        row_copy(t, 1).start()
        return c

    def wait(t, c):
        row_copy(t, 0).wait()
        row_copy(t, 1).wait()
        return c

    lax.fori_loop(0, chunk, start, 0)
    lax.fori_loop(0, chunk, wait, 0)


def dispatch_call(hn, pos, n_rows_padded, chunk=256):
    n, d = hn.shape
    pos3 = pos.reshape(n // chunk, 1, 2 * chunk)
    init = jnp.zeros((n_rows_padded, d), hn.dtype)
    return pl.pallas_call(
        functools.partial(_dispatch_kernel, chunk=chunk),
        grid=(n // chunk,),
        in_specs=[pl.BlockSpec((1, 1, 2 * chunk), lambda c: (c, 0, 0), memory_space=pltpu.SMEM),
                  pl.BlockSpec(memory_space=pl.ANY),
                  pl.BlockSpec(memory_space=pl.ANY)],
        out_specs=pl.BlockSpec(memory_space=pl.ANY),
        out_shape=jax.ShapeDtypeStruct((n_rows_padded, d), hn.dtype),
        scratch_shapes=[pltpu.SemaphoreType.DMA(())],
        input_output_aliases={2: 0},
        compiler_params=_cparams(("arbitrary",), 32),
        name="dispatch_rows",
    )(pos3, hn, init)


def _expert_kernel(te_ref, nv_ref, x_ref, wg_ref, wu_ref, wd_ref, y_ref, wg_s, wu_s, wd_s):
    i = pl.program_id(0)
    valid = i < nv_ref[0]
    changed = jnp.logical_or(i == 0, te_ref[i] != te_ref[jnp.maximum(i - 1, 0)])

    @pl.when(jnp.logical_and(valid, changed))
    def _():
        wg_s[...] = wg_ref[...].astype(BF16)
        wu_s[...] = wu_ref[...].astype(BF16)
        wd_s[...] = wd_ref[...].astype(BF16)

    @pl.when(valid)
    def _():
        x = x_ref[...].astype(BF16)
        a = jnp.dot(x, wg_s[...], preferred_element_type=F32)
        b = jnp.dot(x, wu_s[...], preferred_element_type=F32)
        hmid = (a * _sigmoid(a) * b).astype(BF16)
        y_ref[...] = jnp.dot(hmid, wd_s[...], preferred_element_type=F32)

    @pl.when(jnp.logical_not(valid))
    def _():
        y_ref[...] = jnp.zeros(y_ref.shape, y_ref.dtype)


def expert_call(xs, tile_expert, n_valid_tiles, w_gate, w_up, w_down):
    p, d = xs.shape
    tm = TM_EXPERT
    f = w_gate.shape[-1]
    grid_spec = pltpu.PrefetchScalarGridSpec(
        num_scalar_prefetch=2,
        grid=(p // tm,),
        in_specs=[pl.BlockSpec((tm, d), lambda i, te, nv: (jnp.minimum(i, nv[0] - 1), 0)),
                  pl.BlockSpec((None, d, f), lambda i, te, nv: (te[i], 0, 0)),
                  pl.BlockSpec((None, d, f), lambda i, te, nv: (te[i], 0, 0)),
                  pl.BlockSpec((None, f, d), lambda i, te, nv: (te[i], 0, 0))],
        out_specs=pl.BlockSpec((tm, d), lambda i, te, nv: (i, 0)),
        scratch_shapes=[pltpu.VMEM((d, f), BF16), pltpu.VMEM((d, f), BF16), pltpu.VMEM((f, d), BF16)],
    )
    return pl.pallas_call(
        _expert_kernel,
        grid_spec=grid_spec,
        out_shape=jax.ShapeDtypeStruct((p, d), F32),
        compiler_params=_cparams(("arbitrary",), 58),
        name="expert_swiglu",
    )(tile_expert, n_valid_tiles, xs, w_gate, w_up, w_down)


def _combine_kernel(pos_ref, ys_ref, h_ref, w_ref, gain_ref, h2_ref, xn_ref, buf, sem, *, chunk):
    def row_copy(t, j):
        return pltpu.make_async_copy(ys_ref.at[pl.ds(pos_ref[0, 0, 2 * t + j], 1)],
                                     buf.at[j, pl.ds(t, 1)], sem)

    def start(t, c):
        row_copy(t, 0).start()
        row_copy(t, 1).start()
        return c

    def wait(t, c):
        row_copy(t, 0).wait()
        row_copy(t, 1).wait()
        return c

    lax.fori_loop(0, chunk, start, 0)
    lax.fori_loop(0, chunk, wait, 0)
    w = w_ref[...]
    h2 = h_ref[...] + w[:, 0:1] * buf[0] + w[:, 1:2] * buf[1]
    h2_ref[...] = h2
    xn_ref[...] = _rms(h2, gain_ref[...]).astype(xn_ref.dtype)


def combine_call(ys, pos, h1, weights, next_gain, xn_dtype, chunk=128):
    n, d = h1.shape
    pos3 = pos.reshape(n // chunk, 1, 2 * chunk)
    return pl.pallas_call(
        functools.partial(_combine_kernel, chunk=chunk),
        grid=(n // chunk,),
        in_specs=[pl.BlockSpec((1, 1, 2 * chunk), lambda c: (c, 0, 0), memory_space=pltpu.SMEM),
                  pl.BlockSpec(memory_space=pl.ANY),
                  pl.BlockSpec((chunk, d), lambda c: (c, 0)),
                  pl.BlockSpec((chunk, ROUTER_LANES), lambda c: (c, 0)),
                  pl.BlockSpec((1, d), lambda c: (0, 0))],
        out_specs=[pl.BlockSpec((chunk, d), lambda c: (c, 0)),
                   pl.BlockSpec((chunk, d), lambda c: (c, 0))],
        out_shape=[jax.ShapeDtypeStruct((n, d), F32),
                   jax.ShapeDtypeStruct((n, d), xn_dtype)],
        scratch_shapes=[pltpu.VMEM((2, chunk, d), F32), pltpu.SemaphoreType.DMA(())],
        compiler_params=_cparams(("arbitrary",), 32),
        name="combine_rows",
    )(pos3, ys, h1, weights, next_gain.reshape(1, d))


def _dispatch_plan(expert_ids, n_tiles):
    tm = TM_EXPERT
    onehot = (expert_ids[:, :, None] == jnp.arange(N_EXPERTS, dtype=jnp.int32)).astype(jnp.int32)
    per_token = onehot.sum(axis=1)
    before = jnp.cumsum(per_token, axis=0) - per_token
    rank = jnp.take_along_axis(before, expert_ids, axis=1)
    counts = per_token.sum(axis=0)
    padded = ((counts + tm - 1) // tm) * tm
    ends = jnp.cumsum(padded)
    pos = (ends - padded)[expert_ids] + rank
    n_valid = (ends[-1] // tm).astype(jnp.int32)
    tile_start = jnp.arange(n_tiles, dtype=jnp.int32) * tm
    tile_expert = jnp.searchsorted(ends, tile_start, side="right").astype(jnp.int32)
    last_expert = jnp.searchsorted(ends, ends[-1] - 1, side="right").astype(jnp.int32)
    tile_expert = jnp.where(tile_start < ends[-1], tile_expert, last_expert)
    return pos.astype(jnp.int32), tile_expert, n_valid.reshape(1)


def kernel(x, rel_bias_table, norm_mix_gain, w_in, w_merge_gate, b_merge_gate, pool_mix, pool_scale, w_up_pool, lambda_q1, lambda_k1, lambda_q2, lambda_k2, subln_gain, w_up_attn, w_out, norm_ffn_gain, w_router_group, b_router_group, w_router_expert, b_router_expert, w_expert_gate, w_expert_up, w_expert_down, final_norm_gain):
    batch, seq, d = x.shape
    depth = w_in.shape[0]
    n = batch * seq
    n_tiles = (2 * n + N_EXPERTS * (TM_EXPERT - 1)) // TM_EXPERT + 1
    n_rows_padded = n_tiles * TM_EXPERT

    bias_tiles = bias_tiles_call(rel_bias_table)
    h = x.reshape(n, d)
    xn = rms_norm_call(h, norm_mix_gain[0], BF16)
    q_scale = LOG2E * DIFF_HEAD_DIM ** -0.5

    for l in range(depth):
        w_in_l = w_in[l].astype(BF16)
        u = proj_call(xn, w_in_l, 0, POOL_WIDTH // 512, 512, F32, name="in_proj_pool")
        qkv = proj_call(xn, w_in_l, 1, 3, ATTN_WIDTH, BF16, first_block_scale=q_scale, name="in_proj_qkv")
        mixed = pool_call(u, pool_mix[l].astype(BF16), pool_scale[l], seq)
        lambda_init = 0.8 - 0.6 * math.exp(-0.3 * l)
        lam_params = jnp.stack([lambda_q1[l], lambda_k1[l], lambda_q2[l], lambda_k2[l]])
        attn = attn_call(qkv.reshape(batch, seq, 3 * ATTN_WIDTH), bias_tiles, lam_params, subln_gain[l],
                         lambda_init, batch, seq).reshape(n, ATTN_WIDTH)
        z = merge_call(xn, mixed, attn, w_merge_gate[l].astype(BF16), b_merge_gate[l].reshape(1, -1),
                       w_up_pool[l].astype(BF16), w_up_attn[l].astype(BF16))

        w_router = jnp.concatenate(
            [w_router_group[l], jnp.transpose(w_router_expert[l], (1, 0, 2)).reshape(d, N_EXPERTS),
             jnp.zeros((d, ROUTER_LANES - N_GROUPS - N_EXPERTS), F32)], axis=1).astype(BF16)
        b_router = jnp.concatenate(
            [b_router_group[l], b_router_expert[l].reshape(-1),
             jnp.zeros((ROUTER_LANES - N_GROUPS - N_EXPERTS,), F32)]).reshape(1, ROUTER_LANES)
        h1, hn, logits = outproj_call(z, w_out[l].astype(BF16), h, norm_ffn_gain[l], w_router, b_router)

        ids, weights = route_call(logits)
        pos, tile_expert, n_valid = _dispatch_plan(ids[:, :2], n_tiles)
        xs = dispatch_call(hn, pos, n_rows_padded)
        ys = expert_call(xs, tile_expert, n_valid, w_expert_gate[l], w_expert_up[l], w_expert_down[l])
        last = l == depth - 1
        next_gain = final_norm_gain if last else norm_mix_gain[l + 1]
        h, xn = combine_call(ys, pos, h1, weights, next_gain, F32 if last else BF16)

    return xn.reshape(batch, seq, d)
```

```python
import functools
import math

import numpy as np
import jax
import jax.numpy as jnp
from jax import lax
from jax.experimental import pallas as pl
from jax.experimental.pallas import tpu as pltpu

F32 = jnp.float32
BF16 = jnp.bfloat16

D_MODEL = 2048
POOL_WIDTH = 1024
POOL_WINDOWS = (2, 4, 8, 16)
POOL_GROUP_DIM = 256
POOL_HALO = 16
DIFF_HEADS = 8
DIFF_HEAD_DIM = 64
DIFF_V_DIM = 128
ATTN_WIDTH = 1024
REL_BUCKETS = 32
REL_MAX_DISTANCE = 128
N_GROUPS = 4
EXPERTS_PER_GROUP = 8
N_EXPERTS = 32
D_EXPERT = 512
EPS = 1e-6
NEG_INF = -1e30
LOG2E = 1.4426950408889634

ROUTER_LANES = 128
TQ = 256
TK = 256
TM_EXPERT = 256
MIB = 1024 * 1024


def _cparams(sem, vmem_mib):
    return pltpu.CompilerParams(dimension_semantics=sem, vmem_limit_bytes=vmem_mib * MIB)


def _rms(xf, gain):
    ms = jnp.mean(xf * xf, axis=-1, keepdims=True)
    return xf * lax.rsqrt(ms + EPS) * gain


def _norm_kernel(h_ref, g_ref, o_ref):
    o_ref[...] = _rms(h_ref[...], g_ref[...]).astype(o_ref.dtype)


def rms_norm_call(h, gain, out_dtype, tm=512):
    n, d = h.shape
    return pl.pallas_call(
        _norm_kernel,
        grid=(n // tm,),
        in_specs=[pl.BlockSpec((tm, d), lambda m: (m, 0)),
                  pl.BlockSpec((1, d), lambda m: (0, 0))],
        out_specs=pl.BlockSpec((tm, d), lambda m: (m, 0)),
        out_shape=jax.ShapeDtypeStruct((n, d), out_dtype),
        compiler_params=_cparams(("parallel",), 32),
        name="rms_norm",
    )(h, gain.reshape(1, d))


def _proj_kernel(x_ref, w_ref, o_ref, *, first_block_scale):
    acc = jnp.dot(x_ref[...], w_ref[...], preferred_element_type=F32)
    if first_block_scale is not None:
        acc = acc * jnp.where(pl.program_id(0) == 0, first_block_scale, 1.0).astype(F32)
    o_ref[...] = acc.astype(o_ref.dtype)


def proj_call(x, w, col_block0, n_col_blocks, tn, out_dtype, first_block_scale=None, tm=512, name="proj"):
    n, k = x.shape
    return pl.pallas_call(
        functools.partial(_proj_kernel, first_block_scale=first_block_scale),
        grid=(n_col_blocks, n // tm),
        in_specs=[pl.BlockSpec((tm, k), lambda j, m: (m, 0)),
                  pl.BlockSpec((k, tn), lambda j, m: (0, col_block0 + j))],
        out_specs=pl.BlockSpec((tm, tn), lambda j, m: (m, j)),
        out_shape=jax.ShapeDtypeStruct((n, n_col_blocks * tn), out_dtype),
        compiler_params=_cparams(("parallel", "parallel"), 40),
        name=name,
    )(x, w)


def _pool_kernel(cur_ref, prev_ref, mix_ref, scale_ref, o_ref, *, tm, seq):
    m = pl.program_id(0)
    row0 = (m * tm) % seq
    cur = cur_ref[...]
    prev = jnp.where(row0 == 0, 0.0, prev_ref[...])
    pos = row0 + lax.broadcasted_iota(jnp.int32, (tm, 1), 0)
    outs = []
    for g, w in enumerate(POOL_WINDOWS):
        sl = slice(g * POOL_GROUP_DIM, (g + 1) * POOL_GROUP_DIM)
        x = jnp.concatenate([prev[:, sl], cur[:, sl]], axis=0)
        s, d = x, 1
        while d < w:
            s = s[:-d] + s[d:]
            d *= 2
        start = POOL_HALO - w + 1
        wsum = s[start:start + tm]
        count = jnp.minimum(pos + 1, w).astype(F32)
        pooled = wsum / count - cur[:, sl]
        mixed = jnp.dot(pooled.astype(BF16), mix_ref[g], preferred_element_type=F32)
        outs.append(mixed * scale_ref[:, sl])
    o_ref[...] = jnp.concatenate(outs, axis=1).astype(o_ref.dtype)


def pool_call(u, mix_bf16, scale, seq, tm=512):
    n, c = u.shape
    blocks_per_tile = tm // POOL_HALO
    return pl.pallas_call(
        functools.partial(_pool_kernel, tm=tm, seq=seq),
        grid=(n // tm,),
        in_specs=[pl.BlockSpec((tm, c), lambda m: (m, 0)),
                  pl.BlockSpec((POOL_HALO, c), lambda m: (jnp.maximum(m * blocks_per_tile - 1, 0), 0)),
                  pl.BlockSpec(mix_bf16.shape, lambda m: (0, 0, 0)),
                  pl.BlockSpec((1, c), lambda m: (0, 0))],
        out_specs=pl.BlockSpec((tm, c), lambda m: (m, 0)),
        out_shape=jax.ShapeDtypeStruct((n, c), BF16),
        compiler_params=_cparams(("parallel",), 32),
        name="pool_mixer",
    )(u, u, mix_bf16, scale.reshape(1, c))


def _bucket_tiles():
    kk = np.arange(TK)[:, None]
    qq = np.arange(TQ)[None, :]
    tiles = []
    for rel in range(3):
        n = rel * TK + qq - kk
        max_exact = REL_BUCKETS // 2
        nf = np.maximum(n, 1).astype(np.float64)
        large = max_exact + (np.log(nf / max_exact) / math.log(REL_MAX_DISTANCE / max_exact)
                             * (REL_BUCKETS - max_exact)).astype(np.int64)
        large = np.minimum(large, REL_BUCKETS - 1)
        bucket = np.where(n < max_exact, n, large)
        tiles.append(np.where(n < 0, -1, bucket))
    return np.stack(tiles).astype(np.int32)


def _bias_kernel(table_ref, bucket_ref, o_ref):
    h = pl.program_id(0)
    bucket = bucket_ref[...]
    acc = jnp.full(bucket.shape, NEG_INF, F32)
    for b in range(REL_BUCKETS):
        acc = jnp.where(bucket == b, table_ref[b * DIFF_HEADS + h] * LOG2E, acc)
    o_ref[...] = acc


def bias_tiles_call(rel_table):
    bucket = jnp.asarray(_bucket_tiles())
    return pl.pallas_call(
        _bias_kernel,
        grid=(DIFF_HEADS,),
        in_specs=[pl.BlockSpec(memory_space=pltpu.SMEM),
                  pl.BlockSpec(bucket.shape, lambda h: (0, 0, 0))],
        out_specs=pl.BlockSpec((None,) + bucket.shape, lambda h: (h, 0, 0, 0)),
        out_shape=jax.ShapeDtypeStruct((DIFF_HEADS,) + bucket.shape, F32),
        compiler_params=_cparams(("parallel",), 32),
        name="rel_bias_tiles",
    )(rel_table.reshape(-1), bucket)


def _attn_kernel(q_ref, k_ref, v_ref, bias_ref, lam_ref, gain_ref, o_ref, vt_ref, s_ref, acc_ref,
                 *, lambda_init, n_kv):
    qi = pl.program_id(2)

    @pl.when(qi == 0)
    def _():
        for c in range(n_kv):
            vt_ref[c] = v_ref[c * TK:(c + 1) * TK, :].astype(F32).T.astype(BF16)

    q = q_ref[...]
    lane = lax.broadcasted_iota(jnp.int32, q.shape, 1)
    zero = jnp.zeros_like(q)
    qd = jnp.concatenate([jnp.where(lane < DIFF_HEAD_DIM, q, zero),
                          jnp.where(lane >= DIFF_HEAD_DIM, q, zero)], axis=0)

    def scores(j):
        kb = k_ref[pl.ds(pl.multiple_of(j * TK, TK), TK), :]
        return lax.dot_general(kb, qd, (((1,), (1,)), ((), ())), preferred_element_type=F32)

    def softmax_step(s, bias, vt, m, l):
        s = s + jnp.concatenate([bias, bias], axis=1)
        m_new = jnp.maximum(m, jnp.max(s, axis=0, keepdims=True))
        alpha = jnp.exp2(m - m_new)
        p = jnp.exp2(s - m_new)
        l_new = l * alpha + jnp.sum(p, axis=0, keepdims=True)
        pv = jnp.dot(vt, p.astype(BF16), preferred_element_type=F32)
        acc_ref[...] = acc_ref[...] * alpha + pv
        return m_new, l_new

    s_ref[0] = scores(0)
    acc_ref[...] = jnp.zeros(acc_ref.shape, F32)
    m0 = jnp.full((1, 2 * TQ), NEG_INF, F32)
    l0 = jnp.zeros((1, 2 * TQ), F32)

    def body(j, carry):
        m, l = carry
        s = s_ref[j % 2]
        s_ref[(j + 1) % 2] = scores(j + 1)
        return softmax_step(s, bias_ref[jnp.minimum(qi - j, 2)], vt_ref[j], m, l)

    m, l = lax.fori_loop(0, qi, body, (m0, l0))
    m, l = softmax_step(s_ref[qi % 2], bias_ref[0], vt_ref[qi], m, l)

    lam_p = lam_ref[...]
    lam = (jnp.exp(jnp.sum(lam_p[0:1] * lam_p[1:2], axis=1, keepdims=True))
           - jnp.exp(jnp.sum(lam_p[2:3] * lam_p[3:4], axis=1, keepdims=True)) + lambda_init)
    acc = acc_ref[...]
    o = acc[:, :TQ] / l[:, :TQ] - lam * (acc[:, TQ:] / l[:, TQ:])
    ms = jnp.mean(o * o, axis=0, keepdims=True)
    y = o * lax.rsqrt(ms + EPS) * gain_ref[...] * (1.0 - lambda_init)
    o_ref[...] = y.T.astype(o_ref.dtype)


def attn_call(qkv, bias_tiles, lam_params, subln_gain, lambda_init, batch, seq):
    n_kv = seq // TK
    return pl.pallas_call(
        functools.partial(_attn_kernel, lambda_init=lambda_init, n_kv=n_kv),
        grid=(batch, DIFF_HEADS, seq // TQ),
        in_specs=[pl.BlockSpec((None, TQ, DIFF_V_DIM), lambda b, h, i: (b, i, h)),
                  pl.BlockSpec((None, seq, DIFF_V_DIM), lambda b, h, i: (b, 0, DIFF_HEADS + h)),
                  pl.BlockSpec((None, seq, DIFF_V_DIM), lambda b, h, i: (b, 0, 2 * DIFF_HEADS + h)),
                  pl.BlockSpec((None, 3, TK, TQ), lambda b, h, i: (h, 0, 0, 0)),
                  pl.BlockSpec((4, DIFF_HEAD_DIM), lambda b, h, i: (0, 0)),
                  pl.BlockSpec((DIFF_V_DIM, 1), lambda b, h, i: (0, 0))],
        out_specs=pl.BlockSpec((None, TQ, DIFF_V_DIM), lambda b, h, i: (b, i, h)),
        out_shape=jax.ShapeDtypeStruct((batch, seq, ATTN_WIDTH), BF16),
        scratch_shapes=[pltpu.VMEM((n_kv, DIFF_V_DIM, TK), BF16),
                        pltpu.VMEM((2, TK, 2 * TQ), F32),
                        pltpu.VMEM((DIFF_V_DIM, 2 * TQ), F32)],
        compiler_params=_cparams(("parallel", "parallel", "arbitrary"), 40),
        name="diff_attention",
    )(qkv, qkv, qkv, bias_tiles, lam_params, subln_gain.reshape(DIFF_V_DIM, 1))


def _sigmoid(x):
    return 1.0 / (1.0 + jnp.exp(-x))


def _merge_kernel(xn_ref, mixed_ref, attn_ref, wgp_ref, wga_ref, bgp_ref, bga_ref, wup_ref, wua_ref, z_ref):
    xn = xn_ref[...]
    g_pool = _sigmoid(jnp.dot(xn, wgp_ref[...], preferred_element_type=F32) + bgp_ref[...])
    g_attn = _sigmoid(jnp.dot(xn, wga_ref[...], preferred_element_type=F32) + bga_ref[...])
    y_pool = jnp.dot(mixed_ref[...], wup_ref[...], preferred_element_type=F32)
    y_attn = jnp.dot(attn_ref[...], wua_ref[...], preferred_element_type=F32)
    z_ref[...] = (g_pool * y_pool + g_attn * y_attn).astype(z_ref.dtype)


def merge_call(xn, mixed, attn, w_gate, b_gate, w_up_pool, w_up_attn, tm=512, tn=512):
    n, d = xn.shape
    nb = d // tn
    return pl.pallas_call(
        _merge_kernel,
        grid=(nb, n // tm),
        in_specs=[pl.BlockSpec((tm, d), lambda j, m: (m, 0)),
                  pl.BlockSpec((tm, POOL_WIDTH), lambda j, m: (m, 0)),
                  pl.BlockSpec((tm, ATTN_WIDTH), lambda j, m: (m, 0)),
                  pl.BlockSpec((d, tn), lambda j, m: (0, j)),
                  pl.BlockSpec((d, tn), lambda j, m: (0, nb + j)),
                  pl.BlockSpec((1, tn), lambda j, m: (0, j)),
                  pl.BlockSpec((1, tn), lambda j, m: (0, nb + j)),
                  pl.BlockSpec((POOL_WIDTH, tn), lambda j, m: (0, j)),
                  pl.BlockSpec((ATTN_WIDTH, tn), lambda j, m: (0, j))],
        out_specs=pl.BlockSpec((tm, tn), lambda j, m: (m, j)),
        out_shape=jax.ShapeDtypeStruct((n, d), BF16),
        compiler_params=_cparams(("parallel", "parallel"), 48),
        name="gated_merge",
    )(xn, mixed, attn, w_gate, w_gate, b_gate, b_gate, w_up_pool, w_up_attn)


def _outproj_kernel(z_ref, w_ref, h_ref, gain_ref, wr_ref, br_ref, h1_ref, hn_ref, logit_ref):
    h1 = h_ref[...] + jnp.dot(z_ref[...], w_ref[...], preferred_element_type=F32)
    h1_ref[...] = h1
    hn = _rms(h1, gain_ref[...])
    hn_ref[...] = hn
    logit_ref[...] = jnp.dot(hn.astype(BF16), wr_ref[...], preferred_element_type=F32) + br_ref[...]


def outproj_call(z, w_out, h, gain, w_router, b_router, tm=256):
    n, d = h.shape
    return pl.pallas_call(
        _outproj_kernel,
        grid=(n // tm,),
        in_specs=[pl.BlockSpec((tm, d), lambda m: (m, 0)),
                  pl.BlockSpec((d, d), lambda m: (0, 0)),
                  pl.BlockSpec((tm, d), lambda m: (m, 0)),
                  pl.BlockSpec((1, d), lambda m: (0, 0)),
                  pl.BlockSpec((d, ROUTER_LANES), lambda m: (0, 0)),
                  pl.BlockSpec((1, ROUTER_LANES), lambda m: (0, 0))],
        out_specs=[pl.BlockSpec((tm, d), lambda m: (m, 0)),
                   pl.BlockSpec((tm, d), lambda m: (m, 0)),
                   pl.BlockSpec((tm, ROUTER_LANES), lambda m: (m, 0))],
        out_shape=[jax.ShapeDtypeStruct((n, d), F32),
                   jax.ShapeDtypeStruct((n, d), F32),
                   jax.ShapeDtypeStruct((n, ROUTER_LANES), F32)],
        compiler_params=_cparams(("parallel",), 56),
        name="out_proj_norm_router",
    )(z, w_out, h, gain.reshape(1, d), w_router, b_router)


def _route_kernel(logit_ref, id_ref, w_ref):
    x = logit_ref[...]
    lane = lax.broadcasted_iota(jnp.int32, x.shape, 1).astype(F32)
    big = float(ROUTER_LANES)

    def first_argmax(vals):
        top = jnp.max(vals, axis=1, keepdims=True)
        idx = jnp.min(jnp.where(vals == top, lane, big), axis=1, keepdims=True)
        return top, idx

    gmask = lane < N_GROUPS
    g_top, g_sel = first_argmax(jnp.where(gmask, x, -jnp.inf))
    g_weight = 1.0 / jnp.sum(jnp.where(gmask, jnp.exp(x - g_top), 0.0), axis=1, keepdims=True)
    lo = N_GROUPS + EXPERTS_PER_GROUP * g_sel
    e_vals = jnp.where((lane >= lo) & (lane < lo + EXPERTS_PER_GROUP), x, -jnp.inf)
    v1, i1 = first_argmax(e_vals)
    v2, i2 = first_argmax(jnp.where(lane == i1, -jnp.inf, e_vals))
    t = jnp.exp(v2 - v1)
    w1 = g_weight / (1.0 + t)
    w2 = g_weight * t / (1.0 + t)
    id_ref[...] = jnp.where(lane == 0, i1 - N_GROUPS, jnp.where(lane == 1, i2 - N_GROUPS, 0.0)).astype(jnp.int32)
    w_ref[...] = jnp.where(lane == 0, w1, jnp.where(lane == 1, w2, 0.0))


def route_call(logits, tm=512):
    n = logits.shape[0]
    spec = pl.BlockSpec((tm, ROUTER_LANES), lambda m: (m, 0))
    return pl.pallas_call(
        _route_kernel,
        grid=(n // tm,),
        in_specs=[spec],
        out_specs=[spec, spec],
        out_shape=[jax.ShapeDtypeStruct((n, ROUTER_LANES), jnp.int32),
                   jax.ShapeDtypeStruct((n, ROUTER_LANES), F32)],
        compiler_params=_cparams(("parallel",), 32),
        name="route_top2",
    )(logits)


def _dispatch_kernel(pos_ref, src_ref, init_ref, dst_ref, sem, *, chunk):
    del init_ref

    def row_copy(t, j):
        return pltpu.make_async_copy(src_ref.at[pl.ds(t, 1)],
                                     dst_ref.at[pl.ds(pos_ref[0, 0, 2 * t + j], 1)], sem)

    def start(t, c):
        row_copy(t, 0).start()
        row_copy(t, 1).start()
        return c

    def wait(t, c):
        row_copy(t, 0).wait()
        row_copy(t, 1).wait()
        return c

    lax.fori_loop(0, chunk, start, 0)
    lax.fori_loop(0, chunk, wait, 0)


def dispatch_call(hn, pos, n_rows_padded, chunk=256):
    n, d = hn.shape
    pos3 = pos.reshape(n // chunk, 1, 2 * chunk)
    init = jnp.zeros((n_rows_padded, d), hn.dtype)
    return pl.pallas_call(
        functools.partial(_dispatch_kernel, chunk=chunk),
        grid=(n // chunk,),
        in_specs=[pl.BlockSpec((1, 1, 2 * chunk), lambda c: (c, 0, 0), memory_space=pltpu.SMEM),
                  pl.BlockSpec((chunk, d), lambda c: (c, 0)),
                  pl.BlockSpec(memory_space=pl.ANY)],
        out_specs=pl.BlockSpec(memory_space=pl.ANY),
        out_shape=jax.ShapeDtypeStruct((n_rows_padded, d), hn.dtype),
        scratch_shapes=[pltpu.SemaphoreType.DMA(())],
        input_output_aliases={2: 0},
        compiler_params=_cparams(("arbitrary",), 32),
        name="dispatch_rows",
    )(pos3, hn, init)


def _expert_kernel(te_ref, nv_ref, x_ref, wg_ref, wu_ref, wd_ref, y_ref, wg_s, wu_s, wd_s):
    i = pl.program_id(0)
    valid = i < nv_ref[0]
    changed = jnp.logical_or(i == 0, te_ref[i] != te_ref[jnp.maximum(i - 1, 0)])

    @pl.when(jnp.logical_and(valid, changed))
    def _():
        wg_s[...] = wg_ref[...].astype(BF16)
        wu_s[...] = wu_ref[...].astype(BF16)
        wd_s[...] = wd_ref[...].astype(BF16)

    @pl.when(valid)
    def _():
        x = x_ref[...].astype(BF16)
        a = jnp.dot(x, wg_s[...], preferred_element_type=F32)
        b = jnp.dot(x, wu_s[...], preferred_element_type=F32)
        hmid = (a * _sigmoid(a) * b).astype(BF16)
        y_ref[...] = jnp.dot(hmid, wd_s[...], preferred_element_type=F32)

    @pl.when(jnp.logical_not(valid))
    def _():
        y_ref[...] = jnp.zeros(y_ref.shape, y_ref.dtype)


def expert_call(xs, tile_expert, n_valid_tiles, w_gate, w_up, w_down, layer):
    p, d = xs.shape
    tm = TM_EXPERT
    f = w_gate.shape[-1]
    grid_spec = pltpu.PrefetchScalarGridSpec(
        num_scalar_prefetch=2,
        grid=(p // tm,),
        in_specs=[pl.BlockSpec((tm, d), lambda i, te, nv: (jnp.minimum(i, nv[0] - 1), 0)),
                  pl.BlockSpec((None, None, d, f), lambda i, te, nv: (layer, te[i], 0, 0)),
                  pl.BlockSpec((None, None, d, f), lambda i, te, nv: (layer, te[i], 0, 0)),
                  pl.BlockSpec((None, None, f, d), lambda i, te, nv: (layer, te[i], 0, 0))],
        out_specs=pl.BlockSpec((tm, d), lambda i, te, nv: (i, 0)),
        scratch_shapes=[pltpu.VMEM((d, f), BF16), pltpu.VMEM((d, f), BF16), pltpu.VMEM((f, d), BF16)],
    )
    return pl.pallas_call(
        _expert_kernel,
        grid_spec=grid_spec,
        out_shape=jax.ShapeDtypeStruct((p, d), F32),
        compiler_params=_cparams(("arbitrary",), 58),
        name="expert_swiglu",
    )(tile_expert, n_valid_tiles, xs, w_gate, w_up, w_down)


def _combine_kernel(pos_ref, ys_ref, h_ref, w_ref, gain_ref, h2_ref, xn_ref, buf, sem, *, chunk):
    def row_copy(t, j):
        return pltpu.make_async_copy(ys_ref.at[pl.ds(pos_ref[0, 0, 2 * t + j], 1)],
                                     buf.at[j, pl.ds(t, 1)], sem)

    def start(t, c):
        row_copy(t, 0).start()
        row_copy(t, 1).start()
        return c

    def wait(t, c):
        row_copy(t, 0).wait()
        row_copy(t, 1).wait()
        return c

    lax.fori_loop(0, chunk, start, 0)
    lax.fori_loop(0, chunk, wait, 0)
    w = w_ref[...]
    h2 = h_ref[...] + w[:, 0:1] * buf[0] + w[:, 1:2] * buf[1]
    h2_ref[...] = h2
    xn_ref[...] = _rms(h2, gain_ref[...]).astype(xn_ref.dtype)


def combine_call(ys, pos, h1, weights, next_gain, xn_dtype, chunk=128):
    n, d = h1.shape
    pos3 = pos.reshape(n // chunk, 1, 2 * chunk)
    return pl.pallas_call(
        functools.partial(_combine_kernel, chunk=chunk),
        grid=(n // chunk,),
        in_specs=[pl.BlockSpec((1, 1, 2 * chunk), lambda c: (c, 0, 0), memory_space=pltpu.SMEM),
                  pl.BlockSpec(memory_space=pl.ANY),
                  pl.BlockSpec((chunk, d), lambda c: (c, 0)),
                  pl.BlockSpec((chunk, ROUTER_LANES), lambda c: (c, 0)),
                  pl.BlockSpec((1, d), lambda c: (0, 0))],
        out_specs=[pl.BlockSpec((chunk, d), lambda c: (c, 0)),
                   pl.BlockSpec((chunk, d), lambda c: (c, 0))],
        out_shape=[jax.ShapeDtypeStruct((n, d), F32),
                   jax.ShapeDtypeStruct((n, d), xn_dtype)],
        scratch_shapes=[pltpu.VMEM((2, chunk, d), F32), pltpu.SemaphoreType.DMA(())],
        compiler_params=_cparams(("arbitrary",), 32),
        name="combine_rows",
    )(pos3, ys, h1, weights, next_gain.reshape(1, d))


def _dispatch_plan(expert_ids, n_tiles):
    tm = TM_EXPERT
    onehot = (expert_ids[:, :, None] == jnp.arange(N_EXPERTS, dtype=jnp.int32)).astype(jnp.int32)
    per_token = onehot.sum(axis=1)
    before = jnp.cumsum(per_token, axis=0) - per_token
    rank = jnp.take_along_axis(before, expert_ids, axis=1)
    counts = per_token.sum(axis=0)
    padded = ((counts + tm - 1) // tm) * tm
    ends = jnp.cumsum(padded)
    pos = (ends - padded)[expert_ids] + rank
    n_valid = (ends[-1] // tm).astype(jnp.int32)
    tile_start = jnp.arange(n_tiles, dtype=jnp.int32) * tm
    tile_expert = jnp.searchsorted(ends, tile_start, side="right").astype(jnp.int32)
    last_expert = jnp.searchsorted(ends, ends[-1] - 1, side="right").astype(jnp.int32)
    tile_expert = jnp.where(tile_start < ends[-1], tile_expert, last_expert)
    return pos.astype(jnp.int32), tile_expert, n_valid.reshape(1)


def kernel(x, rel_bias_table, norm_mix_gain, w_in, w_merge_gate, b_merge_gate, pool_mix, pool_scale, w_up_pool, lambda_q1, lambda_k1, lambda_q2, lambda_k2, subln_gain, w_up_attn, w_out, norm_ffn_gain, w_router_group, b_router_group, w_router_expert, b_router_expert, w_expert_gate, w_expert_up, w_expert_down, final_norm_gain):
    batch, seq, d = x.shape
    depth = w_in.shape[0]
    n = batch * seq
    n_tiles = (2 * n + N_EXPERTS * (TM_EXPERT - 1)) // TM_EXPERT + 1
    n_rows_padded = n_tiles * TM_EXPERT

    bias_tiles = bias_tiles_call(rel_bias_table)
    h = x.reshape(n, d)
    xn = rms_norm_call(h, norm_mix_gain[0], BF16)
    q_scale = LOG2E * DIFF_HEAD_DIM ** -0.5

    for l in range(depth):
        w_in_l = w_in[l].astype(BF16)
        u = proj_call(xn, w_in_l, 0, POOL_WIDTH // 512, 512, F32, name="in_proj_pool")
        qkv = proj_call(xn, w_in_l, 1, 3, ATTN_WIDTH, BF16, first_block_scale=q_scale, name="in_proj_qkv")
        mixed = pool_call(u, pool_mix[l].astype(BF16), pool_scale[l], seq)
        lambda_init = 0.8 - 0.6 * math.exp(-0.3 * l)
        lam_params = jnp.stack([lambda_q1[l], lambda_k1[l], lambda_q2[l], lambda_k2[l]])
        attn = attn_call(qkv.reshape(batch, seq, 3 * ATTN_WIDTH), bias_tiles, lam_params, subln_gain[l],
                         lambda_init, batch, seq).reshape(n, ATTN_WIDTH)
        z = merge_call(xn, mixed, attn, w_merge_gate[l].astype(BF16), b_merge_gate[l].reshape(1, -1),
                       w_up_pool[l].astype(BF16), w_up_attn[l].astype(BF16))

        w_router = jnp.concatenate(
            [w_router_group[l], jnp.transpose(w_router_expert[l], (1, 0, 2)).reshape(d, N_EXPERTS),
             jnp.zeros((d, ROUTER_LANES - N_GROUPS - N_EXPERTS), F32)], axis=1).astype(BF16)
        b_router = jnp.concatenate(
            [b_router_group[l], b_router_expert[l].reshape(-1),
             jnp.zeros((ROUTER_LANES - N_GROUPS - N_EXPERTS,), F32)]).reshape(1, ROUTER_LANES)
        h1, hn, logits = outproj_call(z, w_out[l].astype(BF16), h, norm_ffn_gain[l], w_router, b_router)

        ids, weights = route_call(logits)
        pos, tile_expert, n_valid = _dispatch_plan(ids[:, :2], n_tiles)
        xs = dispatch_call(hn, pos, n_rows_padded)
        ys = expert_call(xs, tile_expert, n_valid, w_expert_gate, w_expert_up, w_expert_down, l)
        last = l == depth - 1
        next_gain = final_norm_gain if last else norm_mix_gain[l + 1]
        h, xn = combine_call(ys, pos, h1, weights, next_gain, F32 if last else BF16)

    return xn.reshape(batch, seq, d)
```

```python
import functools
import math

import numpy as np
import jax
import jax.numpy as jnp
from jax import lax
from jax.experimental import pallas as pl
from jax.experimental.pallas import tpu as pltpu

F32 = jnp.float32
BF16 = jnp.bfloat16

D_MODEL = 2048
POOL_WIDTH = 1024
POOL_WINDOWS = (2, 4, 8, 16)
POOL_GROUP_DIM = 256
POOL_HALO = 16
DIFF_HEADS = 8
DIFF_HEAD_DIM = 64
DIFF_V_DIM = 128
ATTN_WIDTH = 1024
REL_BUCKETS = 32
REL_MAX_DISTANCE = 128
N_GROUPS = 4
EXPERTS_PER_GROUP = 8
N_EXPERTS = 32
D_EXPERT = 512
EPS = 1e-6
NEG_INF = -1e30
LOG2E = 1.4426950408889634

ROUTER_LANES = 128
TQ = 256
TK = 256
KV_SUPER = 2
HEADS_PER_STEP = 2
TM_EXPERT = 256
MIB = 1024 * 1024


def _cparams(sem, vmem_mib):
    return pltpu.CompilerParams(dimension_semantics=sem, vmem_limit_bytes=vmem_mib * MIB)


def _rms(xf, gain):
    ms = jnp.mean(xf * xf, axis=-1, keepdims=True)
    return xf * lax.rsqrt(ms + EPS) * gain


def _norm_kernel(h_ref, g_ref, o_ref):
    o_ref[...] = _rms(h_ref[...], g_ref[...]).astype(o_ref.dtype)


def rms_norm_call(h, gain, out_dtype, tm=512):
    n, d = h.shape
    return pl.pallas_call(
        _norm_kernel,
        grid=(n // tm,),
        in_specs=[pl.BlockSpec((tm, d), lambda m: (m, 0)),
                  pl.BlockSpec((1, d), lambda m: (0, 0))],
        out_specs=pl.BlockSpec((tm, d), lambda m: (m, 0)),
        out_shape=jax.ShapeDtypeStruct((n, d), out_dtype),
        compiler_params=_cparams(("parallel",), 32),
        name="rms_norm",
    )(h, gain.reshape(1, d))


def _proj_kernel(x_ref, w_ref, o_ref, *, first_block_scale):
    acc = jnp.dot(x_ref[...], w_ref[...], preferred_element_type=F32)
    if first_block_scale is not None:
        acc = acc * jnp.where(pl.program_id(0) == 0, first_block_scale, 1.0).astype(F32)
    o_ref[...] = acc.astype(o_ref.dtype)


def proj_call(x, w, col_block0, n_col_blocks, tn, out_dtype, first_block_scale=None, tm=512, name="proj"):
    n, k = x.shape
    return pl.pallas_call(
        functools.partial(_proj_kernel, first_block_scale=first_block_scale),
        grid=(n_col_blocks, n // tm),
        in_specs=[pl.BlockSpec((tm, k), lambda j, m: (m, 0)),
                  pl.BlockSpec((k, tn), lambda j, m: (0, col_block0 + j))],
        out_specs=pl.BlockSpec((tm, tn), lambda j, m: (m, j)),
        out_shape=jax.ShapeDtypeStruct((n, n_col_blocks * tn), out_dtype),
        compiler_params=_cparams(("parallel", "parallel"), 40),
        name=name,
    )(x, w)


def _pool_kernel(cur_ref, prev_ref, mix_ref, scale_ref, o_ref, *, tm, seq):
    m = pl.program_id(0)
    row0 = (m * tm) % seq
    cur = cur_ref[...]
    prev = jnp.where(row0 == 0, 0.0, prev_ref[...])
    pos = row0 + lax.broadcasted_iota(jnp.int32, (tm, 1), 0)
    outs = []
    for g, w in enumerate(POOL_WINDOWS):
        sl = slice(g * POOL_GROUP_DIM, (g + 1) * POOL_GROUP_DIM)
        x = jnp.concatenate([prev[:, sl], cur[:, sl]], axis=0)
        s, d = x, 1
        while d < w:
            s = s[:-d] + s[d:]
            d *= 2
        start = POOL_HALO - w + 1
        wsum = s[start:start + tm]
        count = jnp.minimum(pos + 1, w).astype(F32)
        pooled = wsum / count - cur[:, sl]
        mixed = jnp.dot(pooled.astype(BF16), mix_ref[g], preferred_element_type=F32)
        outs.append(mixed * scale_ref[:, sl])
    o_ref[...] = jnp.concatenate(outs, axis=1).astype(o_ref.dtype)


def pool_call(u, mix_bf16, scale, seq, tm=512):
    n, c = u.shape
    blocks_per_tile = tm // POOL_HALO
    return pl.pallas_call(
        functools.partial(_pool_kernel, tm=tm, seq=seq),
        grid=(n // tm,),
        in_specs=[pl.BlockSpec((tm, c), lambda m: (m, 0)),
                  pl.BlockSpec((POOL_HALO, c), lambda m: (jnp.maximum(m * blocks_per_tile - 1, 0), 0)),
                  pl.BlockSpec(mix_bf16.shape, lambda m: (0, 0, 0)),
                  pl.BlockSpec((1, c), lambda m: (0, 0))],
        out_specs=pl.BlockSpec((tm, c), lambda m: (m, 0)),
        out_shape=jax.ShapeDtypeStruct((n, c), BF16),
        compiler_params=_cparams(("parallel",), 32),
        name="pool_mixer",
    )(u, u, mix_bf16, scale.reshape(1, c))


N_BIAS_TILES = 4


def _bucket_tiles():
    kk = np.arange(TK)[:, None]
    qq = np.arange(TQ)[None, :]
    tiles = []
    for rel in (0, 1, 2, -1):
        n = rel * TK + qq - kk
        max_exact = REL_BUCKETS // 2
        nf = np.maximum(n, 1).astype(np.float64)
        large = max_exact + (np.log(nf / max_exact) / math.log(REL_MAX_DISTANCE / max_exact)
                             * (REL_BUCKETS - max_exact)).astype(np.int64)
        large = np.minimum(large, REL_BUCKETS - 1)
        bucket = np.where(n < max_exact, n, large)
        tiles.append(np.where(n < 0, -1, bucket))
    return np.stack(tiles).astype(np.int32)


def _bias_kernel(table_ref, bucket_ref, o_ref):
    h = pl.program_id(0)
    bucket = bucket_ref[...]
    acc = jnp.full(bucket.shape, NEG_INF, F32)
    for b in range(REL_BUCKETS):
        acc = jnp.where(bucket == b, table_ref[b * DIFF_HEADS + h] * LOG2E, acc)
    o_ref[...] = acc


def bias_tiles_call(rel_table):
    bucket = jnp.asarray(_bucket_tiles())
    return pl.pallas_call(
        _bias_kernel,
        grid=(DIFF_HEADS,),
        in_specs=[pl.BlockSpec(memory_space=pltpu.SMEM),
                  pl.BlockSpec(bucket.shape, lambda h: (0, 0, 0))],
        out_specs=pl.BlockSpec((None,) + bucket.shape, lambda h: (h, 0, 0, 0)),
        out_shape=jax.ShapeDtypeStruct((DIFF_HEADS,) + bucket.shape, F32),
        compiler_params=_cparams(("parallel",), 32),
        name="rel_bias_tiles",
    )(rel_table.reshape(-1), bucket)


def _attn_kernel(q_ref, k_ref, v_ref, bias_ref, lam_ref, gain_ref, o_ref, vt_ref, s_ref, acc_ref,
                 *, lambda_init, n_super):
    qi = pl.program_id(2)
    tks = KV_SUPER * TK
    heads = range(HEADS_PER_STEP)

    @pl.when(qi == 0)
    def _():
        for hh in heads:
            cols = slice(hh * DIFF_V_DIM, (hh + 1) * DIFF_V_DIM)
            for c in range(n_super):
                vt_ref[hh, c] = v_ref[c * tks:(c + 1) * tks, cols].astype(F32).T.astype(BF16)

    qds = []
    for hh in heads:
        q = q_ref[:, hh * DIFF_V_DIM:(hh + 1) * DIFF_V_DIM]
        lane = lax.broadcasted_iota(jnp.int32, q.shape, 1)
        zero = jnp.zeros_like(q)
        qds.append(jnp.concatenate([jnp.where(lane < DIFF_HEAD_DIM, q, zero),
                                    jnp.where(lane >= DIFF_HEAD_DIM, q, zero)], axis=0))

    def scores(t, hh):
        kb = k_ref[pl.ds(pl.multiple_of(t * tks, tks), tks), hh * DIFF_V_DIM:(hh + 1) * DIFF_V_DIM]
        s = lax.dot_general(kb, qds[hh], (((1,), (1,)), ((), ())), preferred_element_type=F32)
        parts = []
        for u in range(KV_SUPER):
            rel = qi - (t * KV_SUPER + u)
            bias = bias_ref[hh, jnp.where(rel < 0, N_BIAS_TILES - 1, jnp.minimum(rel, 2))]
            parts.append(s[u * TK:(u + 1) * TK] + jnp.concatenate([bias, bias], axis=1))
        s = jnp.concatenate(parts, axis=0)
        s_ref[hh] = s
        return jnp.max(s, axis=0, keepdims=True)

    def softmax_step(t, hh, m_prev, m_cur, l):
        alpha = jnp.exp2(m_prev - m_cur)
        p = jnp.exp2(s_ref[hh] - m_cur)
        l_new = l * alpha + jnp.sum(p, axis=0, keepdims=True)
        pv = jnp.dot(vt_ref[hh, t], p.astype(BF16), preferred_element_type=F32)
        acc_ref[hh] = acc_ref[hh] * alpha + pv
        return l_new

    last = qi // KV_SUPER
    acc_ref[...] = jnp.zeros(acc_ref.shape, F32)
    neg = jnp.full((1, 2 * TQ), NEG_INF, F32)
    init = tuple((neg, jnp.maximum(neg, scores(0, hh)), jnp.zeros((1, 2 * TQ), F32)) for hh in heads)

    def body(t, carry):
        out = []
        for hh in heads:
            m_prev, m_cur, l = carry[hh]
            l = softmax_step(t, hh, m_prev, m_cur, l)
            m_next = jnp.maximum(m_cur, scores(t + 1, hh))
            out.append((m_cur, m_next, l))
        return tuple(out)

    carry = lax.fori_loop(0, last, body, init)

    lam_p = lam_ref[...]
    lam = (jnp.exp(jnp.sum(lam_p[0:1] * lam_p[1:2], axis=1, keepdims=True))
           - jnp.exp(jnp.sum(lam_p[2:3] * lam_p[3:4], axis=1, keepdims=True)) + lambda_init)
    for hh in heads:
        m_prev, m_cur, l = carry[hh]
        l = softmax_step(last, hh, m_prev, m_cur, l)
        acc = acc_ref[hh]
        o = acc[:, :TQ] / l[:, :TQ] - lam * (acc[:, TQ:] / l[:, TQ:])
        ms = jnp.mean(o * o, axis=0, keepdims=True)
        y = o * lax.rsqrt(ms + EPS) * gain_ref[...] * (1.0 - lambda_init)
        o_ref[:, hh * DIFF_V_DIM:(hh + 1) * DIFF_V_DIM] = y.T.astype(o_ref.dtype)


def attn_call(qkv, bias_tiles, lam_params, subln_gain, lambda_init, batch, seq):
    tks = KV_SUPER * TK
    n_super = seq // tks
    hps = HEADS_PER_STEP
    width = hps * DIFF_V_DIM
    groups = DIFF_HEADS // hps
    return pl.pallas_call(
        functools.partial(_attn_kernel, lambda_init=lambda_init, n_super=n_super),
        grid=(batch, groups, seq // TQ),
        in_specs=[pl.BlockSpec((None, TQ, width), lambda b, g, i: (b, i, g)),
                  pl.BlockSpec((None, seq, width), lambda b, g, i: (b, 0, groups + g)),
                  pl.BlockSpec((None, seq, width), lambda b, g, i: (b, 0, 2 * groups + g)),
                  pl.BlockSpec((hps, N_BIAS_TILES, TK, TQ), lambda b, g, i: (g, 0, 0, 0)),
                  pl.BlockSpec((4, DIFF_HEAD_DIM), lambda b, g, i: (0, 0)),
                  pl.BlockSpec((DIFF_V_DIM, 1), lambda b, g, i: (0, 0))],
        out_specs=pl.BlockSpec((None, TQ, width), lambda b, g, i: (b, i, g)),
        out_shape=jax.ShapeDtypeStruct((batch, seq, ATTN_WIDTH), BF16),
        scratch_shapes=[pltpu.VMEM((hps, n_super, DIFF_V_DIM, tks), BF16),
                        pltpu.VMEM((hps, tks, 2 * TQ), F32),
                        pltpu.VMEM((hps, DIFF_V_DIM, 2 * TQ), F32)],
        compiler_params=_cparams(("parallel", "parallel", "arbitrary"), 48),
        name="diff_attention",
    )(qkv, qkv, qkv, bias_tiles, lam_params, subln_gain.reshape(DIFF_V_DIM, 1))


def _sigmoid(x):
    return 1.0 / (1.0 + jnp.exp(-x))


def _merge_kernel(xn_ref, mixed_ref, attn_ref, wgp_ref, wga_ref, bgp_ref, bga_ref, wup_ref, wua_ref, z_ref):
    xn = xn_ref[...]
    g_pool = _sigmoid(jnp.dot(xn, wgp_ref[...], preferred_element_type=F32) + bgp_ref[...])
    g_attn = _sigmoid(jnp.dot(xn, wga_ref[...], preferred_element_type=F32) + bga_ref[...])
    y_pool = jnp.dot(mixed_ref[...], wup_ref[...], preferred_element_type=F32)
    y_attn = jnp.dot(attn_ref[...], wua_ref[...], preferred_element_type=F32)
    z_ref[...] = (g_pool * y_pool + g_attn * y_attn).astype(z_ref.dtype)


def merge_call(xn, mixed, attn, w_gate, b_gate, w_up_pool, w_up_attn, tm=512, tn=512):
    n, d = xn.shape
    nb = d // tn
    return pl.pallas_call(
        _merge_kernel,
        grid=(nb, n // tm),
        in_specs=[pl.BlockSpec((tm, d), lambda j, m: (m, 0)),
                  pl.BlockSpec((tm, POOL_WIDTH), lambda j, m: (m, 0)),
                  pl.BlockSpec((tm, ATTN_WIDTH), lambda j, m: (m, 0)),
                  pl.BlockSpec((d, tn), lambda j, m: (0, j)),
                  pl.BlockSpec((d, tn), lambda j, m: (0, nb + j)),
                  pl.BlockSpec((1, tn), lambda j, m: (0, j)),
                  pl.BlockSpec((1, tn), lambda j, m: (0, nb + j)),
                  pl.BlockSpec((POOL_WIDTH, tn), lambda j, m: (0, j)),
                  pl.BlockSpec((ATTN_WIDTH, tn), lambda j, m: (0, j))],
        out_specs=pl.BlockSpec((tm, tn), lambda j, m: (m, j)),
        out_shape=jax.ShapeDtypeStruct((n, d), BF16),
        compiler_params=_cparams(("parallel", "parallel"), 48),
        name="gated_merge",
    )(xn, mixed, attn, w_gate, w_gate, b_gate, b_gate, w_up_pool, w_up_attn)


def _outproj_kernel(z_ref, w_ref, h_ref, gain_ref, wr_ref, br_ref, h1_ref, hn_ref, logit_ref):
    h1 = h_ref[...] + jnp.dot(z_ref[...], w_ref[...], preferred_element_type=F32)
    h1_ref[...] = h1
    hn = _rms(h1, gain_ref[...])
    hn_ref[...] = hn
    logit_ref[...] = jnp.dot(hn.astype(BF16), wr_ref[...], preferred_element_type=F32) + br_ref[...]


def outproj_call(z, w_out, h, gain, w_router, b_router, tm=256):
    n, d = h.shape
    return pl.pallas_call(
        _outproj_kernel,
        grid=(n // tm,),
        in_specs=[pl.BlockSpec((tm, d), lambda m: (m, 0)),
                  pl.BlockSpec((d, d), lambda m: (0, 0)),
                  pl.BlockSpec((tm, d), lambda m: (m, 0)),
                  pl.BlockSpec((1, d), lambda m: (0, 0)),
                  pl.BlockSpec((d, ROUTER_LANES), lambda m: (0, 0)),
                  pl.BlockSpec((1, ROUTER_LANES), lambda m: (0, 0))],
        out_specs=[pl.BlockSpec((tm, d), lambda m: (m, 0)),
                   pl.BlockSpec((tm, d), lambda m: (m, 0)),
                   pl.BlockSpec((tm, ROUTER_LANES), lambda m: (m, 0))],
        out_shape=[jax.ShapeDtypeStruct((n, d), F32),
                   jax.ShapeDtypeStruct((n, d), F32),
                   jax.ShapeDtypeStruct((n, ROUTER_LANES), F32)],
        compiler_params=_cparams(("parallel",), 56),
        name="out_proj_norm_router",
    )(z, w_out, h, gain.reshape(1, d), w_router, b_router)


def _route_kernel(logit_ref, id_ref, w_ref):
    x = logit_ref[...]
    lane = lax.broadcasted_iota(jnp.int32, x.shape, 1).astype(F32)
    big = float(ROUTER_LANES)

    def first_argmax(vals):
        top = jnp.max(vals, axis=1, keepdims=True)
        idx = jnp.min(jnp.where(vals == top, lane, big), axis=1, keepdims=True)
        return top, idx

    gmask = lane < N_GROUPS
    g_top, g_sel = first_argmax(jnp.where(gmask, x, -jnp.inf))
    g_weight = 1.0 / jnp.sum(jnp.where(gmask, jnp.exp(x - g_top), 0.0), axis=1, keepdims=True)
    lo = N_GROUPS + EXPERTS_PER_GROUP * g_sel
    e_vals = jnp.where((lane >= lo) & (lane < lo + EXPERTS_PER_GROUP), x, -jnp.inf)
    v1, i1 = first_argmax(e_vals)
    v2, i2 = first_argmax(jnp.where(lane == i1, -jnp.inf, e_vals))
    t = jnp.exp(v2 - v1)
    w1 = g_weight / (1.0 + t)
    w2 = g_weight * t / (1.0 + t)
    id_ref[...] = jnp.where(lane == 0, i1 - N_GROUPS, jnp.where(lane == 1, i2 - N_GROUPS, 0.0)).astype(jnp.int32)
    w_ref[...] = jnp.where(lane == 0, w1, jnp.where(lane == 1, w2, 0.0))


def route_call(logits, tm=512):
    n = logits.shape[0]
    spec = pl.BlockSpec((tm, ROUTER_LANES), lambda m: (m, 0))
    return pl.pallas_call(
        _route_kernel,
        grid=(n // tm,),
        in_specs=[spec],
        out_specs=[spec, spec],
        out_shape=[jax.ShapeDtypeStruct((n, ROUTER_LANES), jnp.int32),
                   jax.ShapeDtypeStruct((n, ROUTER_LANES), F32)],
        compiler_params=_cparams(("parallel",), 32),
        name="route_top2",
    )(logits)


def _dispatch_kernel(pos_ref, src_ref, init_ref, dst_ref, sem, *, chunk):
    del init_ref

    def row_copy(t, j):
        return pltpu.make_async_copy(src_ref.at[pl.ds(t, 1)],
                                     dst_ref.at[pl.ds(pos_ref[0, 0, 2 * t + j], 1)], sem)

    def start(t, c):
        row_copy(t, 0).start()
        row_copy(t, 1).start()
        return c

    def wait(t, c):
        row_copy(t, 0).wait()
        row_copy(t, 1).wait()
        return c

    lax.fori_loop(0, chunk, start, 0)
    lax.fori_loop(0, chunk, wait, 0)


def dispatch_call(hn, pos, n_rows_padded, chunk=256):
    n, d = hn.shape
    pos3 = pos.reshape(n // chunk, 1, 2 * chunk)
    init = jnp.zeros((n_rows_padded, d), hn.dtype)
    return pl.pallas_call(
        functools.partial(_dispatch_kernel, chunk=chunk),
        grid=(n // chunk,),
        in_specs=[pl.BlockSpec((1, 1, 2 * chunk), lambda c: (c, 0, 0), memory_space=pltpu.SMEM),
                  pl.BlockSpec((chunk, d), lambda c: (c, 0)),
                  pl.BlockSpec(memory_space=pl.ANY)],
        out_specs=pl.BlockSpec(memory_space=pl.ANY),
        out_shape=jax.ShapeDtypeStruct((n_rows_padded, d), hn.dtype),
        scratch_shapes=[pltpu.SemaphoreType.DMA(())],
        input_output_aliases={2: 0},
        compiler_params=_cparams(("arbitrary",), 32),
        name="dispatch_rows",
    )(pos3, hn, init)


def _expert_kernel(te_ref, nv_ref, x_ref, wg_ref, wu_ref, wd_ref, y_ref, wg_s, wu_s, wd_s):
    i = pl.program_id(0)
    valid = i < nv_ref[0]
    changed = jnp.logical_or(i == 0, te_ref[i] != te_ref[jnp.maximum(i - 1, 0)])

    @pl.when(jnp.logical_and(valid, changed))
    def _():
        wg_s[...] = wg_ref[...].astype(BF16)
        wu_s[...] = wu_ref[...].astype(BF16)
        wd_s[...] = wd_ref[...].astype(BF16)

    @pl.when(valid)
    def _():
        x = x_ref[...].astype(BF16)
        a = jnp.dot(x, wg_s[...], preferred_element_type=F32)
        b = jnp.dot(x, wu_s[...], preferred_element_type=F32)
        hmid = (a * _sigmoid(a) * b).astype(BF16)
        y_ref[...] = jnp.dot(hmid, wd_s[...], preferred_element_type=F32)

    @pl.when(jnp.logical_not(valid))
    def _():
        y_ref[...] = jnp.zeros(y_ref.shape, y_ref.dtype)


def expert_call(xs, tile_expert, n_valid_tiles, w_gate, w_up, w_down, layer):
    p, d = xs.shape
    tm = TM_EXPERT
    f = w_gate.shape[-1]
    grid_spec = pltpu.PrefetchScalarGridSpec(
        num_scalar_prefetch=2,
        grid=(p // tm,),
        in_specs=[pl.BlockSpec((tm, d), lambda i, te, nv: (jnp.minimum(i, nv[0] - 1), 0)),
                  pl.BlockSpec((None, None, d, f), lambda i, te, nv: (layer, te[i], 0, 0)),
                  pl.BlockSpec((None, None, d, f), lambda i, te, nv: (layer, te[i], 0, 0)),
                  pl.BlockSpec((None, None, f, d), lambda i, te, nv: (layer, te[i], 0, 0))],
        out_specs=pl.BlockSpec((tm, d), lambda i, te, nv: (i, 0)),
        scratch_shapes=[pltpu.VMEM((d, f), BF16), pltpu.VMEM((d, f), BF16), pltpu.VMEM((f, d), BF16)],
    )
    return pl.pallas_call(
        _expert_kernel,
        grid_spec=grid_spec,
        out_shape=jax.ShapeDtypeStruct((p, d), F32),
        compiler_params=_cparams(("arbitrary",), 58),
        name="expert_swiglu",
    )(tile_expert, n_valid_tiles, xs, w_gate, w_up, w_down)


def _combine_kernel(pos_ref, ys_ref, h_ref, w_ref, gain_ref, h2_ref, xn_ref, buf, sem, *, chunk):
    def row_copy(t, j):
        return pltpu.make_async_copy(ys_ref.at[pl.ds(pos_ref[0, 0, 2 * t + j], 1)],
                                     buf.at[j, pl.ds(t, 1)], sem)

    def start(t, c):
        row_copy(t, 0).start()
        row_copy(t, 1).start()
        return c

    def wait(t, c):
        row_copy(t, 0).wait()
        row_copy(t, 1).wait()
        return c

    lax.fori_loop(0, chunk, start, 0)
    lax.fori_loop(0, chunk, wait, 0)
    w = w_ref[...]
    h2 = h_ref[...] + w[:, 0:1] * buf[0] + w[:, 1:2] * buf[1]
    h2_ref[...] = h2
    xn_ref[...] = _rms(h2, gain_ref[...]).astype(xn_ref.dtype)


def combine_call(ys, pos, h1, weights, next_gain, xn_dtype, chunk=128):
    n, d = h1.shape
    pos3 = pos.reshape(n // chunk, 1, 2 * chunk)
    return pl.pallas_call(
        functools.partial(_combine_kernel, chunk=chunk),
        grid=(n // chunk,),
        in_specs=[pl.BlockSpec((1, 1, 2 * chunk), lambda c: (c, 0, 0), memory_space=pltpu.SMEM),
                  pl.BlockSpec(memory_space=pl.ANY),
                  pl.BlockSpec((chunk, d), lambda c: (c, 0)),
                  pl.BlockSpec((chunk, ROUTER_LANES), lambda c: (c, 0)),
                  pl.BlockSpec((1, d), lambda c: (0, 0))],
        out_specs=[pl.BlockSpec((chunk, d), lambda c: (c, 0)),
                   pl.BlockSpec((chunk, d), lambda c: (c, 0))],
        out_shape=[jax.ShapeDtypeStruct((n, d), F32),
                   jax.ShapeDtypeStruct((n, d), xn_dtype)],
        scratch_shapes=[pltpu.VMEM((2, chunk, d), F32), pltpu.SemaphoreType.DMA(())],
        compiler_params=_cparams(("arbitrary",), 32),
        name="combine_rows",
    )(pos3, ys, h1, weights, next_gain.reshape(1, d))


def _dispatch_plan(expert_ids, n_tiles):
    tm = TM_EXPERT
    onehot = (expert_ids[:, :, None] == jnp.arange(N_EXPERTS, dtype=jnp.int32)).astype(jnp.int32)
    per_token = onehot.sum(axis=1)
    before = jnp.cumsum(per_token, axis=0) - per_token
    rank = jnp.take_along_axis(before, expert_ids, axis=1)
    counts = per_token.sum(axis=0)
    padded = ((counts + tm - 1) // tm) * tm
    ends = jnp.cumsum(padded)
    pos = (ends - padded)[expert_ids] + rank
    n_valid = (ends[-1] // tm).astype(jnp.int32)
    tile_start = jnp.arange(n_tiles, dtype=jnp.int32) * tm
    tile_expert = jnp.searchsorted(ends, tile_start, side="right").astype(jnp.int32)
    last_expert = jnp.searchsorted(ends, ends[-1] - 1, side="right").astype(jnp.int32)
    tile_expert = jnp.where(tile_start < ends[-1], tile_expert, last_expert)
    return pos.astype(jnp.int32), tile_expert, n_valid.reshape(1)


def kernel(x, rel_bias_table, norm_mix_gain, w_in, w_merge_gate, b_merge_gate, pool_mix, pool_scale, w_up_pool, lambda_q1, lambda_k1, lambda_q2, lambda_k2, subln_gain, w_up_attn, w_out, norm_ffn_gain, w_router_group, b_router_group, w_router_expert, b_router_expert, w_expert_gate, w_expert_up, w_expert_down, final_norm_gain):
    batch, seq, d = x.shape
    depth = w_in.shape[0]
    n = batch * seq
    n_tiles = (2 * n + N_EXPERTS * (TM_EXPERT - 1)) // TM_EXPERT + 1
    n_rows_padded = n_tiles * TM_EXPERT

    bias_tiles = bias_tiles_call(rel_bias_table)
    h = x.reshape(n, d)
    xn = rms_norm_call(h, norm_mix_gain[0], BF16)
    q_scale = LOG2E * DIFF_HEAD_DIM ** -0.5

    for l in range(depth):
        w_in_l = w_in[l].astype(BF16)
        u = proj_call(xn, w_in_l, 0, POOL_WIDTH // 512, 512, F32, name="in_proj_pool")
        qkv = proj_call(xn, w_in_l, 1, 3, ATTN_WIDTH, BF16, first_block_scale=q_scale, name="in_proj_qkv")
        mixed = pool_call(u, pool_mix[l].astype(BF16), pool_scale[l], seq)
        lambda_init = 0.8 - 0.6 * math.exp(-0.3 * l)
        lam_params = jnp.stack([lambda_q1[l], lambda_k1[l], lambda_q2[l], lambda_k2[l]])
        attn = attn_call(qkv.reshape(batch, seq, 3 * ATTN_WIDTH), bias_tiles, lam_params, subln_gain[l],
                         lambda_init, batch, seq).reshape(n, ATTN_WIDTH)
        z = merge_call(xn, mixed, attn, w_merge_gate[l].astype(BF16), b_merge_gate[l].reshape(1, -1),
                       w_up_pool[l].astype(BF16), w_up_attn[l].astype(BF16))

        w_router = jnp.concatenate(
            [w_router_group[l], jnp.transpose(w_router_expert[l], (1, 0, 2)).reshape(d, N_EXPERTS),
             jnp.zeros((d, ROUTER_LANES - N_GROUPS - N_EXPERTS), F32)], axis=1).astype(BF16)
        b_router = jnp.concatenate(
            [b_router_group[l], b_router_expert[l].reshape(-1),
             jnp.zeros((ROUTER_LANES - N_GROUPS - N_EXPERTS,), F32)]).reshape(1, ROUTER_LANES)
        h1, hn, logits = outproj_call(z, w_out[l].astype(BF16), h, norm_ffn_gain[l], w_router, b_router)

        ids, weights = route_call(logits)
        pos, tile_expert, n_valid = _dispatch_plan(ids[:, :2], n_tiles)
        xs = dispatch_call(hn, pos, n_rows_padded)
        ys = expert_call(xs, tile_expert, n_valid, w_expert_gate, w_expert_up, w_expert_down, l)
        last = l == depth - 1
        next_gain = final_norm_gain if last else norm_mix_gain[l + 1]
        h, xn = combine_call(ys, pos, h1, weights, next_gain, F32 if last else BF16)

    return xn.reshape(batch, seq, d)
```

```python
import functools
import math

import numpy as np
import jax
import jax.numpy as jnp
from jax import lax
from jax.experimental import pallas as pl
from jax.experimental.pallas import tpu as pltpu

F32 = jnp.float32
BF16 = jnp.bfloat16

D_MODEL = 2048
POOL_WIDTH = 1024
POOL_WINDOWS = (2, 4, 8, 16)
POOL_GROUP_DIM = 256
POOL_HALO = 16
DIFF_HEADS = 8
DIFF_HEAD_DIM = 64
DIFF_V_DIM = 128
ATTN_WIDTH = 1024
REL_BUCKETS = 32
REL_MAX_DISTANCE = 128
N_GROUPS = 4
EXPERTS_PER_GROUP = 8
N_EXPERTS = 32
D_EXPERT = 512
EPS = 1e-6
NEG_INF = -1e30
LOG2E = 1.4426950408889634

ROUTER_LANES = 128
TQ = 256
TK = 256
KV_SUPER = 2
HEADS_PER_STEP = 2
TM_EXPERT = 256
SUBLANES = 8
PAD_FILL_ROWS = TM_EXPERT + SUBLANES
ROW_DMA_UNROLL = 8
MIB = 1024 * 1024


def _cparams(sem, vmem_mib):
    return pltpu.CompilerParams(dimension_semantics=sem, vmem_limit_bytes=vmem_mib * MIB)


def _rms(xf, gain):
    ms = jnp.mean(xf * xf, axis=-1, keepdims=True)
    return xf * lax.rsqrt(ms + EPS) * gain


def _norm_kernel(h_ref, g_ref, o_ref):
    o_ref[...] = _rms(h_ref[...], g_ref[...]).astype(o_ref.dtype)


def rms_norm_call(h, gain, out_dtype, tm=512):
    n, d = h.shape
    return pl.pallas_call(
        _norm_kernel,
        grid=(n // tm,),
        in_specs=[pl.BlockSpec((tm, d), lambda m: (m, 0)),
                  pl.BlockSpec((1, d), lambda m: (0, 0))],
        out_specs=pl.BlockSpec((tm, d), lambda m: (m, 0)),
        out_shape=jax.ShapeDtypeStruct((n, d), out_dtype),
        compiler_params=_cparams(("parallel",), 32),
        name="rms_norm",
    )(h, gain.reshape(1, d))


def _proj_kernel(x_ref, w_ref, o_ref, *, first_block_scale):
    acc = jnp.dot(x_ref[...], w_ref[...], preferred_element_type=F32)
    if first_block_scale is not None:
        acc = acc * jnp.where(pl.program_id(0) == 0, first_block_scale, 1.0).astype(F32)
    o_ref[...] = acc.astype(o_ref.dtype)


def proj_call(x, w, col_block0, n_col_blocks, tn, out_dtype, first_block_scale=None, tm=512, name="proj"):
    n, k = x.shape
    return pl.pallas_call(
        functools.partial(_proj_kernel, first_block_scale=first_block_scale),
        grid=(n_col_blocks, n // tm),
        in_specs=[pl.BlockSpec((tm, k), lambda j, m: (m, 0)),
                  pl.BlockSpec((k, tn), lambda j, m: (0, col_block0 + j))],
        out_specs=pl.BlockSpec((tm, tn), lambda j, m: (m, j)),
        out_shape=jax.ShapeDtypeStruct((n, n_col_blocks * tn), out_dtype),
        compiler_params=_cparams(("parallel", "parallel"), 40),
        name=name,
    )(x, w)


def _pool_kernel(cur_ref, prev_ref, mix_ref, scale_ref, o_ref, *, tm, seq):
    m = pl.program_id(0)
    row0 = (m * tm) % seq
    cur = cur_ref[...]
    prev = jnp.where(row0 == 0, 0.0, prev_ref[...])
    pos = row0 + lax.broadcasted_iota(jnp.int32, (tm, 1), 0)
    outs = []
    for g, w in enumerate(POOL_WINDOWS):
        sl = slice(g * POOL_GROUP_DIM, (g + 1) * POOL_GROUP_DIM)
        x = jnp.concatenate([prev[:, sl], cur[:, sl]], axis=0)
        s, d = x, 1
        while d < w:
            s = s[:-d] + s[d:]
            d *= 2
        start = POOL_HALO - w + 1
        wsum = s[start:start + tm]
        count = jnp.minimum(pos + 1, w).astype(F32)
        pooled = wsum / count - cur[:, sl]
        mixed = jnp.dot(pooled.astype(BF16), mix_ref[g], preferred_element_type=F32)
        outs.append(mixed * scale_ref[:, sl])
    o_ref[...] = jnp.concatenate(outs, axis=1).astype(o_ref.dtype)


def pool_call(u, mix_bf16, scale, seq, tm=512):
    n, c = u.shape
    blocks_per_tile = tm // POOL_HALO
    return pl.pallas_call(
        functools.partial(_pool_kernel, tm=tm, seq=seq),
        grid=(n // tm,),
        in_specs=[pl.BlockSpec((tm, c), lambda m: (m, 0)),
                  pl.BlockSpec((POOL_HALO, c), lambda m: (jnp.maximum(m * blocks_per_tile - 1, 0), 0)),
                  pl.BlockSpec(mix_bf16.shape, lambda m: (0, 0, 0)),
                  pl.BlockSpec((1, c), lambda m: (0, 0))],
        out_specs=pl.BlockSpec((tm, c), lambda m: (m, 0)),
        out_shape=jax.ShapeDtypeStruct((n, c), BF16),
        compiler_params=_cparams(("parallel",), 32),
        name="pool_mixer",
    )(u, u, mix_bf16, scale.reshape(1, c))


N_BIAS_TILES = 4


def _bucket_tiles():
    kk = np.arange(TK)[:, None]
    qq = np.arange(TQ)[None, :]
    tiles = []
    for rel in (0, 1, 2, -1):
        n = rel * TK + qq - kk
        max_exact = REL_BUCKETS // 2
        nf = np.maximum(n, 1).astype(np.float64)
        large = max_exact + (np.log(nf / max_exact) / math.log(REL_MAX_DISTANCE / max_exact)
                             * (REL_BUCKETS - max_exact)).astype(np.int64)
        large = np.minimum(large, REL_BUCKETS - 1)
        bucket = np.where(n < max_exact, n, large)
        tiles.append(np.where(n < 0, -1, bucket))
    return np.stack(tiles).astype(np.int32)


def _bias_kernel(table_ref, bucket_ref, o_ref):
    h = pl.program_id(0)
    bucket = bucket_ref[...]
    acc = jnp.full(bucket.shape, NEG_INF, F32)
    for b in range(REL_BUCKETS):
        acc = jnp.where(bucket == b, table_ref[b * DIFF_HEADS + h] * LOG2E, acc)
    o_ref[...] = acc


def bias_tiles_call(rel_table):
    bucket = jnp.asarray(_bucket_tiles())
    return pl.pallas_call(
        _bias_kernel,
        grid=(DIFF_HEADS,),
        in_specs=[pl.BlockSpec(memory_space=pltpu.SMEM),
                  pl.BlockSpec(bucket.shape, lambda h: (0, 0, 0))],
        out_specs=pl.BlockSpec((None,) + bucket.shape, lambda h: (h, 0, 0, 0)),
        out_shape=jax.ShapeDtypeStruct((DIFF_HEADS,) + bucket.shape, F32),
        compiler_params=_cparams(("parallel",), 32),
        name="rel_bias_tiles",
    )(rel_table.reshape(-1), bucket)


def _attn_kernel(q_ref, k_ref, v_ref, bias_ref, lam_ref, gain_ref, o_ref, vt_ref, s_ref, acc_ref,
                 *, lambda_init, n_super):
    qi = pl.program_id(2)
    tks = KV_SUPER * TK
    heads = range(HEADS_PER_STEP)

    @pl.when(qi == 0)
    def _():
        for hh in heads:
            cols = slice(hh * DIFF_V_DIM, (hh + 1) * DIFF_V_DIM)
            for c in range(n_super):
                vt_ref[hh, c] = v_ref[c * tks:(c + 1) * tks, cols].astype(F32).T.astype(BF16)

    qds = []
    for hh in heads:
        q = q_ref[:, hh * DIFF_V_DIM:(hh + 1) * DIFF_V_DIM]
        lane = lax.broadcasted_iota(jnp.int32, q.shape, 1)
        zero = jnp.zeros_like(q)
        qds.append(jnp.concatenate([jnp.where(lane < DIFF_HEAD_DIM, q, zero),
                                    jnp.where(lane >= DIFF_HEAD_DIM, q, zero)], axis=0))

    def scores(t, hh):
        kb = k_ref[pl.ds(pl.multiple_of(t * tks, tks), tks), hh * DIFF_V_DIM:(hh + 1) * DIFF_V_DIM]
        s = lax.dot_general(kb, qds[hh], (((1,), (1,)), ((), ())), preferred_element_type=F32)
        parts = []
        for u in range(KV_SUPER):
            rel = qi - (t * KV_SUPER + u)
            bias = bias_ref[hh, jnp.where(rel < 0, N_BIAS_TILES - 1, jnp.minimum(rel, 2))]
            parts.append(s[u * TK:(u + 1) * TK] + jnp.concatenate([bias, bias], axis=1))
        s = jnp.concatenate(parts, axis=0)
        s_ref[hh] = s
        return jnp.max(s, axis=0, keepdims=True)

    def softmax_step(t, hh, m_prev, m_cur, l):
        alpha = jnp.exp2(m_prev - m_cur)
        p = jnp.exp2(s_ref[hh] - m_cur)
        l_new = l * alpha + jnp.sum(p, axis=0, keepdims=True)
        pv = jnp.dot(vt_ref[hh, t], p.astype(BF16), preferred_element_type=F32)
        acc_ref[hh] = acc_ref[hh] * alpha + pv
        return l_new

    last = qi // KV_SUPER
    acc_ref[...] = jnp.zeros(acc_ref.shape, F32)
    neg = jnp.full((1, 2 * TQ), NEG_INF, F32)
    init = tuple((neg, jnp.maximum(neg, scores(0, hh)), jnp.zeros((1, 2 * TQ), F32)) for hh in heads)

    def body(t, carry):
        out = []
        for hh in heads:
            m_prev, m_cur, l = carry[hh]
            l = softmax_step(t, hh, m_prev, m_cur, l)
            m_next = jnp.maximum(m_cur, scores(t + 1, hh))
            out.append((m_cur, m_next, l))
        return tuple(out)

    carry = lax.fori_loop(0, last, body, init)

    lam_p = lam_ref[...]
    lam = (jnp.exp(jnp.sum(lam_p[0:1] * lam_p[1:2], axis=1, keepdims=True))
           - jnp.exp(jnp.sum(lam_p[2:3] * lam_p[3:4], axis=1, keepdims=True)) + lambda_init)
    for hh in heads:
        m_prev, m_cur, l = carry[hh]
        l = softmax_step(last, hh, m_prev, m_cur, l)
        acc = acc_ref[hh]
        o = acc[:, :TQ] / l[:, :TQ] - lam * (acc[:, TQ:] / l[:, TQ:])
        ms = jnp.mean(o * o, axis=0, keepdims=True)
        y = o * lax.rsqrt(ms + EPS) * gain_ref[...] * (1.0 - lambda_init)
        o_ref[:, hh * DIFF_V_DIM:(hh + 1) * DIFF_V_DIM] = y.T.astype(o_ref.dtype)


def attn_call(qkv, bias_tiles, lam_params, subln_gain, lambda_init, batch, seq):
    tks = KV_SUPER * TK
    n_super = seq // tks
    hps = HEADS_PER_STEP
    width = hps * DIFF_V_DIM
    groups = DIFF_HEADS // hps
    return pl.pallas_call(
        functools.partial(_attn_kernel, lambda_init=lambda_init, n_super=n_super),
        grid=(batch, groups, seq // TQ),
        in_specs=[pl.BlockSpec((None, TQ, width), lambda b, g, i: (b, i, g)),
                  pl.BlockSpec((None, seq, width), lambda b, g, i: (b, 0, groups + g)),
                  pl.BlockSpec((None, seq, width), lambda b, g, i: (b, 0, 2 * groups + g)),
                  pl.BlockSpec((hps, N_BIAS_TILES, TK, TQ), lambda b, g, i: (g, 0, 0, 0)),
                  pl.BlockSpec((4, DIFF_HEAD_DIM), lambda b, g, i: (0, 0)),
                  pl.BlockSpec((DIFF_V_DIM, 1), lambda b, g, i: (0, 0))],
        out_specs=pl.BlockSpec((None, TQ, width), lambda b, g, i: (b, i, g)),
        out_shape=jax.ShapeDtypeStruct((batch, seq, ATTN_WIDTH), BF16),
        scratch_shapes=[pltpu.VMEM((hps, n_super, DIFF_V_DIM, tks), BF16),
                        pltpu.VMEM((hps, tks, 2 * TQ), F32),
                        pltpu.VMEM((hps, DIFF_V_DIM, 2 * TQ), F32)],
        compiler_params=_cparams(("parallel", "parallel", "arbitrary"), 48),
        name="diff_attention",
    )(qkv, qkv, qkv, bias_tiles, lam_params, subln_gain.reshape(DIFF_V_DIM, 1))


def _sigmoid(x):
    return 1.0 / (1.0 + jnp.exp(-x))


def _merge_kernel(xn_ref, mixed_ref, attn_ref, wgp_ref, wga_ref, bgp_ref, bga_ref, wup_ref, wua_ref, z_ref):
    xn = xn_ref[...]
    g_pool = _sigmoid(jnp.dot(xn, wgp_ref[...], preferred_element_type=F32) + bgp_ref[...])
    g_attn = _sigmoid(jnp.dot(xn, wga_ref[...], preferred_element_type=F32) + bga_ref[...])
    y_pool = jnp.dot(mixed_ref[...], wup_ref[...], preferred_element_type=F32)
    y_attn = jnp.dot(attn_ref[...], wua_ref[...], preferred_element_type=F32)
    z_ref[...] = (g_pool * y_pool + g_attn * y_attn).astype(z_ref.dtype)


def merge_call(xn, mixed, attn, w_gate, b_gate, w_up_pool, w_up_attn, tm=512, tn=512):
    n, d = xn.shape
    nb = d // tn
    return pl.pallas_call(
        _merge_kernel,
        grid=(nb, n // tm),
        in_specs=[pl.BlockSpec((tm, d), lambda j, m: (m, 0)),
                  pl.BlockSpec((tm, POOL_WIDTH), lambda j, m: (m, 0)),
                  pl.BlockSpec((tm, ATTN_WIDTH), lambda j, m: (m, 0)),
                  pl.BlockSpec((d, tn), lambda j, m: (0, j)),
                  pl.BlockSpec((d, tn), lambda j, m: (0, nb + j)),
                  pl.BlockSpec((1, tn), lambda j, m: (0, j)),
                  pl.BlockSpec((1, tn), lambda j, m: (0, nb + j)),
                  pl.BlockSpec((POOL_WIDTH, tn), lambda j, m: (0, j)),
                  pl.BlockSpec((ATTN_WIDTH, tn), lambda j, m: (0, j))],
        out_specs=pl.BlockSpec((tm, tn), lambda j, m: (m, j)),
        out_shape=jax.ShapeDtypeStruct((n, d), BF16),
        compiler_params=_cparams(("parallel", "parallel"), 48),
        name="gated_merge",
    )(xn, mixed, attn, w_gate, w_gate, b_gate, b_gate, w_up_pool, w_up_attn)


HIGH_HALF = 0xFFFF0000


def _pack_halves(x):
    c = x.shape[1] // 2
    lo = lax.bitcast_convert_type(x[:, :c].astype(BF16).astype(F32), jnp.uint32)
    hi = lax.bitcast_convert_type(x[:, c:].astype(BF16).astype(F32), jnp.uint32)
    return (lo >> 16) | hi


def _unpack_halves(p):
    lo = lax.bitcast_convert_type(p << 16, F32)
    hi = lax.bitcast_convert_type(p & jnp.uint32(HIGH_HALF), F32)
    return lo, hi


def _outproj_kernel(z_ref, w_ref, h_ref, gain_ref, wr_ref, br_ref, h1_ref, hn_ref, logit_ref):
    h1 = h_ref[...] + jnp.dot(z_ref[...], w_ref[...], preferred_element_type=F32)
    h1_ref[...] = h1
    hn = _rms(h1, gain_ref[...])
    hn_ref[...] = _pack_halves(hn)
    logit_ref[...] = jnp.dot(hn.astype(BF16), wr_ref[...], preferred_element_type=F32) + br_ref[...]


def outproj_call(z, w_out, h, gain, w_router, b_router, tm=256):
    n, d = h.shape
    return pl.pallas_call(
        _outproj_kernel,
        grid=(n // tm,),
        in_specs=[pl.BlockSpec((tm, d), lambda m: (m, 0)),
                  pl.BlockSpec((d, d), lambda m: (0, 0)),
                  pl.BlockSpec((tm, d), lambda m: (m, 0)),
                  pl.BlockSpec((1, d), lambda m: (0, 0)),
                  pl.BlockSpec((d, ROUTER_LANES), lambda m: (0, 0)),
                  pl.BlockSpec((1, ROUTER_LANES), lambda m: (0, 0))],
        out_specs=[pl.BlockSpec((tm, d), lambda m: (m, 0)),
                   pl.BlockSpec((tm, d // 2), lambda m: (m, 0)),
                   pl.BlockSpec((tm, ROUTER_LANES), lambda m: (m, 0))],
        out_shape=[jax.ShapeDtypeStruct((n, d), F32),
                   jax.ShapeDtypeStruct((n, d // 2), jnp.uint32),
                   jax.ShapeDtypeStruct((n, ROUTER_LANES), F32)],
        compiler_params=_cparams(("parallel",), 56),
        name="out_proj_norm_router",
    )(z, w_out, h, gain.reshape(1, d), w_router, b_router)


def _route_kernel(logit_ref, id_ref, w_ref):
    x = logit_ref[...]
    lane = lax.broadcasted_iota(jnp.int32, x.shape, 1).astype(F32)
    big = float(ROUTER_LANES)

    def first_argmax(vals):
        top = jnp.max(vals, axis=1, keepdims=True)
        idx = jnp.min(jnp.where(vals == top, lane, big), axis=1, keepdims=True)
        return top, idx

    gmask = lane < N_GROUPS
    g_top, g_sel = first_argmax(jnp.where(gmask, x, -jnp.inf))
    g_weight = 1.0 / jnp.sum(jnp.where(gmask, jnp.exp(x - g_top), 0.0), axis=1, keepdims=True)
    lo = N_GROUPS + EXPERTS_PER_GROUP * g_sel
    e_vals = jnp.where((lane >= lo) & (lane < lo + EXPERTS_PER_GROUP), x, -jnp.inf)
    v1, i1 = first_argmax(e_vals)
    v2, i2 = first_argmax(jnp.where(lane == i1, -jnp.inf, e_vals))
    t = jnp.exp(v2 - v1)
    w1 = g_weight / (1.0 + t)
    w2 = g_weight * t / (1.0 + t)
    id_ref[...] = jnp.where(lane == 0, i1 - N_GROUPS, jnp.where(lane == 1, i2 - N_GROUPS, 0.0)).astype(jnp.int32)
    w_ref[...] = jnp.where(lane == 0, w1, jnp.where(lane == 1, w2, 0.0))


def route_call(logits, tm=512):
    n = logits.shape[0]
    spec = pl.BlockSpec((tm, ROUTER_LANES), lambda m: (m, 0))
    return pl.pallas_call(
        _route_kernel,
        grid=(n // tm,),
        in_specs=[spec],
        out_specs=[spec, spec],
        out_shape=[jax.ShapeDtypeStruct((n, ROUTER_LANES), jnp.int32),
                   jax.ShapeDtypeStruct((n, ROUTER_LANES), F32)],
        compiler_params=_cparams(("parallel",), 32),
        name="route_top2",
    )(logits)


def _dispatch_kernel(pad_ref, nv_ref, pos_ref, src_ref, dst_ref, zero_buf, sem, *, chunk, n_tiles):
    @pl.when(pl.program_id(0) == 0)
    def _():
        zero_buf[...] = jnp.zeros(zero_buf.shape, zero_buf.dtype)

        def fill(e):
            return pltpu.make_async_copy(
                zero_buf, dst_ref.at[pl.ds(pl.multiple_of(pad_ref[e], SUBLANES), PAD_FILL_ROWS)], sem)

        for e in range(N_EXPERTS):
            fill(e).start()
        for e in range(N_EXPERTS):
            fill(e).wait()

        def fill_tile(i):
            return pltpu.make_async_copy(
                zero_buf.at[pl.ds(0, TM_EXPERT)],
                dst_ref.at[pl.ds(pl.multiple_of(i * TM_EXPERT, TM_EXPERT), TM_EXPERT)], sem)

        def start_tile(i, c):
            fill_tile(i).start()
            return c

        def wait_tile(i, c):
            fill_tile(i).wait()
            return c

        lax.fori_loop(nv_ref[0], n_tiles, start_tile, 0)
        lax.fori_loop(nv_ref[0], n_tiles, wait_tile, 0)

    def start(t, c):
        for j in range(2):
            pltpu.make_async_copy(src_ref.at[pl.ds(t, 1)],
                                  dst_ref.at[pl.ds(pos_ref[0, 0, 2 * t + j], 1)], sem).start()
        return c

    lax.fori_loop(0, chunk, start, 0, unroll=ROW_DMA_UNROLL)
    all_rows = pltpu.make_async_copy(src_ref, dst_ref.at[pl.ds(0, chunk)], sem)
    for j in range(2):
        all_rows.wait()


def dispatch_call(hn, pos, pad_start, n_valid_tiles, n_tiles, chunk=256):
    n, c = hn.shape
    pos3 = pos.reshape(n // chunk, 1, 2 * chunk)
    grid_spec = pltpu.PrefetchScalarGridSpec(
        num_scalar_prefetch=2,
        grid=(n // chunk,),
        in_specs=[pl.BlockSpec((1, 1, 2 * chunk), lambda i, pad, nv: (i, 0, 0), memory_space=pltpu.SMEM),
                  pl.BlockSpec((chunk, c), lambda i, pad, nv: (i, 0))],
        out_specs=pl.BlockSpec(memory_space=pl.ANY),
        scratch_shapes=[pltpu.VMEM((PAD_FILL_ROWS, c), hn.dtype), pltpu.SemaphoreType.DMA(())],
    )
    return pl.pallas_call(
        functools.partial(_dispatch_kernel, chunk=chunk, n_tiles=n_tiles),
        grid_spec=grid_spec,
        out_shape=jax.ShapeDtypeStruct((n_tiles * TM_EXPERT, c), hn.dtype),
        compiler_params=_cparams(("arbitrary",), 32),
        name="dispatch_rows",
    )(pad_start, n_valid_tiles, pos3, hn)


def _expert_kernel(te_ref, nv_ref, x_ref, wg_ref, wu_ref, wd_ref, y_ref, wg_s, wu_s, wd_s):
    i = pl.program_id(0)
    valid = i < nv_ref[0]
    changed = jnp.logical_or(i == 0, te_ref[i] != te_ref[jnp.maximum(i - 1, 0)])

    @pl.when(jnp.logical_and(valid, changed))
    def _():
        wg_s[...] = wg_ref[...].astype(BF16)
        wu_s[...] = wu_ref[...].astype(BF16)
        wd_s[...] = wd_ref[...].astype(BF16)

    @pl.when(valid)
    def _():
        lo, hi = _unpack_halves(x_ref[...])
        x = jnp.concatenate([lo.astype(BF16), hi.astype(BF16)], axis=1)
        a = jnp.dot(x, wg_s[...], preferred_element_type=F32)
        b = jnp.dot(x, wu_s[...], preferred_element_type=F32)
        hmid = (a * _sigmoid(a) * b).astype(BF16)
        y_ref[...] = _pack_halves(jnp.dot(hmid, wd_s[...], preferred_element_type=F32))

    @pl.when(jnp.logical_not(valid))
    def _():
        y_ref[...] = jnp.zeros(y_ref.shape, y_ref.dtype)


def expert_call(xs, tile_expert, n_valid_tiles, w_gate, w_up, w_down, layer):
    p, c = xs.shape
    d = 2 * c
    tm = TM_EXPERT
    f = w_gate.shape[-1]
    grid_spec = pltpu.PrefetchScalarGridSpec(
        num_scalar_prefetch=2,
        grid=(p // tm,),
        in_specs=[pl.BlockSpec((tm, c), lambda i, te, nv: (jnp.minimum(i, nv[0] - 1), 0)),
                  pl.BlockSpec((None, None, d, f), lambda i, te, nv: (layer, te[i], 0, 0)),
                  pl.BlockSpec((None, None, d, f), lambda i, te, nv: (layer, te[i], 0, 0)),
                  pl.BlockSpec((None, None, f, d), lambda i, te, nv: (layer, te[i], 0, 0))],
        out_specs=pl.BlockSpec((tm, c), lambda i, te, nv: (i, 0)),
        scratch_shapes=[pltpu.VMEM((d, f), BF16), pltpu.VMEM((d, f), BF16), pltpu.VMEM((f, d), BF16)],
    )
    return pl.pallas_call(
        _expert_kernel,
        grid_spec=grid_spec,
        out_shape=jax.ShapeDtypeStruct((p, c), jnp.uint32),
        compiler_params=_cparams(("arbitrary",), 58),
        name="expert_swiglu",
    )(tile_expert, n_valid_tiles, xs, w_gate, w_up, w_down)


def _combine_kernel(pos_ref, ys_ref, h_ref, w_ref, gain_ref, h2_ref, xn_ref, buf, sem, *, chunk):
    def start(t, c):
        for j in range(2):
            pltpu.make_async_copy(ys_ref.at[pl.ds(pos_ref[0, 0, 2 * t + j], 1)],
                                  buf.at[j, pl.ds(t, 1)], sem).start()
        return c

    lax.fori_loop(0, chunk, start, 0, unroll=ROW_DMA_UNROLL)
    for j in range(2):
        pltpu.make_async_copy(ys_ref.at[pl.ds(0, chunk)], buf.at[j], sem).wait()
    w = w_ref[...]
    lo0, hi0 = _unpack_halves(buf[0])
    lo1, hi1 = _unpack_halves(buf[1])
    y = jnp.concatenate([w[:, 0:1] * lo0 + w[:, 1:2] * lo1, w[:, 0:1] * hi0 + w[:, 1:2] * hi1], axis=1)
    h2 = h_ref[...] + y
    h2_ref[...] = h2
    xn_ref[...] = _rms(h2, gain_ref[...]).astype(xn_ref.dtype)


def combine_call(ys, pos, h1, weights, next_gain, xn_dtype, chunk=128):
    n, d = h1.shape
    pos3 = pos.reshape(n // chunk, 1, 2 * chunk)
    return pl.pallas_call(
        functools.partial(_combine_kernel, chunk=chunk),
        grid=(n // chunk,),
        in_specs=[pl.BlockSpec((1, 1, 2 * chunk), lambda c: (c, 0, 0), memory_space=pltpu.SMEM),
                  pl.BlockSpec(memory_space=pl.ANY),
                  pl.BlockSpec((chunk, d), lambda c: (c, 0)),
                  pl.BlockSpec((chunk, ROUTER_LANES), lambda c: (c, 0)),
                  pl.BlockSpec((1, d), lambda c: (0, 0))],
        out_specs=[pl.BlockSpec((chunk, d), lambda c: (c, 0)),
                   pl.BlockSpec((chunk, d), lambda c: (c, 0))],
        out_shape=[jax.ShapeDtypeStruct((n, d), F32),
                   jax.ShapeDtypeStruct((n, d), xn_dtype)],
        scratch_shapes=[pltpu.VMEM((2, chunk, d // 2), jnp.uint32), pltpu.SemaphoreType.DMA(())],
        compiler_params=_cparams(("arbitrary",), 32),
        name="combine_rows",
    )(pos3, ys, h1, weights, next_gain.reshape(1, d))


def _dispatch_plan(expert_ids, n_tiles):
    tm = TM_EXPERT
    onehot = (expert_ids[:, :, None] == jnp.arange(N_EXPERTS, dtype=jnp.int32)).astype(jnp.int32)
    per_token = onehot.sum(axis=1)
    before = jnp.cumsum(per_token, axis=0) - per_token
    rank = jnp.take_along_axis(before, expert_ids, axis=1)
    counts = per_token.sum(axis=0)
    padded = ((counts + tm - 1) // tm) * tm
    ends = jnp.cumsum(padded)
    pos = (ends - padded)[expert_ids] + rank
    n_valid = (ends[-1] // tm).astype(jnp.int32)
    tile_start = jnp.arange(n_tiles, dtype=jnp.int32) * tm
    tile_expert = jnp.searchsorted(ends, tile_start, side="right").astype(jnp.int32)
    last_expert = jnp.searchsorted(ends, ends[-1] - 1, side="right").astype(jnp.int32)
    tile_expert = jnp.where(tile_start < ends[-1], tile_expert, last_expert)
    pad_start = ((ends - padded + counts) // SUBLANES) * SUBLANES
    return pos.astype(jnp.int32), tile_expert, n_valid.reshape(1), pad_start.astype(jnp.int32)


def kernel(x, rel_bias_table, norm_mix_gain, w_in, w_merge_gate, b_merge_gate, pool_mix, pool_scale, w_up_pool, lambda_q1, lambda_k1, lambda_q2, lambda_k2, subln_gain, w_up_attn, w_out, norm_ffn_gain, w_router_group, b_router_group, w_router_expert, b_router_expert, w_expert_gate, w_expert_up, w_expert_down, final_norm_gain):
    batch, seq, d = x.shape
    depth = w_in.shape[0]
    n = batch * seq
    n_tiles = (2 * n + N_EXPERTS * (TM_EXPERT - 1)) // TM_EXPERT + 2

    bias_tiles = bias_tiles_call(rel_bias_table)
    h = x.reshape(n, d)
    xn = rms_norm_call(h, norm_mix_gain[0], BF16)
    q_scale = LOG2E * DIFF_HEAD_DIM ** -0.5

    for l in range(depth):
        w_in_l = w_in[l].astype(BF16)
        u = proj_call(xn, w_in_l, 0, POOL_WIDTH // 512, 512, F32, name="in_proj_pool")
        qkv = proj_call(xn, w_in_l, 1, 3, ATTN_WIDTH, BF16, first_block_scale=q_scale, name="in_proj_qkv")
        mixed = pool_call(u, pool_mix[l].astype(BF16), pool_scale[l], seq)
        lambda_init = 0.8 - 0.6 * math.exp(-0.3 * l)
        lam_params = jnp.stack([lambda_q1[l], lambda_k1[l], lambda_q2[l], lambda_k2[l]])
        attn = attn_call(qkv.reshape(batch, seq, 3 * ATTN_WIDTH), bias_tiles, lam_params, subln_gain[l],
                         lambda_init, batch, seq).reshape(n, ATTN_WIDTH)
        z = merge_call(xn, mixed, attn, w_merge_gate[l].astype(BF16), b_merge_gate[l].reshape(1, -1),
                       w_up_pool[l].astype(BF16), w_up_attn[l].astype(BF16))

        w_router = jnp.concatenate(
            [w_router_group[l], jnp.transpose(w_router_expert[l], (1, 0, 2)).reshape(d, N_EXPERTS),
             jnp.zeros((d, ROUTER_LANES - N_GROUPS - N_EXPERTS), F32)], axis=1).astype(BF16)
        b_router = jnp.concatenate(
            [b_router_group[l], b_router_expert[l].reshape(-1),
             jnp.zeros((ROUTER_LANES - N_GROUPS - N_EXPERTS,), F32)]).reshape(1, ROUTER_LANES)
        h1, hn, logits = outproj_call(z, w_out[l].astype(BF16), h, norm_ffn_gain[l], w_router, b_router)

        ids, weights = route_call(logits)
        pos, tile_expert, n_valid, pad_start = _dispatch_plan(ids[:, :2], n_tiles)
        xs = dispatch_call(hn, pos, pad_start, n_valid, n_tiles)
        ys = expert_call(xs, tile_expert, n_valid, w_expert_gate, w_expert_up, w_expert_down, l)
        last = l == depth - 1
        next_gain = final_norm_gain if last else norm_mix_gain[l + 1]
        h, xn = combine_call(ys, pos, h1, weights, next_gain, F32 if last else BF16)

    return xn.reshape(batch, seq, d)
```

```python
import functools
import math

import numpy as np
import jax
import jax.numpy as jnp
from jax import lax
from jax.experimental import pallas as pl
from jax.experimental.pallas import tpu as pltpu

F32 = jnp.float32
BF16 = jnp.bfloat16

D_MODEL = 2048
POOL_WIDTH = 1024
POOL_WINDOWS = (2, 4, 8, 16)
POOL_GROUP_DIM = 256
POOL_HALO = 16
DIFF_HEADS = 8
DIFF_HEAD_DIM = 64
DIFF_V_DIM = 128
ATTN_WIDTH = 1024
REL_BUCKETS = 32
REL_MAX_DISTANCE = 128
N_GROUPS = 4
EXPERTS_PER_GROUP = 8
N_EXPERTS = 32
D_EXPERT = 512
EPS = 1e-6
NEG_INF = -1e30
LOG2E = 1.4426950408889634

ROUTER_LANES = 128
TQ = 256
TK = 256
KV_SUPER = 2
HEADS_PER_STEP = 2
TM_EXPERT = 256
SUBLANES = 8
PAD_FILL_ROWS = TM_EXPERT + SUBLANES
ROW_DMA_UNROLL = 8
MIB = 1024 * 1024


def _cparams(sem, vmem_mib):
    return pltpu.CompilerParams(dimension_semantics=sem, vmem_limit_bytes=vmem_mib * MIB)


def _rms(xf, gain):
    ms = jnp.mean(xf * xf, axis=-1, keepdims=True)
    return xf * lax.rsqrt(ms + EPS) * gain


def _norm_kernel(h_ref, g_ref, o_ref):
    o_ref[...] = _rms(h_ref[...], g_ref[...]).astype(o_ref.dtype)


def rms_norm_call(h, gain, out_dtype, tm=512):
    n, d = h.shape
    return pl.pallas_call(
        _norm_kernel,
        grid=(n // tm,),
        in_specs=[pl.BlockSpec((tm, d), lambda m: (m, 0)),
                  pl.BlockSpec((1, d), lambda m: (0, 0))],
        out_specs=pl.BlockSpec((tm, d), lambda m: (m, 0)),
        out_shape=jax.ShapeDtypeStruct((n, d), out_dtype),
        compiler_params=_cparams(("parallel",), 32),
        name="rms_norm",
    )(h, gain.reshape(1, d))


def _proj_kernel(x_ref, w_ref, o_ref, *, first_block_scale):
    acc = jnp.dot(x_ref[...], w_ref[...], preferred_element_type=F32)
    if first_block_scale is not None:
        acc = acc * jnp.where(pl.program_id(0) == 0, first_block_scale, 1.0).astype(F32)
    o_ref[...] = acc.astype(o_ref.dtype)


def proj_call(x, w, col_block0, n_col_blocks, tn, out_dtype, first_block_scale=None, tm=512, name="proj"):
    n, k = x.shape
    return pl.pallas_call(
        functools.partial(_proj_kernel, first_block_scale=first_block_scale),
        grid=(n_col_blocks, n // tm),
        in_specs=[pl.BlockSpec((tm, k), lambda j, m: (m, 0)),
                  pl.BlockSpec((k, tn), lambda j, m: (0, col_block0 + j))],
        out_specs=pl.BlockSpec((tm, tn), lambda j, m: (m, j)),
        out_shape=jax.ShapeDtypeStruct((n, n_col_blocks * tn), out_dtype),
        compiler_params=_cparams(("parallel", "parallel"), 40),
        name=name,
    )(x, w)


def _pool_kernel(cur_ref, prev_ref, mix_ref, scale_ref, o_ref, *, tm, seq):
    m = pl.program_id(0)
    row0 = (m * tm) % seq
    cur = cur_ref[...]
    prev = jnp.where(row0 == 0, 0.0, prev_ref[...])
    pos = row0 + lax.broadcasted_iota(jnp.int32, (tm, 1), 0)
    outs = []
    for g, w in enumerate(POOL_WINDOWS):
        sl = slice(g * POOL_GROUP_DIM, (g + 1) * POOL_GROUP_DIM)
        x = jnp.concatenate([prev[:, sl], cur[:, sl]], axis=0)
        s, d = x, 1
        while d < w:
            s = s[:-d] + s[d:]
            d *= 2
        start = POOL_HALO - w + 1
        wsum = s[start:start + tm]
        count = jnp.minimum(pos + 1, w).astype(F32)
        pooled = wsum / count - cur[:, sl]
        mixed = jnp.dot(pooled.astype(BF16), mix_ref[g], preferred_element_type=F32)
        outs.append(mixed * scale_ref[:, sl])
    o_ref[...] = jnp.concatenate(outs, axis=1).astype(o_ref.dtype)


def pool_call(u, mix_bf16, scale, seq, tm=512):
    n, c = u.shape
    blocks_per_tile = tm // POOL_HALO
    return pl.pallas_call(
        functools.partial(_pool_kernel, tm=tm, seq=seq),
        grid=(n // tm,),
        in_specs=[pl.BlockSpec((tm, c), lambda m: (m, 0)),
                  pl.BlockSpec((POOL_HALO, c), lambda m: (jnp.maximum(m * blocks_per_tile - 1, 0), 0)),
                  pl.BlockSpec(mix_bf16.shape, lambda m: (0, 0, 0)),
                  pl.BlockSpec((1, c), lambda m: (0, 0))],
        out_specs=pl.BlockSpec((tm, c), lambda m: (m, 0)),
        out_shape=jax.ShapeDtypeStruct((n, c), BF16),
        compiler_params=_cparams(("parallel",), 32),
        name="pool_mixer",
    )(u, u, mix_bf16, scale.reshape(1, c))


N_BIAS_TILES = 4


def _bucket_tiles():
    kk = np.arange(TK)[:, None]
    qq = np.arange(TQ)[None, :]
    tiles = []
    for rel in (0, 1, 2, -1):
        n = rel * TK + qq - kk
        max_exact = REL_BUCKETS // 2
        nf = np.maximum(n, 1).astype(np.float64)
        large = max_exact + (np.log(nf / max_exact) / math.log(REL_MAX_DISTANCE / max_exact)
                             * (REL_BUCKETS - max_exact)).astype(np.int64)
        large = np.minimum(large, REL_BUCKETS - 1)
        bucket = np.where(n < max_exact, n, large)
        tiles.append(np.where(n < 0, -1, bucket))
    return np.stack(tiles).astype(np.int32)


def _bias_kernel(table_ref, bucket_ref, o_ref):
    h = pl.program_id(0)
    bucket = bucket_ref[...]
    acc = jnp.full(bucket.shape, NEG_INF, F32)
    for b in range(REL_BUCKETS):
        acc = jnp.where(bucket == b, table_ref[b * DIFF_HEADS + h] * LOG2E, acc)
    o_ref[...] = acc


def bias_tiles_call(rel_table):
    bucket = jnp.asarray(_bucket_tiles())
    return pl.pallas_call(
        _bias_kernel,
        grid=(DIFF_HEADS,),
        in_specs=[pl.BlockSpec(memory_space=pltpu.SMEM),
                  pl.BlockSpec(bucket.shape, lambda h: (0, 0, 0))],
        out_specs=pl.BlockSpec((None,) + bucket.shape, lambda h: (h, 0, 0, 0)),
        out_shape=jax.ShapeDtypeStruct((DIFF_HEADS,) + bucket.shape, F32),
        compiler_params=_cparams(("parallel",), 32),
        name="rel_bias_tiles",
    )(rel_table.reshape(-1), bucket)


def _attn_kernel(q_ref, k_ref, v_ref, bias_ref, lam_ref, gain_ref, o_ref, vt_ref, s_ref, acc_ref,
                 *, lambda_init, n_super):
    qi = pl.program_id(2)
    tks = KV_SUPER * TK
    heads = range(HEADS_PER_STEP)

    @pl.when(qi == 0)
    def _():
        for hh in heads:
            cols = slice(hh * DIFF_V_DIM, (hh + 1) * DIFF_V_DIM)
            for c in range(n_super):
                vt_ref[hh, c] = v_ref[c * tks:(c + 1) * tks, cols].astype(F32).T.astype(BF16)

    qds = []
    for hh in heads:
        q = q_ref[:, hh * DIFF_V_DIM:(hh + 1) * DIFF_V_DIM]
        lane = lax.broadcasted_iota(jnp.int32, q.shape, 1)
        zero = jnp.zeros_like(q)
        qds.append(jnp.concatenate([jnp.where(lane < DIFF_HEAD_DIM, q, zero),
                                    jnp.where(lane >= DIFF_HEAD_DIM, q, zero)], axis=0))

    def scores(t, hh):
        kb = k_ref[pl.ds(pl.multiple_of(t * tks, tks), tks), hh * DIFF_V_DIM:(hh + 1) * DIFF_V_DIM]
        s = lax.dot_general(kb, qds[hh], (((1,), (1,)), ((), ())), preferred_element_type=F32)
        parts = []
        for u in range(KV_SUPER):
            rel = qi - (t * KV_SUPER + u)
            bias = bias_ref[hh, jnp.where(rel < 0, N_BIAS_TILES - 1, jnp.minimum(rel, 2))]
            parts.append(s[u * TK:(u + 1) * TK] + jnp.concatenate([bias, bias], axis=1))
        s = jnp.concatenate(parts, axis=0)
        s_ref[hh] = s
        return jnp.max(s, axis=0, keepdims=True)

    def softmax_step(t, hh, m_prev, m_cur, l):
        alpha = jnp.exp2(m_prev - m_cur)
        p = jnp.exp2(s_ref[hh] - m_cur)
        l_new = l * alpha + jnp.sum(p, axis=0, keepdims=True)
        pv = jnp.dot(vt_ref[hh, t], p.astype(BF16), preferred_element_type=F32)
        acc_ref[hh] = acc_ref[hh] * alpha + pv
        return l_new

    last = qi // KV_SUPER
    acc_ref[...] = jnp.zeros(acc_ref.shape, F32)
    neg = jnp.full((1, 2 * TQ), NEG_INF, F32)
    init = tuple((neg, jnp.maximum(neg, scores(0, hh)), jnp.zeros((1, 2 * TQ), F32)) for hh in heads)

    def body(t, carry):
        out = []
        for hh in heads:
            m_prev, m_cur, l = carry[hh]
            l = softmax_step(t, hh, m_prev, m_cur, l)
            m_next = jnp.maximum(m_cur, scores(t + 1, hh))
            out.append((m_cur, m_next, l))
        return tuple(out)

    carry = lax.fori_loop(0, last, body, init)

    lam_p = lam_ref[...]
    lam = (jnp.exp(jnp.sum(lam_p[0:1] * lam_p[1:2], axis=1, keepdims=True))
           - jnp.exp(jnp.sum(lam_p[2:3] * lam_p[3:4], axis=1, keepdims=True)) + lambda_init)
    for hh in heads:
        m_prev, m_cur, l = carry[hh]
        l = softmax_step(last, hh, m_prev, m_cur, l)
        acc = acc_ref[hh]
        o = acc[:, :TQ] / l[:, :TQ] - lam * (acc[:, TQ:] / l[:, TQ:])
        ms = jnp.mean(o * o, axis=0, keepdims=True)
        y = o * lax.rsqrt(ms + EPS) * gain_ref[...] * (1.0 - lambda_init)
        o_ref[:, hh * DIFF_V_DIM:(hh + 1) * DIFF_V_DIM] = y.T.astype(o_ref.dtype)


def attn_call(qkv, bias_tiles, lam_params, subln_gain, lambda_init, batch, seq):
    tks = KV_SUPER * TK
    n_super = seq // tks
    hps = HEADS_PER_STEP
    width = hps * DIFF_V_DIM
    groups = DIFF_HEADS // hps
    return pl.pallas_call(
        functools.partial(_attn_kernel, lambda_init=lambda_init, n_super=n_super),
        grid=(batch, groups, seq // TQ),
        in_specs=[pl.BlockSpec((None, TQ, width), lambda b, g, i: (b, i, g)),
                  pl.BlockSpec((None, seq, width), lambda b, g, i: (b, 0, groups + g)),
                  pl.BlockSpec((None, seq, width), lambda b, g, i: (b, 0, 2 * groups + g)),
                  pl.BlockSpec((hps, N_BIAS_TILES, TK, TQ), lambda b, g, i: (g, 0, 0, 0)),
                  pl.BlockSpec((4, DIFF_HEAD_DIM), lambda b, g, i: (0, 0)),
                  pl.BlockSpec((DIFF_V_DIM, 1), lambda b, g, i: (0, 0))],
        out_specs=pl.BlockSpec((None, TQ, width), lambda b, g, i: (b, i, g)),
        out_shape=jax.ShapeDtypeStruct((batch, seq, ATTN_WIDTH), BF16),
        scratch_shapes=[pltpu.VMEM((hps, n_super, DIFF_V_DIM, tks), BF16),
                        pltpu.VMEM((hps, tks, 2 * TQ), F32),
                        pltpu.VMEM((hps, DIFF_V_DIM, 2 * TQ), F32)],
        compiler_params=_cparams(("parallel", "parallel", "arbitrary"), 48),
        name="diff_attention",
    )(qkv, qkv, qkv, bias_tiles, lam_params, subln_gain.reshape(DIFF_V_DIM, 1))


def _sigmoid(x):
    return 1.0 / (1.0 + jnp.exp(-x))


def _merge_kernel(xn_ref, mixed_ref, attn_ref, wgp_ref, wga_ref, bgp_ref, bga_ref, wup_ref, wua_ref, z_ref):
    xn = xn_ref[...]
    g_pool = _sigmoid(jnp.dot(xn, wgp_ref[...], preferred_element_type=F32) + bgp_ref[...])
    g_attn = _sigmoid(jnp.dot(xn, wga_ref[...], preferred_element_type=F32) + bga_ref[...])
    y_pool = jnp.dot(mixed_ref[...], wup_ref[...], preferred_element_type=F32)
    y_attn = jnp.dot(attn_ref[...], wua_ref[...], preferred_element_type=F32)
    z_ref[...] = (g_pool * y_pool + g_attn * y_attn).astype(z_ref.dtype)


def merge_call(xn, mixed, attn, w_gate, b_gate, w_up_pool, w_up_attn, tm=512, tn=512):
    n, d = xn.shape
    nb = d // tn
    return pl.pallas_call(
        _merge_kernel,
        grid=(nb, n // tm),
        in_specs=[pl.BlockSpec((tm, d), lambda j, m: (m, 0)),
                  pl.BlockSpec((tm, POOL_WIDTH), lambda j, m: (m, 0)),
                  pl.BlockSpec((tm, ATTN_WIDTH), lambda j, m: (m, 0)),
                  pl.BlockSpec((d, tn), lambda j, m: (0, j)),
                  pl.BlockSpec((d, tn), lambda j, m: (0, nb + j)),
                  pl.BlockSpec((1, tn), lambda j, m: (0, j)),
                  pl.BlockSpec((1, tn), lambda j, m: (0, nb + j)),
                  pl.BlockSpec((POOL_WIDTH, tn), lambda j, m: (0, j)),
                  pl.BlockSpec((ATTN_WIDTH, tn), lambda j, m: (0, j))],
        out_specs=pl.BlockSpec((tm, tn), lambda j, m: (m, j)),
        out_shape=jax.ShapeDtypeStruct((n, d), BF16),
        compiler_params=_cparams(("parallel", "parallel"), 48),
        name="gated_merge",
    )(xn, mixed, attn, w_gate, w_gate, b_gate, b_gate, w_up_pool, w_up_attn)


HIGH_HALF = 0xFFFF0000


def _pack_halves(x):
    c = x.shape[1] // 2
    lo = lax.bitcast_convert_type(x[:, :c].astype(BF16).astype(F32), jnp.uint32)
    hi = lax.bitcast_convert_type(x[:, c:].astype(BF16).astype(F32), jnp.uint32)
    return (lo >> 16) | hi


def _unpack_halves(p):
    lo = lax.bitcast_convert_type(p << 16, F32)
    hi = lax.bitcast_convert_type(p & jnp.uint32(HIGH_HALF), F32)
    return lo, hi


def _outproj_kernel(z_ref, w_ref, h_ref, gain_ref, wr_ref, br_ref, h1_ref, hn_ref, logit_ref):
    h1 = h_ref[...] + jnp.dot(z_ref[...], w_ref[...], preferred_element_type=F32)
    h1_ref[...] = h1
    hn = _rms(h1, gain_ref[...])
    hn_ref[...] = _pack_halves(hn)
    logit_ref[...] = jnp.dot(hn.astype(BF16), wr_ref[...], preferred_element_type=F32) + br_ref[...]


def outproj_call(z, w_out, h, gain, w_router, b_router, tm=256):
    n, d = h.shape
    return pl.pallas_call(
        _outproj_kernel,
        grid=(n // tm,),
        in_specs=[pl.BlockSpec((tm, d), lambda m: (m, 0)),
                  pl.BlockSpec((d, d), lambda m: (0, 0)),
                  pl.BlockSpec((tm, d), lambda m: (m, 0)),
                  pl.BlockSpec((1, d), lambda m: (0, 0)),
                  pl.BlockSpec((d, ROUTER_LANES), lambda m: (0, 0)),
                  pl.BlockSpec((1, ROUTER_LANES), lambda m: (0, 0))],
        out_specs=[pl.BlockSpec((tm, d), lambda m: (m, 0)),
                   pl.BlockSpec((tm, d // 2), lambda m: (m, 0)),
                   pl.BlockSpec((tm, ROUTER_LANES), lambda m: (m, 0))],
        out_shape=[jax.ShapeDtypeStruct((n, d), F32),
                   jax.ShapeDtypeStruct((n, d // 2), jnp.uint32),
                   jax.ShapeDtypeStruct((n, ROUTER_LANES), F32)],
        compiler_params=_cparams(("parallel",), 56),
        name="out_proj_norm_router",
    )(z, w_out, h, gain.reshape(1, d), w_router, b_router)


def _route_kernel(logit_ref, id_ref, w_ref):
    x = logit_ref[...]
    lane = lax.broadcasted_iota(jnp.int32, x.shape, 1).astype(F32)
    big = float(ROUTER_LANES)

    def first_argmax(vals):
        top = jnp.max(vals, axis=1, keepdims=True)
        idx = jnp.min(jnp.where(vals == top, lane, big), axis=1, keepdims=True)
        return top, idx

    gmask = lane < N_GROUPS
    g_top, g_sel = first_argmax(jnp.where(gmask, x, -jnp.inf))
    g_weight = 1.0 / jnp.sum(jnp.where(gmask, jnp.exp(x - g_top), 0.0), axis=1, keepdims=True)
    lo = N_GROUPS + EXPERTS_PER_GROUP * g_sel
    e_vals = jnp.where((lane >= lo) & (lane < lo + EXPERTS_PER_GROUP), x, -jnp.inf)
    v1, i1 = first_argmax(e_vals)
    v2, i2 = first_argmax(jnp.where(lane == i1, -jnp.inf, e_vals))
    t = jnp.exp(v2 - v1)
    w1 = g_weight / (1.0 + t)
    w2 = g_weight * t / (1.0 + t)
    id_ref[...] = jnp.where(lane == 0, i1 - N_GROUPS, jnp.where(lane == 1, i2 - N_GROUPS, 0.0)).astype(jnp.int32)
    w_ref[...] = jnp.where(lane == 0, w1, jnp.where(lane == 1, w2, 0.0))


def route_call(logits, tm=512):
    n = logits.shape[0]
    spec = pl.BlockSpec((tm, ROUTER_LANES), lambda m: (m, 0))
    return pl.pallas_call(
        _route_kernel,
        grid=(n // tm,),
        in_specs=[spec],
        out_specs=[spec, spec],
        out_shape=[jax.ShapeDtypeStruct((n, ROUTER_LANES), jnp.int32),
                   jax.ShapeDtypeStruct((n, ROUTER_LANES), F32)],
        compiler_params=_cparams(("parallel",), 32),
        name="route_top2",
    )(logits)


def _dispatch_kernel(pad_ref, nv_ref, pos_ref, src_ref, dst_ref, zero_buf, sem, *, chunk, n_tiles):
    @pl.when(pl.program_id(0) == 0)
    def _():
        zero_buf[...] = jnp.zeros(zero_buf.shape, zero_buf.dtype)

        def fill(e):
            return pltpu.make_async_copy(
                zero_buf, dst_ref.at[pl.ds(pl.multiple_of(pad_ref[e], SUBLANES), PAD_FILL_ROWS)], sem)

        for e in range(N_EXPERTS):
            fill(e).start()
        for e in range(N_EXPERTS):
            fill(e).wait()

        def fill_tile(i):
            return pltpu.make_async_copy(
                zero_buf.at[pl.ds(0, TM_EXPERT)],
                dst_ref.at[pl.ds(pl.multiple_of(i * TM_EXPERT, TM_EXPERT), TM_EXPERT)], sem)

        def start_tile(i, c):
            fill_tile(i).start()
            return c

        def wait_tile(i, c):
            fill_tile(i).wait()
            return c

        lax.fori_loop(nv_ref[0], n_tiles, start_tile, 0)
        lax.fori_loop(nv_ref[0], n_tiles, wait_tile, 0)

    def start(t, c):
        for j in range(2):
            pltpu.make_async_copy(src_ref.at[pl.ds(t, 1)],
                                  dst_ref.at[pl.ds(pos_ref[0, 0, 2 * t + j], 1)], sem).start(priority=j)
        return c

    lax.fori_loop(0, chunk, start, 0, unroll=ROW_DMA_UNROLL)
    all_rows = pltpu.make_async_copy(src_ref, dst_ref.at[pl.ds(0, chunk)], sem)
    for j in range(2):
        all_rows.wait()


def dispatch_call(hn, pos, pad_start, n_valid_tiles, n_tiles, chunk=256):
    n, c = hn.shape
    pos3 = pos.reshape(n // chunk, 1, 2 * chunk)
    grid_spec = pltpu.PrefetchScalarGridSpec(
        num_scalar_prefetch=2,
        grid=(n // chunk,),
        in_specs=[pl.BlockSpec((1, 1, 2 * chunk), lambda i, pad, nv: (i, 0, 0), memory_space=pltpu.SMEM),
                  pl.BlockSpec((chunk, c), lambda i, pad, nv: (i, 0))],
        out_specs=pl.BlockSpec(memory_space=pl.ANY),
        scratch_shapes=[pltpu.VMEM((PAD_FILL_ROWS, c), hn.dtype), pltpu.SemaphoreType.DMA(())],
    )
    return pl.pallas_call(
        functools.partial(_dispatch_kernel, chunk=chunk, n_tiles=n_tiles),
        grid_spec=grid_spec,
        out_shape=jax.ShapeDtypeStruct((n_tiles * TM_EXPERT, c), hn.dtype),
        compiler_params=_cparams(("arbitrary",), 32),
        name="dispatch_rows",
    )(pad_start, n_valid_tiles, pos3, hn)


def _expert_kernel(te_ref, nv_ref, nx_ref, x_ref, wg_hbm, wu_hbm, wd_hbm, y_ref,
                   wg_f, wu_f, wd_f, wg_s, wu_s, wd_s, seg_ref, sems, *, layer):
    i = pl.program_id(0)
    valid = i < nv_ref[0]
    expert = te_ref[i]
    changed = jnp.logical_or(i == 0, expert != te_ref[jnp.maximum(i - 1, 0)])

    def fetch(e, slot):
        return [pltpu.make_async_copy(hbm.at[layer, e], buf.at[slot], sems.at[slot])
                for hbm, buf in ((wg_hbm, wg_f), (wu_hbm, wu_f), (wd_hbm, wd_f))]

    @pl.when(i == 0)
    def _():
        seg_ref[0] = 0
        for cp in fetch(expert, 0):
            cp.start()

    @pl.when(jnp.logical_and(valid, changed))
    def _():
        slot = seg_ref[0] % 2
        for cp in fetch(expert, slot):
            cp.wait()
        nxt = nx_ref[i]

        @pl.when(nxt >= 0)
        def _():
            for cp in fetch(nxt, 1 - slot):
                cp.start()

        wg_s[...] = wg_f[slot].astype(BF16)
        wu_s[...] = wu_f[slot].astype(BF16)
        wd_s[...] = wd_f[slot].astype(BF16)
        seg_ref[0] = seg_ref[0] + 1

    @pl.when(valid)
    def _():
        lo, hi = _unpack_halves(x_ref[...])
        x = jnp.concatenate([lo.astype(BF16), hi.astype(BF16)], axis=1)
        a = jnp.dot(x, wg_s[...], preferred_element_type=F32)
        b = jnp.dot(x, wu_s[...], preferred_element_type=F32)
        hmid = (a * _sigmoid(a) * b).astype(BF16)
        y_ref[...] = _pack_halves(jnp.dot(hmid, wd_s[...], preferred_element_type=F32))

    @pl.when(jnp.logical_not(valid))
    def _():
        y_ref[...] = jnp.zeros(y_ref.shape, y_ref.dtype)


def expert_call(xs, tile_expert, n_valid_tiles, next_expert, w_gate, w_up, w_down, layer):
    p, c = xs.shape
    d = 2 * c
    tm = TM_EXPERT
    f = w_gate.shape[-1]
    grid_spec = pltpu.PrefetchScalarGridSpec(
        num_scalar_prefetch=3,
        grid=(p // tm,),
        in_specs=[pl.BlockSpec((tm, c), lambda i, te, nv, nx: (jnp.minimum(i, nv[0] - 1), 0)),
                  pl.BlockSpec(memory_space=pl.ANY),
                  pl.BlockSpec(memory_space=pl.ANY),
                  pl.BlockSpec(memory_space=pl.ANY)],
        out_specs=pl.BlockSpec((tm, c), lambda i, te, nv, nx: (i, 0)),
        scratch_shapes=[pltpu.VMEM((2, d, f), F32), pltpu.VMEM((2, d, f), F32), pltpu.VMEM((2, f, d), F32),
                        pltpu.VMEM((d, f), BF16), pltpu.VMEM((d, f), BF16), pltpu.VMEM((f, d), BF16),
                        pltpu.SMEM((1,), jnp.int32), pltpu.SemaphoreType.DMA((2,))],
    )
    return pl.pallas_call(
        functools.partial(_expert_kernel, layer=layer),
        grid_spec=grid_spec,
        out_shape=jax.ShapeDtypeStruct((p, c), jnp.uint32),
        compiler_params=_cparams(("arbitrary",), 58),
        name="expert_swiglu",
    )(tile_expert, n_valid_tiles, next_expert, xs, w_gate, w_up, w_down)


def _combine_kernel(pos_ref, ys_ref, h_ref, w_ref, gain_ref, h2_ref, xn_ref, buf, sem, *, chunk):
    def start(t, c):
        for j in range(2):
            pltpu.make_async_copy(ys_ref.at[pl.ds(pos_ref[0, 0, 2 * t + j], 1)],
                                  buf.at[j, pl.ds(t, 1)], sem).start(priority=j)
        return c

    lax.fori_loop(0, chunk, start, 0, unroll=ROW_DMA_UNROLL)
    for j in range(2):
        pltpu.make_async_copy(ys_ref.at[pl.ds(0, chunk)], buf.at[j], sem).wait()
    w = w_ref[...]
    lo0, hi0 = _unpack_halves(buf[0])
    lo1, hi1 = _unpack_halves(buf[1])
    y = jnp.concatenate([w[:, 0:1] * lo0 + w[:, 1:2] * lo1, w[:, 0:1] * hi0 + w[:, 1:2] * hi1], axis=1)
    h2 = h_ref[...] + y
    h2_ref[...] = h2
    xn_ref[...] = _rms(h2, gain_ref[...]).astype(xn_ref.dtype)


def combine_call(ys, pos, h1, weights, next_gain, xn_dtype, chunk=128):
    n, d = h1.shape
    pos3 = pos.reshape(n // chunk, 1, 2 * chunk)
    return pl.pallas_call(
        functools.partial(_combine_kernel, chunk=chunk),
        grid=(n // chunk,),
        in_specs=[pl.BlockSpec((1, 1, 2 * chunk), lambda c: (c, 0, 0), memory_space=pltpu.SMEM),
                  pl.BlockSpec(memory_space=pl.ANY),
                  pl.BlockSpec((chunk, d), lambda c: (c, 0)),
                  pl.BlockSpec((chunk, ROUTER_LANES), lambda c: (c, 0)),
                  pl.BlockSpec((1, d), lambda c: (0, 0))],
        out_specs=[pl.BlockSpec((chunk, d), lambda c: (c, 0)),
                   pl.BlockSpec((chunk, d), lambda c: (c, 0))],
        out_shape=[jax.ShapeDtypeStruct((n, d), F32),
                   jax.ShapeDtypeStruct((n, d), xn_dtype)],
        scratch_shapes=[pltpu.VMEM((2, chunk, d // 2), jnp.uint32), pltpu.SemaphoreType.DMA(())],
        compiler_params=_cparams(("arbitrary",), 32),
        name="combine_rows",
    )(pos3, ys, h1, weights, next_gain.reshape(1, d))


def _dispatch_plan(expert_ids, n_tiles):
    tm = TM_EXPERT
    onehot = (expert_ids[:, :, None] == jnp.arange(N_EXPERTS, dtype=jnp.int32)).astype(jnp.int32)
    per_token = onehot.sum(axis=1)
    before = jnp.cumsum(per_token, axis=0) - per_token
    rank = jnp.take_along_axis(before, expert_ids, axis=1)
    counts = per_token.sum(axis=0)
    padded = ((counts + tm - 1) // tm) * tm
    ends = jnp.cumsum(padded)
    pos = (ends - padded)[expert_ids] + rank
    n_valid = (ends[-1] // tm).astype(jnp.int32)
    tile_start = jnp.arange(n_tiles, dtype=jnp.int32) * tm
    tile_expert = jnp.searchsorted(ends, tile_start, side="right").astype(jnp.int32)
    last_expert = jnp.searchsorted(ends, ends[-1] - 1, side="right").astype(jnp.int32)
    tile_expert = jnp.where(tile_start < ends[-1], tile_expert, last_expert)
    pad_start = ((ends - padded + counts) // SUBLANES) * SUBLANES
    following = ends[tile_expert] // tm
    next_expert = jnp.where(following < n_valid, tile_expert[jnp.minimum(following, n_tiles - 1)], -1)
    return (pos.astype(jnp.int32), tile_expert, n_valid.reshape(1), pad_start.astype(jnp.int32),
            next_expert.astype(jnp.int32))


def kernel(x, rel_bias_table, norm_mix_gain, w_in, w_merge_gate, b_merge_gate, pool_mix, pool_scale, w_up_pool, lambda_q1, lambda_k1, lambda_q2, lambda_k2, subln_gain, w_up_attn, w_out, norm_ffn_gain, w_router_group, b_router_group, w_router_expert, b_router_expert, w_expert_gate, w_expert_up, w_expert_down, final_norm_gain):
    batch, seq, d = x.shape
    depth = w_in.shape[0]
    n = batch * seq
    n_tiles = (2 * n + N_EXPERTS * (TM_EXPERT - 1)) // TM_EXPERT + 2

    bias_tiles = bias_tiles_call(rel_bias_table)
    h = x.reshape(n, d)
    xn = rms_norm_call(h, norm_mix_gain[0], BF16)
    q_scale = LOG2E * DIFF_HEAD_DIM ** -0.5

    for l in range(depth):
        w_in_l = w_in[l].astype(BF16)
        u = proj_call(xn, w_in_l, 0, POOL_WIDTH // 512, 512, F32, name="in_proj_pool")
        qkv = proj_call(xn, w_in_l, 1, 3, ATTN_WIDTH, BF16, first_block_scale=q_scale, name="in_proj_qkv")
        mixed = pool_call(u, pool_mix[l].astype(BF16), pool_scale[l], seq)
        lambda_init = 0.8 - 0.6 * math.exp(-0.3 * l)
        lam_params = jnp.stack([lambda_q1[l], lambda_k1[l], lambda_q2[l], lambda_k2[l]])
        attn = attn_call(qkv.reshape(batch, seq, 3 * ATTN_WIDTH), bias_tiles, lam_params, subln_gain[l],
                         lambda_init, batch, seq).reshape(n, ATTN_WIDTH)
        z = merge_call(xn, mixed, attn, w_merge_gate[l].astype(BF16), b_merge_gate[l].reshape(1, -1),
                       w_up_pool[l].astype(BF16), w_up_attn[l].astype(BF16))

        w_router = jnp.concatenate(
            [w_router_group[l], jnp.transpose(w_router_expert[l], (1, 0, 2)).reshape(d, N_EXPERTS),
             jnp.zeros((d, ROUTER_LANES - N_GROUPS - N_EXPERTS), F32)], axis=1).astype(BF16)
        b_router = jnp.concatenate(
            [b_router_group[l], b_router_expert[l].reshape(-1),
             jnp.zeros((ROUTER_LANES - N_GROUPS - N_EXPERTS,), F32)]).reshape(1, ROUTER_LANES)
        h1, hn, logits = outproj_call(z, w_out[l].astype(BF16), h, norm_ffn_gain[l], w_router, b_router)

        ids, weights = route_call(logits)
        pos, tile_expert, n_valid, pad_start, next_expert = _dispatch_plan(ids[:, :2], n_tiles)
        xs = dispatch_call(hn, pos, pad_start, n_valid, n_tiles)
        ys = expert_call(xs, tile_expert, n_valid, next_expert, w_expert_gate, w_expert_up, w_expert_down, l)
        last = l == depth - 1
        next_gain = final_norm_gain if last else norm_mix_gain[l + 1]
        h, xn = combine_call(ys, pos, h1, weights, next_gain, F32 if last else BF16)

    return xn.reshape(batch, seq, d)
```

```python
import functools
import math

import numpy as np
import jax
import jax.numpy as jnp
from jax import lax
from jax.experimental import pallas as pl
from jax.experimental.pallas import tpu as pltpu

F32 = jnp.float32
BF16 = jnp.bfloat16

D_MODEL = 2048
POOL_WIDTH = 1024
POOL_WINDOWS = (2, 4, 8, 16)
POOL_GROUP_DIM = 256
POOL_HALO = 16
DIFF_HEADS = 8
DIFF_HEAD_DIM = 64
DIFF_V_DIM = 128
ATTN_WIDTH = 1024
REL_BUCKETS = 32
REL_MAX_DISTANCE = 128
N_GROUPS = 4
EXPERTS_PER_GROUP = 8
N_EXPERTS = 32
D_EXPERT = 512
EPS = 1e-6
NEG_INF = -1e30
LOG2E = 1.4426950408889634

ROUTER_LANES = 128
TQ = 256
TK = 256
KV_SUPER = 2
HEADS_PER_STEP = 2
TM_EXPERT = 256
LANES = 128
ROW_DMA_UNROLL = 8
MIB = 1024 * 1024


def _cparams(sem, vmem_mib):
    return pltpu.CompilerParams(dimension_semantics=sem, vmem_limit_bytes=vmem_mib * MIB)


def _rms(xf, gain):
    ms = jnp.mean(xf * xf, axis=-1, keepdims=True)
    return xf * lax.rsqrt(ms + EPS) * gain


def _norm_kernel(h_ref, g_ref, o_ref):
    o_ref[...] = _rms(h_ref[...], g_ref[...]).astype(o_ref.dtype)


def rms_norm_call(h, gain, out_dtype, tm=512):
    n, d = h.shape
    return pl.pallas_call(
        _norm_kernel,
        grid=(n // tm,),
        in_specs=[pl.BlockSpec((tm, d), lambda m: (m, 0)),
                  pl.BlockSpec((1, d), lambda m: (0, 0))],
        out_specs=pl.BlockSpec((tm, d), lambda m: (m, 0)),
        out_shape=jax.ShapeDtypeStruct((n, d), out_dtype),
        compiler_params=_cparams(("parallel",), 32),
        name="rms_norm",
    )(h, gain.reshape(1, d))


def _proj_kernel(x_ref, w_ref, o_ref, *, first_block_scale):
    acc = jnp.dot(x_ref[...], w_ref[...], preferred_element_type=F32)
    if first_block_scale is not None:
        acc = acc * jnp.where(pl.program_id(0) == 0, first_block_scale, 1.0).astype(F32)
    o_ref[...] = acc.astype(o_ref.dtype)


def proj_call(x, w, col_block0, n_col_blocks, tn, out_dtype, first_block_scale=None, tm=512, name="proj"):
    n, k = x.shape
    return pl.pallas_call(
        functools.partial(_proj_kernel, first_block_scale=first_block_scale),
        grid=(n_col_blocks, n // tm),
        in_specs=[pl.BlockSpec((tm, k), lambda j, m: (m, 0)),
                  pl.BlockSpec((k, tn), lambda j, m: (0, col_block0 + j))],
        out_specs=pl.BlockSpec((tm, tn), lambda j, m: (m, j)),
        out_shape=jax.ShapeDtypeStruct((n, n_col_blocks * tn), out_dtype),
        compiler_params=_cparams(("parallel", "parallel"), 40),
        name=name,
    )(x, w)


def _pool_kernel(cur_ref, prev_ref, mix_ref, scale_ref, o_ref, *, tm, seq):
    m = pl.program_id(0)
    row0 = (m * tm) % seq
    cur = cur_ref[...]
    prev = jnp.where(row0 == 0, 0.0, prev_ref[...])
    pos = row0 + lax.broadcasted_iota(jnp.int32, (tm, 1), 0)
    outs = []
    for g, w in enumerate(POOL_WINDOWS):
        sl = slice(g * POOL_GROUP_DIM, (g + 1) * POOL_GROUP_DIM)
        x = jnp.concatenate([prev[:, sl], cur[:, sl]], axis=0)
        s, d = x, 1
        while d < w:
            s = s[:-d] + s[d:]
            d *= 2
        start = POOL_HALO - w + 1
        wsum = s[start:start + tm]
        count = jnp.minimum(pos + 1, w).astype(F32)
        pooled = wsum / count - cur[:, sl]
        mixed = jnp.dot(pooled.astype(BF16), mix_ref[g], preferred_element_type=F32)
        outs.append(mixed * scale_ref[:, sl])
    o_ref[...] = jnp.concatenate(outs, axis=1).astype(o_ref.dtype)


def pool_call(u, mix_bf16, scale, seq, tm=512):
    n, c = u.shape
    blocks_per_tile = tm // POOL_HALO
    return pl.pallas_call(
        functools.partial(_pool_kernel, tm=tm, seq=seq),
        grid=(n // tm,),
        in_specs=[pl.BlockSpec((tm, c), lambda m: (m, 0)),
                  pl.BlockSpec((POOL_HALO, c), lambda m: (jnp.maximum(m * blocks_per_tile - 1, 0), 0)),
                  pl.BlockSpec(mix_bf16.shape, lambda m: (0, 0, 0)),
                  pl.BlockSpec((1, c), lambda m: (0, 0))],
        out_specs=pl.BlockSpec((tm, c), lambda m: (m, 0)),
        out_shape=jax.ShapeDtypeStruct((n, c), BF16),
        compiler_params=_cparams(("parallel",), 32),
        name="pool_mixer",
    )(u, u, mix_bf16, scale.reshape(1, c))


N_BIAS_TILES = 4


def _bucket_tiles():
    kk = np.arange(TK)[:, None]
    qq = np.arange(TQ)[None, :]
    tiles = []
    for rel in (0, 1, 2, -1):
        n = rel * TK + qq - kk
        max_exact = REL_BUCKETS // 2
        nf = np.maximum(n, 1).astype(np.float64)
        large = max_exact + (np.log(nf / max_exact) / math.log(REL_MAX_DISTANCE / max_exact)
                             * (REL_BUCKETS - max_exact)).astype(np.int64)
        large = np.minimum(large, REL_BUCKETS - 1)
        bucket = np.where(n < max_exact, n, large)
        tiles.append(np.where(n < 0, -1, bucket))
    return np.stack(tiles).astype(np.int32)


def _bias_kernel(table_ref, bucket_ref, o_ref):
    h = pl.program_id(0)
    bucket = bucket_ref[...]
    acc = jnp.full(bucket.shape, NEG_INF, F32)
    for b in range(REL_BUCKETS):
        acc = jnp.where(bucket == b, table_ref[b * DIFF_HEADS + h] * LOG2E, acc)
    o_ref[...] = acc


def bias_tiles_call(rel_table):
    bucket = jnp.asarray(_bucket_tiles())
    return pl.pallas_call(
        _bias_kernel,
        grid=(DIFF_HEADS,),
        in_specs=[pl.BlockSpec(memory_space=pltpu.SMEM),
                  pl.BlockSpec(bucket.shape, lambda h: (0, 0, 0))],
        out_specs=pl.BlockSpec((None,) + bucket.shape, lambda h: (h, 0, 0, 0)),
        out_shape=jax.ShapeDtypeStruct((DIFF_HEADS,) + bucket.shape, F32),
        compiler_params=_cparams(("parallel",), 32),
        name="rel_bias_tiles",
    )(rel_table.reshape(-1), bucket)


def _attn_kernel(q_ref, k_ref, v_ref, bias_ref, lam_ref, gain_ref, o_ref, vt_ref, s_ref, acc_ref,
                 *, lambda_init, n_super):
    qi = pl.program_id(2)
    tks = KV_SUPER * TK
    heads = range(HEADS_PER_STEP)

    @pl.when(qi == 0)
    def _():
        for hh in heads:
            cols = slice(hh * DIFF_V_DIM, (hh + 1) * DIFF_V_DIM)
            for c in range(n_super):
                vt_ref[hh, c] = v_ref[c * tks:(c + 1) * tks, cols].astype(F32).T.astype(BF16)

    qds = []
    for hh in heads:
        q = q_ref[:, hh * DIFF_V_DIM:(hh + 1) * DIFF_V_DIM]
        lane = lax.broadcasted_iota(jnp.int32, q.shape, 1)
        zero = jnp.zeros_like(q)
        qds.append(jnp.concatenate([jnp.where(lane < DIFF_HEAD_DIM, q, zero),
                                    jnp.where(lane >= DIFF_HEAD_DIM, q, zero)], axis=0))

    def scores(t, hh):
        kb = k_ref[pl.ds(pl.multiple_of(t * tks, tks), tks), hh * DIFF_V_DIM:(hh + 1) * DIFF_V_DIM]
        s = lax.dot_general(kb, qds[hh], (((1,), (1,)), ((), ())), preferred_element_type=F32)
        parts = []
        for u in range(KV_SUPER):
            rel = qi - (t * KV_SUPER + u)
            bias = bias_ref[hh, jnp.where(rel < 0, N_BIAS_TILES - 1, jnp.minimum(rel, 2))]
            parts.append(s[u * TK:(u + 1) * TK] + jnp.concatenate([bias, bias], axis=1))
        s = jnp.concatenate(parts, axis=0)
        s_ref[hh] = s
        return jnp.max(s, axis=0, keepdims=True)

    def softmax_step(t, hh, m_prev, m_cur, l):
        alpha = jnp.exp2(m_prev - m_cur)
        p = jnp.exp2(s_ref[hh] - m_cur)
        l_new = l * alpha + jnp.sum(p, axis=0, keepdims=True)
        pv = jnp.dot(vt_ref[hh, t], p.astype(BF16), preferred_element_type=F32)
        acc_ref[hh] = acc_ref[hh] * alpha + pv
        return l_new

    last = qi // KV_SUPER
    acc_ref[...] = jnp.zeros(acc_ref.shape, F32)
    neg = jnp.full((1, 2 * TQ), NEG_INF, F32)
    init = tuple((neg, jnp.maximum(neg, scores(0, hh)), jnp.zeros((1, 2 * TQ), F32)) for hh in heads)

    def body(t, carry):
        out = []
        for hh in heads:
            m_prev, m_cur, l = carry[hh]
            l = softmax_step(t, hh, m_prev, m_cur, l)
            m_next = jnp.maximum(m_cur, scores(t + 1, hh))
            out.append((m_cur, m_next, l))
        return tuple(out)

    carry = lax.fori_loop(0, last, body, init)

    lam_p = lam_ref[...]
    lam = (jnp.exp(jnp.sum(lam_p[0:1] * lam_p[1:2], axis=1, keepdims=True))
           - jnp.exp(jnp.sum(lam_p[2:3] * lam_p[3:4], axis=1, keepdims=True)) + lambda_init)
    for hh in heads:
        m_prev, m_cur, l = carry[hh]
        l = softmax_step(last, hh, m_prev, m_cur, l)
        acc = acc_ref[hh]
        o = acc[:, :TQ] / l[:, :TQ] - lam * (acc[:, TQ:] / l[:, TQ:])
        ms = jnp.mean(o * o, axis=0, keepdims=True)
        y = o * lax.rsqrt(ms + EPS) * gain_ref[...] * (1.0 - lambda_init)
        o_ref[:, hh * DIFF_V_DIM:(hh + 1) * DIFF_V_DIM] = y.T.astype(o_ref.dtype)


def attn_call(qkv, bias_tiles, lam_params, subln_gain, lambda_init, batch, seq):
    tks = KV_SUPER * TK
    n_super = seq // tks
    hps = HEADS_PER_STEP
    width = hps * DIFF_V_DIM
    groups = DIFF_HEADS // hps
    return pl.pallas_call(
        functools.partial(_attn_kernel, lambda_init=lambda_init, n_super=n_super),
        grid=(batch, groups, seq // TQ),
        in_specs=[pl.BlockSpec((None, TQ, width), lambda b, g, i: (b, i, g)),
                  pl.BlockSpec((None, seq, width), lambda b, g, i: (b, 0, groups + g)),
                  pl.BlockSpec((None, seq, width), lambda b, g, i: (b, 0, 2 * groups + g)),
                  pl.BlockSpec((hps, N_BIAS_TILES, TK, TQ), lambda b, g, i: (g, 0, 0, 0)),
                  pl.BlockSpec((4, DIFF_HEAD_DIM), lambda b, g, i: (0, 0)),
                  pl.BlockSpec((DIFF_V_DIM, 1), lambda b, g, i: (0, 0))],
        out_specs=pl.BlockSpec((None, TQ, width), lambda b, g, i: (b, i, g)),
        out_shape=jax.ShapeDtypeStruct((batch, seq, ATTN_WIDTH), BF16),
        scratch_shapes=[pltpu.VMEM((hps, n_super, DIFF_V_DIM, tks), BF16),
                        pltpu.VMEM((hps, tks, 2 * TQ), F32),
                        pltpu.VMEM((hps, DIFF_V_DIM, 2 * TQ), F32)],
        compiler_params=_cparams(("parallel", "parallel", "arbitrary"), 48),
        name="diff_attention",
    )(qkv, qkv, qkv, bias_tiles, lam_params, subln_gain.reshape(DIFF_V_DIM, 1))


def _sigmoid(x):
    return 1.0 / (1.0 + jnp.exp(-x))


def _merge_kernel(xn_ref, mixed_ref, attn_ref, wgp_ref, wga_ref, bgp_ref, bga_ref, wup_ref, wua_ref, z_ref):
    xn = xn_ref[...]
    g_pool = _sigmoid(jnp.dot(xn, wgp_ref[...], preferred_element_type=F32) + bgp_ref[...])
    g_attn = _sigmoid(jnp.dot(xn, wga_ref[...], preferred_element_type=F32) + bga_ref[...])
    y_pool = jnp.dot(mixed_ref[...], wup_ref[...], preferred_element_type=F32)
    y_attn = jnp.dot(attn_ref[...], wua_ref[...], preferred_element_type=F32)
    z_ref[...] = (g_pool * y_pool + g_attn * y_attn).astype(z_ref.dtype)


def merge_call(xn, mixed, attn, w_gate, b_gate, w_up_pool, w_up_attn, tm=512, tn=512):
    n, d = xn.shape
    nb = d // tn
    return pl.pallas_call(
        _merge_kernel,
        grid=(nb, n // tm),
        in_specs=[pl.BlockSpec((tm, d), lambda j, m: (m, 0)),
                  pl.BlockSpec((tm, POOL_WIDTH), lambda j, m: (m, 0)),
                  pl.BlockSpec((tm, ATTN_WIDTH), lambda j, m: (m, 0)),
                  pl.BlockSpec((d, tn), lambda j, m: (0, j)),
                  pl.BlockSpec((d, tn), lambda j, m: (0, nb + j)),
                  pl.BlockSpec((1, tn), lambda j, m: (0, j)),
                  pl.BlockSpec((1, tn), lambda j, m: (0, nb + j)),
                  pl.BlockSpec((POOL_WIDTH, tn), lambda j, m: (0, j)),
                  pl.BlockSpec((ATTN_WIDTH, tn), lambda j, m: (0, j))],
        out_specs=pl.BlockSpec((tm, tn), lambda j, m: (m, j)),
        out_shape=jax.ShapeDtypeStruct((n, d), BF16),
        compiler_params=_cparams(("parallel", "parallel"), 48),
        name="gated_merge",
    )(xn, mixed, attn, w_gate, w_gate, b_gate, b_gate, w_up_pool, w_up_attn)


HIGH_HALF = 0xFFFF0000


def _pack_halves(x):
    c = x.shape[1] // 2
    lo = lax.bitcast_convert_type(x[:, :c].astype(BF16).astype(F32), jnp.uint32)
    hi = lax.bitcast_convert_type(x[:, c:].astype(BF16).astype(F32), jnp.uint32)
    return (lo >> 16) | hi


def _unpack_halves(p):
    lo = lax.bitcast_convert_type(p << 16, F32)
    hi = lax.bitcast_convert_type(p & jnp.uint32(HIGH_HALF), F32)
    return lo, hi


ROW_TILE = 8


def _store_row_tiles(ref, packed):
    m = packed.shape[0]
    for s in range(ROW_TILE):
        ref[pl.ds(s, m, stride=ROW_TILE), :] = packed[:, s * LANES:(s + 1) * LANES]


def _load_row_tiles(ref):
    m = ref.shape[0] // ROW_TILE
    return jnp.concatenate([ref[pl.ds(s, m, stride=ROW_TILE), :] for s in range(ROW_TILE)], axis=1)


def _rows(ref, first, count=1):
    return ref.at[pl.ds(pl.multiple_of(first * ROW_TILE, ROW_TILE), count * ROW_TILE)]


def _outproj_kernel(z_ref, w_ref, h_ref, gain_ref, wr_ref, br_ref, h1_ref, hn_ref, logit_ref):
    h1 = h_ref[...] + jnp.dot(z_ref[...], w_ref[...], preferred_element_type=F32)
    h1_ref[...] = h1
    hn = _rms(h1, gain_ref[...])
    _store_row_tiles(hn_ref, _pack_halves(hn))
    logit_ref[...] = jnp.dot(hn.astype(BF16), wr_ref[...], preferred_element_type=F32) + br_ref[...]


def outproj_call(z, w_out, h, gain, w_router, b_router, tm=256):
    n, d = h.shape
    return pl.pallas_call(
        _outproj_kernel,
        grid=(n // tm,),
        in_specs=[pl.BlockSpec((tm, d), lambda m: (m, 0)),
                  pl.BlockSpec((d, d), lambda m: (0, 0)),
                  pl.BlockSpec((tm, d), lambda m: (m, 0)),
                  pl.BlockSpec((1, d), lambda m: (0, 0)),
                  pl.BlockSpec((d, ROUTER_LANES), lambda m: (0, 0)),
                  pl.BlockSpec((1, ROUTER_LANES), lambda m: (0, 0))],
        out_specs=[pl.BlockSpec((tm, d), lambda m: (m, 0)),
                   pl.BlockSpec((tm * ROW_TILE, LANES), lambda m: (m, 0)),
                   pl.BlockSpec((tm, ROUTER_LANES), lambda m: (m, 0))],
        out_shape=[jax.ShapeDtypeStruct((n, d), F32),
                   jax.ShapeDtypeStruct((n * ROW_TILE, LANES), jnp.uint32),
                   jax.ShapeDtypeStruct((n, ROUTER_LANES), F32)],
        compiler_params=_cparams(("parallel",), 56),
        name="out_proj_norm_router",
    )(z, w_out, h, gain.reshape(1, d), w_router, b_router)


def _route_kernel(logit_ref, id_ref, w_ref):
    x = logit_ref[...]
    lane = lax.broadcasted_iota(jnp.int32, x.shape, 1).astype(F32)
    big = float(ROUTER_LANES)

    def first_argmax(vals):
        top = jnp.max(vals, axis=1, keepdims=True)
        idx = jnp.min(jnp.where(vals == top, lane, big), axis=1, keepdims=True)
        return top, idx

    gmask = lane < N_GROUPS
    g_top, g_sel = first_argmax(jnp.where(gmask, x, -jnp.inf))
    g_weight = 1.0 / jnp.sum(jnp.where(gmask, jnp.exp(x - g_top), 0.0), axis=1, keepdims=True)
    lo = N_GROUPS + EXPERTS_PER_GROUP * g_sel
    e_vals = jnp.where((lane >= lo) & (lane < lo + EXPERTS_PER_GROUP), x, -jnp.inf)
    v1, i1 = first_argmax(e_vals)
    v2, i2 = first_argmax(jnp.where(lane == i1, -jnp.inf, e_vals))
    t = jnp.exp(v2 - v1)
    w1 = g_weight / (1.0 + t)
    w2 = g_weight * t / (1.0 + t)
    id_ref[...] = jnp.where(lane == 0, i1 - N_GROUPS, jnp.where(lane == 1, i2 - N_GROUPS, 0.0)).astype(jnp.int32)
    w_ref[...] = jnp.where(lane == 0, w1, jnp.where(lane == 1, w2, 0.0))


def route_call(logits, tm=512):
    n = logits.shape[0]
    spec = pl.BlockSpec((tm, ROUTER_LANES), lambda m: (m, 0))
    return pl.pallas_call(
        _route_kernel,
        grid=(n // tm,),
        in_specs=[spec],
        out_specs=[spec, spec],
        out_shape=[jax.ShapeDtypeStruct((n, ROUTER_LANES), jnp.int32),
                   jax.ShapeDtypeStruct((n, ROUTER_LANES), F32)],
        compiler_params=_cparams(("parallel",), 32),
        name="route_top2",
    )(logits)


def _dispatch_kernel(pad_ref, nv_ref, pos_ref, src_ref, dst_ref, zero_buf, sem, *, chunk, n_tiles):
    @pl.when(pl.program_id(0) == 0)
    def _():
        zero_buf[...] = jnp.zeros(zero_buf.shape, zero_buf.dtype)

        def fill(e):
            return pltpu.make_async_copy(zero_buf, _rows(dst_ref, pad_ref[e], TM_EXPERT), sem)

        for e in range(N_EXPERTS):
            fill(e).start()
        for e in range(N_EXPERTS):
            fill(e).wait()

        def fill_tile(i):
            return pltpu.make_async_copy(zero_buf, _rows(dst_ref, i * TM_EXPERT, TM_EXPERT), sem)

        def start_tile(i, c):
            fill_tile(i).start()
            return c

        def wait_tile(i, c):
            fill_tile(i).wait()
            return c

        lax.fori_loop(nv_ref[0], n_tiles, start_tile, 0)
        lax.fori_loop(nv_ref[0], n_tiles, wait_tile, 0)

    def start(t, c):
        for j in range(2):
            pltpu.make_async_copy(_rows(src_ref, t), _rows(dst_ref, pos_ref[0, 0, 2 * t + j]), sem).start(priority=j)
        return c

    lax.fori_loop(0, chunk, start, 0, unroll=ROW_DMA_UNROLL)
    all_rows = pltpu.make_async_copy(src_ref, _rows(dst_ref, 0, chunk), sem)
    for j in range(2):
        all_rows.wait()


def dispatch_call(hn, pos, pad_start, n_valid_tiles, n_tiles, chunk=256):
    n = hn.shape[0] // ROW_TILE
    pos3 = pos.reshape(n // chunk, 1, 2 * chunk)
    grid_spec = pltpu.PrefetchScalarGridSpec(
        num_scalar_prefetch=2,
        grid=(n // chunk,),
        in_specs=[pl.BlockSpec((1, 1, 2 * chunk), lambda i, pad, nv: (i, 0, 0), memory_space=pltpu.SMEM),
                  pl.BlockSpec((chunk * ROW_TILE, LANES), lambda i, pad, nv: (i, 0))],
        out_specs=pl.BlockSpec(memory_space=pl.ANY),
        scratch_shapes=[pltpu.VMEM((TM_EXPERT * ROW_TILE, LANES), hn.dtype), pltpu.SemaphoreType.DMA(())],
    )
    return pl.pallas_call(
        functools.partial(_dispatch_kernel, chunk=chunk, n_tiles=n_tiles),
        grid_spec=grid_spec,
        out_shape=jax.ShapeDtypeStruct((n_tiles * TM_EXPERT * ROW_TILE, LANES), hn.dtype),
        compiler_params=_cparams(("arbitrary",), 32),
        name="dispatch_rows",
    )(pad_start, n_valid_tiles, pos3, hn)


def _expert_kernel(te_ref, nv_ref, nx_ref, x_ref, wg_hbm, wu_hbm, wd_hbm, y_ref,
                   wg_f, wu_f, wd_f, wg_s, wu_s, wd_s, seg_ref, sems, *, layer):
    i = pl.program_id(0)
    valid = i < nv_ref[0]
    expert = te_ref[i]
    changed = jnp.logical_or(i == 0, expert != te_ref[jnp.maximum(i - 1, 0)])

    def fetch(e, slot):
        return [pltpu.make_async_copy(hbm.at[layer, e], buf.at[slot], sems.at[slot])
                for hbm, buf in ((wg_hbm, wg_f), (wu_hbm, wu_f), (wd_hbm, wd_f))]

    @pl.when(i == 0)
    def _():
        seg_ref[0] = 0
        for cp in fetch(expert, 0):
            cp.start()

    @pl.when(jnp.logical_and(valid, changed))
    def _():
        slot = seg_ref[0] % 2
        for cp in fetch(expert, slot):
            cp.wait()
        nxt = nx_ref[i]

        @pl.when(nxt >= 0)
        def _():
            for cp in fetch(nxt, 1 - slot):
                cp.start()

        wg_s[...] = wg_f[slot].astype(BF16)
        wu_s[...] = wu_f[slot].astype(BF16)
        wd_s[...] = wd_f[slot].astype(BF16)
        seg_ref[0] = seg_ref[0] + 1

    @pl.when(valid)
    def _():
        lo, hi = _unpack_halves(_load_row_tiles(x_ref))
        x = jnp.concatenate([lo.astype(BF16), hi.astype(BF16)], axis=1)
        a = jnp.dot(x, wg_s[...], preferred_element_type=F32)
        b = jnp.dot(x, wu_s[...], preferred_element_type=F32)
        hmid = (a * _sigmoid(a) * b).astype(BF16)
        _store_row_tiles(y_ref, _pack_halves(jnp.dot(hmid, wd_s[...], preferred_element_type=F32)))

    @pl.when(jnp.logical_not(valid))
    def _():
        y_ref[...] = jnp.zeros(y_ref.shape, y_ref.dtype)


def expert_call(xs, tile_expert, n_valid_tiles, next_expert, w_gate, w_up, w_down, layer):
    p = xs.shape[0] // ROW_TILE
    d = 2 * ROW_TILE * LANES
    tm = TM_EXPERT
    f = w_gate.shape[-1]
    grid_spec = pltpu.PrefetchScalarGridSpec(
        num_scalar_prefetch=3,
        grid=(p // tm,),
        in_specs=[pl.BlockSpec((tm * ROW_TILE, LANES), lambda i, te, nv, nx: (jnp.minimum(i, nv[0] - 1), 0)),
                  pl.BlockSpec(memory_space=pl.ANY),
                  pl.BlockSpec(memory_space=pl.ANY),
                  pl.BlockSpec(memory_space=pl.ANY)],
        out_specs=pl.BlockSpec((tm * ROW_TILE, LANES), lambda i, te, nv, nx: (i, 0)),
        scratch_shapes=[pltpu.VMEM((2, d, f), F32), pltpu.VMEM((2, d, f), F32), pltpu.VMEM((2, f, d), F32),
                        pltpu.VMEM((d, f), BF16), pltpu.VMEM((d, f), BF16), pltpu.VMEM((f, d), BF16),
                        pltpu.SMEM((1,), jnp.int32), pltpu.SemaphoreType.DMA((2,))],
    )
    return pl.pallas_call(
        functools.partial(_expert_kernel, layer=layer),
        grid_spec=grid_spec,
        out_shape=jax.ShapeDtypeStruct(xs.shape, jnp.uint32),
        compiler_params=_cparams(("arbitrary",), 58),
        name="expert_swiglu",
    )(tile_expert, n_valid_tiles, next_expert, xs, w_gate, w_up, w_down)


def _combine_kernel(pos_ref, ys_ref, h_ref, w_ref, gain_ref, h2_ref, xn_ref, buf, sem, *, chunk):
    def start(t, c):
        for j in range(2):
            pltpu.make_async_copy(_rows(ys_ref, pos_ref[0, 0, 2 * t + j]), _rows(buf.at[j], t),
                                  sem).start(priority=j)
        return c

    lax.fori_loop(0, chunk, start, 0, unroll=ROW_DMA_UNROLL)
    for j in range(2):
        pltpu.make_async_copy(_rows(ys_ref, 0, chunk), buf.at[j], sem).wait()
    w = w_ref[...]
    lo0, hi0 = _unpack_halves(_load_row_tiles(buf.at[0]))
    lo1, hi1 = _unpack_halves(_load_row_tiles(buf.at[1]))
    y = jnp.concatenate([w[:, 0:1] * lo0 + w[:, 1:2] * lo1, w[:, 0:1] * hi0 + w[:, 1:2] * hi1], axis=1)
    h2 = h_ref[...] + y
    h2_ref[...] = h2
    xn_ref[...] = _rms(h2, gain_ref[...]).astype(xn_ref.dtype)


def combine_call(ys, pos, h1, weights, next_gain, xn_dtype, chunk=128):
    n, d = h1.shape
    pos3 = pos.reshape(n // chunk, 1, 2 * chunk)
    return pl.pallas_call(
        functools.partial(_combine_kernel, chunk=chunk),
        grid=(n // chunk,),
        in_specs=[pl.BlockSpec((1, 1, 2 * chunk), lambda c: (c, 0, 0), memory_space=pltpu.SMEM),
                  pl.BlockSpec(memory_space=pl.ANY),
                  pl.BlockSpec((chunk, d), lambda c: (c, 0)),
                  pl.BlockSpec((chunk, ROUTER_LANES), lambda c: (c, 0)),
                  pl.BlockSpec((1, d), lambda c: (0, 0))],
        out_specs=[pl.BlockSpec((chunk, d), lambda c: (c, 0)),
                   pl.BlockSpec((chunk, d), lambda c: (c, 0))],
        out_shape=[jax.ShapeDtypeStruct((n, d), F32),
                   jax.ShapeDtypeStruct((n, d), xn_dtype)],
        scratch_shapes=[pltpu.VMEM((2, chunk * ROW_TILE, LANES), jnp.uint32), pltpu.SemaphoreType.DMA(())],
        compiler_params=_cparams(("arbitrary",), 32),
        name="combine_rows",
    )(pos3, ys, h1, weights, next_gain.reshape(1, d))


def _dispatch_plan(expert_ids, n_tiles):
    tm = TM_EXPERT
    onehot = (expert_ids[:, :, None] == jnp.arange(N_EXPERTS, dtype=jnp.int32)).astype(jnp.int32)
    per_token = onehot.sum(axis=1)
    before = jnp.cumsum(per_token, axis=0) - per_token
    rank = jnp.take_along_axis(before, expert_ids, axis=1)
    counts = per_token.sum(axis=0)
    padded = ((counts + tm - 1) // tm) * tm
    ends = jnp.cumsum(padded)
    pos = (ends - padded)[expert_ids] + rank
    n_valid = (ends[-1] // tm).astype(jnp.int32)
    tile_start = jnp.arange(n_tiles, dtype=jnp.int32) * tm
    tile_expert = jnp.searchsorted(ends, tile_start, side="right").astype(jnp.int32)
    last_expert = jnp.searchsorted(ends, ends[-1] - 1, side="right").astype(jnp.int32)
    tile_expert = jnp.where(tile_start < ends[-1], tile_expert, last_expert)
    pad_start = ends - padded + counts
    following = ends[tile_expert] // tm
    next_expert = jnp.where(following < n_valid, tile_expert[jnp.minimum(following, n_tiles - 1)], -1)
    return (pos.astype(jnp.int32), tile_expert, n_valid.reshape(1), pad_start.astype(jnp.int32),
            next_expert.astype(jnp.int32))


def kernel(x, rel_bias_table, norm_mix_gain, w_in, w_merge_gate, b_merge_gate, pool_mix, pool_scale, w_up_pool, lambda_q1, lambda_k1, lambda_q2, lambda_k2, subln_gain, w_up_attn, w_out, norm_ffn_gain, w_router_group, b_router_group, w_router_expert, b_router_expert, w_expert_gate, w_expert_up, w_expert_down, final_norm_gain):
    batch, seq, d = x.shape
    depth = w_in.shape[0]
    n = batch * seq
    n_tiles = (2 * n + N_EXPERTS * (TM_EXPERT - 1)) // TM_EXPERT + 1

    bias_tiles = bias_tiles_call(rel_bias_table)
    h = x.reshape(n, d)
    xn = rms_norm_call(h, norm_mix_gain[0], BF16)
    q_scale = LOG2E * DIFF_HEAD_DIM ** -0.5

    for l in range(depth):
        w_in_l = w_in[l].astype(BF16)
        u = proj_call(xn, w_in_l, 0, POOL_WIDTH // 512, 512, F32, name="in_proj_pool")
        qkv = proj_call(xn, w_in_l, 1, 3, ATTN_WIDTH, BF16, first_block_scale=q_scale, name="in_proj_qkv")
        mixed = pool_call(u, pool_mix[l].astype(BF16), pool_scale[l], seq)
        lambda_init = 0.8 - 0.6 * math.exp(-0.3 * l)
        lam_params = jnp.stack([lambda_q1[l], lambda_k1[l], lambda_q2[l], lambda_k2[l]])
        attn = attn_call(qkv.reshape(batch, seq, 3 * ATTN_WIDTH), bias_tiles, lam_params, subln_gain[l],
                         lambda_init, batch, seq).reshape(n, ATTN_WIDTH)
        z = merge_call(xn, mixed, attn, w_merge_gate[l].astype(BF16), b_merge_gate[l].reshape(1, -1),
                       w_up_pool[l].astype(BF16), w_up_attn[l].astype(BF16))

        w_router = jnp.concatenate(
            [w_router_group[l], jnp.transpose(w_router_expert[l], (1, 0, 2)).reshape(d, N_EXPERTS),
             jnp.zeros((d, ROUTER_LANES - N_GROUPS - N_EXPERTS), F32)], axis=1).astype(BF16)
        b_router = jnp.concatenate(
            [b_router_group[l], b_router_expert[l].reshape(-1),
             jnp.zeros((ROUTER_LANES - N_GROUPS - N_EXPERTS,), F32)]).reshape(1, ROUTER_LANES)
        h1, hn, logits = outproj_call(z, w_out[l].astype(BF16), h, norm_ffn_gain[l], w_router, b_router)

        ids, weights = route_call(logits)
        pos, tile_expert, n_valid, pad_start, next_expert = _dispatch_plan(ids[:, :2], n_tiles)
        xs = dispatch_call(hn, pos, pad_start, n_valid, n_tiles)
        ys = expert_call(xs, tile_expert, n_valid, next_expert, w_expert_gate, w_expert_up, w_expert_down, l)
        last = l == depth - 1
        next_gain = final_norm_gain if last else norm_mix_gain[l + 1]
        h, xn = combine_call(ys, pos, h1, weights, next_gain, F32 if last else BF16)

    return xn.reshape(batch, seq, d)
```

```python
import functools
import math

import numpy as np
import jax
import jax.numpy as jnp
from jax import lax
from jax.experimental import pallas as pl
from jax.experimental.pallas import tpu as pltpu

F32 = jnp.float32
BF16 = jnp.bfloat16

D_MODEL = 2048
POOL_WIDTH = 1024
POOL_WINDOWS = (2, 4, 8, 16)
POOL_GROUP_DIM = 256
POOL_HALO = 16
DIFF_HEADS = 8
DIFF_HEAD_DIM = 64
DIFF_V_DIM = 128
ATTN_WIDTH = 1024
REL_BUCKETS = 32
REL_MAX_DISTANCE = 128
N_GROUPS = 4
EXPERTS_PER_GROUP = 8
N_EXPERTS = 32
D_EXPERT = 512
EPS = 1e-6
NEG_INF = -1e30
LOG2E = 1.4426950408889634

ROUTER_LANES = 128
TQ = 256
TK = 256
KV_SUPER = 2
HEADS_PER_STEP = 2
TM_EXPERT = 256
LANES = 128
ROW_DMA_UNROLL = 8
MIB = 1024 * 1024


def _cparams(sem, vmem_mib):
    return pltpu.CompilerParams(dimension_semantics=sem, vmem_limit_bytes=vmem_mib * MIB)


def _rms(xf, gain):
    ms = jnp.mean(xf * xf, axis=-1, keepdims=True)
    return xf * lax.rsqrt(ms + EPS) * gain


def _norm_kernel(h_ref, g_ref, o_ref):
    o_ref[...] = _rms(h_ref[...], g_ref[...]).astype(o_ref.dtype)


def rms_norm_call(h, gain, out_dtype, tm=512):
    n, d = h.shape
    return pl.pallas_call(
        _norm_kernel,
        grid=(n // tm,),
        in_specs=[pl.BlockSpec((tm, d), lambda m: (m, 0)),
                  pl.BlockSpec((1, d), lambda m: (0, 0))],
        out_specs=pl.BlockSpec((tm, d), lambda m: (m, 0)),
        out_shape=jax.ShapeDtypeStruct((n, d), out_dtype),
        compiler_params=_cparams(("parallel",), 32),
        name="rms_norm",
    )(h, gain.reshape(1, d))


def _proj_kernel(x_ref, w_ref, o_ref, w_s, *, first_block_scale):
    @pl.when(pl.program_id(1) == 0)
    def _():
        w_s[...] = w_ref[...].astype(BF16)

    acc = jnp.dot(x_ref[...], w_s[...], preferred_element_type=F32)
    if first_block_scale is not None:
        acc = acc * jnp.where(pl.program_id(0) == 0, first_block_scale, 1.0).astype(F32)
    o_ref[...] = acc.astype(o_ref.dtype)


def proj_call(x, w, layer, col_block0, n_col_blocks, tn, out_dtype, first_block_scale=None, tm=512, name="proj"):
    n, k = x.shape
    return pl.pallas_call(
        functools.partial(_proj_kernel, first_block_scale=first_block_scale),
        grid=(n_col_blocks, n // tm),
        in_specs=[pl.BlockSpec((tm, k), lambda j, m: (m, 0)),
                  pl.BlockSpec((None, k, tn), lambda j, m: (layer, 0, col_block0 + j))],
        out_specs=pl.BlockSpec((tm, tn), lambda j, m: (m, j)),
        out_shape=jax.ShapeDtypeStruct((n, n_col_blocks * tn), out_dtype),
        scratch_shapes=[pltpu.VMEM((k, tn), BF16)],
        compiler_params=_cparams(("arbitrary", "arbitrary"), 48),
        name=name,
    )(x, w)


def _pool_kernel(cur_ref, prev_ref, mix_ref, scale_ref, o_ref, *, tm, seq):
    m = pl.program_id(0)
    row0 = (m * tm) % seq
    cur = cur_ref[...]
    prev = jnp.where(row0 == 0, 0.0, prev_ref[...])
    pos = row0 + lax.broadcasted_iota(jnp.int32, (tm, 1), 0)
    outs = []
    for g, w in enumerate(POOL_WINDOWS):
        sl = slice(g * POOL_GROUP_DIM, (g + 1) * POOL_GROUP_DIM)
        x = jnp.concatenate([prev[:, sl], cur[:, sl]], axis=0)
        s, d = x, 1
        while d < w:
            s = s[:-d] + s[d:]
            d *= 2
        start = POOL_HALO - w + 1
        wsum = s[start:start + tm]
        count = jnp.minimum(pos + 1, w).astype(F32)
        pooled = wsum / count - cur[:, sl]
        mixed = jnp.dot(pooled.astype(BF16), mix_ref[g], preferred_element_type=F32)
        outs.append(mixed * scale_ref[:, sl])
    o_ref[...] = jnp.concatenate(outs, axis=1).astype(o_ref.dtype)


def pool_call(u, mix_bf16, scale, seq, tm=512):
    n, c = u.shape
    blocks_per_tile = tm // POOL_HALO
    return pl.pallas_call(
        functools.partial(_pool_kernel, tm=tm, seq=seq),
        grid=(n // tm,),
        in_specs=[pl.BlockSpec((tm, c), lambda m: (m, 0)),
                  pl.BlockSpec((POOL_HALO, c), lambda m: (jnp.maximum(m * blocks_per_tile - 1, 0), 0)),
                  pl.BlockSpec(mix_bf16.shape, lambda m: (0, 0, 0)),
                  pl.BlockSpec((1, c), lambda m: (0, 0))],
        out_specs=pl.BlockSpec((tm, c), lambda m: (m, 0)),
        out_shape=jax.ShapeDtypeStruct((n, c), BF16),
        compiler_params=_cparams(("parallel",), 32),
        name="pool_mixer",
    )(u, u, mix_bf16, scale.reshape(1, c))


N_BIAS_TILES = 4


def _bucket_tiles():
    kk = np.arange(TK)[:, None]
    qq = np.arange(TQ)[None, :]
    tiles = []
    for rel in (0, 1, 2, -1):
        n = rel * TK + qq - kk
        max_exact = REL_BUCKETS // 2
        nf = np.maximum(n, 1).astype(np.float64)
        large = max_exact + (np.log(nf / max_exact) / math.log(REL_MAX_DISTANCE / max_exact)
                             * (REL_BUCKETS - max_exact)).astype(np.int64)
        large = np.minimum(large, REL_BUCKETS - 1)
        bucket = np.where(n < max_exact, n, large)
        tiles.append(np.where(n < 0, -1, bucket))
    return np.stack(tiles).astype(np.int32)


def _bias_kernel(table_ref, bucket_ref, o_ref):
    h = pl.program_id(0)
    bucket = bucket_ref[...]
    acc = jnp.full(bucket.shape, NEG_INF, F32)
    for b in range(REL_BUCKETS):
        acc = jnp.where(bucket == b, table_ref[b * DIFF_HEADS + h] * LOG2E, acc)
    o_ref[...] = acc


def bias_tiles_call(rel_table):
    bucket = jnp.asarray(_bucket_tiles())
    return pl.pallas_call(
        _bias_kernel,
        grid=(DIFF_HEADS,),
        in_specs=[pl.BlockSpec(memory_space=pltpu.SMEM),
                  pl.BlockSpec(bucket.shape, lambda h: (0, 0, 0))],
        out_specs=pl.BlockSpec((None,) + bucket.shape, lambda h: (h, 0, 0, 0)),
        out_shape=jax.ShapeDtypeStruct((DIFF_HEADS,) + bucket.shape, F32),
        compiler_params=_cparams(("parallel",), 32),
        name="rel_bias_tiles",
    )(rel_table.reshape(-1), bucket)


def _attn_kernel(q_ref, k_ref, v_ref, bias_ref, lam_ref, gain_ref, o_ref, vt_ref, s_ref, acc_ref,
                 *, lambda_init, n_super):
    qi = pl.program_id(2)
    tks = KV_SUPER * TK
    heads = range(HEADS_PER_STEP)

    @pl.when(qi == 0)
    def _():
        for hh in heads:
            cols = slice(hh * DIFF_V_DIM, (hh + 1) * DIFF_V_DIM)
            for c in range(n_super):
                vt_ref[hh, c] = v_ref[c * tks:(c + 1) * tks, cols].astype(F32).T.astype(BF16)

    qds = []
    for hh in heads:
        q = q_ref[:, hh * DIFF_V_DIM:(hh + 1) * DIFF_V_DIM]
        lane = lax.broadcasted_iota(jnp.int32, q.shape, 1)
        zero = jnp.zeros_like(q)
        qds.append(jnp.concatenate([jnp.where(lane < DIFF_HEAD_DIM, q, zero),
                                    jnp.where(lane >= DIFF_HEAD_DIM, q, zero)], axis=0))

    def scores(t, hh):
        kb = k_ref[pl.ds(pl.multiple_of(t * tks, tks), tks), hh * DIFF_V_DIM:(hh + 1) * DIFF_V_DIM]
        s = lax.dot_general(kb, qds[hh], (((1,), (1,)), ((), ())), preferred_element_type=F32)
        parts = []
        for u in range(KV_SUPER):
            rel = qi - (t * KV_SUPER + u)
            bias = bias_ref[hh, jnp.where(rel < 0, N_BIAS_TILES - 1, jnp.minimum(rel, 2))]
            parts.append(s[u * TK:(u + 1) * TK] + jnp.concatenate([bias, bias], axis=1))
        s = jnp.concatenate(parts, axis=0)
        s_ref[hh] = s
        return jnp.max(s, axis=0, keepdims=True)

    def softmax_step(t, hh, m_prev, m_cur, l):
        alpha = jnp.exp2(m_prev - m_cur)
        p = jnp.exp2(s_ref[hh] - m_cur)
        l_new = l * alpha + jnp.sum(p, axis=0, keepdims=True)
        pv = jnp.dot(vt_ref[hh, t], p.astype(BF16), preferred_element_type=F32)
        acc_ref[hh] = acc_ref[hh] * alpha + pv
        return l_new

    last = qi // KV_SUPER
    acc_ref[...] = jnp.zeros(acc_ref.shape, F32)
    neg = jnp.full((1, 2 * TQ), NEG_INF, F32)
    init = tuple((neg, jnp.maximum(neg, scores(0, hh)), jnp.zeros((1, 2 * TQ), F32)) for hh in heads)

    def body(t, carry):
        out = []
        for hh in heads:
            m_prev, m_cur, l = carry[hh]
            l = softmax_step(t, hh, m_prev, m_cur, l)
            m_next = jnp.maximum(m_cur, scores(t + 1, hh))
            out.append((m_cur, m_next, l))
        return tuple(out)

    carry = lax.fori_loop(0, last, body, init)

    lam_p = lam_ref[...]
    lam = (jnp.exp(jnp.sum(lam_p[0:1] * lam_p[1:2], axis=1, keepdims=True))
           - jnp.exp(jnp.sum(lam_p[2:3] * lam_p[3:4], axis=1, keepdims=True)) + lambda_init)
    for hh in heads:
        m_prev, m_cur, l = carry[hh]
        l = softmax_step(last, hh, m_prev, m_cur, l)
        acc = acc_ref[hh]
        o = acc[:, :TQ] / l[:, :TQ] - lam * (acc[:, TQ:] / l[:, TQ:])
        ms = jnp.mean(o * o, axis=0, keepdims=True)
        y = o * lax.rsqrt(ms + EPS) * gain_ref[...] * (1.0 - lambda_init)
        o_ref[:, hh * DIFF_V_DIM:(hh + 1) * DIFF_V_DIM] = y.T.astype(o_ref.dtype)


def attn_call(qkv, bias_tiles, lam_params, subln_gain, lambda_init, batch, seq):
    tks = KV_SUPER * TK
    n_super = seq // tks
    hps = HEADS_PER_STEP
    width = hps * DIFF_V_DIM
    groups = DIFF_HEADS // hps
    return pl.pallas_call(
        functools.partial(_attn_kernel, lambda_init=lambda_init, n_super=n_super),
        grid=(batch, groups, seq // TQ),
        in_specs=[pl.BlockSpec((None, TQ, width), lambda b, g, i: (b, i, g)),
                  pl.BlockSpec((None, seq, width), lambda b, g, i: (b, 0, groups + g)),
                  pl.BlockSpec((None, seq, width), lambda b, g, i: (b, 0, 2 * groups + g)),
                  pl.BlockSpec((hps, N_BIAS_TILES, TK, TQ), lambda b, g, i: (g, 0, 0, 0)),
                  pl.BlockSpec((4, DIFF_HEAD_DIM), lambda b, g, i: (0, 0)),
                  pl.BlockSpec((DIFF_V_DIM, 1), lambda b, g, i: (0, 0))],
        out_specs=pl.BlockSpec((None, TQ, width), lambda b, g, i: (b, i, g)),
        out_shape=jax.ShapeDtypeStruct((batch, seq, ATTN_WIDTH), BF16),
        scratch_shapes=[pltpu.VMEM((hps, n_super, DIFF_V_DIM, tks), BF16),
                        pltpu.VMEM((hps, tks, 2 * TQ), F32),
                        pltpu.VMEM((hps, DIFF_V_DIM, 2 * TQ), F32)],
        compiler_params=_cparams(("parallel", "parallel", "arbitrary"), 48),
        name="diff_attention",
    )(qkv, qkv, qkv, bias_tiles, lam_params, subln_gain.reshape(DIFF_V_DIM, 1))


def _sigmoid(x):
    return 1.0 / (1.0 + jnp.exp(-x))


def _merge_kernel(xn_ref, mixed_ref, attn_ref, wgp_ref, wga_ref, bgp_ref, bga_ref, wup_ref, wua_ref, z_ref,
                  wgp_s, wga_s, wup_s, wua_s):
    @pl.when(pl.program_id(1) == 0)
    def _():
        wgp_s[...] = wgp_ref[...].astype(BF16)
        wga_s[...] = wga_ref[...].astype(BF16)
        wup_s[...] = wup_ref[...].astype(BF16)
        wua_s[...] = wua_ref[...].astype(BF16)

    xn = xn_ref[...]
    g_pool = _sigmoid(jnp.dot(xn, wgp_s[...], preferred_element_type=F32) + bgp_ref[...])
    g_attn = _sigmoid(jnp.dot(xn, wga_s[...], preferred_element_type=F32) + bga_ref[...])
    y_pool = jnp.dot(mixed_ref[...], wup_s[...], preferred_element_type=F32)
    y_attn = jnp.dot(attn_ref[...], wua_s[...], preferred_element_type=F32)
    z_ref[...] = (g_pool * y_pool + g_attn * y_attn).astype(z_ref.dtype)


def merge_call(xn, mixed, attn, w_gate, b_gate, w_up_pool, w_up_attn, layer, tm=512, tn=512):
    n, d = xn.shape
    nb = d // tn
    return pl.pallas_call(
        _merge_kernel,
        grid=(nb, n // tm),
        in_specs=[pl.BlockSpec((tm, d), lambda j, m: (m, 0)),
                  pl.BlockSpec((tm, POOL_WIDTH), lambda j, m: (m, 0)),
                  pl.BlockSpec((tm, ATTN_WIDTH), lambda j, m: (m, 0)),
                  pl.BlockSpec((None, d, tn), lambda j, m: (layer, 0, j)),
                  pl.BlockSpec((None, d, tn), lambda j, m: (layer, 0, nb + j)),
                  pl.BlockSpec((None, 1, tn), lambda j, m: (layer, 0, j)),
                  pl.BlockSpec((None, 1, tn), lambda j, m: (layer, 0, nb + j)),
                  pl.BlockSpec((None, POOL_WIDTH, tn), lambda j, m: (layer, 0, j)),
                  pl.BlockSpec((None, ATTN_WIDTH, tn), lambda j, m: (layer, 0, j))],
        out_specs=pl.BlockSpec((tm, tn), lambda j, m: (m, j)),
        out_shape=jax.ShapeDtypeStruct((n, d), BF16),
        scratch_shapes=[pltpu.VMEM((d, tn), BF16), pltpu.VMEM((d, tn), BF16),
                        pltpu.VMEM((POOL_WIDTH, tn), BF16), pltpu.VMEM((ATTN_WIDTH, tn), BF16)],
        compiler_params=_cparams(("arbitrary", "arbitrary"), 56),
        name="gated_merge",
    )(xn, mixed, attn, w_gate, w_gate, b_gate, b_gate, w_up_pool, w_up_attn)


HIGH_HALF = 0xFFFF0000


def _pack_halves(x):
    c = x.shape[1] // 2
    lo = lax.bitcast_convert_type(x[:, :c].astype(BF16).astype(F32), jnp.uint32)
    hi = lax.bitcast_convert_type(x[:, c:].astype(BF16).astype(F32), jnp.uint32)
    return (lo >> 16) | hi


def _unpack_halves(p):
    lo = lax.bitcast_convert_type(p << 16, F32)
    hi = lax.bitcast_convert_type(p & jnp.uint32(HIGH_HALF), F32)
    return lo, hi


ROW_TILE = 8


def _store_row_tiles(ref, packed):
    m = packed.shape[0]
    for s in range(ROW_TILE):
        ref[pl.ds(s, m, stride=ROW_TILE), :] = packed[:, s * LANES:(s + 1) * LANES]


def _load_row_tiles(ref):
    m = ref.shape[0] // ROW_TILE
    return jnp.concatenate([ref[pl.ds(s, m, stride=ROW_TILE), :] for s in range(ROW_TILE)], axis=1)


def _rows(ref, first, count=1):
    return ref.at[pl.ds(pl.multiple_of(first * ROW_TILE, ROW_TILE), count * ROW_TILE)]


def _outproj_kernel(z_ref, w_ref, h_ref, gain_ref, wr_ref, br_ref, h1_ref, hn_ref, logit_ref, w_s):
    @pl.when(pl.program_id(0) == 0)
    def _():
        w_s[...] = w_ref[...].astype(BF16)

    h1 = h_ref[...] + jnp.dot(z_ref[...], w_s[...], preferred_element_type=F32)
    h1_ref[...] = h1
    hn = _rms(h1, gain_ref[...])
    _store_row_tiles(hn_ref, _pack_halves(hn))
    logit_ref[...] = jnp.dot(hn.astype(BF16), wr_ref[...], preferred_element_type=F32) + br_ref[...]


def outproj_call(z, w_out, layer, h, gain, w_router, b_router, tm=256):
    n, d = h.shape
    return pl.pallas_call(
        _outproj_kernel,
        grid=(n // tm,),
        in_specs=[pl.BlockSpec((tm, d), lambda m: (m, 0)),
                  pl.BlockSpec((None, d, d), lambda m: (layer, 0, 0), pipeline_mode=pl.Buffered(1)),
                  pl.BlockSpec((tm, d), lambda m: (m, 0)),
                  pl.BlockSpec((1, d), lambda m: (0, 0)),
                  pl.BlockSpec((d, ROUTER_LANES), lambda m: (0, 0)),
                  pl.BlockSpec((1, ROUTER_LANES), lambda m: (0, 0))],
        out_specs=[pl.BlockSpec((tm, d), lambda m: (m, 0)),
                   pl.BlockSpec((tm * ROW_TILE, LANES), lambda m: (m, 0)),
                   pl.BlockSpec((tm, ROUTER_LANES), lambda m: (m, 0))],
        out_shape=[jax.ShapeDtypeStruct((n, d), F32),
                   jax.ShapeDtypeStruct((n * ROW_TILE, LANES), jnp.uint32),
                   jax.ShapeDtypeStruct((n, ROUTER_LANES), F32)],
        scratch_shapes=[pltpu.VMEM((d, d), BF16)],
        compiler_params=_cparams(("arbitrary",), 56),
        name="out_proj_norm_router",
    )(z, w_out, h, gain.reshape(1, d), w_router, b_router)


def _route(x, lane):
    big = float(ROUTER_LANES)

    def first_argmax(vals):
        top = jnp.max(vals, axis=1, keepdims=True)
        idx = jnp.min(jnp.where(vals == top, lane, big), axis=1, keepdims=True)
        return top, idx

    gmask = lane < N_GROUPS
    g_top, g_sel = first_argmax(jnp.where(gmask, x, -jnp.inf))
    g_weight = 1.0 / jnp.sum(jnp.where(gmask, jnp.exp(x - g_top), 0.0), axis=1, keepdims=True)
    lo = N_GROUPS + EXPERTS_PER_GROUP * g_sel
    e_vals = jnp.where((lane >= lo) & (lane < lo + EXPERTS_PER_GROUP), x, -jnp.inf)
    v1, i1 = first_argmax(e_vals)
    v2, i2 = first_argmax(jnp.where(lane == i1, -jnp.inf, e_vals))
    t = jnp.exp(v2 - v1)
    return i1 - N_GROUPS, i2 - N_GROUPS, g_weight / (1.0 + t), g_weight * t / (1.0 + t)


PLAN_BLOCK = 1024
PLAN_CHUNK = 256
META_TILE_EXPERT, META_NEXT_EXPERT, META_PAD_START, META_N_VALID = 0, 1, 2, 3


def _route_plan_kernel(logit_ref, w_ref, pos_ref, meta_ref, tri_s, cnt_s, base_s, off_s):
    phase = pl.program_id(0)
    blk = pl.program_id(1)
    tb = logit_ref.shape[0]
    lane = lax.broadcasted_iota(jnp.int32, (tb, ROUTER_LANES), 1).astype(F32)
    e0, e1, w0, w1 = _route(logit_ref[...], lane)
    onehot = jnp.where(lane == e0, 1.0, 0.0) + jnp.where(lane == e1, 1.0, 0.0)
    block_counts = jnp.sum(onehot, axis=0, keepdims=True)

    @pl.when(jnp.logical_and(phase == 0, blk == 0))
    def _():
        cnt_s[...] = jnp.zeros(cnt_s.shape, F32)

    @pl.when(phase == 0)
    def _():
        cnt_s[...] = cnt_s[...] + block_counts

    @pl.when(jnp.logical_and(phase == 1, blk == 0))
    def _():
        r = lax.broadcasted_iota(jnp.int32, (tb, tb), 0)
        c = lax.broadcasted_iota(jnp.int32, (tb, tb), 1)
        tri_s[...] = jnp.where(c < r, 1.0, 0.0).astype(BF16)
        lane1 = lane[0:1]
        cnt = cnt_s[...]
        tiles = jnp.floor((cnt + (TM_EXPERT - 1)) * (1.0 / TM_EXPERT))
        ri = lax.broadcasted_iota(jnp.int32, (ROUTER_LANES, ROUTER_LANES), 0)
        ci = lax.broadcasted_iota(jnp.int32, (ROUTER_LANES, ROUTER_LANES), 1)
        upper = jnp.where(ri <= ci, 1.0, 0.0).astype(BF16)
        ends = jnp.dot(jnp.broadcast_to(tiles, (8, ROUTER_LANES)).astype(BF16), upper,
                       preferred_element_type=F32)[0:1]
        off_s[...] = (ends - tiles) * TM_EXPERT
        base_s[...] = jnp.zeros(base_s.shape, F32)

        def pick(vec, e):
            return jnp.sum(jnp.where(lane1 == e, vec, 0.0), axis=1, keepdims=True)

        end_of = [pick(ends, e) for e in range(N_EXPERTS)]
        n_valid = end_of[N_EXPERTS - 1]

        def segment_of(tile):
            return sum(jnp.where(tile >= end_e, 1.0, 0.0) for end_e in end_of)

        tile_expert = jnp.where(lane1 < n_valid, segment_of(lane1), segment_of(n_valid - 1.0))
        following = sum(jnp.where(tile_expert == e, end_of[e], 0.0) for e in range(N_EXPERTS))
        next_expert = jnp.where(following < n_valid, segment_of(following), -1.0)
        rows = [tile_expert, next_expert, off_s[...] + cnt, jnp.broadcast_to(n_valid, (1, ROUTER_LANES))]
        rows += [jnp.zeros((1, ROUTER_LANES), F32)] * (meta_ref.shape[0] - len(rows))
        meta_ref[...] = jnp.concatenate(rows, axis=0).astype(jnp.int32)

    @pl.when(phase == 1)
    def _():
        before = jnp.dot(tri_s[...], onehot.astype(BF16), preferred_element_type=F32)
        row = before + base_s[...] + off_s[...]
        base_s[...] = base_s[...] + block_counts
        w_ref[...] = jnp.where(lane == 0, w0, jnp.where(lane == 1, w1, 0.0))
        eye = (lax.broadcasted_iota(jnp.int32, (LANES, LANES), 0)
               == lax.broadcasted_iota(jnp.int32, (LANES, LANES), 1))
        for j, e in enumerate((e0, e1)):
            col = jnp.sum(jnp.where(lane == e, row, 0.0), axis=1, keepdims=True)
            for q in range(tb // PLAN_CHUNK):
                parts = []
                for g in range(PLAN_CHUNK // LANES):
                    t0 = q * PLAN_CHUNK + g * LANES
                    square = jnp.broadcast_to(col[t0:t0 + LANES], (LANES, LANES))
                    parts.append(jnp.sum(jnp.where(eye, square, 0.0), axis=0, keepdims=True))
                pos_ref[j, q] = jnp.concatenate(parts, axis=1).astype(jnp.int32)


def route_plan_call(logits, n_tiles):
    n = logits.shape[0]
    tb = PLAN_BLOCK
    chunks = tb // PLAN_CHUNK
    assert n_tiles <= ROUTER_LANES and N_EXPERTS <= ROUTER_LANES
    return pl.pallas_call(
        _route_plan_kernel,
        grid=(2, n // tb),
        in_specs=[pl.BlockSpec((tb, ROUTER_LANES), lambda p, b: (b, 0))],
        out_specs=[pl.BlockSpec((tb, ROUTER_LANES), lambda p, b: (p * b, 0)),
                   pl.BlockSpec((2, chunks, 1, PLAN_CHUNK), lambda p, b: (0, p * b, 0, 0)),
                   pl.BlockSpec((8, ROUTER_LANES), lambda p, b: (0, 0))],
        out_shape=[jax.ShapeDtypeStruct((n, ROUTER_LANES), F32),
                   jax.ShapeDtypeStruct((2, n // PLAN_CHUNK, 1, PLAN_CHUNK), jnp.int32),
                   jax.ShapeDtypeStruct((8, ROUTER_LANES), jnp.int32)],
        scratch_shapes=[pltpu.VMEM((tb, tb), BF16), pltpu.VMEM((1, ROUTER_LANES), F32),
                        pltpu.VMEM((1, ROUTER_LANES), F32), pltpu.VMEM((1, ROUTER_LANES), F32)],
        compiler_params=_cparams(("arbitrary", "arbitrary"), 32),
        name="route_plan",
    )(logits)


def _dispatch_kernel(pad_ref, nv_ref, pos0_ref, pos1_ref, src_ref, dst_ref, zero_buf, sem, *, chunk, n_tiles):
    @pl.when(pl.program_id(0) == 0)
    def _():
        zero_buf[...] = jnp.zeros(zero_buf.shape, zero_buf.dtype)

        def fill(e):
            return pltpu.make_async_copy(zero_buf, _rows(dst_ref, pad_ref[e], TM_EXPERT), sem)

        for e in range(N_EXPERTS):
            fill(e).start()
        for e in range(N_EXPERTS):
            fill(e).wait()

        def fill_tile(i):
            return pltpu.make_async_copy(zero_buf, _rows(dst_ref, i * TM_EXPERT, TM_EXPERT), sem)

        def start_tile(i, c):
            fill_tile(i).start()
            return c

        def wait_tile(i, c):
            fill_tile(i).wait()
            return c

        lax.fori_loop(nv_ref[0], n_tiles, start_tile, 0)
        lax.fori_loop(nv_ref[0], n_tiles, wait_tile, 0)

    def start(t, c):
        for j, pos_ref in enumerate((pos0_ref, pos1_ref)):
            pltpu.make_async_copy(_rows(src_ref, t), _rows(dst_ref, pos_ref[0, 0, t]), sem).start(priority=j)
        return c

    lax.fori_loop(0, chunk, start, 0, unroll=ROW_DMA_UNROLL)
    all_rows = pltpu.make_async_copy(src_ref, _rows(dst_ref, 0, chunk), sem)
    for j in range(2):
        all_rows.wait()


def dispatch_call(hn, pos, pad_start, n_valid_tiles, n_tiles):
    n = hn.shape[0] // ROW_TILE
    chunk = PLAN_CHUNK
    grid_spec = pltpu.PrefetchScalarGridSpec(
        num_scalar_prefetch=2,
        grid=(n // chunk,),
        in_specs=[pl.BlockSpec((None, 1, 1, chunk), lambda i, pad, nv: (0, i, 0, 0), memory_space=pltpu.SMEM),
                  pl.BlockSpec((None, 1, 1, chunk), lambda i, pad, nv: (1, i, 0, 0), memory_space=pltpu.SMEM),
                  pl.BlockSpec((chunk * ROW_TILE, LANES), lambda i, pad, nv: (i, 0))],
        out_specs=pl.BlockSpec(memory_space=pl.ANY),
        scratch_shapes=[pltpu.VMEM((TM_EXPERT * ROW_TILE, LANES), hn.dtype), pltpu.SemaphoreType.DMA(())],
    )
    return pl.pallas_call(
        functools.partial(_dispatch_kernel, chunk=chunk, n_tiles=n_tiles),
        grid_spec=grid_spec,
        out_shape=jax.ShapeDtypeStruct((n_tiles * TM_EXPERT * ROW_TILE, LANES), hn.dtype),
        compiler_params=_cparams(("arbitrary",), 32),
        name="dispatch_rows",
    )(pad_start, n_valid_tiles, pos, pos, hn)


def _expert_kernel(te_ref, nv_ref, nx_ref, x_ref, wg_hbm, wu_hbm, wd_hbm, y_ref,
                   wg_f, wu_f, wd_f, wg_s, wu_s, wd_s, seg_ref, sems, *, layer):
    i = pl.program_id(0)
    valid = i < nv_ref[0]
    expert = te_ref[i]
    changed = jnp.logical_or(i == 0, expert != te_ref[jnp.maximum(i - 1, 0)])

    def fetch(e, slot):
        return [pltpu.make_async_copy(hbm.at[layer, e], buf.at[slot], sems.at[slot])
                for hbm, buf in ((wg_hbm, wg_f), (wu_hbm, wu_f), (wd_hbm, wd_f))]

    @pl.when(i == 0)
    def _():
        seg_ref[0] = 0
        for cp in fetch(expert, 0):
            cp.start()

    @pl.when(jnp.logical_and(valid, changed))
    def _():
        slot = seg_ref[0] % 2
        for cp in fetch(expert, slot):
            cp.wait()
        nxt = nx_ref[i]

        @pl.when(nxt >= 0)
        def _():
            for cp in fetch(nxt, 1 - slot):
                cp.start()

        wg_s[...] = wg_f[slot].astype(BF16)
        wu_s[...] = wu_f[slot].astype(BF16)
        wd_s[...] = wd_f[slot].astype(BF16)
        seg_ref[0] = seg_ref[0] + 1

    @pl.when(valid)
    def _():
        lo, hi = _unpack_halves(_load_row_tiles(x_ref))
        x = jnp.concatenate([lo.astype(BF16), hi.astype(BF16)], axis=1)
        a = jnp.dot(x, wg_s[...], preferred_element_type=F32)
        b = jnp.dot(x, wu_s[...], preferred_element_type=F32)
        hmid = (a * _sigmoid(a) * b).astype(BF16)
        _store_row_tiles(y_ref, _pack_halves(jnp.dot(hmid, wd_s[...], preferred_element_type=F32)))

    @pl.when(jnp.logical_not(valid))
    def _():
        y_ref[...] = jnp.zeros(y_ref.shape, y_ref.dtype)


def expert_call(xs, tile_expert, n_valid_tiles, next_expert, w_gate, w_up, w_down, layer):
    p = xs.shape[0] // ROW_TILE
    d = 2 * ROW_TILE * LANES
    tm = TM_EXPERT
    f = w_gate.shape[-1]
    grid_spec = pltpu.PrefetchScalarGridSpec(
        num_scalar_prefetch=3,
        grid=(p // tm,),
        in_specs=[pl.BlockSpec((tm * ROW_TILE, LANES), lambda i, te, nv, nx: (jnp.minimum(i, nv[0] - 1), 0)),
                  pl.BlockSpec(memory_space=pl.ANY),
                  pl.BlockSpec(memory_space=pl.ANY),
                  pl.BlockSpec(memory_space=pl.ANY)],
        out_specs=pl.BlockSpec((tm * ROW_TILE, LANES), lambda i, te, nv, nx: (i, 0)),
        scratch_shapes=[pltpu.VMEM((2, d, f), F32), pltpu.VMEM((2, d, f), F32), pltpu.VMEM((2, f, d), F32),
                        pltpu.VMEM((d, f), BF16), pltpu.VMEM((d, f), BF16), pltpu.VMEM((f, d), BF16),
                        pltpu.SMEM((1,), jnp.int32), pltpu.SemaphoreType.DMA((2,))],
    )
    return pl.pallas_call(
        functools.partial(_expert_kernel, layer=layer),
        grid_spec=grid_spec,
        out_shape=jax.ShapeDtypeStruct(xs.shape, jnp.uint32),
        compiler_params=_cparams(("arbitrary",), 58),
        name="expert_swiglu",
    )(tile_expert, n_valid_tiles, next_expert, xs, w_gate, w_up, w_down)


def _combine_kernel(pos0_ref, pos1_ref, ys_ref, h_ref, w_ref, gain_ref, h2_ref, xn_ref, buf, sem, *, chunk):
    def start(t, c):
        for j, pos_ref in enumerate((pos0_ref, pos1_ref)):
            pltpu.make_async_copy(_rows(ys_ref, pos_ref[0, 0, t]), _rows(buf.at[j], t), sem).start(priority=j)
        return c

    lax.fori_loop(0, chunk, start, 0, unroll=ROW_DMA_UNROLL)
    for j in range(2):
        pltpu.make_async_copy(_rows(ys_ref, 0, chunk), buf.at[j], sem).wait()
    w = w_ref[...]
    lo0, hi0 = _unpack_halves(_load_row_tiles(buf.at[0]))
    lo1, hi1 = _unpack_halves(_load_row_tiles(buf.at[1]))
    y = jnp.concatenate([w[:, 0:1] * lo0 + w[:, 1:2] * lo1, w[:, 0:1] * hi0 + w[:, 1:2] * hi1], axis=1)
    h2 = h_ref[...] + y
    h2_ref[...] = h2
    xn_ref[...] = _rms(h2, gain_ref[...]).astype(xn_ref.dtype)


def combine_call(ys, pos, h1, weights, next_gain, xn_dtype, chunk=128):
    n, d = h1.shape
    per_row = PLAN_CHUNK // chunk
    return pl.pallas_call(
        functools.partial(_combine_kernel, chunk=chunk),
        grid=(n // chunk,),
        in_specs=[pl.BlockSpec((None, 1, 1, chunk), lambda c: (0, c // per_row, 0, c % per_row),
                               memory_space=pltpu.SMEM),
                  pl.BlockSpec((None, 1, 1, chunk), lambda c: (1, c // per_row, 0, c % per_row),
                               memory_space=pltpu.SMEM),
                  pl.BlockSpec(memory_space=pl.ANY),
                  pl.BlockSpec((chunk, d), lambda c: (c, 0)),
                  pl.BlockSpec((chunk, ROUTER_LANES), lambda c: (c, 0)),
                  pl.BlockSpec((1, d), lambda c: (0, 0))],
        out_specs=[pl.BlockSpec((chunk, d), lambda c: (c, 0)),
                   pl.BlockSpec((chunk, d), lambda c: (c, 0))],
        out_shape=[jax.ShapeDtypeStruct((n, d), F32),
                   jax.ShapeDtypeStruct((n, d), xn_dtype)],
        scratch_shapes=[pltpu.VMEM((2, chunk * ROW_TILE, LANES), jnp.uint32), pltpu.SemaphoreType.DMA(())],
        compiler_params=_cparams(("arbitrary",), 32),
        name="combine_rows",
    )(pos, pos, ys, h1, weights, next_gain.reshape(1, d))


def kernel(x, rel_bias_table, norm_mix_gain, w_in, w_merge_gate, b_merge_gate, pool_mix, pool_scale, w_up_pool, lambda_q1, lambda_k1, lambda_q2, lambda_k2, subln_gain, w_up_attn, w_out, norm_ffn_gain, w_router_group, b_router_group, w_router_expert, b_router_expert, w_expert_gate, w_expert_up, w_expert_down, final_norm_gain):
    batch, seq, d = x.shape
    depth = w_in.shape[0]
    n = batch * seq
    n_tiles = (2 * n + N_EXPERTS * (TM_EXPERT - 1)) // TM_EXPERT + 1

    bias_tiles = bias_tiles_call(rel_bias_table)
    h = x.reshape(n, d)
    xn = rms_norm_call(h, norm_mix_gain[0], BF16)
    q_scale = LOG2E * DIFF_HEAD_DIM ** -0.5

    for l in range(depth):
        u = proj_call(xn, w_in, l, 0, POOL_WIDTH // 512, 512, F32, name="in_proj_pool")
        qkv = proj_call(xn, w_in, l, 1, 3, ATTN_WIDTH, BF16, first_block_scale=q_scale, name="in_proj_qkv")
        mixed = pool_call(u, pool_mix[l].astype(BF16), pool_scale[l], seq)
        lambda_init = 0.8 - 0.6 * math.exp(-0.3 * l)
        lam_params = jnp.stack([lambda_q1[l], lambda_k1[l], lambda_q2[l], lambda_k2[l]])
        attn = attn_call(qkv.reshape(batch, seq, 3 * ATTN_WIDTH), bias_tiles, lam_params, subln_gain[l],
                         lambda_init, batch, seq).reshape(n, ATTN_WIDTH)
        z = merge_call(xn, mixed, attn, w_merge_gate, b_merge_gate.reshape(depth, 1, -1), w_up_pool, w_up_attn, l)

        w_router = jnp.concatenate(
            [w_router_group[l], jnp.transpose(w_router_expert[l], (1, 0, 2)).reshape(d, N_EXPERTS),
             jnp.zeros((d, ROUTER_LANES - N_GROUPS - N_EXPERTS), F32)], axis=1).astype(BF16)
        b_router = jnp.concatenate(
            [b_router_group[l], b_router_expert[l].reshape(-1),
             jnp.zeros((ROUTER_LANES - N_GROUPS - N_EXPERTS,), F32)]).reshape(1, ROUTER_LANES)
        h1, hn, logits = outproj_call(z, w_out, l, h, norm_ffn_gain[l], w_router, b_router)

        weights, pos, meta = route_plan_call(logits, n_tiles)
        tile_expert = meta[META_TILE_EXPERT, :n_tiles]
        next_expert = meta[META_NEXT_EXPERT, :n_tiles]
        pad_start = meta[META_PAD_START, :N_EXPERTS]
        n_valid = meta[META_N_VALID, :1]
        xs = dispatch_call(hn, pos, pad_start, n_valid, n_tiles)
        ys = expert_call(xs, tile_expert, n_valid, next_expert, w_expert_gate, w_expert_up, w_expert_down, l)
        last = l == depth - 1
        next_gain = final_norm_gain if last else norm_mix_gain[l + 1]
        h, xn = combine_call(ys, pos, h1, weights, next_gain, F32 if last else BF16)

    return xn.reshape(batch, seq, d)
```

```python
import functools
import math

import numpy as np
import jax
import jax.numpy as jnp
from jax import lax
from jax.experimental import pallas as pl
from jax.experimental.pallas import tpu as pltpu

F32 = jnp.float32
BF16 = jnp.bfloat16

D_MODEL = 2048
POOL_WIDTH = 1024
POOL_WINDOWS = (2, 4, 8, 16)
POOL_GROUP_DIM = 256
POOL_HALO = 16
DIFF_HEADS = 8
DIFF_HEAD_DIM = 64
DIFF_V_DIM = 128
ATTN_WIDTH = 1024
REL_BUCKETS = 32
REL_MAX_DISTANCE = 128
N_GROUPS = 4
EXPERTS_PER_GROUP = 8
N_EXPERTS = 32
D_EXPERT = 512
EPS = 1e-6
NEG_INF = -1e30
LOG2E = 1.4426950408889634

ROUTER_LANES = 128
TQ = 256
TK = 256
KV_SUPER = 2
V_ROWS = DIFF_V_DIM + 16
HEADS_PER_STEP = 4
TM_EXPERT = 256
LANES = 128
ROW_DMA_UNROLL = 8
MIB = 1024 * 1024


def _cparams(sem, vmem_mib):
    return pltpu.CompilerParams(dimension_semantics=sem, vmem_limit_bytes=vmem_mib * MIB)


def _rms(xf, gain):
    ms = jnp.mean(xf * xf, axis=-1, keepdims=True)
    return xf * lax.rsqrt(ms + EPS) * gain


def _norm_kernel(h_ref, g_ref, o_ref):
    o_ref[...] = _rms(h_ref[...], g_ref[...]).astype(o_ref.dtype)


def rms_norm_call(h, gain, out_dtype, tm=512):
    n, d = h.shape
    return pl.pallas_call(
        _norm_kernel,
        grid=(n // tm,),
        in_specs=[pl.BlockSpec((tm, d), lambda m: (m, 0)),
                  pl.BlockSpec((1, d), lambda m: (0, 0))],
        out_specs=pl.BlockSpec((tm, d), lambda m: (m, 0)),
        out_shape=jax.ShapeDtypeStruct((n, d), out_dtype),
        compiler_params=_cparams(("parallel",), 32),
        name="rms_norm",
    )(h, gain.reshape(1, d))


def _proj_kernel(x_ref, w_ref, o_ref, w_s, *, first_block_scale):
    @pl.when(pl.program_id(1) == 0)
    def _():
        w_s[...] = w_ref[...].astype(BF16)

    acc = jnp.dot(x_ref[...], w_s[...], preferred_element_type=F32)
    if first_block_scale is not None:
        acc = acc * jnp.where(pl.program_id(0) == 0, first_block_scale, 1.0).astype(F32)
    o_ref[...] = acc.astype(o_ref.dtype)


def proj_call(x, w, layer, col_block0, n_col_blocks, tn, out_dtype, first_block_scale=None, tm=512, name="proj"):
    n, k = x.shape
    return pl.pallas_call(
        functools.partial(_proj_kernel, first_block_scale=first_block_scale),
        grid=(n_col_blocks, n // tm),
        in_specs=[pl.BlockSpec((tm, k), lambda j, m: (m, 0)),
                  pl.BlockSpec((None, k, tn), lambda j, m: (layer, 0, col_block0 + j))],
        out_specs=pl.BlockSpec((tm, tn), lambda j, m: (m, j)),
        out_shape=jax.ShapeDtypeStruct((n, n_col_blocks * tn), out_dtype),
        scratch_shapes=[pltpu.VMEM((k, tn), BF16)],
        compiler_params=_cparams(("arbitrary", "arbitrary"), 48),
        name=name,
    )(x, w)


def _pool_kernel(cur_ref, prev_ref, mix_ref, scale_ref, o_ref, *, tm, seq):
    m = pl.program_id(0)
    row0 = (m * tm) % seq
    cur = cur_ref[...]
    prev = jnp.where(row0 == 0, 0.0, prev_ref[...])
    pos = row0 + lax.broadcasted_iota(jnp.int32, (tm, 1), 0)
    outs = []
    for g, w in enumerate(POOL_WINDOWS):
        sl = slice(g * POOL_GROUP_DIM, (g + 1) * POOL_GROUP_DIM)
        x = jnp.concatenate([prev[:, sl], cur[:, sl]], axis=0)
        s, d = x, 1
        while d < w:
            s = s[:-d] + s[d:]
            d *= 2
        start = POOL_HALO - w + 1
        wsum = s[start:start + tm]
        count = jnp.minimum(pos + 1, w).astype(F32)
        pooled = wsum / count - cur[:, sl]
        mixed = jnp.dot(pooled.astype(BF16), mix_ref[g], preferred_element_type=F32)
        outs.append(mixed * scale_ref[:, sl])
    o_ref[...] = jnp.concatenate(outs, axis=1).astype(o_ref.dtype)


def pool_call(u, mix_bf16, scale, seq, tm=512):
    n, c = u.shape
    blocks_per_tile = tm // POOL_HALO
    return pl.pallas_call(
        functools.partial(_pool_kernel, tm=tm, seq=seq),
        grid=(n // tm,),
        in_specs=[pl.BlockSpec((tm, c), lambda m: (m, 0)),
                  pl.BlockSpec((POOL_HALO, c), lambda m: (jnp.maximum(m * blocks_per_tile - 1, 0), 0)),
                  pl.BlockSpec(mix_bf16.shape, lambda m: (0, 0, 0)),
                  pl.BlockSpec((1, c), lambda m: (0, 0))],
        out_specs=pl.BlockSpec((tm, c), lambda m: (m, 0)),
        out_shape=jax.ShapeDtypeStruct((n, c), BF16),
        compiler_params=_cparams(("parallel",), 32),
        name="pool_mixer",
    )(u, u, mix_bf16, scale.reshape(1, c))


N_BIAS_TILES = 4


def _bucket_tiles():
    kk = np.arange(TK)[:, None]
    qq = np.arange(TQ)[None, :]
    tiles = []
    for rel in (0, 1, 2, -1):
        n = rel * TK + qq - kk
        max_exact = REL_BUCKETS // 2
        nf = np.maximum(n, 1).astype(np.float64)
        large = max_exact + (np.log(nf / max_exact) / math.log(REL_MAX_DISTANCE / max_exact)
                             * (REL_BUCKETS - max_exact)).astype(np.int64)
        large = np.minimum(large, REL_BUCKETS - 1)
        bucket = np.where(n < max_exact, n, large)
        tiles.append(np.where(n < 0, -1, bucket))
    return np.stack(tiles).astype(np.int32)


def _bias_kernel(table_ref, bucket_ref, o_ref):
    h = pl.program_id(0)
    bucket = bucket_ref[...]
    acc = jnp.full(bucket.shape, NEG_INF, F32)
    for b in range(REL_BUCKETS):
        acc = jnp.where(bucket == b, table_ref[b * DIFF_HEADS + h] * LOG2E, acc)
    o_ref[...] = acc


def bias_tiles_call(rel_table):
    bucket = jnp.asarray(_bucket_tiles())
    return pl.pallas_call(
        _bias_kernel,
        grid=(DIFF_HEADS,),
        in_specs=[pl.BlockSpec(memory_space=pltpu.SMEM),
                  pl.BlockSpec(bucket.shape, lambda h: (0, 0, 0))],
        out_specs=pl.BlockSpec((None,) + bucket.shape, lambda h: (h, 0, 0, 0)),
        out_shape=jax.ShapeDtypeStruct((DIFF_HEADS,) + bucket.shape, F32),
        compiler_params=_cparams(("parallel",), 32),
        name="rel_bias_tiles",
    )(rel_table.reshape(-1), bucket)


def _attn_kernel(q_ref, k_ref, v_ref, bias_ref, lam_ref, gain_ref, o_ref, vt_ref, s_ref, acc_ref,
                 *, lambda_init, n_super):
    qi = pl.program_id(2)
    tks = KV_SUPER * TK
    heads = range(HEADS_PER_STEP)

    @pl.when(qi == 0)
    def _():
        extra = (lax.broadcasted_iota(jnp.int32, (V_ROWS - DIFF_V_DIM, tks), 0) == 0).astype(BF16)
        for hh in heads:
            cols = slice(hh * DIFF_V_DIM, (hh + 1) * DIFF_V_DIM)
            for c in range(n_super):
                vt = v_ref[c * tks:(c + 1) * tks, cols].astype(F32).T.astype(BF16)
                vt_ref[hh, c] = jnp.concatenate([vt, extra], axis=0)

    qds = []
    for hh in heads:
        q = q_ref[:, hh * DIFF_V_DIM:(hh + 1) * DIFF_V_DIM]
        lane = lax.broadcasted_iota(jnp.int32, q.shape, 1)
        zero = jnp.zeros_like(q)
        qds.append(jnp.concatenate([jnp.where(lane < DIFF_HEAD_DIM, q, zero),
                                    jnp.where(lane >= DIFF_HEAD_DIM, q, zero)], axis=0))

    def scores(t, hh):
        kb = k_ref[pl.ds(pl.multiple_of(t * tks, tks), tks), hh * DIFF_V_DIM:(hh + 1) * DIFF_V_DIM]
        s = lax.dot_general(kb, qds[hh], (((1,), (1,)), ((), ())), preferred_element_type=F32)
        parts = []
        for u in range(KV_SUPER):
            rel = qi - (t * KV_SUPER + u)
            bias = bias_ref[hh, jnp.where(rel < 0, N_BIAS_TILES - 1, jnp.minimum(rel, 2))]
            parts.append(s[u * TK:(u + 1) * TK] + jnp.concatenate([bias, bias], axis=1))
        s = jnp.concatenate(parts, axis=0)
        s_ref[hh] = s
        return jnp.max(s, axis=0, keepdims=True)

    def softmax_step(t, hh, m_prev, m_cur):
        alpha = jnp.exp2(m_prev - m_cur)
        p = jnp.exp2(s_ref[hh] - m_cur)
        pv = jnp.dot(vt_ref[hh, t], p.astype(BF16), preferred_element_type=F32)
        acc_ref[hh] = acc_ref[hh] * alpha + pv

    last = qi // KV_SUPER
    acc_ref[...] = jnp.zeros(acc_ref.shape, F32)
    neg = jnp.full((1, 2 * TQ), NEG_INF, F32)
    init = tuple((neg, jnp.maximum(neg, scores(0, hh))) for hh in heads)

    def body(t, carry):
        out = []
        for hh in heads:
            m_prev, m_cur = carry[hh]
            softmax_step(t, hh, m_prev, m_cur)
            m_next = jnp.maximum(m_cur, scores(t + 1, hh))
            out.append((m_cur, m_next))
        return tuple(out)

    carry = lax.fori_loop(0, last, body, init)

    lam_p = lam_ref[...]
    lam = (jnp.exp(jnp.sum(lam_p[0:1] * lam_p[1:2], axis=1, keepdims=True))
           - jnp.exp(jnp.sum(lam_p[2:3] * lam_p[3:4], axis=1, keepdims=True)) + lambda_init)
    for hh in heads:
        m_prev, m_cur = carry[hh]
        softmax_step(last, hh, m_prev, m_cur)
        acc = acc_ref[hh, :DIFF_V_DIM, :]
        l = acc_ref[hh, DIFF_V_DIM:DIFF_V_DIM + 1, :]
        o = acc[:, :TQ] / l[:, :TQ] - lam * (acc[:, TQ:] / l[:, TQ:])
        ms = jnp.mean(o * o, axis=0, keepdims=True)
        y = o * lax.rsqrt(ms + EPS) * gain_ref[...] * (1.0 - lambda_init)
        o_ref[:, hh * DIFF_V_DIM:(hh + 1) * DIFF_V_DIM] = y.T.astype(o_ref.dtype)


def attn_call(qkv, bias_tiles, lam_params, subln_gain, lambda_init, batch, seq):
    tks = KV_SUPER * TK
    n_super = seq // tks
    hps = HEADS_PER_STEP
    width = hps * DIFF_V_DIM
    groups = DIFF_HEADS // hps
    return pl.pallas_call(
        functools.partial(_attn_kernel, lambda_init=lambda_init, n_super=n_super),
        grid=(batch, groups, seq // TQ),
        in_specs=[pl.BlockSpec((None, TQ, width), lambda b, g, i: (b, i, g)),
                  pl.BlockSpec((None, seq, width), lambda b, g, i: (b, 0, groups + g)),
                  pl.BlockSpec((None, seq, width), lambda b, g, i: (b, 0, 2 * groups + g)),
                  pl.BlockSpec((hps, N_BIAS_TILES, TK, TQ), lambda b, g, i: (g, 0, 0, 0)),
                  pl.BlockSpec((4, DIFF_HEAD_DIM), lambda b, g, i: (0, 0)),
                  pl.BlockSpec((DIFF_V_DIM, 1), lambda b, g, i: (0, 0))],
        out_specs=pl.BlockSpec((None, TQ, width), lambda b, g, i: (b, i, g)),
        out_shape=jax.ShapeDtypeStruct((batch, seq, ATTN_WIDTH), BF16),
        scratch_shapes=[pltpu.VMEM((hps, n_super, V_ROWS, tks), BF16),
                        pltpu.VMEM((hps, tks, 2 * TQ), F32),
                        pltpu.VMEM((hps, V_ROWS, 2 * TQ), F32)],
        compiler_params=_cparams(("parallel", "parallel", "arbitrary"), 56),
        name="diff_attention",
    )(qkv, qkv, qkv, bias_tiles, lam_params, subln_gain.reshape(DIFF_V_DIM, 1))


def _sigmoid(x):
    return 1.0 / (1.0 + jnp.exp(-x))


def _merge_kernel(xn_ref, mixed_ref, attn_ref, wgp_ref, wga_ref, bgp_ref, bga_ref, wup_ref, wua_ref, z_ref,
                  wgp_s, wga_s, wup_s, wua_s):
    @pl.when(pl.program_id(1) == 0)
    def _():
        wgp_s[...] = wgp_ref[...].astype(BF16)
        wga_s[...] = wga_ref[...].astype(BF16)
        wup_s[...] = wup_ref[...].astype(BF16)
        wua_s[...] = wua_ref[...].astype(BF16)

    xn = xn_ref[...]
    g_pool = _sigmoid(jnp.dot(xn, wgp_s[...], preferred_element_type=F32) + bgp_ref[...])
    g_attn = _sigmoid(jnp.dot(xn, wga_s[...], preferred_element_type=F32) + bga_ref[...])
    y_pool = jnp.dot(mixed_ref[...], wup_s[...], preferred_element_type=F32)
    y_attn = jnp.dot(attn_ref[...], wua_s[...], preferred_element_type=F32)
    z_ref[...] = (g_pool * y_pool + g_attn * y_attn).astype(z_ref.dtype)


def merge_call(xn, mixed, attn, w_gate, b_gate, w_up_pool, w_up_attn, layer, tm=512, tn=512):
    n, d = xn.shape
    nb = d // tn
    return pl.pallas_call(
        _merge_kernel,
        grid=(nb, n // tm),
        in_specs=[pl.BlockSpec((tm, d), lambda j, m: (m, 0)),
                  pl.BlockSpec((tm, POOL_WIDTH), lambda j, m: (m, 0)),
                  pl.BlockSpec((tm, ATTN_WIDTH), lambda j, m: (m, 0)),
                  pl.BlockSpec((None, d, tn), lambda j, m: (layer, 0, j)),
                  pl.BlockSpec((None, d, tn), lambda j, m: (layer, 0, nb + j)),
                  pl.BlockSpec((None, 1, tn), lambda j, m: (layer, 0, j)),
                  pl.BlockSpec((None, 1, tn), lambda j, m: (layer, 0, nb + j)),
                  pl.BlockSpec((None, POOL_WIDTH, tn), lambda j, m: (layer, 0, j)),
                  pl.BlockSpec((None, ATTN_WIDTH, tn), lambda j, m: (layer, 0, j))],
        out_specs=pl.BlockSpec((tm, tn), lambda j, m: (m, j)),
        out_shape=jax.ShapeDtypeStruct((n, d), BF16),
        scratch_shapes=[pltpu.VMEM((d, tn), BF16), pltpu.VMEM((d, tn), BF16),
                        pltpu.VMEM((POOL_WIDTH, tn), BF16), pltpu.VMEM((ATTN_WIDTH, tn), BF16)],
        compiler_params=_cparams(("arbitrary", "arbitrary"), 56),
        name="gated_merge",
    )(xn, mixed, attn, w_gate, w_gate, b_gate, b_gate, w_up_pool, w_up_attn)


HIGH_HALF = 0xFFFF0000


def _pack_halves(x):
    c = x.shape[1] // 2
    lo = lax.bitcast_convert_type(x[:, :c].astype(BF16).astype(F32), jnp.uint32)
    hi = lax.bitcast_convert_type(x[:, c:].astype(BF16).astype(F32), jnp.uint32)
    return (lo >> 16) | hi


def _unpack_halves(p):
    lo = lax.bitcast_convert_type(p << 16, F32)
    hi = lax.bitcast_convert_type(p & jnp.uint32(HIGH_HALF), F32)
    return lo, hi


ROW_TILE = 8


def _store_row_tiles(ref, packed):
    m = packed.shape[0]
    for s in range(ROW_TILE):
        ref[pl.ds(s, m, stride=ROW_TILE), :] = packed[:, s * LANES:(s + 1) * LANES]


def _load_row_tiles(ref):
    m = ref.shape[0] // ROW_TILE
    return jnp.concatenate([ref[pl.ds(s, m, stride=ROW_TILE), :] for s in range(ROW_TILE)], axis=1)


def _rows(ref, first, count=1):
    return ref.at[pl.ds(pl.multiple_of(first * ROW_TILE, ROW_TILE), count * ROW_TILE)]


def _outproj_kernel(z_ref, w_ref, h_ref, gain_ref, wr_ref, br_ref, h1_ref, hn_ref, logit_ref, w_s):
    @pl.when(pl.program_id(0) == 0)
    def _():
        w_s[...] = w_ref[...].astype(BF16)

    h1 = h_ref[...] + jnp.dot(z_ref[...], w_s[...], preferred_element_type=F32)
    h1_ref[...] = h1
    hn = _rms(h1, gain_ref[...])
    _store_row_tiles(hn_ref, _pack_halves(hn))
    logit_ref[...] = jnp.dot(hn.astype(BF16), wr_ref[...], preferred_element_type=F32) + br_ref[...]


def outproj_call(z, w_out, layer, h, gain, w_router, b_router, tm=256):
    n, d = h.shape
    return pl.pallas_call(
        _outproj_kernel,
        grid=(n // tm,),
        in_specs=[pl.BlockSpec((tm, d), lambda m: (m, 0)),
                  pl.BlockSpec((None, d, d), lambda m: (layer, 0, 0), pipeline_mode=pl.Buffered(1)),
                  pl.BlockSpec((tm, d), lambda m: (m, 0)),
                  pl.BlockSpec((1, d), lambda m: (0, 0)),
                  pl.BlockSpec((d, ROUTER_LANES), lambda m: (0, 0)),
                  pl.BlockSpec((1, ROUTER_LANES), lambda m: (0, 0))],
        out_specs=[pl.BlockSpec((tm, d), lambda m: (m, 0)),
                   pl.BlockSpec((tm * ROW_TILE, LANES), lambda m: (m, 0)),
                   pl.BlockSpec((tm, ROUTER_LANES), lambda m: (m, 0))],
        out_shape=[jax.ShapeDtypeStruct((n, d), F32),
                   jax.ShapeDtypeStruct((n * ROW_TILE, LANES), jnp.uint32),
                   jax.ShapeDtypeStruct((n, ROUTER_LANES), F32)],
        scratch_shapes=[pltpu.VMEM((d, d), BF16)],
        compiler_params=_cparams(("arbitrary",), 56),
        name="out_proj_norm_router",
    )(z, w_out, h, gain.reshape(1, d), w_router, b_router)


def _route(x, lane):
    big = float(ROUTER_LANES)

    def first_argmax(vals):
        top = jnp.max(vals, axis=1, keepdims=True)
        idx = jnp.min(jnp.where(vals == top, lane, big), axis=1, keepdims=True)
        return top, idx

    gmask = lane < N_GROUPS
    g_top, g_sel = first_argmax(jnp.where(gmask, x, -jnp.inf))
    g_weight = 1.0 / jnp.sum(jnp.where(gmask, jnp.exp(x - g_top), 0.0), axis=1, keepdims=True)
    lo = N_GROUPS + EXPERTS_PER_GROUP * g_sel
    e_vals = jnp.where((lane >= lo) & (lane < lo + EXPERTS_PER_GROUP), x, -jnp.inf)
    v1, i1 = first_argmax(e_vals)
    v2, i2 = first_argmax(jnp.where(lane == i1, -jnp.inf, e_vals))
    t = jnp.exp(v2 - v1)
    return i1 - N_GROUPS, i2 - N_GROUPS, g_weight / (1.0 + t), g_weight * t / (1.0 + t)


PLAN_BLOCK = 1024
PLAN_CHUNK = 256
META_TILE_EXPERT, META_NEXT_EXPERT, META_PAD_START, META_N_VALID = 0, 1, 2, 3


def _route_plan_kernel(logit_ref, w_ref, pos_ref, meta_ref, tri_s, cnt_s, base_s, off_s):
    phase = pl.program_id(0)
    blk = pl.program_id(1)
    tb = logit_ref.shape[0]
    lane = lax.broadcasted_iota(jnp.int32, (tb, ROUTER_LANES), 1).astype(F32)
    e0, e1, w0, w1 = _route(logit_ref[...], lane)
    onehot = jnp.where(lane == e0, 1.0, 0.0) + jnp.where(lane == e1, 1.0, 0.0)
    block_counts = jnp.sum(onehot, axis=0, keepdims=True)

    @pl.when(jnp.logical_and(phase == 0, blk == 0))
    def _():
        cnt_s[...] = jnp.zeros(cnt_s.shape, F32)

    @pl.when(phase == 0)
    def _():
        cnt_s[...] = cnt_s[...] + block_counts

    @pl.when(jnp.logical_and(phase == 1, blk == 0))
    def _():
        r = lax.broadcasted_iota(jnp.int32, (tb, tb), 0)
        c = lax.broadcasted_iota(jnp.int32, (tb, tb), 1)
        tri_s[...] = jnp.where(c < r, 1.0, 0.0).astype(BF16)
        lane1 = lane[0:1]
        cnt = cnt_s[...]
        tiles = jnp.floor((cnt + (TM_EXPERT - 1)) * (1.0 / TM_EXPERT))
        ri = lax.broadcasted_iota(jnp.int32, (ROUTER_LANES, ROUTER_LANES), 0)
        ci = lax.broadcasted_iota(jnp.int32, (ROUTER_LANES, ROUTER_LANES), 1)
        upper = jnp.where(ri <= ci, 1.0, 0.0).astype(BF16)
        ends = jnp.dot(jnp.broadcast_to(tiles, (8, ROUTER_LANES)).astype(BF16), upper,
                       preferred_element_type=F32)[0:1]
        off_s[...] = (ends - tiles) * TM_EXPERT
        base_s[...] = jnp.zeros(base_s.shape, F32)

        def pick(vec, e):
            return jnp.sum(jnp.where(lane1 == e, vec, 0.0), axis=1, keepdims=True)

        end_of = [pick(ends, e) for e in range(N_EXPERTS)]
        n_valid = end_of[N_EXPERTS - 1]

        def segment_of(tile):
            return sum(jnp.where(tile >= end_e, 1.0, 0.0) for end_e in end_of)

        tile_expert = jnp.where(lane1 < n_valid, segment_of(lane1), segment_of(n_valid - 1.0))
        following = sum(jnp.where(tile_expert == e, end_of[e], 0.0) for e in range(N_EXPERTS))
        next_expert = jnp.where(following < n_valid, segment_of(following), -1.0)
        rows = [tile_expert, next_expert, off_s[...] + cnt, jnp.broadcast_to(n_valid, (1, ROUTER_LANES))]
        rows += [jnp.zeros((1, ROUTER_LANES), F32)] * (meta_ref.shape[0] - len(rows))
        meta_ref[...] = jnp.concatenate(rows, axis=0).astype(jnp.int32)

    @pl.when(phase == 1)
    def _():
        before = jnp.dot(tri_s[...], onehot.astype(BF16), preferred_element_type=F32)
        row = before + base_s[...] + off_s[...]
        base_s[...] = base_s[...] + block_counts
        w_ref[...] = jnp.where(lane == 0, w0, jnp.where(lane == 1, w1, 0.0))
        eye = (lax.broadcasted_iota(jnp.int32, (LANES, LANES), 0)
               == lax.broadcasted_iota(jnp.int32, (LANES, LANES), 1))
        for j, e in enumerate((e0, e1)):
            col = jnp.sum(jnp.where(lane == e, row, 0.0), axis=1, keepdims=True)
            for q in range(tb // PLAN_CHUNK):
                parts = []
                for g in range(PLAN_CHUNK // LANES):
                    t0 = q * PLAN_CHUNK + g * LANES
                    square = jnp.broadcast_to(col[t0:t0 + LANES], (LANES, LANES))
                    parts.append(jnp.sum(jnp.where(eye, square, 0.0), axis=0, keepdims=True))
                pos_ref[j, q] = jnp.concatenate(parts, axis=1).astype(jnp.int32)


def route_plan_call(logits, n_tiles):
    n = logits.shape[0]
    tb = PLAN_BLOCK
    chunks = tb // PLAN_CHUNK
    assert n_tiles <= ROUTER_LANES and N_EXPERTS <= ROUTER_LANES
    return pl.pallas_call(
        _route_plan_kernel,
        grid=(2, n // tb),
        in_specs=[pl.BlockSpec((tb, ROUTER_LANES), lambda p, b: (b, 0))],
        out_specs=[pl.BlockSpec((tb, ROUTER_LANES), lambda p, b: (p * b, 0)),
                   pl.BlockSpec((2, chunks, 1, PLAN_CHUNK), lambda p, b: (0, p * b, 0, 0)),
                   pl.BlockSpec((8, ROUTER_LANES), lambda p, b: (0, 0))],
        out_shape=[jax.ShapeDtypeStruct((n, ROUTER_LANES), F32),
                   jax.ShapeDtypeStruct((2, n // PLAN_CHUNK, 1, PLAN_CHUNK), jnp.int32),
                   jax.ShapeDtypeStruct((8, ROUTER_LANES), jnp.int32)],
        scratch_shapes=[pltpu.VMEM((tb, tb), BF16), pltpu.VMEM((1, ROUTER_LANES), F32),
                        pltpu.VMEM((1, ROUTER_LANES), F32), pltpu.VMEM((1, ROUTER_LANES), F32)],
        compiler_params=_cparams(("arbitrary", "arbitrary"), 32),
        name="route_plan",
    )(logits)


def _dispatch_kernel(pad_ref, nv_ref, pos0_ref, pos1_ref, src_ref, dst_ref, zero_buf, sem, *, chunk, n_tiles):
    @pl.when(pl.program_id(0) == 0)
    def _():
        zero_buf[...] = jnp.zeros(zero_buf.shape, zero_buf.dtype)

        def fill(e):
            return pltpu.make_async_copy(zero_buf, _rows(dst_ref, pad_ref[e], TM_EXPERT), sem)

        for e in range(N_EXPERTS):
            fill(e).start()
        for e in range(N_EXPERTS):
            fill(e).wait()

        def fill_tile(i):
            return pltpu.make_async_copy(zero_buf, _rows(dst_ref, i * TM_EXPERT, TM_EXPERT), sem)

        def start_tile(i, c):
            fill_tile(i).start()
            return c

        def wait_tile(i, c):
            fill_tile(i).wait()
            return c

        lax.fori_loop(nv_ref[0], n_tiles, start_tile, 0)
        lax.fori_loop(nv_ref[0], n_tiles, wait_tile, 0)

    def start(t, c):
        for j, pos_ref in enumerate((pos0_ref, pos1_ref)):
            pltpu.make_async_copy(_rows(src_ref, t), _rows(dst_ref, pos_ref[0, 0, t]), sem).start(priority=j)
        return c

    lax.fori_loop(0, chunk, start, 0, unroll=ROW_DMA_UNROLL)
    all_rows = pltpu.make_async_copy(src_ref, _rows(dst_ref, 0, chunk), sem)
    for j in range(2):
        all_rows.wait()


def dispatch_call(hn, pos, pad_start, n_valid_tiles, n_tiles):
    n = hn.shape[0] // ROW_TILE
    chunk = PLAN_CHUNK
    grid_spec = pltpu.PrefetchScalarGridSpec(
        num_scalar_prefetch=2,
        grid=(n // chunk,),
        in_specs=[pl.BlockSpec((None, 1, 1, chunk), lambda i, pad, nv: (0, i, 0, 0), memory_space=pltpu.SMEM),
                  pl.BlockSpec((None, 1, 1, chunk), lambda i, pad, nv: (1, i, 0, 0), memory_space=pltpu.SMEM),
                  pl.BlockSpec((chunk * ROW_TILE, LANES), lambda i, pad, nv: (i, 0))],
        out_specs=pl.BlockSpec(memory_space=pl.ANY),
        scratch_shapes=[pltpu.VMEM((TM_EXPERT * ROW_TILE, LANES), hn.dtype), pltpu.SemaphoreType.DMA(())],
    )
    return pl.pallas_call(
        functools.partial(_dispatch_kernel, chunk=chunk, n_tiles=n_tiles),
        grid_spec=grid_spec,
        out_shape=jax.ShapeDtypeStruct((n_tiles * TM_EXPERT * ROW_TILE, LANES), hn.dtype),
        compiler_params=_cparams(("arbitrary",), 32),
        name="dispatch_rows",
    )(pad_start, n_valid_tiles, pos, pos, hn)


def _expert_kernel(te_ref, nv_ref, nx_ref, x_ref, wg_hbm, wu_hbm, wd_hbm, y_ref,
                   wg_f, wu_f, wd_f, wg_s, wu_s, wd_s, seg_ref, sems, *, layer):
    i = pl.program_id(0)
    valid = i < nv_ref[0]
    expert = te_ref[i]
    changed = jnp.logical_or(i == 0, expert != te_ref[jnp.maximum(i - 1, 0)])

    def fetch(e, slot):
        return [pltpu.make_async_copy(hbm.at[layer, e], buf.at[slot], sems.at[slot])
                for hbm, buf in ((wg_hbm, wg_f), (wu_hbm, wu_f), (wd_hbm, wd_f))]

    @pl.when(i == 0)
    def _():
        seg_ref[0] = 0
        for cp in fetch(expert, 0):
            cp.start()

    @pl.when(jnp.logical_and(valid, changed))
    def _():
        slot = seg_ref[0] % 2
        for cp in fetch(expert, slot):
            cp.wait()
        nxt = nx_ref[i]

        @pl.when(nxt >= 0)
        def _():
            for cp in fetch(nxt, 1 - slot):
                cp.start()

        wg_s[...] = wg_f[slot].astype(BF16)
        wu_s[...] = wu_f[slot].astype(BF16)
        wd_s[...] = wd_f[slot].astype(BF16)
        seg_ref[0] = seg_ref[0] + 1

    @pl.when(valid)
    def _():
        lo, hi = _unpack_halves(_load_row_tiles(x_ref))
        x = jnp.concatenate([lo.astype(BF16), hi.astype(BF16)], axis=1)
        a = jnp.dot(x, wg_s[...], preferred_element_type=F32)
        b = jnp.dot(x, wu_s[...], preferred_element_type=F32)
        hmid = (a * _sigmoid(a) * b).astype(BF16)
        _store_row_tiles(y_ref, _pack_halves(jnp.dot(hmid, wd_s[...], preferred_element_type=F32)))

    @pl.when(jnp.logical_not(valid))
    def _():
        y_ref[...] = jnp.zeros(y_ref.shape, y_ref.dtype)


def expert_call(xs, tile_expert, n_valid_tiles, next_expert, w_gate, w_up, w_down, layer):
    p = xs.shape[0] // ROW_TILE
    d = 2 * ROW_TILE * LANES
    tm = TM_EXPERT
    f = w_gate.shape[-1]
    grid_spec = pltpu.PrefetchScalarGridSpec(
        num_scalar_prefetch=3,
        grid=(p // tm,),
        in_specs=[pl.BlockSpec((tm * ROW_TILE, LANES), lambda i, te, nv, nx: (jnp.minimum(i, nv[0] - 1), 0)),
                  pl.BlockSpec(memory_space=pl.ANY),
                  pl.BlockSpec(memory_space=pl.ANY),
                  pl.BlockSpec(memory_space=pl.ANY)],
        out_specs=pl.BlockSpec((tm * ROW_TILE, LANES), lambda i, te, nv, nx: (i, 0)),
        scratch_shapes=[pltpu.VMEM((2, d, f), F32), pltpu.VMEM((2, d, f), F32), pltpu.VMEM((2, f, d), F32),
                        pltpu.VMEM((d, f), BF16), pltpu.VMEM((d, f), BF16), pltpu.VMEM((f, d), BF16),
                        pltpu.SMEM((1,), jnp.int32), pltpu.SemaphoreType.DMA((2,))],
    )
    return pl.pallas_call(
        functools.partial(_expert_kernel, layer=layer),
        grid_spec=grid_spec,
        out_shape=jax.ShapeDtypeStruct(xs.shape, jnp.uint32),
        compiler_params=_cparams(("arbitrary",), 58),
        name="expert_swiglu",
    )(tile_expert, n_valid_tiles, next_expert, xs, w_gate, w_up, w_down)


def _combine_kernel(pos0_ref, pos1_ref, ys_ref, h_ref, w_ref, gain_ref, h2_ref, xn_ref, buf, sem, *, chunk):
    def start(t, c):
        for j, pos_ref in enumerate((pos0_ref, pos1_ref)):
            pltpu.make_async_copy(_rows(ys_ref, pos_ref[0, 0, t]), _rows(buf.at[j], t), sem).start(priority=j)
        return c

    lax.fori_loop(0, chunk, start, 0, unroll=ROW_DMA_UNROLL)
    for j in range(2):
        pltpu.make_async_copy(_rows(ys_ref, 0, chunk), buf.at[j], sem).wait()
    w = w_ref[...]
    lo0, hi0 = _unpack_halves(_load_row_tiles(buf.at[0]))
    lo1, hi1 = _unpack_halves(_load_row_tiles(buf.at[1]))
    y = jnp.concatenate([w[:, 0:1] * lo0 + w[:, 1:2] * lo1, w[:, 0:1] * hi0 + w[:, 1:2] * hi1], axis=1)
    h2 = h_ref[...] + y
    h2_ref[...] = h2
    xn_ref[...] = _rms(h2, gain_ref[...]).astype(xn_ref.dtype)


def combine_call(ys, pos, h1, weights, next_gain, xn_dtype, chunk=128):
    n, d = h1.shape
    per_row = PLAN_CHUNK // chunk
    return pl.pallas_call(
        functools.partial(_combine_kernel, chunk=chunk),
        grid=(n // chunk,),
        in_specs=[pl.BlockSpec((None, 1, 1, chunk), lambda c: (0, c // per_row, 0, c % per_row),
                               memory_space=pltpu.SMEM),
                  pl.BlockSpec((None, 1, 1, chunk), lambda c: (1, c // per_row, 0, c % per_row),
                               memory_space=pltpu.SMEM),
                  pl.BlockSpec(memory_space=pl.ANY),
                  pl.BlockSpec((chunk, d), lambda c: (c, 0)),
                  pl.BlockSpec((chunk, ROUTER_LANES), lambda c: (c, 0)),
                  pl.BlockSpec((1, d), lambda c: (0, 0))],
        out_specs=[pl.BlockSpec((chunk, d), lambda c: (c, 0)),
                   pl.BlockSpec((chunk, d), lambda c: (c, 0))],
        out_shape=[jax.ShapeDtypeStruct((n, d), F32),
                   jax.ShapeDtypeStruct((n, d), xn_dtype)],
        scratch_shapes=[pltpu.VMEM((2, chunk * ROW_TILE, LANES), jnp.uint32), pltpu.SemaphoreType.DMA(())],
        compiler_params=_cparams(("arbitrary",), 32),
        name="combine_rows",
    )(pos, pos, ys, h1, weights, next_gain.reshape(1, d))


def kernel(x, rel_bias_table, norm_mix_gain, w_in, w_merge_gate, b_merge_gate, pool_mix, pool_scale, w_up_pool, lambda_q1, lambda_k1, lambda_q2, lambda_k2, subln_gain, w_up_attn, w_out, norm_ffn_gain, w_router_group, b_router_group, w_router_expert, b_router_expert, w_expert_gate, w_expert_up, w_expert_down, final_norm_gain):
    batch, seq, d = x.shape
    depth = w_in.shape[0]
    n = batch * seq
    n_tiles = (2 * n + N_EXPERTS * (TM_EXPERT - 1)) // TM_EXPERT + 1

    bias_tiles = bias_tiles_call(rel_bias_table)
    h = x.reshape(n, d)
    xn = rms_norm_call(h, norm_mix_gain[0], BF16)
    q_scale = LOG2E * DIFF_HEAD_DIM ** -0.5

    for l in range(depth):
        u = proj_call(xn, w_in, l, 0, POOL_WIDTH // 512, 512, F32, name="in_proj_pool")
        qkv = proj_call(xn, w_in, l, 1, 3, ATTN_WIDTH, BF16, first_block_scale=q_scale, name="in_proj_qkv")
        mixed = pool_call(u, pool_mix[l].astype(BF16), pool_scale[l], seq)
        lambda_init = 0.8 - 0.6 * math.exp(-0.3 * l)
        lam_params = jnp.stack([lambda_q1[l], lambda_k1[l], lambda_q2[l], lambda_k2[l]])
        attn = attn_call(qkv.reshape(batch, seq, 3 * ATTN_WIDTH), bias_tiles, lam_params, subln_gain[l],
                         lambda_init, batch, seq).reshape(n, ATTN_WIDTH)
        z = merge_call(xn, mixed, attn, w_merge_gate, b_merge_gate.reshape(depth, 1, -1), w_up_pool, w_up_attn, l)

        w_router = jnp.concatenate(
            [w_router_group[l], jnp.transpose(w_router_expert[l], (1, 0, 2)).reshape(d, N_EXPERTS),
             jnp.zeros((d, ROUTER_LANES - N_GROUPS - N_EXPERTS), F32)], axis=1).astype(BF16)
        b_router = jnp.concatenate(
            [b_router_group[l], b_router_expert[l].reshape(-1),
             jnp.zeros((ROUTER_LANES - N_GROUPS - N_EXPERTS,), F32)]).reshape(1, ROUTER_LANES)
        h1, hn, logits = outproj_call(z, w_out, l, h, norm_ffn_gain[l], w_router, b_router)

        weights, pos, meta = route_plan_call(logits, n_tiles)
        tile_expert = meta[META_TILE_EXPERT, :n_tiles]
        next_expert = meta[META_NEXT_EXPERT, :n_tiles]
        pad_start = meta[META_PAD_START, :N_EXPERTS]
        n_valid = meta[META_N_VALID, :1]
        xs = dispatch_call(hn, pos, pad_start, n_valid, n_tiles)
        ys = expert_call(xs, tile_expert, n_valid, next_expert, w_expert_gate, w_expert_up, w_expert_down, l)
        last = l == depth - 1
        next_gain = final_norm_gain if last else norm_mix_gain[l + 1]
        h, xn = combine_call(ys, pos, h1, weights, next_gain, F32 if last else BF16)

    return xn.reshape(batch, seq, d)
```

```python
import functools
import math

import numpy as np
import jax
import jax.numpy as jnp
from jax import lax
from jax.experimental import pallas as pl
from jax.experimental.pallas import tpu as pltpu

F32 = jnp.float32
BF16 = jnp.bfloat16

D_MODEL = 2048
POOL_WIDTH = 1024
POOL_WINDOWS = (2, 4, 8, 16)
POOL_GROUP_DIM = 256
POOL_HALO = 16
DIFF_HEADS = 8
DIFF_HEAD_DIM = 64
DIFF_V_DIM = 128
ATTN_WIDTH = 1024
REL_BUCKETS = 32
REL_MAX_DISTANCE = 128
N_GROUPS = 4
EXPERTS_PER_GROUP = 8
N_EXPERTS = 32
D_EXPERT = 512
EPS = 1e-6
NEG_INF = -1e30
LOG2E = 1.4426950408889634

ROUTER_LANES = 128
TQ = 256
TK = 256
KV_SUPER = 2
V_ROWS = DIFF_V_DIM + 16
HEADS_PER_STEP = 4
TM_EXPERT = 256
LANES = 128
ROW_DMA_UNROLL = 8
MIB = 1024 * 1024


def _cparams(sem, vmem_mib):
    return pltpu.CompilerParams(dimension_semantics=sem, vmem_limit_bytes=vmem_mib * MIB)


def _rms(xf, gain):
    ms = jnp.mean(xf * xf, axis=-1, keepdims=True)
    return xf * lax.rsqrt(ms + EPS) * gain


def _norm_kernel(h_ref, g_ref, o_ref):
    o_ref[...] = _rms(h_ref[...], g_ref[...]).astype(o_ref.dtype)


def rms_norm_call(h, gain, out_dtype, tm=512):
    n, d = h.shape
    return pl.pallas_call(
        _norm_kernel,
        grid=(n // tm,),
        in_specs=[pl.BlockSpec((tm, d), lambda m: (m, 0)),
                  pl.BlockSpec((1, d), lambda m: (0, 0))],
        out_specs=pl.BlockSpec((tm, d), lambda m: (m, 0)),
        out_shape=jax.ShapeDtypeStruct((n, d), out_dtype),
        compiler_params=_cparams(("parallel",), 32),
        name="rms_norm",
    )(h, gain.reshape(1, d))


def _proj_kernel(x_ref, w_ref, o_ref, w_s, *, first_block_scale):
    @pl.when(pl.program_id(1) == 0)
    def _():
        w_s[...] = w_ref[...].astype(BF16)

    acc = jnp.dot(x_ref[...], w_s[...], preferred_element_type=F32)
    if first_block_scale is not None:
        acc = acc * jnp.where(pl.program_id(0) == 0, first_block_scale, 1.0).astype(F32)
    o_ref[...] = acc.astype(o_ref.dtype)


def proj_call(x, w, layer, col_block0, n_col_blocks, tn, out_dtype, first_block_scale=None, tm=512, name="proj"):
    n, k = x.shape
    return pl.pallas_call(
        functools.partial(_proj_kernel, first_block_scale=first_block_scale),
        grid=(n_col_blocks, n // tm),
        in_specs=[pl.BlockSpec((tm, k), lambda j, m: (m, 0)),
                  pl.BlockSpec((None, k, tn), lambda j, m: (layer, 0, col_block0 + j))],
        out_specs=pl.BlockSpec((tm, tn), lambda j, m: (m, j)),
        out_shape=jax.ShapeDtypeStruct((n, n_col_blocks * tn), out_dtype),
        scratch_shapes=[pltpu.VMEM((k, tn), BF16)],
        compiler_params=_cparams(("arbitrary", "arbitrary"), 48),
        name=name,
    )(x, w)


def _pool_kernel(cur_ref, prev_ref, mix_ref, scale_ref, o_ref, *, tm, seq):
    m = pl.program_id(0)
    row0 = (m * tm) % seq
    cur = cur_ref[...]
    prev = jnp.where(row0 == 0, 0.0, prev_ref[...])
    pos = row0 + lax.broadcasted_iota(jnp.int32, (tm, 1), 0)
    outs = []
    for g, w in enumerate(POOL_WINDOWS):
        sl = slice(g * POOL_GROUP_DIM, (g + 1) * POOL_GROUP_DIM)
        x = jnp.concatenate([prev[:, sl], cur[:, sl]], axis=0)
        s, d = x, 1
        while d < w:
            s = s[:-d] + s[d:]
            d *= 2
        start = POOL_HALO - w + 1
        wsum = s[start:start + tm]
        count = jnp.minimum(pos + 1, w).astype(F32)
        pooled = wsum / count - cur[:, sl]
        mixed = jnp.dot(pooled.astype(BF16), mix_ref[g], preferred_element_type=F32)
        outs.append(mixed * scale_ref[:, sl])
    o_ref[...] = jnp.concatenate(outs, axis=1).astype(o_ref.dtype)


def pool_call(u, mix_bf16, scale, seq, tm=512):
    n, c = u.shape
    blocks_per_tile = tm // POOL_HALO
    return pl.pallas_call(
        functools.partial(_pool_kernel, tm=tm, seq=seq),
        grid=(n // tm,),
        in_specs=[pl.BlockSpec((tm, c), lambda m: (m, 0)),
                  pl.BlockSpec((POOL_HALO, c), lambda m: (jnp.maximum(m * blocks_per_tile - 1, 0), 0)),
                  pl.BlockSpec(mix_bf16.shape, lambda m: (0, 0, 0)),
                  pl.BlockSpec((1, c), lambda m: (0, 0))],
        out_specs=pl.BlockSpec((tm, c), lambda m: (m, 0)),
        out_shape=jax.ShapeDtypeStruct((n, c), BF16),
        compiler_params=_cparams(("parallel",), 32),
        name="pool_mixer",
    )(u, u, mix_bf16, scale.reshape(1, c))


N_BIAS_TILES = 4


def _bucket_tiles():
    kk = np.arange(TK)[:, None]
    qq = np.arange(TQ)[None, :]
    tiles = []
    for rel in (0, 1, 2, -1):
        n = rel * TK + qq - kk
        max_exact = REL_BUCKETS // 2
        nf = np.maximum(n, 1).astype(np.float64)
        large = max_exact + (np.log(nf / max_exact) / math.log(REL_MAX_DISTANCE / max_exact)
                             * (REL_BUCKETS - max_exact)).astype(np.int64)
        large = np.minimum(large, REL_BUCKETS - 1)
        bucket = np.where(n < max_exact, n, large)
        tiles.append(np.where(n < 0, -1, bucket))
    return np.stack(tiles).astype(np.int32)


def _bias_kernel(table_ref, bucket_ref, o_ref):
    h = pl.program_id(0)
    bucket = bucket_ref[...]
    acc = jnp.full(bucket.shape, NEG_INF, F32)
    for b in range(REL_BUCKETS):
        acc = jnp.where(bucket == b, table_ref[b * DIFF_HEADS + h] * LOG2E, acc)
    o_ref[...] = acc


def bias_tiles_call(rel_table):
    bucket = jnp.asarray(_bucket_tiles())
    return pl.pallas_call(
        _bias_kernel,
        grid=(DIFF_HEADS,),
        in_specs=[pl.BlockSpec(memory_space=pltpu.SMEM),
                  pl.BlockSpec(bucket.shape, lambda h: (0, 0, 0))],
        out_specs=pl.BlockSpec((None,) + bucket.shape, lambda h: (h, 0, 0, 0)),
        out_shape=jax.ShapeDtypeStruct((DIFF_HEADS,) + bucket.shape, F32),
        compiler_params=_cparams(("parallel",), 32),
        name="rel_bias_tiles",
    )(rel_table.reshape(-1), bucket)


def _attn_kernel(q_ref, k_ref, v_ref, bias_ref, lam_ref, gain_ref, o_ref, vt_ref, s_ref, acc_ref,
                 *, lambda_init, n_super):
    qi = pl.program_id(2)
    tks = KV_SUPER * TK
    heads = range(HEADS_PER_STEP)

    @pl.when(qi == 0)
    def _():
        extra = (lax.broadcasted_iota(jnp.int32, (V_ROWS - DIFF_V_DIM, tks), 0) == 0).astype(BF16)
        for hh in heads:
            cols = slice(hh * DIFF_V_DIM, (hh + 1) * DIFF_V_DIM)
            for c in range(n_super):
                vt = v_ref[c * tks:(c + 1) * tks, cols].astype(F32).T.astype(BF16)
                vt_ref[hh, c] = jnp.concatenate([vt, extra], axis=0)

    qds = []
    for hh in heads:
        q = q_ref[:, hh * DIFF_V_DIM:(hh + 1) * DIFF_V_DIM]
        lane = lax.broadcasted_iota(jnp.int32, q.shape, 1)
        zero = jnp.zeros_like(q)
        qds.append(jnp.concatenate([jnp.where(lane < DIFF_HEAD_DIM, q, zero),
                                    jnp.where(lane >= DIFF_HEAD_DIM, q, zero)], axis=0))

    def scores(t, hh):
        kb = k_ref[pl.ds(pl.multiple_of(t * tks, tks), tks), hh * DIFF_V_DIM:(hh + 1) * DIFF_V_DIM]
        s = lax.dot_general(kb, qds[hh], (((1,), (1,)), ((), ())), preferred_element_type=F32)
        parts = []
        for u in range(KV_SUPER):
            rel = qi - (t * KV_SUPER + u)
            bias = bias_ref[hh, jnp.where(rel < 0, N_BIAS_TILES - 1, jnp.minimum(rel, 2))]
            parts.append(s[u * TK:(u + 1) * TK] + jnp.concatenate([bias, bias], axis=1))
        s = jnp.concatenate(parts, axis=0)
        s_ref[hh] = s
        return jnp.max(s, axis=0, keepdims=True)

    def softmax_step(t, hh, m_prev, m_cur):
        alpha = jnp.exp2(m_prev - m_cur)
        p = jnp.exp2(s_ref[hh] - m_cur)
        pv = jnp.dot(vt_ref[hh, t], p.astype(BF16), preferred_element_type=F32)
        acc_ref[hh] = acc_ref[hh] * alpha + pv

    last = qi // KV_SUPER
    acc_ref[...] = jnp.zeros(acc_ref.shape, F32)
    neg = jnp.full((1, 2 * TQ), NEG_INF, F32)
    init = tuple((neg, jnp.maximum(neg, scores(0, hh))) for hh in heads)

    def body(t, carry):
        out = []
        for hh in heads:
            m_prev, m_cur = carry[hh]
            softmax_step(t, hh, m_prev, m_cur)
            m_next = jnp.maximum(m_cur, scores(t + 1, hh))
            out.append((m_cur, m_next))
        return tuple(out)

    carry = lax.fori_loop(0, last, body, init)

    lam_p = lam_ref[...]
    lam = (jnp.exp(jnp.sum(lam_p[0:1] * lam_p[1:2], axis=1, keepdims=True))
           - jnp.exp(jnp.sum(lam_p[2:3] * lam_p[3:4], axis=1, keepdims=True)) + lambda_init)
    for hh in heads:
        m_prev, m_cur = carry[hh]
        softmax_step(last, hh, m_prev, m_cur)
        acc = acc_ref[hh, :DIFF_V_DIM, :]
        l = acc_ref[hh, DIFF_V_DIM:DIFF_V_DIM + 1, :]
        o = acc[:, :TQ] / l[:, :TQ] - lam * (acc[:, TQ:] / l[:, TQ:])
        ms = jnp.mean(o * o, axis=0, keepdims=True)
        y = o * lax.rsqrt(ms + EPS) * gain_ref[...] * (1.0 - lambda_init)
        o_ref[:, hh * DIFF_V_DIM:(hh + 1) * DIFF_V_DIM] = y.T.astype(o_ref.dtype)


def attn_call(qkv, bias_tiles, lam_params, subln_gain, lambda_init, batch, seq):
    tks = KV_SUPER * TK
    n_super = seq // tks
    hps = HEADS_PER_STEP
    width = hps * DIFF_V_DIM
    groups = DIFF_HEADS // hps
    return pl.pallas_call(
        functools.partial(_attn_kernel, lambda_init=lambda_init, n_super=n_super),
        grid=(batch, groups, seq // TQ),
        in_specs=[pl.BlockSpec((None, TQ, width), lambda b, g, i: (b, i, g)),
                  pl.BlockSpec((None, seq, width), lambda b, g, i: (b, 0, groups + g)),
                  pl.BlockSpec((None, seq, width), lambda b, g, i: (b, 0, 2 * groups + g)),
                  pl.BlockSpec((hps, N_BIAS_TILES, TK, TQ), lambda b, g, i: (g, 0, 0, 0)),
                  pl.BlockSpec((4, DIFF_HEAD_DIM), lambda b, g, i: (0, 0)),
                  pl.BlockSpec((DIFF_V_DIM, 1), lambda b, g, i: (0, 0))],
        out_specs=pl.BlockSpec((None, TQ, width), lambda b, g, i: (b, i, g)),
        out_shape=jax.ShapeDtypeStruct((batch, seq, ATTN_WIDTH), BF16),
        scratch_shapes=[pltpu.VMEM((hps, n_super, V_ROWS, tks), BF16),
                        pltpu.VMEM((hps, tks, 2 * TQ), F32),
                        pltpu.VMEM((hps, V_ROWS, 2 * TQ), F32)],
        compiler_params=_cparams(("parallel", "parallel", "arbitrary"), 56),
        name="diff_attention",
    )(qkv, qkv, qkv, bias_tiles, lam_params, subln_gain.reshape(DIFF_V_DIM, 1))


def _sigmoid(x):
    return 1.0 / (1.0 + jnp.exp(-x))


def _merge_kernel(xn_ref, mixed_ref, attn_ref, wgp_ref, wga_ref, bgp_ref, bga_ref, wup_ref, wua_ref, z_ref,
                  wgp_s, wga_s, wup_s, wua_s):
    @pl.when(pl.program_id(1) == 0)
    def _():
        wgp_s[...] = wgp_ref[...].astype(BF16)
        wga_s[...] = wga_ref[...].astype(BF16)
        wup_s[...] = wup_ref[...].astype(BF16)
        wua_s[...] = wua_ref[...].astype(BF16)

    xn = xn_ref[...]
    g_pool = _sigmoid(jnp.dot(xn, wgp_s[...], preferred_element_type=F32) + bgp_ref[...])
    g_attn = _sigmoid(jnp.dot(xn, wga_s[...], preferred_element_type=F32) + bga_ref[...])
    y_pool = jnp.dot(mixed_ref[...], wup_s[...], preferred_element_type=F32)
    y_attn = jnp.dot(attn_ref[...], wua_s[...], preferred_element_type=F32)
    z_ref[...] = (g_pool * y_pool + g_attn * y_attn).astype(z_ref.dtype)


def merge_call(xn, mixed, attn, w_gate, b_gate, w_up_pool, w_up_attn, layer, tm=512, tn=512):
    n, d = xn.shape
    nb = d // tn
    return pl.pallas_call(
        _merge_kernel,
        grid=(nb, n // tm),
        in_specs=[pl.BlockSpec((tm, d), lambda j, m: (m, 0)),
                  pl.BlockSpec((tm, POOL_WIDTH), lambda j, m: (m, 0)),
                  pl.BlockSpec((tm, ATTN_WIDTH), lambda j, m: (m, 0)),
                  pl.BlockSpec((None, d, tn), lambda j, m: (layer, 0, j)),
                  pl.BlockSpec((None, d, tn), lambda j, m: (layer, 0, nb + j)),
                  pl.BlockSpec((None, 1, tn), lambda j, m: (layer, 0, j)),
                  pl.BlockSpec((None, 1, tn), lambda j, m: (layer, 0, nb + j)),
                  pl.BlockSpec((None, POOL_WIDTH, tn), lambda j, m: (layer, 0, j)),
                  pl.BlockSpec((None, ATTN_WIDTH, tn), lambda j, m: (layer, 0, j))],
        out_specs=pl.BlockSpec((tm, tn), lambda j, m: (m, j)),
        out_shape=jax.ShapeDtypeStruct((n, d), BF16),
        scratch_shapes=[pltpu.VMEM((d, tn), BF16), pltpu.VMEM((d, tn), BF16),
                        pltpu.VMEM((POOL_WIDTH, tn), BF16), pltpu.VMEM((ATTN_WIDTH, tn), BF16)],
        compiler_params=_cparams(("arbitrary", "arbitrary"), 56),
        name="gated_merge",
    )(xn, mixed, attn, w_gate, w_gate, b_gate, b_gate, w_up_pool, w_up_attn)


HIGH_HALF = 0xFFFF0000


def _pack_halves(x):
    c = x.shape[1] // 2
    lo = lax.bitcast_convert_type(x[:, :c].astype(BF16).astype(F32), jnp.uint32)
    hi = lax.bitcast_convert_type(x[:, c:].astype(BF16).astype(F32), jnp.uint32)
    return (lo >> 16) | hi


def _unpack_halves(p):
    lo = lax.bitcast_convert_type(p << 16, F32)
    hi = lax.bitcast_convert_type(p & jnp.uint32(HIGH_HALF), F32)
    return lo, hi


ROW_TILE = 8


def _store_row_tiles(ref, packed):
    m = packed.shape[0]
    for s in range(ROW_TILE):
        ref[pl.ds(s, m, stride=ROW_TILE), :] = packed[:, s * LANES:(s + 1) * LANES]


def _load_row_tiles(ref):
    m = ref.shape[0] // ROW_TILE
    return jnp.concatenate([ref[pl.ds(s, m, stride=ROW_TILE), :] for s in range(ROW_TILE)], axis=1)


def _rows(ref, first, count=1):
    return ref.at[pl.ds(pl.multiple_of(first * ROW_TILE, ROW_TILE), count * ROW_TILE)]


def _outproj_kernel(z_ref, w_ref, h_ref, gain_ref, wr_ref, br_ref, h1_ref, hn_ref, logit_ref, w_s):
    @pl.when(pl.program_id(0) == 0)
    def _():
        w_s[...] = w_ref[...].astype(BF16)

    h1 = h_ref[...] + jnp.dot(z_ref[...], w_s[...], preferred_element_type=F32)
    h1_ref[...] = h1
    hn = _rms(h1, gain_ref[...])
    _store_row_tiles(hn_ref, _pack_halves(hn))
    logit_ref[...] = jnp.dot(hn.astype(BF16), wr_ref[...], preferred_element_type=F32) + br_ref[...]


def outproj_call(z, w_out, layer, h, gain, w_router, b_router, tm=256):
    n, d = h.shape
    return pl.pallas_call(
        _outproj_kernel,
        grid=(n // tm,),
        in_specs=[pl.BlockSpec((tm, d), lambda m: (m, 0)),
                  pl.BlockSpec((None, d, d), lambda m: (layer, 0, 0), pipeline_mode=pl.Buffered(1)),
                  pl.BlockSpec((tm, d), lambda m: (m, 0)),
                  pl.BlockSpec((1, d), lambda m: (0, 0)),
                  pl.BlockSpec((d, ROUTER_LANES), lambda m: (0, 0)),
                  pl.BlockSpec((1, ROUTER_LANES), lambda m: (0, 0))],
        out_specs=[pl.BlockSpec((tm, d), lambda m: (m, 0)),
                   pl.BlockSpec((tm * ROW_TILE, LANES), lambda m: (m, 0)),
                   pl.BlockSpec((tm, ROUTER_LANES), lambda m: (m, 0))],
        out_shape=[jax.ShapeDtypeStruct((n, d), F32),
                   jax.ShapeDtypeStruct((n * ROW_TILE, LANES), jnp.uint32),
                   jax.ShapeDtypeStruct((n, ROUTER_LANES), F32)],
        scratch_shapes=[pltpu.VMEM((d, d), BF16)],
        compiler_params=_cparams(("arbitrary",), 56),
        name="out_proj_norm_router",
    )(z, w_out, h, gain.reshape(1, d), w_router, b_router)


def _route(x, lane):
    big = float(ROUTER_LANES)

    def first_argmax(vals):
        top = jnp.max(vals, axis=1, keepdims=True)
        idx = jnp.min(jnp.where(vals == top, lane, big), axis=1, keepdims=True)
        return top, idx

    gmask = lane < N_GROUPS
    g_top, g_sel = first_argmax(jnp.where(gmask, x, -jnp.inf))
    g_weight = 1.0 / jnp.sum(jnp.where(gmask, jnp.exp(x - g_top), 0.0), axis=1, keepdims=True)
    lo = N_GROUPS + EXPERTS_PER_GROUP * g_sel
    e_vals = jnp.where((lane >= lo) & (lane < lo + EXPERTS_PER_GROUP), x, -jnp.inf)
    v1, i1 = first_argmax(e_vals)
    v2, i2 = first_argmax(jnp.where(lane == i1, -jnp.inf, e_vals))
    t = jnp.exp(v2 - v1)
    return i1 - N_GROUPS, i2 - N_GROUPS, g_weight / (1.0 + t), g_weight * t / (1.0 + t)


PLAN_BLOCK = 1024
PLAN_CHUNK = 256
META_TILE_EXPERT, META_NEXT_EXPERT, META_PAD_START, META_N_VALID = 0, 1, 2, 3


def _route_plan_kernel(logit_ref, w_ref, pos_ref, meta_ref, tri_s, cnt_s, base_s, off_s):
    phase = pl.program_id(0)
    blk = pl.program_id(1)
    tb = logit_ref.shape[0]
    lane = lax.broadcasted_iota(jnp.int32, (tb, ROUTER_LANES), 1).astype(F32)
    e0, e1, w0, w1 = _route(logit_ref[...], lane)
    onehot = jnp.where(lane == e0, 1.0, 0.0) + jnp.where(lane == e1, 1.0, 0.0)
    block_counts = jnp.sum(onehot, axis=0, keepdims=True)

    @pl.when(jnp.logical_and(phase == 0, blk == 0))
    def _():
        cnt_s[...] = jnp.zeros(cnt_s.shape, F32)

    @pl.when(phase == 0)
    def _():
        cnt_s[...] = cnt_s[...] + block_counts

    @pl.when(jnp.logical_and(phase == 1, blk == 0))
    def _():
        r = lax.broadcasted_iota(jnp.int32, (tb, tb), 0)
        c = lax.broadcasted_iota(jnp.int32, (tb, tb), 1)
        tri_s[...] = jnp.where(c < r, 1.0, 0.0).astype(BF16)
        lane1 = lane[0:1]
        cnt = cnt_s[...]
        tiles = jnp.floor((cnt + (TM_EXPERT - 1)) * (1.0 / TM_EXPERT))
        ri = lax.broadcasted_iota(jnp.int32, (ROUTER_LANES, ROUTER_LANES), 0)
        ci = lax.broadcasted_iota(jnp.int32, (ROUTER_LANES, ROUTER_LANES), 1)
        upper = jnp.where(ri <= ci, 1.0, 0.0).astype(BF16)
        ends = jnp.dot(jnp.broadcast_to(tiles, (8, ROUTER_LANES)).astype(BF16), upper,
                       preferred_element_type=F32)[0:1]
        off_s[...] = (ends - tiles) * TM_EXPERT
        base_s[...] = jnp.zeros(base_s.shape, F32)

        def pick(vec, e):
            return jnp.sum(jnp.where(lane1 == e, vec, 0.0), axis=1, keepdims=True)

        end_of = [pick(ends, e) for e in range(N_EXPERTS)]
        n_valid = end_of[N_EXPERTS - 1]

        def segment_of(tile):
            return sum(jnp.where(tile >= end_e, 1.0, 0.0) for end_e in end_of)

        tile_expert = jnp.where(lane1 < n_valid, segment_of(lane1), segment_of(n_valid - 1.0))
        following = sum(jnp.where(tile_expert == e, end_of[e], 0.0) for e in range(N_EXPERTS))
        next_expert = jnp.where(following < n_valid, segment_of(following), -1.0)
        rows = [tile_expert, next_expert, off_s[...] + cnt, jnp.broadcast_to(n_valid, (1, ROUTER_LANES))]
        rows += [jnp.zeros((1, ROUTER_LANES), F32)] * (meta_ref.shape[0] - len(rows))
        meta_ref[...] = jnp.concatenate(rows, axis=0).astype(jnp.int32)

    @pl.when(phase == 1)
    def _():
        before = jnp.dot(tri_s[...], onehot.astype(BF16), preferred_element_type=F32)
        row = before + base_s[...] + off_s[...]
        base_s[...] = base_s[...] + block_counts
        w_ref[...] = jnp.where(lane == 0, w0, jnp.where(lane == 1, w1, 0.0))
        eye = (lax.broadcasted_iota(jnp.int32, (LANES, LANES), 0)
               == lax.broadcasted_iota(jnp.int32, (LANES, LANES), 1))
        for j, e in enumerate((e0, e1)):
            col = jnp.sum(jnp.where(lane == e, row, 0.0), axis=1, keepdims=True)
            for q in range(tb // PLAN_CHUNK):
                parts = []
                for g in range(PLAN_CHUNK // LANES):
                    t0 = q * PLAN_CHUNK + g * LANES
                    square = jnp.broadcast_to(col[t0:t0 + LANES], (LANES, LANES))
                    parts.append(jnp.sum(jnp.where(eye, square, 0.0), axis=0, keepdims=True))
                pos_ref[j, q] = jnp.concatenate(parts, axis=1).astype(jnp.int32)


def route_plan_call(logits, n_tiles):
    n = logits.shape[0]
    tb = PLAN_BLOCK
    chunks = tb // PLAN_CHUNK
    assert n_tiles <= ROUTER_LANES and N_EXPERTS <= ROUTER_LANES
    return pl.pallas_call(
        _route_plan_kernel,
        grid=(2, n // tb),
        in_specs=[pl.BlockSpec((tb, ROUTER_LANES), lambda p, b: (b, 0))],
        out_specs=[pl.BlockSpec((tb, ROUTER_LANES), lambda p, b: (p * b, 0)),
                   pl.BlockSpec((2, chunks, 1, PLAN_CHUNK), lambda p, b: (0, p * b, 0, 0)),
                   pl.BlockSpec((8, ROUTER_LANES), lambda p, b: (0, 0))],
        out_shape=[jax.ShapeDtypeStruct((n, ROUTER_LANES), F32),
                   jax.ShapeDtypeStruct((2, n // PLAN_CHUNK, 1, PLAN_CHUNK), jnp.int32),
                   jax.ShapeDtypeStruct((8, ROUTER_LANES), jnp.int32)],
        scratch_shapes=[pltpu.VMEM((tb, tb), BF16), pltpu.VMEM((1, ROUTER_LANES), F32),
                        pltpu.VMEM((1, ROUTER_LANES), F32), pltpu.VMEM((1, ROUTER_LANES), F32)],
        compiler_params=_cparams(("arbitrary", "arbitrary"), 32),
        name="route_plan",
    )(logits)


def _dispatch_kernel(pad_ref, nv_ref, pos0_ref, pos1_ref, src_ref, dst_ref, zero_buf, sem, *, chunk, n_tiles):
    @pl.when(pl.program_id(0) == 0)
    def _():
        zero_buf[...] = jnp.zeros(zero_buf.shape, zero_buf.dtype)

        def fill(e):
            return pltpu.make_async_copy(zero_buf, _rows(dst_ref, pad_ref[e], TM_EXPERT), sem)

        for e in range(N_EXPERTS):
            fill(e).start()
        for e in range(N_EXPERTS):
            fill(e).wait()

        def fill_tile(i):
            return pltpu.make_async_copy(zero_buf, _rows(dst_ref, i * TM_EXPERT, TM_EXPERT), sem)

        def start_tile(i, c):
            fill_tile(i).start()
            return c

        def wait_tile(i, c):
            fill_tile(i).wait()
            return c

        lax.fori_loop(nv_ref[0], n_tiles, start_tile, 0)
        lax.fori_loop(nv_ref[0], n_tiles, wait_tile, 0)

    def start(t, c):
        for j, pos_ref in enumerate((pos0_ref, pos1_ref)):
            pltpu.make_async_copy(_rows(src_ref, t), _rows(dst_ref, pos_ref[0, 0, t]), sem).start(priority=j)
        return c

    lax.fori_loop(0, chunk, start, 0, unroll=ROW_DMA_UNROLL)
    all_rows = pltpu.make_async_copy(src_ref, _rows(dst_ref, 0, chunk), sem)
    for j in range(2):
        all_rows.wait()


def dispatch_call(hn, pos, pad_start, n_valid_tiles, n_tiles):
    n = hn.shape[0] // ROW_TILE
    chunk = PLAN_CHUNK
    grid_spec = pltpu.PrefetchScalarGridSpec(
        num_scalar_prefetch=2,
        grid=(n // chunk,),
        in_specs=[pl.BlockSpec((None, 1, 1, chunk), lambda i, pad, nv: (0, i, 0, 0), memory_space=pltpu.SMEM),
                  pl.BlockSpec((None, 1, 1, chunk), lambda i, pad, nv: (1, i, 0, 0), memory_space=pltpu.SMEM),
                  pl.BlockSpec((chunk * ROW_TILE, LANES), lambda i, pad, nv: (i, 0))],
        out_specs=pl.BlockSpec(memory_space=pl.ANY),
        scratch_shapes=[pltpu.VMEM((TM_EXPERT * ROW_TILE, LANES), hn.dtype), pltpu.SemaphoreType.DMA(())],
    )
    return pl.pallas_call(
        functools.partial(_dispatch_kernel, chunk=chunk, n_tiles=n_tiles),
        grid_spec=grid_spec,
        out_shape=jax.ShapeDtypeStruct((n_tiles * TM_EXPERT * ROW_TILE, LANES), hn.dtype),
        compiler_params=_cparams(("arbitrary",), 32),
        name="dispatch_rows",
    )(pad_start, n_valid_tiles, pos, pos, hn)


def _expert_kernel(te_ref, nv_ref, nx_ref, x_ref, wg_hbm, wu_hbm, wd_hbm, y_ref,
                   wg_f, wu_f, wd_f, wg_s, wu_s, wd_s, seg_ref, sems, *, layer):
    i = pl.program_id(0)
    valid = i < nv_ref[0]
    expert = te_ref[i]
    changed = jnp.logical_or(i == 0, expert != te_ref[jnp.maximum(i - 1, 0)])

    def fetch(e, slot):
        return [pltpu.make_async_copy(hbm.at[layer, e], buf.at[slot], sems.at[slot])
                for hbm, buf in ((wg_hbm, wg_f), (wu_hbm, wu_f), (wd_hbm, wd_f))]

    @pl.when(i == 0)
    def _():
        seg_ref[0] = 0
        for cp in fetch(expert, 0):
            cp.start()

    @pl.when(jnp.logical_and(valid, changed))
    def _():
        slot = seg_ref[0] % 2
        for cp in fetch(expert, slot):
            cp.wait()
        nxt = nx_ref[i]

        @pl.when(nxt >= 0)
        def _():
            for cp in fetch(nxt, 1 - slot):
                cp.start()

        wg_s[...] = wg_f[slot].astype(BF16)
        wu_s[...] = wu_f[slot].astype(BF16)
        wd_s[...] = wd_f[slot].astype(BF16)
        seg_ref[0] = seg_ref[0] + 1

    @pl.when(valid)
    def _():
        lo, hi = _unpack_halves(_load_row_tiles(x_ref))
        x = jnp.concatenate([lo.astype(BF16), hi.astype(BF16)], axis=1)
        a = jnp.dot(x, wg_s[...], preferred_element_type=F32)
        b = jnp.dot(x, wu_s[...], preferred_element_type=F32)
        hmid = (a * _sigmoid(a) * b).astype(BF16)
        _store_row_tiles(y_ref, _pack_halves(jnp.dot(hmid, wd_s[...], preferred_element_type=F32)))

    @pl.when(jnp.logical_not(valid))
    def _():
        y_ref[...] = jnp.zeros(y_ref.shape, y_ref.dtype)


def expert_call(xs, tile_expert, n_valid_tiles, next_expert, w_gate, w_up, w_down, layer):
    p = xs.shape[0] // ROW_TILE
    d = 2 * ROW_TILE * LANES
    tm = TM_EXPERT
    f = w_gate.shape[-1]
    grid_spec = pltpu.PrefetchScalarGridSpec(
        num_scalar_prefetch=3,
        grid=(p // tm,),
        in_specs=[pl.BlockSpec((tm * ROW_TILE, LANES), lambda i, te, nv, nx: (jnp.minimum(i, nv[0] - 1), 0)),
                  pl.BlockSpec(memory_space=pl.ANY),
                  pl.BlockSpec(memory_space=pl.ANY),
                  pl.BlockSpec(memory_space=pl.ANY)],
        out_specs=pl.BlockSpec((tm * ROW_TILE, LANES), lambda i, te, nv, nx: (i, 0)),
        scratch_shapes=[pltpu.VMEM((2, d, f), F32), pltpu.VMEM((2, d, f), F32), pltpu.VMEM((2, f, d), F32),
                        pltpu.VMEM((d, f), BF16), pltpu.VMEM((d, f), BF16), pltpu.VMEM((f, d), BF16),
                        pltpu.SMEM((1,), jnp.int32), pltpu.SemaphoreType.DMA((2,))],
    )
    return pl.pallas_call(
        functools.partial(_expert_kernel, layer=layer),
        grid_spec=grid_spec,
        out_shape=jax.ShapeDtypeStruct(xs.shape, jnp.uint32),
        compiler_params=_cparams(("arbitrary",), 58),
        name="expert_swiglu",
    )(tile_expert, n_valid_tiles, next_expert, xs, w_gate, w_up, w_down)


def _combine_kernel(pos0_ref, pos1_ref, nxt0_ref, nxt1_ref, ys_ref, h_ref, w_ref, gain_ref, h2_ref, xn_ref,
                    buf, sems, *, chunk):
    i = pl.program_id(0)
    slot = i % 2

    def gather(p0_ref, p1_ref, into):
        def start(t, c):
            for j, pos_ref in enumerate((p0_ref, p1_ref)):
                pltpu.make_async_copy(_rows(ys_ref, pos_ref[0, 0, t]), _rows(buf.at[into, j], t),
                                      sems.at[into]).start(priority=j)
            return c

        lax.fori_loop(0, chunk, start, 0, unroll=ROW_DMA_UNROLL)

    @pl.when(i == 0)
    def _():
        gather(pos0_ref, pos1_ref, 0)

    @pl.when(i + 1 < pl.num_programs(0))
    def _():
        gather(nxt0_ref, nxt1_ref, 1 - slot)

    for j in range(2):
        pltpu.make_async_copy(_rows(ys_ref, 0, chunk), buf.at[slot, j], sems.at[slot]).wait()
    w = w_ref[...]
    lo0, hi0 = _unpack_halves(_load_row_tiles(buf.at[slot, 0]))
    lo1, hi1 = _unpack_halves(_load_row_tiles(buf.at[slot, 1]))
    y = jnp.concatenate([w[:, 0:1] * lo0 + w[:, 1:2] * lo1, w[:, 0:1] * hi0 + w[:, 1:2] * hi1], axis=1)
    h2 = h_ref[...] + y
    h2_ref[...] = h2
    xn_ref[...] = _rms(h2, gain_ref[...]).astype(xn_ref.dtype)


def combine_call(ys, pos, h1, weights, next_gain, xn_dtype, chunk=128):
    n, d = h1.shape
    per_row = PLAN_CHUNK // chunk
    steps = n // chunk

    def pos_spec(j, ahead):
        def index(c):
            c = jnp.minimum(c + ahead, steps - 1)
            return (j, c // per_row, 0, c % per_row)
        return pl.BlockSpec((None, 1, 1, chunk), index, memory_space=pltpu.SMEM)

    return pl.pallas_call(
        functools.partial(_combine_kernel, chunk=chunk),
        grid=(steps,),
        in_specs=[pos_spec(0, 0), pos_spec(1, 0), pos_spec(0, 1), pos_spec(1, 1),
                  pl.BlockSpec(memory_space=pl.ANY),
                  pl.BlockSpec((chunk, d), lambda c: (c, 0)),
                  pl.BlockSpec((chunk, ROUTER_LANES), lambda c: (c, 0)),
                  pl.BlockSpec((1, d), lambda c: (0, 0))],
        out_specs=[pl.BlockSpec((chunk, d), lambda c: (c, 0)),
                   pl.BlockSpec((chunk, d), lambda c: (c, 0))],
        out_shape=[jax.ShapeDtypeStruct((n, d), F32),
                   jax.ShapeDtypeStruct((n, d), xn_dtype)],
        scratch_shapes=[pltpu.VMEM((2, 2, chunk * ROW_TILE, LANES), jnp.uint32), pltpu.SemaphoreType.DMA((2,))],
        compiler_params=_cparams(("arbitrary",), 32),
        name="combine_rows",
    )(pos, pos, pos, pos, ys, h1, weights, next_gain.reshape(1, d))


def kernel(x, rel_bias_table, norm_mix_gain, w_in, w_merge_gate, b_merge_gate, pool_mix, pool_scale, w_up_pool, lambda_q1, lambda_k1, lambda_q2, lambda_k2, subln_gain, w_up_attn, w_out, norm_ffn_gain, w_router_group, b_router_group, w_router_expert, b_router_expert, w_expert_gate, w_expert_up, w_expert_down, final_norm_gain):
    batch, seq, d = x.shape
    depth = w_in.shape[0]
    n = batch * seq
    n_tiles = (2 * n + N_EXPERTS * (TM_EXPERT - 1)) // TM_EXPERT + 1

    bias_tiles = bias_tiles_call(rel_bias_table)
    h = x.reshape(n, d)
    xn = rms_norm_call(h, norm_mix_gain[0], BF16)
    q_scale = LOG2E * DIFF_HEAD_DIM ** -0.5

    for l in range(depth):
        u = proj_call(xn, w_in, l, 0, POOL_WIDTH // 512, 512, F32, name="in_proj_pool")
        qkv = proj_call(xn, w_in, l, 1, 3, ATTN_WIDTH, BF16, first_block_scale=q_scale, name="in_proj_qkv")
        mixed = pool_call(u, pool_mix[l].astype(BF16), pool_scale[l], seq)
        lambda_init = 0.8 - 0.6 * math.exp(-0.3 * l)
        lam_params = jnp.stack([lambda_q1[l], lambda_k1[l], lambda_q2[l], lambda_k2[l]])
        attn = attn_call(qkv.reshape(batch, seq, 3 * ATTN_WIDTH), bias_tiles, lam_params, subln_gain[l],
                         lambda_init, batch, seq).reshape(n, ATTN_WIDTH)
        z = merge_call(xn, mixed, attn, w_merge_gate, b_merge_gate.reshape(depth, 1, -1), w_up_pool, w_up_attn, l)

        w_router = jnp.concatenate(
            [w_router_group[l], jnp.transpose(w_router_expert[l], (1, 0, 2)).reshape(d, N_EXPERTS),
             jnp.zeros((d, ROUTER_LANES - N_GROUPS - N_EXPERTS), F32)], axis=1).astype(BF16)
        b_router = jnp.concatenate(
            [b_router_group[l], b_router_expert[l].reshape(-1),
             jnp.zeros((ROUTER_LANES - N_GROUPS - N_EXPERTS,), F32)]).reshape(1, ROUTER_LANES)
        h1, hn, logits = outproj_call(z, w_out, l, h, norm_ffn_gain[l], w_router, b_router)

        weights, pos, meta = route_plan_call(logits, n_tiles)
        tile_expert = meta[META_TILE_EXPERT, :n_tiles]
        next_expert = meta[META_NEXT_EXPERT, :n_tiles]
        pad_start = meta[META_PAD_START, :N_EXPERTS]
        n_valid = meta[META_N_VALID, :1]
        xs = dispatch_call(hn, pos, pad_start, n_valid, n_tiles)
        ys = expert_call(xs, tile_expert, n_valid, next_expert, w_expert_gate, w_expert_up, w_expert_down, l)
        last = l == depth - 1
        next_gain = final_norm_gain if last else norm_mix_gain[l + 1]
        h, xn = combine_call(ys, pos, h1, weights, next_gain, F32 if last else BF16)

    return xn.reshape(batch, seq, d)
```

```python
import functools
import math

import numpy as np
import jax
import jax.numpy as jnp
from jax import lax
from jax.experimental import pallas as pl
from jax.experimental.pallas import tpu as pltpu

F32 = jnp.float32
BF16 = jnp.bfloat16

D_MODEL = 2048
POOL_WIDTH = 1024
POOL_WINDOWS = (2, 4, 8, 16)
POOL_GROUP_DIM = 256
POOL_HALO = 16
DIFF_HEADS = 8
DIFF_HEAD_DIM = 64
DIFF_V_DIM = 128
ATTN_WIDTH = 1024
REL_BUCKETS = 32
REL_MAX_DISTANCE = 128
N_GROUPS = 4
EXPERTS_PER_GROUP = 8
N_EXPERTS = 32
D_EXPERT = 512
EPS = 1e-6
NEG_INF = -1e30
LOG2E = 1.4426950408889634

ROUTER_LANES = 128
TQ = 256
TK = 256
KV_SUPER = 2
V_ROWS = DIFF_V_DIM + 16
HEADS_PER_STEP = 4
TM_EXPERT = 256
LANES = 128
ROW_DMA_UNROLL = 8
MIB = 1024 * 1024


def _cparams(sem, vmem_mib):
    return pltpu.CompilerParams(dimension_semantics=sem, vmem_limit_bytes=vmem_mib * MIB)


def _rms(xf, gain):
    ms = jnp.mean(xf * xf, axis=-1, keepdims=True)
    return xf * lax.rsqrt(ms + EPS) * gain


def _norm_kernel(h_ref, g_ref, o_ref):
    o_ref[...] = _rms(h_ref[...], g_ref[...]).astype(o_ref.dtype)


def rms_norm_call(h, gain, out_dtype, tm=512):
    n, d = h.shape
    return pl.pallas_call(
        _norm_kernel,
        grid=(n // tm,),
        in_specs=[pl.BlockSpec((tm, d), lambda m: (m, 0)),
                  pl.BlockSpec((1, d), lambda m: (0, 0))],
        out_specs=pl.BlockSpec((tm, d), lambda m: (m, 0)),
        out_shape=jax.ShapeDtypeStruct((n, d), out_dtype),
        compiler_params=_cparams(("parallel",), 32),
        name="rms_norm",
    )(h, gain.reshape(1, d))


def _proj_kernel(x_ref, w_ref, o_ref, w_s, *, first_block_scale):
    @pl.when(pl.program_id(1) == 0)
    def _():
        w_s[...] = w_ref[...].astype(BF16)

    acc = jnp.dot(x_ref[...], w_s[...], preferred_element_type=F32)
    if first_block_scale is not None:
        acc = acc * jnp.where(pl.program_id(0) == 0, first_block_scale, 1.0).astype(F32)
    o_ref[...] = acc.astype(o_ref.dtype)


def proj_call(x, w, layer, col_block0, n_col_blocks, tn, out_dtype, first_block_scale=None, tm=1024, name="proj"):
    n, k = x.shape
    return pl.pallas_call(
        functools.partial(_proj_kernel, first_block_scale=first_block_scale),
        grid=(n_col_blocks, n // tm),
        in_specs=[pl.BlockSpec((tm, k), lambda j, m: (m, 0)),
                  pl.BlockSpec((None, k, tn), lambda j, m: (layer, 0, col_block0 + j))],
        out_specs=pl.BlockSpec((tm, tn), lambda j, m: (m, j)),
        out_shape=jax.ShapeDtypeStruct((n, n_col_blocks * tn), out_dtype),
        scratch_shapes=[pltpu.VMEM((k, tn), BF16)],
        compiler_params=_cparams(("arbitrary", "arbitrary"), 48),
        name=name,
    )(x, w)


def _pool_kernel(cur_ref, prev_ref, mix_ref, scale_ref, o_ref, *, tm, seq):
    m = pl.program_id(0)
    row0 = (m * tm) % seq
    cur = cur_ref[...]
    prev = jnp.where(row0 == 0, 0.0, prev_ref[...])
    pos = row0 + lax.broadcasted_iota(jnp.int32, (tm, 1), 0)
    outs = []
    for g, w in enumerate(POOL_WINDOWS):
        sl = slice(g * POOL_GROUP_DIM, (g + 1) * POOL_GROUP_DIM)
        x = jnp.concatenate([prev[:, sl], cur[:, sl]], axis=0)
        s, d = x, 1
        while d < w:
            s = s[:-d] + s[d:]
            d *= 2
        start = POOL_HALO - w + 1
        wsum = s[start:start + tm]
        count = jnp.minimum(pos + 1, w).astype(F32)
        pooled = wsum / count - cur[:, sl]
        mixed = jnp.dot(pooled.astype(BF16), mix_ref[g], preferred_element_type=F32)
        outs.append(mixed * scale_ref[:, sl])
    o_ref[...] = jnp.concatenate(outs, axis=1).astype(o_ref.dtype)


def pool_call(u, mix_bf16, scale, seq, tm=512):
    n, c = u.shape
    blocks_per_tile = tm // POOL_HALO
    return pl.pallas_call(
        functools.partial(_pool_kernel, tm=tm, seq=seq),
        grid=(n // tm,),
        in_specs=[pl.BlockSpec((tm, c), lambda m: (m, 0)),
                  pl.BlockSpec((POOL_HALO, c), lambda m: (jnp.maximum(m * blocks_per_tile - 1, 0), 0)),
                  pl.BlockSpec(mix_bf16.shape, lambda m: (0, 0, 0)),
                  pl.BlockSpec((1, c), lambda m: (0, 0))],
        out_specs=pl.BlockSpec((tm, c), lambda m: (m, 0)),
        out_shape=jax.ShapeDtypeStruct((n, c), BF16),
        compiler_params=_cparams(("parallel",), 32),
        name="pool_mixer",
    )(u, u, mix_bf16, scale.reshape(1, c))


N_BIAS_TILES = 4


def _bucket_tiles():
    kk = np.arange(TK)[:, None]
    qq = np.arange(TQ)[None, :]
    tiles = []
    for rel in (0, 1, 2, -1):
        n = rel * TK + qq - kk
        max_exact = REL_BUCKETS // 2
        nf = np.maximum(n, 1).astype(np.float64)
        large = max_exact + (np.log(nf / max_exact) / math.log(REL_MAX_DISTANCE / max_exact)
                             * (REL_BUCKETS - max_exact)).astype(np.int64)
        large = np.minimum(large, REL_BUCKETS - 1)
        bucket = np.where(n < max_exact, n, large)
        tiles.append(np.where(n < 0, -1, bucket))
    return np.stack(tiles).astype(np.int32)


def _bias_kernel(table_ref, bucket_ref, o_ref):
    h = pl.program_id(0)
    bucket = bucket_ref[...]
    acc = jnp.full(bucket.shape, NEG_INF, F32)
    for b in range(REL_BUCKETS):
        acc = jnp.where(bucket == b, table_ref[b * DIFF_HEADS + h] * LOG2E, acc)
    o_ref[...] = acc


def bias_tiles_call(rel_table):
    bucket = jnp.asarray(_bucket_tiles())
    return pl.pallas_call(
        _bias_kernel,
        grid=(DIFF_HEADS,),
        in_specs=[pl.BlockSpec(memory_space=pltpu.SMEM),
                  pl.BlockSpec(bucket.shape, lambda h: (0, 0, 0))],
        out_specs=pl.BlockSpec((None,) + bucket.shape, lambda h: (h, 0, 0, 0)),
        out_shape=jax.ShapeDtypeStruct((DIFF_HEADS,) + bucket.shape, F32),
        compiler_params=_cparams(("parallel",), 32),
        name="rel_bias_tiles",
    )(rel_table.reshape(-1), bucket)


def _attn_kernel(q_ref, k_ref, v_ref, bias_ref, lam_ref, gain_ref, o_ref, vt_ref, s_ref, acc_ref,
                 *, lambda_init, n_super):
    qi = pl.program_id(2)
    tks = KV_SUPER * TK
    heads = range(HEADS_PER_STEP)

    @pl.when(qi == 0)
    def _():
        extra = (lax.broadcasted_iota(jnp.int32, (V_ROWS - DIFF_V_DIM, tks), 0) == 0).astype(BF16)
        for hh in heads:
            cols = slice(hh * DIFF_V_DIM, (hh + 1) * DIFF_V_DIM)
            for c in range(n_super):
                vt = v_ref[c * tks:(c + 1) * tks, cols].astype(F32).T.astype(BF16)
                vt_ref[hh, c] = jnp.concatenate([vt, extra], axis=0)

    qds = []
    for hh in heads:
        q = q_ref[:, hh * DIFF_V_DIM:(hh + 1) * DIFF_V_DIM]
        lane = lax.broadcasted_iota(jnp.int32, q.shape, 1)
        zero = jnp.zeros_like(q)
        qds.append(jnp.concatenate([jnp.where(lane < DIFF_HEAD_DIM, q, zero),
                                    jnp.where(lane >= DIFF_HEAD_DIM, q, zero)], axis=0))

    def scores(t, hh):
        kb = k_ref[pl.ds(pl.multiple_of(t * tks, tks), tks), hh * DIFF_V_DIM:(hh + 1) * DIFF_V_DIM]
        s = lax.dot_general(kb, qds[hh], (((1,), (1,)), ((), ())), preferred_element_type=F32)
        parts = []
        for u in range(KV_SUPER):
            rel = qi - (t * KV_SUPER + u)
            bias = bias_ref[hh, jnp.where(rel < 0, N_BIAS_TILES - 1, jnp.minimum(rel, 2))]
            parts.append(s[u * TK:(u + 1) * TK] + jnp.concatenate([bias, bias], axis=1))
        s = jnp.concatenate(parts, axis=0)
        s_ref[hh] = s
        return jnp.max(s, axis=0, keepdims=True)

    def softmax_step(t, hh, m_prev, m_cur):
        alpha = jnp.exp2(m_prev - m_cur)
        p = jnp.exp2(s_ref[hh] - m_cur)
        pv = jnp.dot(vt_ref[hh, t], p.astype(BF16), preferred_element_type=F32)
        acc_ref[hh] = acc_ref[hh] * alpha + pv

    last = qi // KV_SUPER
    acc_ref[...] = jnp.zeros(acc_ref.shape, F32)
    neg = jnp.full((1, 2 * TQ), NEG_INF, F32)
    init = tuple((neg, jnp.maximum(neg, scores(0, hh))) for hh in heads)

    def body(t, carry):
        out = []
        for hh in heads:
            m_prev, m_cur = carry[hh]
            softmax_step(t, hh, m_prev, m_cur)
            m_next = jnp.maximum(m_cur, scores(t + 1, hh))
            out.append((m_cur, m_next))
        return tuple(out)

    carry = lax.fori_loop(0, last, body, init)

    lam_p = lam_ref[...]
    lam = (jnp.exp(jnp.sum(lam_p[0:1] * lam_p[1:2], axis=1, keepdims=True))
           - jnp.exp(jnp.sum(lam_p[2:3] * lam_p[3:4], axis=1, keepdims=True)) + lambda_init)
    for hh in heads:
        m_prev, m_cur = carry[hh]
        softmax_step(last, hh, m_prev, m_cur)
        acc = acc_ref[hh, :DIFF_V_DIM, :]
        l = acc_ref[hh, DIFF_V_DIM:DIFF_V_DIM + 1, :]
        o = acc[:, :TQ] / l[:, :TQ] - lam * (acc[:, TQ:] / l[:, TQ:])
        ms = jnp.mean(o * o, axis=0, keepdims=True)
        y = o * lax.rsqrt(ms + EPS) * gain_ref[...] * (1.0 - lambda_init)
        o_ref[:, hh * DIFF_V_DIM:(hh + 1) * DIFF_V_DIM] = y.T.astype(o_ref.dtype)


def attn_call(qkv, bias_tiles, lam_params, subln_gain, lambda_init, batch, seq):
    tks = KV_SUPER * TK
    n_super = seq // tks
    hps = HEADS_PER_STEP
    width = hps * DIFF_V_DIM
    groups = DIFF_HEADS // hps
    return pl.pallas_call(
        functools.partial(_attn_kernel, lambda_init=lambda_init, n_super=n_super),
        grid=(batch, groups, seq // TQ),
        in_specs=[pl.BlockSpec((None, TQ, width), lambda b, g, i: (b, i, g)),
                  pl.BlockSpec((None, seq, width), lambda b, g, i: (b, 0, groups + g)),
                  pl.BlockSpec((None, seq, width), lambda b, g, i: (b, 0, 2 * groups + g)),
                  pl.BlockSpec((hps, N_BIAS_TILES, TK, TQ), lambda b, g, i: (g, 0, 0, 0)),
                  pl.BlockSpec((4, DIFF_HEAD_DIM), lambda b, g, i: (0, 0)),
                  pl.BlockSpec((DIFF_V_DIM, 1), lambda b, g, i: (0, 0))],
        out_specs=pl.BlockSpec((None, TQ, width), lambda b, g, i: (b, i, g)),
        out_shape=jax.ShapeDtypeStruct((batch, seq, ATTN_WIDTH), BF16),
        scratch_shapes=[pltpu.VMEM((hps, n_super, V_ROWS, tks), BF16),
                        pltpu.VMEM((hps, tks, 2 * TQ), F32),
                        pltpu.VMEM((hps, V_ROWS, 2 * TQ), F32)],
        compiler_params=_cparams(("parallel", "parallel", "arbitrary"), 56),
        name="diff_attention",
    )(qkv, qkv, qkv, bias_tiles, lam_params, subln_gain.reshape(DIFF_V_DIM, 1))


def _sigmoid(x):
    return 1.0 / (1.0 + jnp.exp(-x))


def _merge_kernel(xn_ref, mixed_ref, attn_ref, wgp_ref, wga_ref, bgp_ref, bga_ref, wup_ref, wua_ref, z_ref,
                  wgp_s, wga_s, wup_s, wua_s):
    @pl.when(pl.program_id(1) == 0)
    def _():
        wgp_s[...] = wgp_ref[...].astype(BF16)
        wga_s[...] = wga_ref[...].astype(BF16)
        wup_s[...] = wup_ref[...].astype(BF16)
        wua_s[...] = wua_ref[...].astype(BF16)

    xn = xn_ref[...]
    g_pool = _sigmoid(jnp.dot(xn, wgp_s[...], preferred_element_type=F32) + bgp_ref[...])
    g_attn = _sigmoid(jnp.dot(xn, wga_s[...], preferred_element_type=F32) + bga_ref[...])
    y_pool = jnp.dot(mixed_ref[...], wup_s[...], preferred_element_type=F32)
    y_attn = jnp.dot(attn_ref[...], wua_s[...], preferred_element_type=F32)
    z_ref[...] = (g_pool * y_pool + g_attn * y_attn).astype(z_ref.dtype)


def merge_call(xn, mixed, attn, w_gate, b_gate, w_up_pool, w_up_attn, layer, tm=1024, tn=512):
    n, d = xn.shape
    nb = d // tn
    return pl.pallas_call(
        _merge_kernel,
        grid=(nb, n // tm),
        in_specs=[pl.BlockSpec((tm, d), lambda j, m: (m, 0)),
                  pl.BlockSpec((tm, POOL_WIDTH), lambda j, m: (m, 0)),
                  pl.BlockSpec((tm, ATTN_WIDTH), lambda j, m: (m, 0)),
                  pl.BlockSpec((None, d, tn), lambda j, m: (layer, 0, j)),
                  pl.BlockSpec((None, d, tn), lambda j, m: (layer, 0, nb + j)),
                  pl.BlockSpec((None, 1, tn), lambda j, m: (layer, 0, j)),
                  pl.BlockSpec((None, 1, tn), lambda j, m: (layer, 0, nb + j)),
                  pl.BlockSpec((None, POOL_WIDTH, tn), lambda j, m: (layer, 0, j)),
                  pl.BlockSpec((None, ATTN_WIDTH, tn), lambda j, m: (layer, 0, j))],
        out_specs=pl.BlockSpec((tm, tn), lambda j, m: (m, j)),
        out_shape=jax.ShapeDtypeStruct((n, d), BF16),
        scratch_shapes=[pltpu.VMEM((d, tn), BF16), pltpu.VMEM((d, tn), BF16),
                        pltpu.VMEM((POOL_WIDTH, tn), BF16), pltpu.VMEM((ATTN_WIDTH, tn), BF16)],
        compiler_params=_cparams(("arbitrary", "arbitrary"), 56),
        name="gated_merge",
    )(xn, mixed, attn, w_gate, w_gate, b_gate, b_gate, w_up_pool, w_up_attn)


HIGH_HALF = 0xFFFF0000


def _pack_halves(x):
    c = x.shape[1] // 2
    lo = lax.bitcast_convert_type(x[:, :c].astype(BF16).astype(F32), jnp.uint32)
    hi = lax.bitcast_convert_type(x[:, c:].astype(BF16).astype(F32), jnp.uint32)
    return (lo >> 16) | hi


def _unpack_halves(p):
    lo = lax.bitcast_convert_type(p << 16, F32)
    hi = lax.bitcast_convert_type(p & jnp.uint32(HIGH_HALF), F32)
    return lo, hi


ROW_TILE = 8


def _store_row_tiles(ref, packed):
    m = packed.shape[0]
    for s in range(ROW_TILE):
        ref[pl.ds(s, m, stride=ROW_TILE), :] = packed[:, s * LANES:(s + 1) * LANES]


def _load_row_tiles(ref):
    m = ref.shape[0] // ROW_TILE
    return jnp.concatenate([ref[pl.ds(s, m, stride=ROW_TILE), :] for s in range(ROW_TILE)], axis=1)


def _rows(ref, first, count=1):
    return ref.at[pl.ds(pl.multiple_of(first * ROW_TILE, ROW_TILE), count * ROW_TILE)]


def _outproj_kernel(z_ref, w_ref, h_ref, gain_ref, wr_ref, br_ref, h1_ref, hn_ref, logit_ref, w_s):
    @pl.when(pl.program_id(0) == 0)
    def _():
        w_s[...] = w_ref[...].astype(BF16)

    h1 = h_ref[...] + jnp.dot(z_ref[...], w_s[...], preferred_element_type=F32)
    h1_ref[...] = h1
    hn = _rms(h1, gain_ref[...])
    _store_row_tiles(hn_ref, _pack_halves(hn))
    logit_ref[...] = jnp.dot(hn.astype(BF16), wr_ref[...], preferred_element_type=F32) + br_ref[...]


def outproj_call(z, w_out, layer, h, gain, w_router, b_router, tm=256):
    n, d = h.shape
    return pl.pallas_call(
        _outproj_kernel,
        grid=(n // tm,),
        in_specs=[pl.BlockSpec((tm, d), lambda m: (m, 0)),
                  pl.BlockSpec((None, d, d), lambda m: (layer, 0, 0), pipeline_mode=pl.Buffered(1)),
                  pl.BlockSpec((tm, d), lambda m: (m, 0)),
                  pl.BlockSpec((1, d), lambda m: (0, 0)),
                  pl.BlockSpec((d, ROUTER_LANES), lambda m: (0, 0)),
                  pl.BlockSpec((1, ROUTER_LANES), lambda m: (0, 0))],
        out_specs=[pl.BlockSpec((tm, d), lambda m: (m, 0)),
                   pl.BlockSpec((tm * ROW_TILE, LANES), lambda m: (m, 0)),
                   pl.BlockSpec((tm, ROUTER_LANES), lambda m: (m, 0))],
        out_shape=[jax.ShapeDtypeStruct((n, d), F32),
                   jax.ShapeDtypeStruct((n * ROW_TILE, LANES), jnp.uint32),
                   jax.ShapeDtypeStruct((n, ROUTER_LANES), F32)],
        scratch_shapes=[pltpu.VMEM((d, d), BF16)],
        compiler_params=_cparams(("arbitrary",), 56),
        name="out_proj_norm_router",
    )(z, w_out, h, gain.reshape(1, d), w_router, b_router)


def _route(x, lane):
    big = float(ROUTER_LANES)

    def first_argmax(vals):
        top = jnp.max(vals, axis=1, keepdims=True)
        idx = jnp.min(jnp.where(vals == top, lane, big), axis=1, keepdims=True)
        return top, idx

    gmask = lane < N_GROUPS
    g_top, g_sel = first_argmax(jnp.where(gmask, x, -jnp.inf))
    g_weight = 1.0 / jnp.sum(jnp.where(gmask, jnp.exp(x - g_top), 0.0), axis=1, keepdims=True)
    lo = N_GROUPS + EXPERTS_PER_GROUP * g_sel
    e_vals = jnp.where((lane >= lo) & (lane < lo + EXPERTS_PER_GROUP), x, -jnp.inf)
    v1, i1 = first_argmax(e_vals)
    v2, i2 = first_argmax(jnp.where(lane == i1, -jnp.inf, e_vals))
    t = jnp.exp(v2 - v1)
    return i1 - N_GROUPS, i2 - N_GROUPS, g_weight / (1.0 + t), g_weight * t / (1.0 + t)


PLAN_BLOCK = 1024
PLAN_CHUNK = 256
META_TILE_EXPERT, META_NEXT_EXPERT, META_PAD_START, META_N_VALID = 0, 1, 2, 3


def _route_plan_kernel(logit_ref, w_ref, pos_ref, meta_ref, tri_s, cnt_s, base_s, off_s):
    phase = pl.program_id(0)
    blk = pl.program_id(1)
    tb = logit_ref.shape[0]
    lane = lax.broadcasted_iota(jnp.int32, (tb, ROUTER_LANES), 1).astype(F32)
    e0, e1, w0, w1 = _route(logit_ref[...], lane)
    onehot = jnp.where(lane == e0, 1.0, 0.0) + jnp.where(lane == e1, 1.0, 0.0)
    block_counts = jnp.sum(onehot, axis=0, keepdims=True)

    @pl.when(jnp.logical_and(phase == 0, blk == 0))
    def _():
        cnt_s[...] = jnp.zeros(cnt_s.shape, F32)

    @pl.when(phase == 0)
    def _():
        cnt_s[...] = cnt_s[...] + block_counts

    @pl.when(jnp.logical_and(phase == 1, blk == 0))
    def _():
        r = lax.broadcasted_iota(jnp.int32, (tb, tb), 0)
        c = lax.broadcasted_iota(jnp.int32, (tb, tb), 1)
        tri_s[...] = jnp.where(c < r, 1.0, 0.0).astype(BF16)
        lane1 = lane[0:1]
        cnt = cnt_s[...]
        tiles = jnp.floor((cnt + (TM_EXPERT - 1)) * (1.0 / TM_EXPERT))
        ri = lax.broadcasted_iota(jnp.int32, (ROUTER_LANES, ROUTER_LANES), 0)
        ci = lax.broadcasted_iota(jnp.int32, (ROUTER_LANES, ROUTER_LANES), 1)
        upper = jnp.where(ri <= ci, 1.0, 0.0).astype(BF16)
        ends = jnp.dot(jnp.broadcast_to(tiles, (8, ROUTER_LANES)).astype(BF16), upper,
                       preferred_element_type=F32)[0:1]
        off_s[...] = (ends - tiles) * TM_EXPERT
        base_s[...] = jnp.zeros(base_s.shape, F32)

        def pick(vec, e):
            return jnp.sum(jnp.where(lane1 == e, vec, 0.0), axis=1, keepdims=True)

        end_of = [pick(ends, e) for e in range(N_EXPERTS)]
        n_valid = end_of[N_EXPERTS - 1]

        def segment_of(tile):
            return sum(jnp.where(tile >= end_e, 1.0, 0.0) for end_e in end_of)

        tile_expert = jnp.where(lane1 < n_valid, segment_of(lane1), segment_of(n_valid - 1.0))
        following = sum(jnp.where(tile_expert == e, end_of[e], 0.0) for e in range(N_EXPERTS))
        next_expert = jnp.where(following < n_valid, segment_of(following), -1.0)
        rows = [tile_expert, next_expert, off_s[...] + cnt, jnp.broadcast_to(n_valid, (1, ROUTER_LANES))]
        rows += [jnp.zeros((1, ROUTER_LANES), F32)] * (meta_ref.shape[0] - len(rows))
        meta_ref[...] = jnp.concatenate(rows, axis=0).astype(jnp.int32)

    @pl.when(phase == 1)
    def _():
        before = jnp.dot(tri_s[...], onehot.astype(BF16), preferred_element_type=F32)
        row = before + base_s[...] + off_s[...]
        base_s[...] = base_s[...] + block_counts
        w_ref[...] = jnp.where(lane == 0, w0, jnp.where(lane == 1, w1, 0.0))
        eye = (lax.broadcasted_iota(jnp.int32, (LANES, LANES), 0)
               == lax.broadcasted_iota(jnp.int32, (LANES, LANES), 1))
        for j, e in enumerate((e0, e1)):
            col = jnp.sum(jnp.where(lane == e, row, 0.0), axis=1, keepdims=True)
            for q in range(tb // PLAN_CHUNK):
                parts = []
                for g in range(PLAN_CHUNK // LANES):
                    t0 = q * PLAN_CHUNK + g * LANES
                    square = jnp.broadcast_to(col[t0:t0 + LANES], (LANES, LANES))
                    parts.append(jnp.sum(jnp.where(eye, square, 0.0), axis=0, keepdims=True))
                pos_ref[j, q] = jnp.concatenate(parts, axis=1).astype(jnp.int32)


def route_plan_call(logits, n_tiles):
    n = logits.shape[0]
    tb = PLAN_BLOCK
    chunks = tb // PLAN_CHUNK
    assert n_tiles <= ROUTER_LANES and N_EXPERTS <= ROUTER_LANES
    return pl.pallas_call(
        _route_plan_kernel,
        grid=(2, n // tb),
        in_specs=[pl.BlockSpec((tb, ROUTER_LANES), lambda p, b: (b, 0))],
        out_specs=[pl.BlockSpec((tb, ROUTER_LANES), lambda p, b: (p * b, 0)),
                   pl.BlockSpec((2, chunks, 1, PLAN_CHUNK), lambda p, b: (0, p * b, 0, 0)),
                   pl.BlockSpec((8, ROUTER_LANES), lambda p, b: (0, 0))],
        out_shape=[jax.ShapeDtypeStruct((n, ROUTER_LANES), F32),
                   jax.ShapeDtypeStruct((2, n // PLAN_CHUNK, 1, PLAN_CHUNK), jnp.int32),
                   jax.ShapeDtypeStruct((8, ROUTER_LANES), jnp.int32)],
        scratch_shapes=[pltpu.VMEM((tb, tb), BF16), pltpu.VMEM((1, ROUTER_LANES), F32),
                        pltpu.VMEM((1, ROUTER_LANES), F32), pltpu.VMEM((1, ROUTER_LANES), F32)],
        compiler_params=_cparams(("arbitrary", "arbitrary"), 32),
        name="route_plan",
    )(logits)


def _dispatch_kernel(pad_ref, nv_ref, pos0_ref, pos1_ref, src_ref, dst_ref, zero_buf, sem, *, chunk, n_tiles):
    @pl.when(pl.program_id(0) == 0)
    def _():
        zero_buf[...] = jnp.zeros(zero_buf.shape, zero_buf.dtype)

        def fill(e):
            return pltpu.make_async_copy(zero_buf, _rows(dst_ref, pad_ref[e], TM_EXPERT), sem)

        for e in range(N_EXPERTS):
            fill(e).start()
        for e in range(N_EXPERTS):
            fill(e).wait()

        def fill_tile(i):
            return pltpu.make_async_copy(zero_buf, _rows(dst_ref, i * TM_EXPERT, TM_EXPERT), sem)

        def start_tile(i, c):
            fill_tile(i).start()
            return c

        def wait_tile(i, c):
            fill_tile(i).wait()
            return c

        lax.fori_loop(nv_ref[0], n_tiles, start_tile, 0)
        lax.fori_loop(nv_ref[0], n_tiles, wait_tile, 0)

    def start(t, c):
        for j, pos_ref in enumerate((pos0_ref, pos1_ref)):
            pltpu.make_async_copy(_rows(src_ref, t), _rows(dst_ref, pos_ref[0, 0, t]), sem).start(priority=j)
        return c

    lax.fori_loop(0, chunk, start, 0, unroll=ROW_DMA_UNROLL)
    all_rows = pltpu.make_async_copy(src_ref, _rows(dst_ref, 0, chunk), sem)
    for j in range(2):
        all_rows.wait()


def dispatch_call(hn, pos, pad_start, n_valid_tiles, n_tiles):
    n = hn.shape[0] // ROW_TILE
    chunk = PLAN_CHUNK
    grid_spec = pltpu.PrefetchScalarGridSpec(
        num_scalar_prefetch=2,
        grid=(n // chunk,),
        in_specs=[pl.BlockSpec((None, 1, 1, chunk), lambda i, pad, nv: (0, i, 0, 0), memory_space=pltpu.SMEM),
                  pl.BlockSpec((None, 1, 1, chunk), lambda i, pad, nv: (1, i, 0, 0), memory_space=pltpu.SMEM),
                  pl.BlockSpec((chunk * ROW_TILE, LANES), lambda i, pad, nv: (i, 0))],
        out_specs=pl.BlockSpec(memory_space=pl.ANY),
        scratch_shapes=[pltpu.VMEM((TM_EXPERT * ROW_TILE, LANES), hn.dtype), pltpu.SemaphoreType.DMA(())],
    )
    return pl.pallas_call(
        functools.partial(_dispatch_kernel, chunk=chunk, n_tiles=n_tiles),
        grid_spec=grid_spec,
        out_shape=jax.ShapeDtypeStruct((n_tiles * TM_EXPERT * ROW_TILE, LANES), hn.dtype),
        compiler_params=_cparams(("arbitrary",), 32),
        name="dispatch_rows",
    )(pad_start, n_valid_tiles, pos, pos, hn)


def _expert_kernel(te_ref, nv_ref, nx_ref, x_ref, wg_hbm, wu_hbm, wd_hbm, y_ref,
                   wg_f, wu_f, wd_f, wg_s, wu_s, wd_s, seg_ref, sems, *, layer):
    i = pl.program_id(0)
    valid = i < nv_ref[0]
    expert = te_ref[i]
    changed = jnp.logical_or(i == 0, expert != te_ref[jnp.maximum(i - 1, 0)])

    def fetch(e, slot):
        return [pltpu.make_async_copy(hbm.at[layer, e], buf.at[slot], sems.at[slot])
                for hbm, buf in ((wg_hbm, wg_f), (wu_hbm, wu_f), (wd_hbm, wd_f))]

    def swiglu_tile():
        lo, hi = _unpack_halves(_load_row_tiles(x_ref))
        x = jnp.concatenate([lo.astype(BF16), hi.astype(BF16)], axis=1)
        a = jnp.dot(x, wg_s[...], preferred_element_type=F32)
        b = jnp.dot(x, wu_s[...], preferred_element_type=F32)
        hmid = (a * _sigmoid(a) * b).astype(BF16)
        _store_row_tiles(y_ref, _pack_halves(jnp.dot(hmid, wd_s[...], preferred_element_type=F32)))

    @pl.when(i == 0)
    def _():
        seg_ref[0] = 0
        for cp in fetch(expert, 0):
            cp.start()

    @pl.when(jnp.logical_and(valid, changed))
    def _():
        slot = seg_ref[0] % 2
        for cp in fetch(expert, slot):
            cp.wait()
        nxt = nx_ref[i]

        @pl.when(nxt >= 0)
        def _():
            for cp in fetch(nxt, 1 - slot):
                cp.start()

        seg_ref[0] = seg_ref[0] + 1
        wg_s[...] = wg_f[slot].astype(BF16)
        wu_s[...] = wu_f[slot].astype(BF16)
        wd_s[...] = wd_f[slot].astype(BF16)
        swiglu_tile()

    @pl.when(jnp.logical_and(valid, jnp.logical_not(changed)))
    def _():
        swiglu_tile()

    @pl.when(jnp.logical_not(valid))
    def _():
        y_ref[...] = jnp.zeros(y_ref.shape, y_ref.dtype)


def expert_call(xs, tile_expert, n_valid_tiles, next_expert, w_gate, w_up, w_down, layer):
    p = xs.shape[0] // ROW_TILE
    d = 2 * ROW_TILE * LANES
    tm = TM_EXPERT
    f = w_gate.shape[-1]
    grid_spec = pltpu.PrefetchScalarGridSpec(
        num_scalar_prefetch=3,
        grid=(p // tm,),
        in_specs=[pl.BlockSpec((tm * ROW_TILE, LANES), lambda i, te, nv, nx: (jnp.minimum(i, nv[0] - 1), 0)),
                  pl.BlockSpec(memory_space=pl.ANY),
                  pl.BlockSpec(memory_space=pl.ANY),
                  pl.BlockSpec(memory_space=pl.ANY)],
        out_specs=pl.BlockSpec((tm * ROW_TILE, LANES), lambda i, te, nv, nx: (i, 0)),
        scratch_shapes=[pltpu.VMEM((2, d, f), F32), pltpu.VMEM((2, d, f), F32), pltpu.VMEM((2, f, d), F32),
                        pltpu.VMEM((d, f), BF16), pltpu.VMEM((d, f), BF16), pltpu.VMEM((f, d), BF16),
                        pltpu.SMEM((1,), jnp.int32), pltpu.SemaphoreType.DMA((2,))],
    )
    return pl.pallas_call(
        functools.partial(_expert_kernel, layer=layer),
        grid_spec=grid_spec,
        out_shape=jax.ShapeDtypeStruct(xs.shape, jnp.uint32),
        compiler_params=_cparams(("arbitrary",), 58),
        name="expert_swiglu",
    )(tile_expert, n_valid_tiles, next_expert, xs, w_gate, w_up, w_down)


def _combine_kernel(pos0_ref, pos1_ref, nxt0_ref, nxt1_ref, ys_ref, h_ref, w_ref, gain_ref, h2_ref, xn_ref,
                    buf, sems, *, chunk):
    i = pl.program_id(0)
    slot = i % 2

    def gather(p0_ref, p1_ref, into):
        def start(t, c):
            for j, pos_ref in enumerate((p0_ref, p1_ref)):
                pltpu.make_async_copy(_rows(ys_ref, pos_ref[0, 0, t]), _rows(buf.at[into, j], t),
                                      sems.at[into]).start(priority=j)
            return c

        lax.fori_loop(0, chunk, start, 0, unroll=ROW_DMA_UNROLL)

    @pl.when(i == 0)
    def _():
        gather(pos0_ref, pos1_ref, 0)

    @pl.when(i + 1 < pl.num_programs(0))
    def _():
        gather(nxt0_ref, nxt1_ref, 1 - slot)

    for j in range(2):
        pltpu.make_async_copy(_rows(ys_ref, 0, chunk), buf.at[slot, j], sems.at[slot]).wait()
    w = w_ref[...]
    lo0, hi0 = _unpack_halves(_load_row_tiles(buf.at[slot, 0]))
    lo1, hi1 = _unpack_halves(_load_row_tiles(buf.at[slot, 1]))
    y = jnp.concatenate([w[:, 0:1] * lo0 + w[:, 1:2] * lo1, w[:, 0:1] * hi0 + w[:, 1:2] * hi1], axis=1)
    h2 = h_ref[...] + y
    h2_ref[...] = h2
    xn_ref[...] = _rms(h2, gain_ref[...]).astype(xn_ref.dtype)


def combine_call(ys, pos, h1, weights, next_gain, xn_dtype, chunk=128):
    n, d = h1.shape
    per_row = PLAN_CHUNK // chunk
    steps = n // chunk

    def pos_spec(j, ahead):
        def index(c):
            c = jnp.minimum(c + ahead, steps - 1)
            return (j, c // per_row, 0, c % per_row)
        return pl.BlockSpec((None, 1, 1, chunk), index, memory_space=pltpu.SMEM)

    return pl.pallas_call(
        functools.partial(_combine_kernel, chunk=chunk),
        grid=(steps,),
        in_specs=[pos_spec(0, 0), pos_spec(1, 0), pos_spec(0, 1), pos_spec(1, 1),
                  pl.BlockSpec(memory_space=pl.ANY),
                  pl.BlockSpec((chunk, d), lambda c: (c, 0)),
                  pl.BlockSpec((chunk, ROUTER_LANES), lambda c: (c, 0)),
                  pl.BlockSpec((1, d), lambda c: (0, 0))],
        out_specs=[pl.BlockSpec((chunk, d), lambda c: (c, 0)),
                   pl.BlockSpec((chunk, d), lambda c: (c, 0))],
        out_shape=[jax.ShapeDtypeStruct((n, d), F32),
                   jax.ShapeDtypeStruct((n, d), xn_dtype)],
        scratch_shapes=[pltpu.VMEM((2, 2, chunk * ROW_TILE, LANES), jnp.uint32), pltpu.SemaphoreType.DMA((2,))],
        compiler_params=_cparams(("arbitrary",), 32),
        name="combine_rows",
    )(pos, pos, pos, pos, ys, h1, weights, next_gain.reshape(1, d))


def kernel(x, rel_bias_table, norm_mix_gain, w_in, w_merge_gate, b_merge_gate, pool_mix, pool_scale, w_up_pool, lambda_q1, lambda_k1, lambda_q2, lambda_k2, subln_gain, w_up_attn, w_out, norm_ffn_gain, w_router_group, b_router_group, w_router_expert, b_router_expert, w_expert_gate, w_expert_up, w_expert_down, final_norm_gain):
    batch, seq, d = x.shape
    depth = w_in.shape[0]
    n = batch * seq
    n_tiles = (2 * n + N_EXPERTS * (TM_EXPERT - 1)) // TM_EXPERT + 1

    bias_tiles = bias_tiles_call(rel_bias_table)
    h = x.reshape(n, d)
    xn = rms_norm_call(h, norm_mix_gain[0], BF16)
    q_scale = LOG2E * DIFF_HEAD_DIM ** -0.5

    for l in range(depth):
        u = proj_call(xn, w_in, l, 0, POOL_WIDTH // 512, 512, F32, name="in_proj_pool")
        qkv = proj_call(xn, w_in, l, 1, 3, ATTN_WIDTH, BF16, first_block_scale=q_scale, name="in_proj_qkv")
        mixed = pool_call(u, pool_mix[l].astype(BF16), pool_scale[l], seq)
        lambda_init = 0.8 - 0.6 * math.exp(-0.3 * l)
        lam_params = jnp.stack([lambda_q1[l], lambda_k1[l], lambda_q2[l], lambda_k2[l]])
        attn = attn_call(qkv.reshape(batch, seq, 3 * ATTN_WIDTH), bias_tiles, lam_params, subln_gain[l],
                         lambda_init, batch, seq).reshape(n, ATTN_WIDTH)
        z = merge_call(xn, mixed, attn, w_merge_gate, b_merge_gate.reshape(depth, 1, -1), w_up_pool, w_up_attn, l)

        w_router = jnp.concatenate(
            [w_router_group[l], jnp.transpose(w_router_expert[l], (1, 0, 2)).reshape(d, N_EXPERTS),
             jnp.zeros((d, ROUTER_LANES - N_GROUPS - N_EXPERTS), F32)], axis=1).astype(BF16)
        b_router = jnp.concatenate(
            [b_router_group[l], b_router_expert[l].reshape(-1),
             jnp.zeros((ROUTER_LANES - N_GROUPS - N_EXPERTS,), F32)]).reshape(1, ROUTER_LANES)
        h1, hn, logits = outproj_call(z, w_out, l, h, norm_ffn_gain[l], w_router, b_router)

        weights, pos, meta = route_plan_call(logits, n_tiles)
        tile_expert = meta[META_TILE_EXPERT, :n_tiles]
        next_expert = meta[META_NEXT_EXPERT, :n_tiles]
        pad_start = meta[META_PAD_START, :N_EXPERTS]
        n_valid = meta[META_N_VALID, :1]
        xs = dispatch_call(hn, pos, pad_start, n_valid, n_tiles)
        ys = expert_call(xs, tile_expert, n_valid, next_expert, w_expert_gate, w_expert_up, w_expert_down, l)
        last = l == depth - 1
        next_gain = final_norm_gain if last else norm_mix_gain[l + 1]
        h, xn = combine_call(ys, pos, h1, weights, next_gain, F32 if last else BF16)

    return xn.reshape(batch, seq, d)
```

```python
import functools
import math

import numpy as np
import jax
import jax.numpy as jnp
from jax import lax
from jax.experimental import pallas as pl
from jax.experimental.pallas import tpu as pltpu

F32 = jnp.float32
BF16 = jnp.bfloat16

D_MODEL = 2048
POOL_WIDTH = 1024
POOL_WINDOWS = (2, 4, 8, 16)
POOL_GROUP_DIM = 256
POOL_HALO = 16
DIFF_HEADS = 8
DIFF_HEAD_DIM = 64
DIFF_V_DIM = 128
ATTN_WIDTH = 1024
REL_BUCKETS = 32
REL_MAX_DISTANCE = 128
N_GROUPS = 4
EXPERTS_PER_GROUP = 8
N_EXPERTS = 32
D_EXPERT = 512
EPS = 1e-6
NEG_INF = -1e30
LOG2E = 1.4426950408889634

ROUTER_LANES = 128
TQ = 256
TQ_BLOCK = 256
TK = 256
KV_SUPER = 2
V_ROWS = DIFF_V_DIM + 16
HEADS_PER_STEP = 4
TM_EXPERT = 256
LANES = 128
ROW_DMA_UNROLL = 8
MIB = 1024 * 1024


def _cparams(sem, vmem_mib):
    return pltpu.CompilerParams(dimension_semantics=sem, vmem_limit_bytes=vmem_mib * MIB)


def _rms(xf, gain):
    ms = jnp.mean(xf * xf, axis=-1, keepdims=True)
    return xf * lax.rsqrt(ms + EPS) * gain


def _norm_kernel(h_ref, g_ref, o_ref):
    o_ref[...] = _rms(h_ref[...], g_ref[...]).astype(o_ref.dtype)


def rms_norm_call(h, gain, out_dtype, tm=512):
    n, d = h.shape
    return pl.pallas_call(
        _norm_kernel,
        grid=(n // tm,),
        in_specs=[pl.BlockSpec((tm, d), lambda m: (m, 0)),
                  pl.BlockSpec((1, d), lambda m: (0, 0))],
        out_specs=pl.BlockSpec((tm, d), lambda m: (m, 0)),
        out_shape=jax.ShapeDtypeStruct((n, d), out_dtype),
        compiler_params=_cparams(("parallel",), 32),
        name="rms_norm",
    )(h, gain.reshape(1, d))


def _proj_kernel(x_ref, w_ref, o_ref, w_s, *, first_block_scale):
    @pl.when(pl.program_id(1) == 0)
    def _():
        w_s[...] = w_ref[...].astype(BF16)

    acc = jnp.dot(x_ref[...], w_s[...], preferred_element_type=F32)
    if first_block_scale is not None:
        acc = acc * jnp.where(pl.program_id(0) == 0, first_block_scale, 1.0).astype(F32)
    o_ref[...] = acc.astype(o_ref.dtype)


def proj_call(x, w, layer, col_block0, n_col_blocks, tn, out_dtype, first_block_scale=None, tm=1024, name="proj"):
    n, k = x.shape
    return pl.pallas_call(
        functools.partial(_proj_kernel, first_block_scale=first_block_scale),
        grid=(n_col_blocks, n // tm),
        in_specs=[pl.BlockSpec((tm, k), lambda j, m: (m, 0)),
                  pl.BlockSpec((None, k, tn), lambda j, m: (layer, 0, col_block0 + j))],
        out_specs=pl.BlockSpec((tm, tn), lambda j, m: (m, j)),
        out_shape=jax.ShapeDtypeStruct((n, n_col_blocks * tn), out_dtype),
        scratch_shapes=[pltpu.VMEM((k, tn), BF16)],
        compiler_params=_cparams(("arbitrary", "arbitrary"), 48),
        name=name,
    )(x, w)


def _pool_kernel(cur_ref, prev_ref, mix_ref, scale_ref, o_ref, *, tm, seq):
    m = pl.program_id(0)
    row0 = (m * tm) % seq
    cur = cur_ref[...]
    prev = jnp.where(row0 == 0, 0.0, prev_ref[...])
    pos = row0 + lax.broadcasted_iota(jnp.int32, (tm, 1), 0)
    outs = []
    for g, w in enumerate(POOL_WINDOWS):
        sl = slice(g * POOL_GROUP_DIM, (g + 1) * POOL_GROUP_DIM)
        x = jnp.concatenate([prev[:, sl], cur[:, sl]], axis=0)
        s, d = x, 1
        while d < w:
            s = s[:-d] + s[d:]
            d *= 2
        start = POOL_HALO - w + 1
        wsum = s[start:start + tm]
        count = jnp.minimum(pos + 1, w).astype(F32)
        pooled = wsum / count - cur[:, sl]
        mixed = jnp.dot(pooled.astype(BF16), mix_ref[g], preferred_element_type=F32)
        outs.append(mixed * scale_ref[:, sl])
    o_ref[...] = jnp.concatenate(outs, axis=1).astype(o_ref.dtype)


def pool_call(u, mix_bf16, scale, seq, tm=512):
    n, c = u.shape
    blocks_per_tile = tm // POOL_HALO
    return pl.pallas_call(
        functools.partial(_pool_kernel, tm=tm, seq=seq),
        grid=(n // tm,),
        in_specs=[pl.BlockSpec((tm, c), lambda m: (m, 0)),
                  pl.BlockSpec((POOL_HALO, c), lambda m: (jnp.maximum(m * blocks_per_tile - 1, 0), 0)),
                  pl.BlockSpec(mix_bf16.shape, lambda m: (0, 0, 0)),
                  pl.BlockSpec((1, c), lambda m: (0, 0))],
        out_specs=pl.BlockSpec((tm, c), lambda m: (m, 0)),
        out_shape=jax.ShapeDtypeStruct((n, c), BF16),
        compiler_params=_cparams(("parallel",), 32),
        name="pool_mixer",
    )(u, u, mix_bf16, scale.reshape(1, c))


N_BIAS_TILES = 4


def _bucket_tiles():
    kk = np.arange(TK)[:, None]
    qq = np.arange(TQ_BLOCK)[None, :]
    tiles = []
    for rel in (0, 1, 2, -1):
        n = rel * TK + qq - kk
        max_exact = REL_BUCKETS // 2
        nf = np.maximum(n, 1).astype(np.float64)
        large = max_exact + (np.log(nf / max_exact) / math.log(REL_MAX_DISTANCE / max_exact)
                             * (REL_BUCKETS - max_exact)).astype(np.int64)
        large = np.minimum(large, REL_BUCKETS - 1)
        bucket = np.where(n < max_exact, n, large)
        tiles.append(np.where(n < 0, -1, bucket))
    return np.stack(tiles).astype(np.int32)


def _bias_kernel(table_ref, bucket_ref, o_ref):
    h = pl.program_id(0)
    bucket = bucket_ref[...]
    acc = jnp.full(bucket.shape, NEG_INF, F32)
    for b in range(REL_BUCKETS):
        acc = jnp.where(bucket == b, table_ref[b * DIFF_HEADS + h] * LOG2E, acc)
    o_ref[...] = acc


def bias_tiles_call(rel_table):
    bucket = jnp.asarray(_bucket_tiles())
    return pl.pallas_call(
        _bias_kernel,
        grid=(DIFF_HEADS,),
        in_specs=[pl.BlockSpec(memory_space=pltpu.SMEM),
                  pl.BlockSpec(bucket.shape, lambda h: (0, 0, 0))],
        out_specs=pl.BlockSpec((None,) + bucket.shape, lambda h: (h, 0, 0, 0)),
        out_shape=jax.ShapeDtypeStruct((DIFF_HEADS,) + bucket.shape, F32),
        compiler_params=_cparams(("parallel",), 32),
        name="rel_bias_tiles",
    )(rel_table.reshape(-1), bucket)


def _attn_kernel(q_ref, qn_ref, k_ref, v_ref, bias_ref, lam_ref, gain_ref, o_ref, vt_ref, s_ref, acc_ref, m0_ref,
                 *, lambda_init, n_super):
    qi = pl.program_id(2)
    tks = KV_SUPER * TK
    heads = range(HEADS_PER_STEP)

    def block_diag(ref, hh):
        q = ref[:, hh * DIFF_V_DIM:(hh + 1) * DIFF_V_DIM]
        lane = lax.broadcasted_iota(jnp.int32, q.shape, 1)
        zero = jnp.zeros_like(q)
        return jnp.concatenate([jnp.where(lane < DIFF_HEAD_DIM, q, zero),
                                jnp.where(lane >= DIFF_HEAD_DIM, q, zero)], axis=0)

    def scores(t, hh, qd, tile):
        kb = k_ref[pl.ds(pl.multiple_of(t * tks, tks), tks), hh * DIFF_V_DIM:(hh + 1) * DIFF_V_DIM]
        s = lax.dot_general(kb, qd, (((1,), (1,)), ((), ())), preferred_element_type=F32)
        parts = []
        for u in range(KV_SUPER):
            tiles = []
            for c in range(TQ // TQ_BLOCK):
                rel = tile * (TQ // TQ_BLOCK) + c - (t * KV_SUPER + u)
                tiles.append(bias_ref[hh, jnp.where(rel < 0, N_BIAS_TILES - 1, jnp.minimum(rel, 2))])
            parts.append(s[u * TK:(u + 1) * TK] + jnp.concatenate(tiles + tiles, axis=1))
        s = jnp.concatenate(parts, axis=0)
        s_ref[hh] = s
        return jnp.max(s, axis=0, keepdims=True)

    @pl.when(qi == 0)
    def _():
        extra = (lax.broadcasted_iota(jnp.int32, (V_ROWS - DIFF_V_DIM, tks), 0) == 0).astype(BF16)
        for hh in heads:
            cols = slice(hh * DIFF_V_DIM, (hh + 1) * DIFF_V_DIM)
            for c in range(n_super):
                vt = v_ref[c * tks:(c + 1) * tks, cols].astype(F32).T.astype(BF16)
                vt_ref[hh, c] = jnp.concatenate([vt, extra], axis=0)
            m0_ref[hh] = scores(0, hh, block_diag(q_ref, hh), qi)

    qds = [block_diag(q_ref, hh) for hh in heads]

    def softmax_step(t, hh, m_prev, m_cur):
        alpha = jnp.exp2(m_prev - m_cur)
        p = jnp.exp2(s_ref[hh] - m_cur)
        pv = jnp.dot(vt_ref[hh, t], p.astype(BF16), preferred_element_type=F32)
        acc_ref[hh] = acc_ref[hh] * alpha + pv

    last = ((qi + 1) * TQ - 1) // tks
    acc_ref[...] = jnp.zeros(acc_ref.shape, F32)
    neg = jnp.full((1, 2 * TQ), NEG_INF, F32)
    init = tuple((neg, jnp.maximum(neg, m0_ref[hh])) for hh in heads)

    def body(t, carry):
        out = []
        for hh in heads:
            m_prev, m_cur = carry[hh]
            softmax_step(t, hh, m_prev, m_cur)
            m_next = jnp.maximum(m_cur, scores(t + 1, hh, qds[hh], qi))
            out.append((m_cur, m_next))
        return tuple(out)

    carry = lax.fori_loop(0, last, body, init)

    lam_p = lam_ref[...]
    lam = (jnp.exp(jnp.sum(lam_p[0:1] * lam_p[1:2], axis=1, keepdims=True))
           - jnp.exp(jnp.sum(lam_p[2:3] * lam_p[3:4], axis=1, keepdims=True)) + lambda_init)
    for hh in heads:
        m_prev, m_cur = carry[hh]
        softmax_step(last, hh, m_prev, m_cur)
        m0_ref[hh] = scores(0, hh, block_diag(qn_ref, hh), qi + 1)
        acc = acc_ref[hh, :DIFF_V_DIM, :]
        l = acc_ref[hh, DIFF_V_DIM:DIFF_V_DIM + 1, :]
        o = acc[:, :TQ] / l[:, :TQ] - lam * (acc[:, TQ:] / l[:, TQ:])
        ms = jnp.mean(o * o, axis=0, keepdims=True)
        y = o * lax.rsqrt(ms + EPS) * gain_ref[...] * (1.0 - lambda_init)
        o_ref[:, hh * DIFF_V_DIM:(hh + 1) * DIFF_V_DIM] = y.T.astype(o_ref.dtype)


def attn_call(qkv, bias_tiles, lam_params, subln_gain, lambda_init, batch, seq):
    tks = KV_SUPER * TK
    n_super = seq // tks
    hps = HEADS_PER_STEP
    width = hps * DIFF_V_DIM
    groups = DIFF_HEADS // hps
    n_q = seq // TQ
    return pl.pallas_call(
        functools.partial(_attn_kernel, lambda_init=lambda_init, n_super=n_super),
        grid=(batch, groups, n_q),
        in_specs=[pl.BlockSpec((None, TQ, width), lambda b, g, i: (b, i, g)),
                  pl.BlockSpec((None, TQ, width), lambda b, g, i: (b, jnp.minimum(i + 1, n_q - 1), g)),
                  pl.BlockSpec((None, seq, width), lambda b, g, i: (b, 0, groups + g)),
                  pl.BlockSpec((None, seq, width), lambda b, g, i: (b, 0, 2 * groups + g)),
                  pl.BlockSpec((hps, N_BIAS_TILES, TK, TQ_BLOCK), lambda b, g, i: (g, 0, 0, 0)),
                  pl.BlockSpec((4, DIFF_HEAD_DIM), lambda b, g, i: (0, 0)),
                  pl.BlockSpec((DIFF_V_DIM, 1), lambda b, g, i: (0, 0))],
        out_specs=pl.BlockSpec((None, TQ, width), lambda b, g, i: (b, i, g)),
        out_shape=jax.ShapeDtypeStruct((batch, seq, ATTN_WIDTH), BF16),
        scratch_shapes=[pltpu.VMEM((hps, n_super, V_ROWS, tks), BF16),
                        pltpu.VMEM((hps, tks, 2 * TQ), F32),
                        pltpu.VMEM((hps, V_ROWS, 2 * TQ), F32),
                        pltpu.VMEM((hps, 1, 2 * TQ), F32)],
        compiler_params=_cparams(("parallel", "parallel", "arbitrary"), 56),
        name="diff_attention",
    )(qkv, qkv, qkv, qkv, bias_tiles, lam_params, subln_gain.reshape(DIFF_V_DIM, 1))


def _sigmoid(x):
    return 1.0 / (1.0 + jnp.exp(-x))


def _merge_kernel(xn_ref, mixed_ref, attn_ref, wgp_ref, wga_ref, bgp_ref, bga_ref, wup_ref, wua_ref, z_ref,
                  wgp_s, wga_s, wup_s, wua_s):
    @pl.when(pl.program_id(1) == 0)
    def _():
        wgp_s[...] = wgp_ref[...].astype(BF16)
        wga_s[...] = wga_ref[...].astype(BF16)
        wup_s[...] = wup_ref[...].astype(BF16)
        wua_s[...] = wua_ref[...].astype(BF16)

    xn = xn_ref[...]
    g_pool = _sigmoid(jnp.dot(xn, wgp_s[...], preferred_element_type=F32) + bgp_ref[...])
    g_attn = _sigmoid(jnp.dot(xn, wga_s[...], preferred_element_type=F32) + bga_ref[...])
    y_pool = jnp.dot(mixed_ref[...], wup_s[...], preferred_element_type=F32)
    y_attn = jnp.dot(attn_ref[...], wua_s[...], preferred_element_type=F32)
    z_ref[...] = (g_pool * y_pool + g_attn * y_attn).astype(z_ref.dtype)


def merge_call(xn, mixed, attn, w_gate, b_gate, w_up_pool, w_up_attn, layer, tm=1024, tn=512):
    n, d = xn.shape
    nb = d // tn
    return pl.pallas_call(
        _merge_kernel,
        grid=(nb, n // tm),
        in_specs=[pl.BlockSpec((tm, d), lambda j, m: (m, 0)),
                  pl.BlockSpec((tm, POOL_WIDTH), lambda j, m: (m, 0)),
                  pl.BlockSpec((tm, ATTN_WIDTH), lambda j, m: (m, 0)),
                  pl.BlockSpec((None, d, tn), lambda j, m: (layer, 0, j)),
                  pl.BlockSpec((None, d, tn), lambda j, m: (layer, 0, nb + j)),
                  pl.BlockSpec((None, 1, tn), lambda j, m: (layer, 0, j)),
                  pl.BlockSpec((None, 1, tn), lambda j, m: (layer, 0, nb + j)),
                  pl.BlockSpec((None, POOL_WIDTH, tn), lambda j, m: (layer, 0, j)),
                  pl.BlockSpec((None, ATTN_WIDTH, tn), lambda j, m: (layer, 0, j))],
        out_specs=pl.BlockSpec((tm, tn), lambda j, m: (m, j)),
        out_shape=jax.ShapeDtypeStruct((n, d), BF16),
        scratch_shapes=[pltpu.VMEM((d, tn), BF16), pltpu.VMEM((d, tn), BF16),
                        pltpu.VMEM((POOL_WIDTH, tn), BF16), pltpu.VMEM((ATTN_WIDTH, tn), BF16)],
        compiler_params=_cparams(("arbitrary", "arbitrary"), 56),
        name="gated_merge",
    )(xn, mixed, attn, w_gate, w_gate, b_gate, b_gate, w_up_pool, w_up_attn)


HIGH_HALF = 0xFFFF0000


def _pack_halves(x):
    c = x.shape[1] // 2
    lo = lax.bitcast_convert_type(x[:, :c].astype(BF16).astype(F32), jnp.uint32)
    hi = lax.bitcast_convert_type(x[:, c:].astype(BF16).astype(F32), jnp.uint32)
    return (lo >> 16) | hi


def _unpack_halves(p):
    lo = lax.bitcast_convert_type(p << 16, F32)
    hi = lax.bitcast_convert_type(p & jnp.uint32(HIGH_HALF), F32)
    return lo, hi


ROW_TILE = 8


def _store_row_tiles(ref, packed):
    m = packed.shape[0]
    for s in range(ROW_TILE):
        ref[pl.ds(s, m, stride=ROW_TILE), :] = packed[:, s * LANES:(s + 1) * LANES]


def _load_row_tiles(ref):
    m = ref.shape[0] // ROW_TILE
    return jnp.concatenate([ref[pl.ds(s, m, stride=ROW_TILE), :] for s in range(ROW_TILE)], axis=1)


def _rows(ref, first, count=1):
    return ref.at[pl.ds(pl.multiple_of(first * ROW_TILE, ROW_TILE), count * ROW_TILE)]


def _outproj_kernel(z_ref, w_ref, h_ref, gain_ref, wr_ref, br_ref, h1_ref, hn_ref, logit_ref, w_s):
    @pl.when(pl.program_id(0) == 0)
    def _():
        w_s[...] = w_ref[...].astype(BF16)

    h1 = h_ref[...] + jnp.dot(z_ref[...], w_s[...], preferred_element_type=F32)
    h1_ref[...] = h1
    hn = _rms(h1, gain_ref[...])
    _store_row_tiles(hn_ref, _pack_halves(hn))
    logit_ref[...] = jnp.dot(hn.astype(BF16), wr_ref[...], preferred_element_type=F32) + br_ref[...]


def outproj_call(z, w_out, layer, h, gain, w_router, b_router, tm=256):
    n, d = h.shape
    return pl.pallas_call(
        _outproj_kernel,
        grid=(n // tm,),
        in_specs=[pl.BlockSpec((tm, d), lambda m: (m, 0)),
                  pl.BlockSpec((None, d, d), lambda m: (layer, 0, 0), pipeline_mode=pl.Buffered(1)),
                  pl.BlockSpec((tm, d), lambda m: (m, 0)),
                  pl.BlockSpec((1, d), lambda m: (0, 0)),
                  pl.BlockSpec((d, ROUTER_LANES), lambda m: (0, 0)),
                  pl.BlockSpec((1, ROUTER_LANES), lambda m: (0, 0))],
        out_specs=[pl.BlockSpec((tm, d), lambda m: (m, 0)),
                   pl.BlockSpec((tm * ROW_TILE, LANES), lambda m: (m, 0)),
                   pl.BlockSpec((tm, ROUTER_LANES), lambda m: (m, 0))],
        out_shape=[jax.ShapeDtypeStruct((n, d), F32),
                   jax.ShapeDtypeStruct((n * ROW_TILE, LANES), jnp.uint32),
                   jax.ShapeDtypeStruct((n, ROUTER_LANES), F32)],
        scratch_shapes=[pltpu.VMEM((d, d), BF16)],
        compiler_params=_cparams(("arbitrary",), 56),
        name="out_proj_norm_router",
    )(z, w_out, h, gain.reshape(1, d), w_router, b_router)


def _route(x, lane):
    big = float(ROUTER_LANES)

    def first_argmax(vals):
        top = jnp.max(vals, axis=1, keepdims=True)
        idx = jnp.min(jnp.where(vals == top, lane, big), axis=1, keepdims=True)
        return top, idx

    gmask = lane < N_GROUPS
    g_top, g_sel = first_argmax(jnp.where(gmask, x, -jnp.inf))
    g_weight = 1.0 / jnp.sum(jnp.where(gmask, jnp.exp(x - g_top), 0.0), axis=1, keepdims=True)
    lo = N_GROUPS + EXPERTS_PER_GROUP * g_sel
    e_vals = jnp.where((lane >= lo) & (lane < lo + EXPERTS_PER_GROUP), x, -jnp.inf)
    v1, i1 = first_argmax(e_vals)
    v2, i2 = first_argmax(jnp.where(lane == i1, -jnp.inf, e_vals))
    t = jnp.exp(v2 - v1)
    return i1 - N_GROUPS, i2 - N_GROUPS, g_weight / (1.0 + t), g_weight * t / (1.0 + t)


PLAN_BLOCK = 1024
PLAN_CHUNK = 256
META_TILE_EXPERT, META_NEXT_EXPERT, META_PAD_START, META_N_VALID = 0, 1, 2, 3


def _route_plan_kernel(logit_ref, w_ref, pos_ref, meta_ref, tri_s, cnt_s, base_s, off_s):
    phase = pl.program_id(0)
    blk = pl.program_id(1)
    tb = logit_ref.shape[0]
    lane = lax.broadcasted_iota(jnp.int32, (tb, ROUTER_LANES), 1).astype(F32)
    e0, e1, w0, w1 = _route(logit_ref[...], lane)
    onehot = jnp.where(lane == e0, 1.0, 0.0) + jnp.where(lane == e1, 1.0, 0.0)
    block_counts = jnp.sum(onehot, axis=0, keepdims=True)

    @pl.when(jnp.logical_and(phase == 0, blk == 0))
    def _():
        cnt_s[...] = jnp.zeros(cnt_s.shape, F32)

    @pl.when(phase == 0)
    def _():
        cnt_s[...] = cnt_s[...] + block_counts

    @pl.when(jnp.logical_and(phase == 1, blk == 0))
    def _():
        r = lax.broadcasted_iota(jnp.int32, (tb, tb), 0)
        c = lax.broadcasted_iota(jnp.int32, (tb, tb), 1)
        tri_s[...] = jnp.where(c < r, 1.0, 0.0).astype(BF16)
        lane1 = lane[0:1]
        cnt = cnt_s[...]
        tiles = jnp.floor((cnt + (TM_EXPERT - 1)) * (1.0 / TM_EXPERT))
        ri = lax.broadcasted_iota(jnp.int32, (ROUTER_LANES, ROUTER_LANES), 0)
        ci = lax.broadcasted_iota(jnp.int32, (ROUTER_LANES, ROUTER_LANES), 1)
        upper = jnp.where(ri <= ci, 1.0, 0.0).astype(BF16)
        ends = jnp.dot(jnp.broadcast_to(tiles, (8, ROUTER_LANES)).astype(BF16), upper,
                       preferred_element_type=F32)[0:1]
        off_s[...] = (ends - tiles) * TM_EXPERT
        base_s[...] = jnp.zeros(base_s.shape, F32)

        def pick(vec, e):
            return jnp.sum(jnp.where(lane1 == e, vec, 0.0), axis=1, keepdims=True)

        end_of = [pick(ends, e) for e in range(N_EXPERTS)]
        n_valid = end_of[N_EXPERTS - 1]

        def segment_of(tile):
            return sum(jnp.where(tile >= end_e, 1.0, 0.0) for end_e in end_of)

        tile_expert = jnp.where(lane1 < n_valid, segment_of(lane1), segment_of(n_valid - 1.0))
        following = sum(jnp.where(tile_expert == e, end_of[e], 0.0) for e in range(N_EXPERTS))
        next_expert = jnp.where(following < n_valid, segment_of(following), -1.0)
        rows = [tile_expert, next_expert, off_s[...] + cnt, jnp.broadcast_to(n_valid, (1, ROUTER_LANES))]
        rows += [jnp.zeros((1, ROUTER_LANES), F32)] * (meta_ref.shape[0] - len(rows))
        meta_ref[...] = jnp.concatenate(rows, axis=0).astype(jnp.int32)

    @pl.when(phase == 1)
    def _():
        before = jnp.dot(tri_s[...], onehot.astype(BF16), preferred_element_type=F32)
        row = before + base_s[...] + off_s[...]
        base_s[...] = base_s[...] + block_counts
        w_ref[...] = jnp.where(lane == 0, w0, jnp.where(lane == 1, w1, 0.0))
        eye = (lax.broadcasted_iota(jnp.int32, (LANES, LANES), 0)
               == lax.broadcasted_iota(jnp.int32, (LANES, LANES), 1))
        for j, e in enumerate((e0, e1)):
            col = jnp.sum(jnp.where(lane == e, row, 0.0), axis=1, keepdims=True)
            for q in range(tb // PLAN_CHUNK):
                parts = []
                for g in range(PLAN_CHUNK // LANES):
                    t0 = q * PLAN_CHUNK + g * LANES
                    square = jnp.broadcast_to(col[t0:t0 + LANES], (LANES, LANES))
                    parts.append(jnp.sum(jnp.where(eye, square, 0.0), axis=0, keepdims=True))
                pos_ref[j, q] = jnp.concatenate(parts, axis=1).astype(jnp.int32)


def route_plan_call(logits, n_tiles):
    n = logits.shape[0]
    tb = PLAN_BLOCK
    chunks = tb // PLAN_CHUNK
    assert n_tiles <= ROUTER_LANES and N_EXPERTS <= ROUTER_LANES
    return pl.pallas_call(
        _route_plan_kernel,
        grid=(2, n // tb),
        in_specs=[pl.BlockSpec((tb, ROUTER_LANES), lambda p, b: (b, 0))],
        out_specs=[pl.BlockSpec((tb, ROUTER_LANES), lambda p, b: (p * b, 0)),
                   pl.BlockSpec((2, chunks, 1, PLAN_CHUNK), lambda p, b: (0, p * b, 0, 0)),
                   pl.BlockSpec((8, ROUTER_LANES), lambda p, b: (0, 0))],
        out_shape=[jax.ShapeDtypeStruct((n, ROUTER_LANES), F32),
                   jax.ShapeDtypeStruct((2, n // PLAN_CHUNK, 1, PLAN_CHUNK), jnp.int32),
                   jax.ShapeDtypeStruct((8, ROUTER_LANES), jnp.int32)],
        scratch_shapes=[pltpu.VMEM((tb, tb), BF16), pltpu.VMEM((1, ROUTER_LANES), F32),
                        pltpu.VMEM((1, ROUTER_LANES), F32), pltpu.VMEM((1, ROUTER_LANES), F32)],
        compiler_params=_cparams(("arbitrary", "arbitrary"), 32),
        name="route_plan",
    )(logits)


def _dispatch_kernel(pad_ref, nv_ref, pos0_ref, pos1_ref, src_ref, dst_ref, zero_buf, sem, *, chunk, n_tiles):
    @pl.when(pl.program_id(0) == 0)
    def _():
        zero_buf[...] = jnp.zeros(zero_buf.shape, zero_buf.dtype)

        def fill(e):
            return pltpu.make_async_copy(zero_buf, _rows(dst_ref, pad_ref[e], TM_EXPERT), sem)

        for e in range(N_EXPERTS):
            fill(e).start()
        for e in range(N_EXPERTS):
            fill(e).wait()

        def fill_tile(i):
            return pltpu.make_async_copy(zero_buf, _rows(dst_ref, i * TM_EXPERT, TM_EXPERT), sem)

        def start_tile(i, c):
            fill_tile(i).start()
            return c

        def wait_tile(i, c):
            fill_tile(i).wait()
            return c

        lax.fori_loop(nv_ref[0], n_tiles, start_tile, 0)
        lax.fori_loop(nv_ref[0], n_tiles, wait_tile, 0)

    def start(t, c):
        for j, pos_ref in enumerate((pos0_ref, pos1_ref)):
            pltpu.make_async_copy(_rows(src_ref, t), _rows(dst_ref, pos_ref[0, 0, t]), sem).start(priority=j)
        return c

    lax.fori_loop(0, chunk, start, 0, unroll=ROW_DMA_UNROLL)
    all_rows = pltpu.make_async_copy(src_ref, _rows(dst_ref, 0, chunk), sem)
    for j in range(2):
        all_rows.wait()


def dispatch_call(hn, pos, pad_start, n_valid_tiles, n_tiles):
    n = hn.shape[0] // ROW_TILE
    chunk = PLAN_CHUNK
    grid_spec = pltpu.PrefetchScalarGridSpec(
        num_scalar_prefetch=2,
        grid=(n // chunk,),
        in_specs=[pl.BlockSpec((None, 1, 1, chunk), lambda i, pad, nv: (0, i, 0, 0), memory_space=pltpu.SMEM),
                  pl.BlockSpec((None, 1, 1, chunk), lambda i, pad, nv: (1, i, 0, 0), memory_space=pltpu.SMEM),
                  pl.BlockSpec((chunk * ROW_TILE, LANES), lambda i, pad, nv: (i, 0))],
        out_specs=pl.BlockSpec(memory_space=pl.ANY),
        scratch_shapes=[pltpu.VMEM((TM_EXPERT * ROW_TILE, LANES), hn.dtype), pltpu.SemaphoreType.DMA(())],
    )
    return pl.pallas_call(
        functools.partial(_dispatch_kernel, chunk=chunk, n_tiles=n_tiles),
        grid_spec=grid_spec,
        out_shape=jax.ShapeDtypeStruct((n_tiles * TM_EXPERT * ROW_TILE, LANES), hn.dtype),
        compiler_params=_cparams(("arbitrary",), 32),
        name="dispatch_rows",
    )(pad_start, n_valid_tiles, pos, pos, hn)


def _expert_kernel(te_ref, nv_ref, nx_ref, x_ref, wg_hbm, wu_hbm, wd_hbm, y_ref,
                   wg_f, wu_f, wd_f, wg_s, wu_s, wd_s, seg_ref, sems, *, layer):
    i = pl.program_id(0)
    valid = i < nv_ref[0]
    expert = te_ref[i]
    changed = jnp.logical_or(i == 0, expert != te_ref[jnp.maximum(i - 1, 0)])

    def fetch(e, slot):
        return [pltpu.make_async_copy(hbm.at[layer, e], buf.at[slot], sems.at[slot])
                for hbm, buf in ((wg_hbm, wg_f), (wu_hbm, wu_f), (wd_hbm, wd_f))]

    def swiglu_tile():
        lo, hi = _unpack_halves(_load_row_tiles(x_ref))
        x = jnp.concatenate([lo.astype(BF16), hi.astype(BF16)], axis=1)
        a = jnp.dot(x, wg_s[...], preferred_element_type=F32)
        b = jnp.dot(x, wu_s[...], preferred_element_type=F32)
        hmid = (a * _sigmoid(a) * b).astype(BF16)
        _store_row_tiles(y_ref, _pack_halves(jnp.dot(hmid, wd_s[...], preferred_element_type=F32)))

    @pl.when(i == 0)
    def _():
        seg_ref[0] = 0
        for cp in fetch(expert, 0):
            cp.start()

    @pl.when(jnp.logical_and(valid, changed))
    def _():
        slot = seg_ref[0] % 2
        for cp in fetch(expert, slot):
            cp.wait()
        nxt = nx_ref[i]

        @pl.when(nxt >= 0)
        def _():
            for cp in fetch(nxt, 1 - slot):
                cp.start()

        seg_ref[0] = seg_ref[0] + 1
        wg_s[...] = wg_f[slot].astype(BF16)
        wu_s[...] = wu_f[slot].astype(BF16)
        wd_s[...] = wd_f[slot].astype(BF16)
        swiglu_tile()

    @pl.when(jnp.logical_and(valid, jnp.logical_not(changed)))
    def _():
        swiglu_tile()

    @pl.when(jnp.logical_not(valid))
    def _():
        y_ref[...] = jnp.zeros(y_ref.shape, y_ref.dtype)


def expert_call(xs, tile_expert, n_valid_tiles, next_expert, w_gate, w_up, w_down, layer):
    p = xs.shape[0] // ROW_TILE
    d = 2 * ROW_TILE * LANES
    tm = TM_EXPERT
    f = w_gate.shape[-1]
    grid_spec = pltpu.PrefetchScalarGridSpec(
        num_scalar_prefetch=3,
        grid=(p // tm,),
        in_specs=[pl.BlockSpec((tm * ROW_TILE, LANES), lambda i, te, nv, nx: (jnp.minimum(i, nv[0] - 1), 0)),
                  pl.BlockSpec(memory_space=pl.ANY),
                  pl.BlockSpec(memory_space=pl.ANY),
                  pl.BlockSpec(memory_space=pl.ANY)],
        out_specs=pl.BlockSpec((tm * ROW_TILE, LANES), lambda i, te, nv, nx: (i, 0)),
        scratch_shapes=[pltpu.VMEM((2, d, f), F32), pltpu.VMEM((2, d, f), F32), pltpu.VMEM((2, f, d), F32),
                        pltpu.VMEM((d, f), BF16), pltpu.VMEM((d, f), BF16), pltpu.VMEM((f, d), BF16),
                        pltpu.SMEM((1,), jnp.int32), pltpu.SemaphoreType.DMA((2,))],
    )
    return pl.pallas_call(
        functools.partial(_expert_kernel, layer=layer),
        grid_spec=grid_spec,
        out_shape=jax.ShapeDtypeStruct(xs.shape, jnp.uint32),
        compiler_params=_cparams(("arbitrary",), 58),
        name="expert_swiglu",
    )(tile_expert, n_valid_tiles, next_expert, xs, w_gate, w_up, w_down)


def _combine_kernel(pos0_ref, pos1_ref, nxt0_ref, nxt1_ref, ys_ref, h_ref, w_ref, gain_ref, h2_ref, xn_ref,
                    buf, sems, *, chunk):
    i = pl.program_id(0)
    slot = i % 2

    def gather(p0_ref, p1_ref, into):
        def start(t, c):
            for j, pos_ref in enumerate((p0_ref, p1_ref)):
                pltpu.make_async_copy(_rows(ys_ref, pos_ref[0, 0, t]), _rows(buf.at[into, j], t),
                                      sems.at[into]).start(priority=j)
            return c

        lax.fori_loop(0, chunk, start, 0, unroll=ROW_DMA_UNROLL)

    @pl.when(i == 0)
    def _():
        gather(pos0_ref, pos1_ref, 0)

    @pl.when(i + 1 < pl.num_programs(0))
    def _():
        gather(nxt0_ref, nxt1_ref, 1 - slot)

    for j in range(2):
        pltpu.make_async_copy(_rows(ys_ref, 0, chunk), buf.at[slot, j], sems.at[slot]).wait()
    w = w_ref[...]
    lo0, hi0 = _unpack_halves(_load_row_tiles(buf.at[slot, 0]))
    lo1, hi1 = _unpack_halves(_load_row_tiles(buf.at[slot, 1]))
    y = jnp.concatenate([w[:, 0:1] * lo0 + w[:, 1:2] * lo1, w[:, 0:1] * hi0 + w[:, 1:2] * hi1], axis=1)
    h2 = h_ref[...] + y
    h2_ref[...] = h2
    xn_ref[...] = _rms(h2, gain_ref[...]).astype(xn_ref.dtype)


def combine_call(ys, pos, h1, weights, next_gain, xn_dtype, chunk=128):
    n, d = h1.shape
    per_row = PLAN_CHUNK // chunk
    steps = n // chunk

    def pos_spec(j, ahead):
        def index(c):
            c = jnp.minimum(c + ahead, steps - 1)
            return (j, c // per_row, 0, c % per_row)
        return pl.BlockSpec((None, 1, 1, chunk), index, memory_space=pltpu.SMEM)

    return pl.pallas_call(
        functools.partial(_combine_kernel, chunk=chunk),
        grid=(steps,),
        in_specs=[pos_spec(0, 0), pos_spec(1, 0), pos_spec(0, 1), pos_spec(1, 1),
                  pl.BlockSpec(memory_space=pl.ANY),
                  pl.BlockSpec((chunk, d), lambda c: (c, 0)),
                  pl.BlockSpec((chunk, ROUTER_LANES), lambda c: (c, 0)),
                  pl.BlockSpec((1, d), lambda c: (0, 0))],
        out_specs=[pl.BlockSpec((chunk, d), lambda c: (c, 0)),
                   pl.BlockSpec((chunk, d), lambda c: (c, 0))],
        out_shape=[jax.ShapeDtypeStruct((n, d), F32),
                   jax.ShapeDtypeStruct((n, d), xn_dtype)],
        scratch_shapes=[pltpu.VMEM((2, 2, chunk * ROW_TILE, LANES), jnp.uint32), pltpu.SemaphoreType.DMA((2,))],
        compiler_params=_cparams(("arbitrary",), 32),
        name="combine_rows",
    )(pos, pos, pos, pos, ys, h1, weights, next_gain.reshape(1, d))


def kernel(x, rel_bias_table, norm_mix_gain, w_in, w_merge_gate, b_merge_gate, pool_mix, pool_scale, w_up_pool, lambda_q1, lambda_k1, lambda_q2, lambda_k2, subln_gain, w_up_attn, w_out, norm_ffn_gain, w_router_group, b_router_group, w_router_expert, b_router_expert, w_expert_gate, w_expert_up, w_expert_down, final_norm_gain):
    batch, seq, d = x.shape
    depth = w_in.shape[0]
    n = batch * seq
    n_tiles = (2 * n + N_EXPERTS * (TM_EXPERT - 1)) // TM_EXPERT + 1

    bias_tiles = bias_tiles_call(rel_bias_table)
    h = x.reshape(n, d)
    xn = rms_norm_call(h, norm_mix_gain[0], BF16)
    q_scale = LOG2E * DIFF_HEAD_DIM ** -0.5

    for l in range(depth):
        u = proj_call(xn, w_in, l, 0, POOL_WIDTH // 512, 512, F32, name="in_proj_pool")
        qkv = proj_call(xn, w_in, l, 1, 3, ATTN_WIDTH, BF16, first_block_scale=q_scale, name="in_proj_qkv")
        mixed = pool_call(u, pool_mix[l].astype(BF16), pool_scale[l], seq)
        lambda_init = 0.8 - 0.6 * math.exp(-0.3 * l)
        lam_params = jnp.stack([lambda_q1[l], lambda_k1[l], lambda_q2[l], lambda_k2[l]])
        attn = attn_call(qkv.reshape(batch, seq, 3 * ATTN_WIDTH), bias_tiles, lam_params, subln_gain[l],
                         lambda_init, batch, seq).reshape(n, ATTN_WIDTH)
        z = merge_call(xn, mixed, attn, w_merge_gate, b_merge_gate.reshape(depth, 1, -1), w_up_pool, w_up_attn, l)

        w_router = jnp.concatenate(
            [w_router_group[l], jnp.transpose(w_router_expert[l], (1, 0, 2)).reshape(d, N_EXPERTS),
             jnp.zeros((d, ROUTER_LANES - N_GROUPS - N_EXPERTS), F32)], axis=1).astype(BF16)
        b_router = jnp.concatenate(
            [b_router_group[l], b_router_expert[l].reshape(-1),
             jnp.zeros((ROUTER_LANES - N_GROUPS - N_EXPERTS,), F32)]).reshape(1, ROUTER_LANES)
        h1, hn, logits = outproj_call(z, w_out, l, h, norm_ffn_gain[l], w_router, b_router)

        weights, pos, meta = route_plan_call(logits, n_tiles)
        tile_expert = meta[META_TILE_EXPERT, :n_tiles]
        next_expert = meta[META_NEXT_EXPERT, :n_tiles]
        pad_start = meta[META_PAD_START, :N_EXPERTS]
        n_valid = meta[META_N_VALID, :1]
        xs = dispatch_call(hn, pos, pad_start, n_valid, n_tiles)
        ys = expert_call(xs, tile_expert, n_valid, next_expert, w_expert_gate, w_expert_up, w_expert_down, l)
        last = l == depth - 1
        next_gain = final_norm_gain if last else norm_mix_gain[l + 1]
        h, xn = combine_call(ys, pos, h1, weights, next_gain, F32 if last else BF16)

    return xn.reshape(batch, seq, d)
```

```python
import functools
import math

import numpy as np
import jax
import jax.numpy as jnp
from jax import lax
from jax.experimental import pallas as pl
from jax.experimental.pallas import tpu as pltpu

F32 = jnp.float32
BF16 = jnp.bfloat16

D_MODEL = 2048
POOL_WIDTH = 1024
POOL_WINDOWS = (2, 4, 8, 16)
POOL_GROUP_DIM = 256
POOL_HALO = 16
DIFF_HEADS = 8
DIFF_HEAD_DIM = 64
DIFF_V_DIM = 128
ATTN_WIDTH = 1024
REL_BUCKETS = 32
REL_MAX_DISTANCE = 128
N_GROUPS = 4
EXPERTS_PER_GROUP = 8
N_EXPERTS = 32
D_EXPERT = 512
EPS = 1e-6
NEG_INF = -1e30
LOG2E = 1.4426950408889634

ROUTER_LANES = 128
TQ = 256
TQ_BLOCK = 256
TK = 256
KV_SUPER = 2
V_ROWS = DIFF_V_DIM + 16
HEADS_PER_STEP = 4
TM_EXPERT = 256
LANES = 128
ROW_DMA_UNROLL = 8
MIB = 1024 * 1024


def _cparams(sem, vmem_mib):
    return pltpu.CompilerParams(dimension_semantics=sem, vmem_limit_bytes=vmem_mib * MIB)


def _rms(xf, gain):
    ms = jnp.mean(xf * xf, axis=-1, keepdims=True)
    return xf * lax.rsqrt(ms + EPS) * gain


def _norm_kernel(h_ref, g_ref, o_ref):
    o_ref[...] = _rms(h_ref[...], g_ref[...]).astype(o_ref.dtype)


def rms_norm_call(h, gain, out_dtype, tm=512):
    n, d = h.shape
    return pl.pallas_call(
        _norm_kernel,
        grid=(n // tm,),
        in_specs=[pl.BlockSpec((tm, d), lambda m: (m, 0)),
                  pl.BlockSpec((1, d), lambda m: (0, 0))],
        out_specs=pl.BlockSpec((tm, d), lambda m: (m, 0)),
        out_shape=jax.ShapeDtypeStruct((n, d), out_dtype),
        compiler_params=_cparams(("parallel",), 32),
        name="rms_norm",
    )(h, gain.reshape(1, d))


def _proj_kernel(x_ref, w_ref, o_ref, w_s, *, first_block_scale):
    @pl.when(pl.program_id(1) == 0)
    def _():
        w_s[...] = w_ref[...].astype(BF16)

    acc = jnp.dot(x_ref[...], w_s[...], preferred_element_type=F32)
    if first_block_scale is not None:
        acc = acc * jnp.where(pl.program_id(0) == 0, first_block_scale, 1.0).astype(F32)
    o_ref[...] = acc.astype(o_ref.dtype)


def proj_call(x, w, layer, col_block0, n_col_blocks, tn, out_dtype, first_block_scale=None, tm=1024, name="proj"):
    n, k = x.shape
    return pl.pallas_call(
        functools.partial(_proj_kernel, first_block_scale=first_block_scale),
        grid=(n_col_blocks, n // tm),
        in_specs=[pl.BlockSpec((tm, k), lambda j, m: (m, 0)),
                  pl.BlockSpec((None, k, tn), lambda j, m: (layer, 0, col_block0 + j))],
        out_specs=pl.BlockSpec((tm, tn), lambda j, m: (m, j)),
        out_shape=jax.ShapeDtypeStruct((n, n_col_blocks * tn), out_dtype),
        scratch_shapes=[pltpu.VMEM((k, tn), BF16)],
        compiler_params=_cparams(("arbitrary", "arbitrary"), 48),
        name=name,
    )(x, w)


def _pool_kernel(cur_ref, prev_ref, mix_ref, scale_ref, o_ref, *, tm, seq):
    m = pl.program_id(0)
    row0 = (m * tm) % seq
    cur = cur_ref[...]
    prev = jnp.where(row0 == 0, 0.0, prev_ref[...])
    pos = row0 + lax.broadcasted_iota(jnp.int32, (tm, 1), 0)
    outs = []
    for g, w in enumerate(POOL_WINDOWS):
        sl = slice(g * POOL_GROUP_DIM, (g + 1) * POOL_GROUP_DIM)
        x = jnp.concatenate([prev[:, sl], cur[:, sl]], axis=0)
        s, d = x, 1
        while d < w:
            s = s[:-d] + s[d:]
            d *= 2
        start = POOL_HALO - w + 1
        wsum = s[start:start + tm]
        count = jnp.minimum(pos + 1, w).astype(F32)
        pooled = wsum / count - cur[:, sl]
        mixed = jnp.dot(pooled.astype(BF16), mix_ref[g], preferred_element_type=F32)
        outs.append(mixed * scale_ref[:, sl])
    o_ref[...] = jnp.concatenate(outs, axis=1).astype(o_ref.dtype)


def pool_call(u, mix_bf16, scale, seq, tm=512):
    n, c = u.shape
    blocks_per_tile = tm // POOL_HALO
    return pl.pallas_call(
        functools.partial(_pool_kernel, tm=tm, seq=seq),
        grid=(n // tm,),
        in_specs=[pl.BlockSpec((tm, c), lambda m: (m, 0)),
                  pl.BlockSpec((POOL_HALO, c), lambda m: (jnp.maximum(m * blocks_per_tile - 1, 0), 0)),
                  pl.BlockSpec(mix_bf16.shape, lambda m: (0, 0, 0)),
                  pl.BlockSpec((1, c), lambda m: (0, 0))],
        out_specs=pl.BlockSpec((tm, c), lambda m: (m, 0)),
        out_shape=jax.ShapeDtypeStruct((n, c), BF16),
        compiler_params=_cparams(("parallel",), 32),
        name="pool_mixer",
    )(u, u, mix_bf16, scale.reshape(1, c))


N_BIAS_TILES = 4


def _bucket_tiles():
    kk = np.arange(TK)[:, None]
    qq = np.arange(TQ_BLOCK)[None, :]
    tiles = []
    for rel in (0, 1, 2, -1):
        n = rel * TK + qq - kk
        max_exact = REL_BUCKETS // 2
        nf = np.maximum(n, 1).astype(np.float64)
        large = max_exact + (np.log(nf / max_exact) / math.log(REL_MAX_DISTANCE / max_exact)
                             * (REL_BUCKETS - max_exact)).astype(np.int64)
        large = np.minimum(large, REL_BUCKETS - 1)
        bucket = np.where(n < max_exact, n, large)
        tiles.append(np.where(n < 0, -1, bucket))
    return np.stack(tiles).astype(np.int32)


def _bias_kernel(table_ref, bucket_ref, o_ref):
    h = pl.program_id(0)
    bucket = bucket_ref[...]
    acc = jnp.full(bucket.shape, NEG_INF, F32)
    for b in range(REL_BUCKETS):
        acc = jnp.where(bucket == b, table_ref[b * DIFF_HEADS + h] * LOG2E, acc)
    o_ref[...] = acc


def bias_tiles_call(rel_table):
    bucket = jnp.asarray(_bucket_tiles())
    return pl.pallas_call(
        _bias_kernel,
        grid=(DIFF_HEADS,),
        in_specs=[pl.BlockSpec(memory_space=pltpu.SMEM),
                  pl.BlockSpec(bucket.shape, lambda h: (0, 0, 0))],
        out_specs=pl.BlockSpec((None,) + bucket.shape, lambda h: (h, 0, 0, 0)),
        out_shape=jax.ShapeDtypeStruct((DIFF_HEADS,) + bucket.shape, F32),
        compiler_params=_cparams(("parallel",), 32),
        name="rel_bias_tiles",
    )(rel_table.reshape(-1), bucket)


def _attn_kernel(q_ref, k_ref, v_ref, bias_ref, lam_ref, gain_ref, o_ref, vt_ref, s_ref, acc_ref,
                 *, lambda_init, n_super):
    qi = pl.program_id(2)
    tks = KV_SUPER * TK
    heads = range(HEADS_PER_STEP)

    def block_diag(ref, hh):
        q = ref[:, hh * DIFF_V_DIM:(hh + 1) * DIFF_V_DIM]
        lane = lax.broadcasted_iota(jnp.int32, q.shape, 1)
        zero = jnp.zeros_like(q)
        return jnp.concatenate([jnp.where(lane < DIFF_HEAD_DIM, q, zero),
                                jnp.where(lane >= DIFF_HEAD_DIM, q, zero)], axis=0)

    def scores(t, hh, qd, tile):
        kb = k_ref[pl.ds(pl.multiple_of(t * tks, tks), tks), hh * DIFF_V_DIM:(hh + 1) * DIFF_V_DIM]
        s = lax.dot_general(kb, qd, (((1,), (1,)), ((), ())), preferred_element_type=F32)
        parts = []
        for u in range(KV_SUPER):
            tiles = []
            for c in range(TQ // TQ_BLOCK):
                rel = tile * (TQ // TQ_BLOCK) + c - (t * KV_SUPER + u)
                tiles.append(bias_ref[hh, jnp.where(rel < 0, N_BIAS_TILES - 1, jnp.minimum(rel, 2))])
            parts.append(s[u * TK:(u + 1) * TK] + jnp.concatenate(tiles + tiles, axis=1))
        s = jnp.concatenate(parts, axis=0)
        s_ref[hh] = s
        return jnp.max(s, axis=0, keepdims=True)

    @pl.when(qi == 0)
    def _():
        extra = (lax.broadcasted_iota(jnp.int32, (V_ROWS - DIFF_V_DIM, tks), 0) == 0).astype(BF16)
        for hh in heads:
            cols = slice(hh * DIFF_V_DIM, (hh + 1) * DIFF_V_DIM)
            for c in range(n_super):
                vt = v_ref[c * tks:(c + 1) * tks, cols].astype(F32).T.astype(BF16)
                vt_ref[hh, c] = jnp.concatenate([vt, extra], axis=0)

    qds = [block_diag(q_ref, hh) for hh in heads]

    def softmax_step(t, hh, m_prev, m_cur):
        alpha = jnp.exp2(m_prev - m_cur)
        p = jnp.exp2(s_ref[hh] - m_cur)
        pv = jnp.dot(vt_ref[hh, t], p.astype(BF16), preferred_element_type=F32)
        acc_ref[hh] = acc_ref[hh] * alpha + pv

    last = ((qi + 1) * TQ - 1) // tks
    acc_ref[...] = jnp.zeros(acc_ref.shape, F32)
    neg = jnp.full((1, 2 * TQ), NEG_INF, F32)
    init = tuple((neg, jnp.maximum(neg, scores(0, hh, qds[hh], qi))) for hh in heads)

    def body(t, carry):
        out = []
        for hh in heads:
            m_prev, m_cur = carry[hh]
            softmax_step(t, hh, m_prev, m_cur)
            m_next = jnp.maximum(m_cur, scores(t + 1, hh, qds[hh], qi))
            out.append((m_cur, m_next))
        return tuple(out)

    carry = lax.fori_loop(0, last, body, init)

    lam_p = lam_ref[...]
    lam = (jnp.exp(jnp.sum(lam_p[0:1] * lam_p[1:2], axis=1, keepdims=True))
           - jnp.exp(jnp.sum(lam_p[2:3] * lam_p[3:4], axis=1, keepdims=True)) + lambda_init)
    for hh in heads:
        m_prev, m_cur = carry[hh]
        softmax_step(last, hh, m_prev, m_cur)
        acc = acc_ref[hh, :DIFF_V_DIM, :]
        l = acc_ref[hh, DIFF_V_DIM:DIFF_V_DIM + 1, :]
        o = acc[:, :TQ] / l[:, :TQ] - lam * (acc[:, TQ:] / l[:, TQ:])
        ms = jnp.mean(o * o, axis=0, keepdims=True)
        y = o * lax.rsqrt(ms + EPS) * gain_ref[...] * (1.0 - lambda_init)
        o_ref[:, hh * DIFF_V_DIM:(hh + 1) * DIFF_V_DIM] = y.T.astype(o_ref.dtype)


def attn_call(qkv, bias_tiles, lam_params, subln_gain, lambda_init, batch, seq):
    tks = KV_SUPER * TK
    n_super = seq // tks
    hps = HEADS_PER_STEP
    width = hps * DIFF_V_DIM
    groups = DIFF_HEADS // hps
    n_q = seq // TQ
    return pl.pallas_call(
        functools.partial(_attn_kernel, lambda_init=lambda_init, n_super=n_super),
        grid=(batch, groups, n_q),
        in_specs=[pl.BlockSpec((None, TQ, width), lambda b, g, i: (b, i, g)),
                  pl.BlockSpec((None, seq, width), lambda b, g, i: (b, 0, groups + g)),
                  pl.BlockSpec((None, seq, width), lambda b, g, i: (b, 0, 2 * groups + g)),
                  pl.BlockSpec((hps, N_BIAS_TILES, TK, TQ_BLOCK), lambda b, g, i: (g, 0, 0, 0)),
                  pl.BlockSpec((4, DIFF_HEAD_DIM), lambda b, g, i: (0, 0)),
                  pl.BlockSpec((DIFF_V_DIM, 1), lambda b, g, i: (0, 0))],
        out_specs=pl.BlockSpec((None, TQ, width), lambda b, g, i: (b, i, g)),
        out_shape=jax.ShapeDtypeStruct((batch, seq, ATTN_WIDTH), BF16),
        scratch_shapes=[pltpu.VMEM((hps, n_super, V_ROWS, tks), BF16),
                        pltpu.VMEM((hps, tks, 2 * TQ), F32),
                        pltpu.VMEM((hps, V_ROWS, 2 * TQ), F32)],
        compiler_params=_cparams(("parallel", "parallel", "arbitrary"), 56),
        name="diff_attention",
    )(qkv, qkv, qkv, bias_tiles, lam_params, subln_gain.reshape(DIFF_V_DIM, 1))


def _sigmoid(x):
    return 1.0 / (1.0 + jnp.exp(-x))


def _merge_kernel(xn_ref, mixed_ref, attn_ref, wgp_ref, wga_ref, bgp_ref, bga_ref, wup_ref, wua_ref, z_ref,
                  wgp_s, wga_s, wup_s, wua_s):
    @pl.when(pl.program_id(1) == 0)
    def _():
        wgp_s[...] = wgp_ref[...].astype(BF16)
        wga_s[...] = wga_ref[...].astype(BF16)
        wup_s[...] = wup_ref[...].astype(BF16)
        wua_s[...] = wua_ref[...].astype(BF16)

    xn = xn_ref[...]
    g_pool = _sigmoid(jnp.dot(xn, wgp_s[...], preferred_element_type=F32) + bgp_ref[...])
    g_attn = _sigmoid(jnp.dot(xn, wga_s[...], preferred_element_type=F32) + bga_ref[...])
    y_pool = jnp.dot(mixed_ref[...], wup_s[...], preferred_element_type=F32)
    y_attn = jnp.dot(attn_ref[...], wua_s[...], preferred_element_type=F32)
    z_ref[...] = (g_pool * y_pool + g_attn * y_attn).astype(z_ref.dtype)


def merge_call(xn, mixed, attn, w_gate, b_gate, w_up_pool, w_up_attn, layer, tm=1024, tn=512):
    n, d = xn.shape
    nb = d // tn
    return pl.pallas_call(
        _merge_kernel,
        grid=(nb, n // tm),
        in_specs=[pl.BlockSpec((tm, d), lambda j, m: (m, 0)),
                  pl.BlockSpec((tm, POOL_WIDTH), lambda j, m: (m, 0)),
                  pl.BlockSpec((tm, ATTN_WIDTH), lambda j, m: (m, 0)),
                  pl.BlockSpec((None, d, tn), lambda j, m: (layer, 0, j)),
                  pl.BlockSpec((None, d, tn), lambda j, m: (layer, 0, nb + j)),
                  pl.BlockSpec((None, 1, tn), lambda j, m: (layer, 0, j)),
                  pl.BlockSpec((None, 1, tn), lambda j, m: (layer, 0, nb + j)),
                  pl.BlockSpec((None, POOL_WIDTH, tn), lambda j, m: (layer, 0, j)),
                  pl.BlockSpec((None, ATTN_WIDTH, tn), lambda j, m: (layer, 0, j))],
        out_specs=pl.BlockSpec((tm, tn), lambda j, m: (m, j)),
        out_shape=jax.ShapeDtypeStruct((n, d), BF16),
        scratch_shapes=[pltpu.VMEM((d, tn), BF16), pltpu.VMEM((d, tn), BF16),
                        pltpu.VMEM((POOL_WIDTH, tn), BF16), pltpu.VMEM((ATTN_WIDTH, tn), BF16)],
        compiler_params=_cparams(("arbitrary", "arbitrary"), 56),
        name="gated_merge",
    )(xn, mixed, attn, w_gate, w_gate, b_gate, b_gate, w_up_pool, w_up_attn)


HIGH_HALF = 0xFFFF0000


def _pack_halves(x):
    c = x.shape[1] // 2
    lo = lax.bitcast_convert_type(x[:, :c].astype(BF16).astype(F32), jnp.uint32)
    hi = lax.bitcast_convert_type(x[:, c:].astype(BF16).astype(F32), jnp.uint32)
    return (lo >> 16) | hi


def _unpack_halves(p):
    lo = lax.bitcast_convert_type(p << 16, F32)
    hi = lax.bitcast_convert_type(p & jnp.uint32(HIGH_HALF), F32)
    return lo, hi


ROW_TILE = 8


def _store_row_tiles(ref, packed):
    m = packed.shape[0]
    for s in range(ROW_TILE):
        ref[pl.ds(s, m, stride=ROW_TILE), :] = packed[:, s * LANES:(s + 1) * LANES]


def _load_row_tiles(ref):
    m = ref.shape[0] // ROW_TILE
    return jnp.concatenate([ref[pl.ds(s, m, stride=ROW_TILE), :] for s in range(ROW_TILE)], axis=1)


def _rows(ref, first, count=1):
    return ref.at[pl.ds(pl.multiple_of(first * ROW_TILE, ROW_TILE), count * ROW_TILE)]


def _outproj_kernel(z_ref, w_ref, h_ref, gain_ref, wr_ref, br_ref, h1_ref, hn_ref, logit_ref, w_s):
    @pl.when(pl.program_id(0) == 0)
    def _():
        w_s[...] = w_ref[...].astype(BF16)

    h1 = h_ref[...] + jnp.dot(z_ref[...], w_s[...], preferred_element_type=F32)
    h1_ref[...] = h1
    hn = _rms(h1, gain_ref[...])
    _store_row_tiles(hn_ref, _pack_halves(hn))
    logit_ref[...] = jnp.dot(hn.astype(BF16), wr_ref[...], preferred_element_type=F32) + br_ref[...]


def outproj_call(z, w_out, layer, h, gain, w_router, b_router, tm=256):
    n, d = h.shape
    return pl.pallas_call(
        _outproj_kernel,
        grid=(n // tm,),
        in_specs=[pl.BlockSpec((tm, d), lambda m: (m, 0)),
                  pl.BlockSpec((None, d, d), lambda m: (layer, 0, 0), pipeline_mode=pl.Buffered(1)),
                  pl.BlockSpec((tm, d), lambda m: (m, 0)),
                  pl.BlockSpec((1, d), lambda m: (0, 0)),
                  pl.BlockSpec((d, ROUTER_LANES), lambda m: (0, 0)),
                  pl.BlockSpec((1, ROUTER_LANES), lambda m: (0, 0))],
        out_specs=[pl.BlockSpec((tm, d), lambda m: (m, 0)),
                   pl.BlockSpec((tm * ROW_TILE, LANES), lambda m: (m, 0)),
                   pl.BlockSpec((tm, ROUTER_LANES), lambda m: (m, 0))],
        out_shape=[jax.ShapeDtypeStruct((n, d), F32),
                   jax.ShapeDtypeStruct((n * ROW_TILE, LANES), jnp.uint32),
                   jax.ShapeDtypeStruct((n, ROUTER_LANES), F32)],
        scratch_shapes=[pltpu.VMEM((d, d), BF16)],
        compiler_params=_cparams(("arbitrary",), 56),
        name="out_proj_norm_router",
    )(z, w_out, h, gain.reshape(1, d), w_router, b_router)


def _route(x, lane):
    big = float(ROUTER_LANES)

    def first_argmax(vals):
        top = jnp.max(vals, axis=1, keepdims=True)
        idx = jnp.min(jnp.where(vals == top, lane, big), axis=1, keepdims=True)
        return top, idx

    gmask = lane < N_GROUPS
    g_top, g_sel = first_argmax(jnp.where(gmask, x, -jnp.inf))
    g_weight = 1.0 / jnp.sum(jnp.where(gmask, jnp.exp(x - g_top), 0.0), axis=1, keepdims=True)
    lo = N_GROUPS + EXPERTS_PER_GROUP * g_sel
    e_vals = jnp.where((lane >= lo) & (lane < lo + EXPERTS_PER_GROUP), x, -jnp.inf)
    v1, i1 = first_argmax(e_vals)
    v2, i2 = first_argmax(jnp.where(lane == i1, -jnp.inf, e_vals))
    t = jnp.exp(v2 - v1)
    return i1 - N_GROUPS, i2 - N_GROUPS, g_weight / (1.0 + t), g_weight * t / (1.0 + t)


PLAN_BLOCK = 1024
PLAN_CHUNK = 256
META_TILE_EXPERT, META_NEXT_EXPERT, META_PAD_START, META_N_VALID = 0, 1, 2, 3


def _route_plan_kernel(logit_ref, w_ref, pos_ref, meta_ref, tri_s, cnt_s, base_s, off_s):
    phase = pl.program_id(0)
    blk = pl.program_id(1)
    tb = logit_ref.shape[0]
    lane = lax.broadcasted_iota(jnp.int32, (tb, ROUTER_LANES), 1).astype(F32)
    e0, e1, w0, w1 = _route(logit_ref[...], lane)
    onehot = jnp.where(lane == e0, 1.0, 0.0) + jnp.where(lane == e1, 1.0, 0.0)
    block_counts = jnp.sum(onehot, axis=0, keepdims=True)

    @pl.when(jnp.logical_and(phase == 0, blk == 0))
    def _():
        cnt_s[...] = jnp.zeros(cnt_s.shape, F32)

    @pl.when(phase == 0)
    def _():
        cnt_s[...] = cnt_s[...] + block_counts

    @pl.when(jnp.logical_and(phase == 1, blk == 0))
    def _():
        r = lax.broadcasted_iota(jnp.int32, (tb, tb), 0)
        c = lax.broadcasted_iota(jnp.int32, (tb, tb), 1)
        tri_s[...] = jnp.where(c < r, 1.0, 0.0).astype(BF16)
        lane1 = lane[0:1]
        cnt = cnt_s[...]
        tiles = jnp.floor((cnt + (TM_EXPERT - 1)) * (1.0 / TM_EXPERT))
        ri = lax.broadcasted_iota(jnp.int32, (ROUTER_LANES, ROUTER_LANES), 0)
        ci = lax.broadcasted_iota(jnp.int32, (ROUTER_LANES, ROUTER_LANES), 1)
        upper = jnp.where(ri <= ci, 1.0, 0.0).astype(BF16)
        ends = jnp.dot(jnp.broadcast_to(tiles, (8, ROUTER_LANES)).astype(BF16), upper,
                       preferred_element_type=F32)[0:1]
        off_s[...] = (ends - tiles) * TM_EXPERT
        base_s[...] = jnp.zeros(base_s.shape, F32)

        def pick(vec, e):
            return jnp.sum(jnp.where(lane1 == e, vec, 0.0), axis=1, keepdims=True)

        end_of = [pick(ends, e) for e in range(N_EXPERTS)]
        n_valid = end_of[N_EXPERTS - 1]

        def segment_of(tile):
            return sum(jnp.where(tile >= end_e, 1.0, 0.0) for end_e in end_of)

        tile_expert = jnp.where(lane1 < n_valid, segment_of(lane1), segment_of(n_valid - 1.0))
        following = sum(jnp.where(tile_expert == e, end_of[e], 0.0) for e in range(N_EXPERTS))
        next_expert = jnp.where(following < n_valid, segment_of(following), -1.0)
        rows = [tile_expert, next_expert, off_s[...] + cnt, jnp.broadcast_to(n_valid, (1, ROUTER_LANES))]
        rows += [jnp.zeros((1, ROUTER_LANES), F32)] * (meta_ref.shape[0] - len(rows))
        meta_ref[...] = jnp.concatenate(rows, axis=0).astype(jnp.int32)

    @pl.when(phase == 1)
    def _():
        before = jnp.dot(tri_s[...], onehot.astype(BF16), preferred_element_type=F32)
        row = before + base_s[...] + off_s[...]
        base_s[...] = base_s[...] + block_counts
        w_ref[...] = jnp.where(lane == 0, w0, jnp.where(lane == 1, w1, 0.0))
        eye = (lax.broadcasted_iota(jnp.int32, (LANES, LANES), 0)
               == lax.broadcasted_iota(jnp.int32, (LANES, LANES), 1))
        for j, e in enumerate((e0, e1)):
            col = jnp.sum(jnp.where(lane == e, row, 0.0), axis=1, keepdims=True)
            for q in range(tb // PLAN_CHUNK):
                parts = []
                for g in range(PLAN_CHUNK // LANES):
                    t0 = q * PLAN_CHUNK + g * LANES
                    square = jnp.broadcast_to(col[t0:t0 + LANES], (LANES, LANES))
                    parts.append(jnp.sum(jnp.where(eye, square, 0.0), axis=0, keepdims=True))
                pos_ref[j, q] = jnp.concatenate(parts, axis=1).astype(jnp.int32)


def route_plan_call(logits, n_tiles):
    n = logits.shape[0]
    tb = PLAN_BLOCK
    chunks = tb // PLAN_CHUNK
    assert n_tiles <= ROUTER_LANES and N_EXPERTS <= ROUTER_LANES
    return pl.pallas_call(
        _route_plan_kernel,
        grid=(2, n // tb),
        in_specs=[pl.BlockSpec((tb, ROUTER_LANES), lambda p, b: (b, 0))],
        out_specs=[pl.BlockSpec((tb, ROUTER_LANES), lambda p, b: (p * b, 0)),
                   pl.BlockSpec((2, chunks, 1, PLAN_CHUNK), lambda p, b: (0, p * b, 0, 0)),
                   pl.BlockSpec((8, ROUTER_LANES), lambda p, b: (0, 0))],
        out_shape=[jax.ShapeDtypeStruct((n, ROUTER_LANES), F32),
                   jax.ShapeDtypeStruct((2, n // PLAN_CHUNK, 1, PLAN_CHUNK), jnp.int32),
                   jax.ShapeDtypeStruct((8, ROUTER_LANES), jnp.int32)],
        scratch_shapes=[pltpu.VMEM((tb, tb), BF16), pltpu.VMEM((1, ROUTER_LANES), F32),
                        pltpu.VMEM((1, ROUTER_LANES), F32), pltpu.VMEM((1, ROUTER_LANES), F32)],
        compiler_params=_cparams(("arbitrary", "arbitrary"), 32),
        name="route_plan",
    )(logits)


DISPATCH_SLOTS = 3


def _dispatch_kernel(pad_ref, nv_ref, pos0_ref, pos1_ref, src_hbm, dst_ref, zero_buf, ring, sem, in_sems, row_sems,
                     *, chunk, n_tiles):
    i = pl.program_id(0)
    steps = pl.num_programs(0)
    slot = i % DISPATCH_SLOTS

    def load(block, into):
        return pltpu.make_async_copy(_rows(src_hbm, block * chunk, chunk), ring.at[into], in_sems.at[into])

    def wait_rows(of_slot):
        for j in range(2):
            pltpu.make_async_copy(ring.at[of_slot], _rows(dst_ref, 0, chunk), row_sems.at[of_slot]).wait()

    @pl.when(i == 0)
    def _():
        load(0, 0).start()
        zero_buf[...] = jnp.zeros(zero_buf.shape, zero_buf.dtype)

        def fill(e):
            return pltpu.make_async_copy(zero_buf, _rows(dst_ref, pad_ref[e], TM_EXPERT), sem)

        for e in range(N_EXPERTS):
            fill(e).start()
        for e in range(N_EXPERTS):
            fill(e).wait()

        def fill_tile(i):
            return pltpu.make_async_copy(zero_buf, _rows(dst_ref, i * TM_EXPERT, TM_EXPERT), sem)

        def start_tile(i, c):
            fill_tile(i).start()
            return c

        def wait_tile(i, c):
            fill_tile(i).wait()
            return c

        lax.fori_loop(nv_ref[0], n_tiles, start_tile, 0)
        lax.fori_loop(nv_ref[0], n_tiles, wait_tile, 0)

    nxt = (i + 1) % DISPATCH_SLOTS

    @pl.when(i >= DISPATCH_SLOTS - 1)
    def _():
        wait_rows(nxt)

    @pl.when(i + 1 < steps)
    def _():
        load(i + 1, nxt).start()

    load(i, slot).wait()
    src_ref = ring.at[slot]

    def start(t, c):
        for j, pos_ref in enumerate((pos0_ref, pos1_ref)):
            pltpu.make_async_copy(_rows(src_ref, t), _rows(dst_ref, pos_ref[0, 0, t]),
                                  row_sems.at[slot]).start(priority=j)
        return c

    lax.fori_loop(0, chunk, start, 0, unroll=ROW_DMA_UNROLL)

    @pl.when(i == steps - 1)
    def _():
        for back in range(DISPATCH_SLOTS - 1):
            @pl.when(i - back >= 0)
            def _():
                wait_rows((i - back) % DISPATCH_SLOTS)


def dispatch_call(hn, pos, pad_start, n_valid_tiles, n_tiles):
    n = hn.shape[0] // ROW_TILE
    chunk = PLAN_CHUNK
    grid_spec = pltpu.PrefetchScalarGridSpec(
        num_scalar_prefetch=2,
        grid=(n // chunk,),
        in_specs=[pl.BlockSpec((None, 1, 1, chunk), lambda i, pad, nv: (0, i, 0, 0), memory_space=pltpu.SMEM),
                  pl.BlockSpec((None, 1, 1, chunk), lambda i, pad, nv: (1, i, 0, 0), memory_space=pltpu.SMEM),
                  pl.BlockSpec(memory_space=pl.ANY)],
        out_specs=pl.BlockSpec(memory_space=pl.ANY),
        scratch_shapes=[pltpu.VMEM((TM_EXPERT * ROW_TILE, LANES), hn.dtype),
                        pltpu.VMEM((DISPATCH_SLOTS, chunk * ROW_TILE, LANES), hn.dtype),
                        pltpu.SemaphoreType.DMA(()), pltpu.SemaphoreType.DMA((DISPATCH_SLOTS,)),
                        pltpu.SemaphoreType.DMA((DISPATCH_SLOTS,))],
    )
    return pl.pallas_call(
        functools.partial(_dispatch_kernel, chunk=chunk, n_tiles=n_tiles),
        grid_spec=grid_spec,
        out_shape=jax.ShapeDtypeStruct((n_tiles * TM_EXPERT * ROW_TILE, LANES), hn.dtype),
        compiler_params=_cparams(("arbitrary",), 32),
        name="dispatch_rows",
    )(pad_start, n_valid_tiles, pos, pos, hn)


def _expert_kernel(te_ref, nv_ref, nx_ref, x_ref, wg_hbm, wu_hbm, wd_hbm, y_ref,
                   wg_f, wu_f, wd_f, wg_s, wu_s, wd_s, seg_ref, sems, *, layer):
    i = pl.program_id(0)
    valid = i < nv_ref[0]
    expert = te_ref[i]
    changed = jnp.logical_or(i == 0, expert != te_ref[jnp.maximum(i - 1, 0)])

    def fetch(e, slot):
        return [pltpu.make_async_copy(hbm.at[layer, e], buf.at[slot], sems.at[slot])
                for hbm, buf in ((wg_hbm, wg_f), (wu_hbm, wu_f), (wd_hbm, wd_f))]

    def swiglu_tile():
        lo, hi = _unpack_halves(_load_row_tiles(x_ref))
        x = jnp.concatenate([lo.astype(BF16), hi.astype(BF16)], axis=1)
        a = jnp.dot(x, wg_s[...], preferred_element_type=F32)
        b = jnp.dot(x, wu_s[...], preferred_element_type=F32)
        hmid = (a * _sigmoid(a) * b).astype(BF16)
        _store_row_tiles(y_ref, _pack_halves(jnp.dot(hmid, wd_s[...], preferred_element_type=F32)))

    @pl.when(i == 0)
    def _():
        seg_ref[0] = 0
        for cp in fetch(expert, 0):
            cp.start()

    @pl.when(jnp.logical_and(valid, changed))
    def _():
        slot = seg_ref[0] % 2
        for cp in fetch(expert, slot):
            cp.wait()
        nxt = nx_ref[i]

        @pl.when(nxt >= 0)
        def _():
            for cp in fetch(nxt, 1 - slot):
                cp.start()

        seg_ref[0] = seg_ref[0] + 1
        wg_s[...] = wg_f[slot].astype(BF16)
        wu_s[...] = wu_f[slot].astype(BF16)
        wd_s[...] = wd_f[slot].astype(BF16)
        swiglu_tile()

    @pl.when(jnp.logical_and(valid, jnp.logical_not(changed)))
    def _():
        swiglu_tile()

    @pl.when(jnp.logical_not(valid))
    def _():
        y_ref[...] = jnp.zeros(y_ref.shape, y_ref.dtype)


def expert_call(xs, tile_expert, n_valid_tiles, next_expert, w_gate, w_up, w_down, layer):
    p = xs.shape[0] // ROW_TILE
    d = 2 * ROW_TILE * LANES
    tm = TM_EXPERT
    f = w_gate.shape[-1]
    grid_spec = pltpu.PrefetchScalarGridSpec(
        num_scalar_prefetch=3,
        grid=(p // tm,),
        in_specs=[pl.BlockSpec((tm * ROW_TILE, LANES), lambda i, te, nv, nx: (jnp.minimum(i, nv[0] - 1), 0)),
                  pl.BlockSpec(memory_space=pl.ANY),
                  pl.BlockSpec(memory_space=pl.ANY),
                  pl.BlockSpec(memory_space=pl.ANY)],
        out_specs=pl.BlockSpec((tm * ROW_TILE, LANES), lambda i, te, nv, nx: (i, 0)),
        scratch_shapes=[pltpu.VMEM((2, d, f), F32), pltpu.VMEM((2, d, f), F32), pltpu.VMEM((2, f, d), F32),
                        pltpu.VMEM((d, f), BF16), pltpu.VMEM((d, f), BF16), pltpu.VMEM((f, d), BF16),
                        pltpu.SMEM((1,), jnp.int32), pltpu.SemaphoreType.DMA((2,))],
    )
    return pl.pallas_call(
        functools.partial(_expert_kernel, layer=layer),
        grid_spec=grid_spec,
        out_shape=jax.ShapeDtypeStruct(xs.shape, jnp.uint32),
        compiler_params=_cparams(("arbitrary",), 58),
        name="expert_swiglu",
    )(tile_expert, n_valid_tiles, next_expert, xs, w_gate, w_up, w_down)


def _combine_kernel(pos0_ref, pos1_ref, nxt0_ref, nxt1_ref, ys_ref, h_ref, w_ref, gain_ref, *rest, chunk):
    *out_refs, buf, sems = rest
    xn_ref, h2_ref = out_refs if len(out_refs) == 2 else (out_refs[0], None)
    i = pl.program_id(0)
    slot = i % 2

    def gather(p0_ref, p1_ref, into):
        def start(t, c):
            for j, pos_ref in enumerate((p0_ref, p1_ref)):
                pltpu.make_async_copy(_rows(ys_ref, pos_ref[0, 0, t]), _rows(buf.at[into, j], t),
                                      sems.at[into]).start(priority=j)
            return c

        lax.fori_loop(0, chunk, start, 0, unroll=ROW_DMA_UNROLL)

    @pl.when(i == 0)
    def _():
        gather(pos0_ref, pos1_ref, 0)

    @pl.when(i + 1 < pl.num_programs(0))
    def _():
        gather(nxt0_ref, nxt1_ref, 1 - slot)

    for j in range(2):
        pltpu.make_async_copy(_rows(ys_ref, 0, chunk), buf.at[slot, j], sems.at[slot]).wait()
    w = w_ref[...]
    lo0, hi0 = _unpack_halves(_load_row_tiles(buf.at[slot, 0]))
    lo1, hi1 = _unpack_halves(_load_row_tiles(buf.at[slot, 1]))
    y = jnp.concatenate([w[:, 0:1] * lo0 + w[:, 1:2] * lo1, w[:, 0:1] * hi0 + w[:, 1:2] * hi1], axis=1)
    h2 = h_ref[...] + y
    if h2_ref is not None:
        h2_ref[...] = h2
    xn_ref[...] = _rms(h2, gain_ref[...]).astype(xn_ref.dtype)


def combine_call(ys, pos, h1, weights, next_gain, xn_dtype, keep_residual, chunk=128):
    n, d = h1.shape
    row_spec = pl.BlockSpec((chunk, d), lambda c: (c, 0))
    out_specs = [row_spec, row_spec] if keep_residual else [row_spec]
    out_shape = [jax.ShapeDtypeStruct((n, d), xn_dtype)] + ([jax.ShapeDtypeStruct((n, d), F32)] if keep_residual else [])
    per_row = PLAN_CHUNK // chunk
    steps = n // chunk

    def pos_spec(j, ahead):
        def index(c):
            c = jnp.minimum(c + ahead, steps - 1)
            return (j, c // per_row, 0, c % per_row)
        return pl.BlockSpec((None, 1, 1, chunk), index, memory_space=pltpu.SMEM)

    outs = pl.pallas_call(
        functools.partial(_combine_kernel, chunk=chunk),
        grid=(steps,),
        in_specs=[pos_spec(0, 0), pos_spec(1, 0), pos_spec(0, 1), pos_spec(1, 1),
                  pl.BlockSpec(memory_space=pl.ANY),
                  row_spec,
                  pl.BlockSpec((chunk, ROUTER_LANES), lambda c: (c, 0)),
                  pl.BlockSpec((1, d), lambda c: (0, 0))],
        out_specs=out_specs,
        out_shape=out_shape,
        scratch_shapes=[pltpu.VMEM((2, 2, chunk * ROW_TILE, LANES), jnp.uint32), pltpu.SemaphoreType.DMA((2,))],
        compiler_params=_cparams(("arbitrary",), 32),
        name="combine_rows",
    )(pos, pos, pos, pos, ys, h1, weights, next_gain.reshape(1, d))
    return (outs[0], outs[1]) if keep_residual else (outs[0], None)


def kernel(x, rel_bias_table, norm_mix_gain, w_in, w_merge_gate, b_merge_gate, pool_mix, pool_scale, w_up_pool, lambda_q1, lambda_k1, lambda_q2, lambda_k2, subln_gain, w_up_attn, w_out, norm_ffn_gain, w_router_group, b_router_group, w_router_expert, b_router_expert, w_expert_gate, w_expert_up, w_expert_down, final_norm_gain):
    batch, seq, d = x.shape
    depth = w_in.shape[0]
    n = batch * seq
    n_tiles = (2 * n + N_EXPERTS * (TM_EXPERT - 1)) // TM_EXPERT + 1

    bias_tiles = bias_tiles_call(rel_bias_table)
    h = x.reshape(n, d)
    xn = rms_norm_call(h, norm_mix_gain[0], BF16)
    q_scale = LOG2E * DIFF_HEAD_DIM ** -0.5

    for l in range(depth):
        u = proj_call(xn, w_in, l, 0, POOL_WIDTH // 512, 512, F32, name="in_proj_pool")
        qkv = proj_call(xn, w_in, l, 1, 3, ATTN_WIDTH, BF16, first_block_scale=q_scale, name="in_proj_qkv")
        mixed = pool_call(u, pool_mix[l].astype(BF16), pool_scale[l], seq)
        lambda_init = 0.8 - 0.6 * math.exp(-0.3 * l)
        lam_params = jnp.stack([lambda_q1[l], lambda_k1[l], lambda_q2[l], lambda_k2[l]])
        attn = attn_call(qkv.reshape(batch, seq, 3 * ATTN_WIDTH), bias_tiles, lam_params, subln_gain[l],
                         lambda_init, batch, seq).reshape(n, ATTN_WIDTH)
        z = merge_call(xn, mixed, attn, w_merge_gate, b_merge_gate.reshape(depth, 1, -1), w_up_pool, w_up_attn, l)

        w_router = jnp.concatenate(
            [w_router_group[l], jnp.transpose(w_router_expert[l], (1, 0, 2)).reshape(d, N_EXPERTS),
             jnp.zeros((d, ROUTER_LANES - N_GROUPS - N_EXPERTS), F32)], axis=1).astype(BF16)
        b_router = jnp.concatenate(
            [b_router_group[l], b_router_expert[l].reshape(-1),
             jnp.zeros((ROUTER_LANES - N_GROUPS - N_EXPERTS,), F32)]).reshape(1, ROUTER_LANES)
        h1, hn, logits = outproj_call(z, w_out, l, h, norm_ffn_gain[l], w_router, b_router)

        weights, pos, meta = route_plan_call(logits, n_tiles)
        tile_expert = meta[META_TILE_EXPERT, :n_tiles]
        next_expert = meta[META_NEXT_EXPERT, :n_tiles]
        pad_start = meta[META_PAD_START, :N_EXPERTS]
        n_valid = meta[META_N_VALID, :1]
        xs = dispatch_call(hn, pos, pad_start, n_valid, n_tiles)
        ys = expert_call(xs, tile_expert, n_valid, next_expert, w_expert_gate, w_expert_up, w_expert_down, l)
        last = l == depth - 1
        next_gain = final_norm_gain if last else norm_mix_gain[l + 1]
        xn, h = combine_call(ys, pos, h1, weights, next_gain, F32 if last else BF16, keep_residual=not last)

    return xn.reshape(batch, seq, d)
```

```python
import functools
import math

import numpy as np
import jax
import jax.numpy as jnp
from jax import lax
from jax.experimental import pallas as pl
from jax.experimental.pallas import tpu as pltpu

F32 = jnp.float32
BF16 = jnp.bfloat16

D_MODEL = 2048
POOL_WIDTH = 1024
POOL_WINDOWS = (2, 4, 8, 16)
POOL_GROUP_DIM = 256
POOL_HALO = 16
DIFF_HEADS = 8
DIFF_HEAD_DIM = 64
DIFF_V_DIM = 128
ATTN_WIDTH = 1024
REL_BUCKETS = 32
REL_MAX_DISTANCE = 128
N_GROUPS = 4
EXPERTS_PER_GROUP = 8
N_EXPERTS = 32
D_EXPERT = 512
EPS = 1e-6
NEG_INF = -1e30
LOG2E = 1.4426950408889634

ROUTER_LANES = 128
TQ = 256
TQ_BLOCK = 256
TK = 256
KV_SUPER = 2
V_ROWS = DIFF_V_DIM + 16
HEADS_PER_STEP = 8
TM_EXPERT = 256
LANES = 128
ROW_DMA_UNROLL = 8
MIB = 1024 * 1024


def _cparams(sem, vmem_mib):
    return pltpu.CompilerParams(dimension_semantics=sem, vmem_limit_bytes=vmem_mib * MIB)


def _rms(xf, gain):
    ms = jnp.mean(xf * xf, axis=-1, keepdims=True)
    return xf * lax.rsqrt(ms + EPS) * gain


def _norm_kernel(h_ref, g_ref, o_ref):
    o_ref[...] = _rms(h_ref[...], g_ref[...]).astype(o_ref.dtype)


def rms_norm_call(h, gain, out_dtype, tm=512):
    n, d = h.shape
    return pl.pallas_call(
        _norm_kernel,
        grid=(n // tm,),
        in_specs=[pl.BlockSpec((tm, d), lambda m: (m, 0)),
                  pl.BlockSpec((1, d), lambda m: (0, 0))],
        out_specs=pl.BlockSpec((tm, d), lambda m: (m, 0)),
        out_shape=jax.ShapeDtypeStruct((n, d), out_dtype),
        compiler_params=_cparams(("parallel",), 32),
        name="rms_norm",
    )(h, gain.reshape(1, d))


def _proj_kernel(x_ref, w_ref, o_ref, w_s, *, first_block_scale):
    @pl.when(pl.program_id(1) == 0)
    def _():
        w_s[...] = w_ref[...].astype(BF16)

    acc = jnp.dot(x_ref[...], w_s[...], preferred_element_type=F32)
    if first_block_scale is not None:
        acc = acc * jnp.where(pl.program_id(0) == 0, first_block_scale, 1.0).astype(F32)
    o_ref[...] = acc.astype(o_ref.dtype)


def proj_call(x, w, layer, col_block0, n_col_blocks, tn, out_dtype, first_block_scale=None, tm=1024, name="proj"):
    n, k = x.shape
    return pl.pallas_call(
        functools.partial(_proj_kernel, first_block_scale=first_block_scale),
        grid=(n_col_blocks, n // tm),
        in_specs=[pl.BlockSpec((tm, k), lambda j, m: (m, 0)),
                  pl.BlockSpec((None, k, tn), lambda j, m: (layer, 0, col_block0 + j))],
        out_specs=pl.BlockSpec((tm, tn), lambda j, m: (m, j)),
        out_shape=jax.ShapeDtypeStruct((n, n_col_blocks * tn), out_dtype),
        scratch_shapes=[pltpu.VMEM((k, tn), BF16)],
        compiler_params=_cparams(("arbitrary", "arbitrary"), 48),
        name=name,
    )(x, w)


def _pool_kernel(cur_ref, prev_ref, mix_ref, scale_ref, o_ref, *, tm, seq):
    m = pl.program_id(0)
    row0 = (m * tm) % seq
    cur = cur_ref[...]
    prev = jnp.where(row0 == 0, 0.0, prev_ref[...])
    pos = row0 + lax.broadcasted_iota(jnp.int32, (tm, 1), 0)
    outs = []
    for g, w in enumerate(POOL_WINDOWS):
        sl = slice(g * POOL_GROUP_DIM, (g + 1) * POOL_GROUP_DIM)
        x = jnp.concatenate([prev[:, sl], cur[:, sl]], axis=0)
        s, d = x, 1
        while d < w:
            s = s[:-d] + s[d:]
            d *= 2
        start = POOL_HALO - w + 1
        wsum = s[start:start + tm]
        count = jnp.minimum(pos + 1, w).astype(F32)
        pooled = wsum / count - cur[:, sl]
        mixed = jnp.dot(pooled.astype(BF16), mix_ref[g], preferred_element_type=F32)
        outs.append(mixed * scale_ref[:, sl])
    o_ref[...] = jnp.concatenate(outs, axis=1).astype(o_ref.dtype)


def pool_call(u, mix_bf16, scale, seq, tm=512):
    n, c = u.shape
    blocks_per_tile = tm // POOL_HALO
    return pl.pallas_call(
        functools.partial(_pool_kernel, tm=tm, seq=seq),
        grid=(n // tm,),
        in_specs=[pl.BlockSpec((tm, c), lambda m: (m, 0)),
                  pl.BlockSpec((POOL_HALO, c), lambda m: (jnp.maximum(m * blocks_per_tile - 1, 0), 0)),
                  pl.BlockSpec(mix_bf16.shape, lambda m: (0, 0, 0)),
                  pl.BlockSpec((1, c), lambda m: (0, 0))],
        out_specs=pl.BlockSpec((tm, c), lambda m: (m, 0)),
        out_shape=jax.ShapeDtypeStruct((n, c), BF16),
        compiler_params=_cparams(("parallel",), 32),
        name="pool_mixer",
    )(u, u, mix_bf16, scale.reshape(1, c))


N_BIAS_TILES = 4


def _bucket_tiles():
    kk = np.arange(TK)[:, None]
    qq = np.arange(TQ_BLOCK)[None, :]
    tiles = []
    for rel in (0, 1, 2, -1):
        n = rel * TK + qq - kk
        max_exact = REL_BUCKETS // 2
        nf = np.maximum(n, 1).astype(np.float64)
        large = max_exact + (np.log(nf / max_exact) / math.log(REL_MAX_DISTANCE / max_exact)
                             * (REL_BUCKETS - max_exact)).astype(np.int64)
        large = np.minimum(large, REL_BUCKETS - 1)
        bucket = np.where(n < max_exact, n, large)
        tiles.append(np.where(n < 0, -1, bucket))
    return np.stack(tiles).astype(np.int32)


def _bias_kernel(table_ref, bucket_ref, o_ref):
    h = pl.program_id(0)
    bucket = bucket_ref[...]
    acc = jnp.full(bucket.shape, NEG_INF, F32)
    for b in range(REL_BUCKETS):
        acc = jnp.where(bucket == b, table_ref[b * DIFF_HEADS + h] * LOG2E, acc)
    o_ref[...] = acc


def bias_tiles_call(rel_table):
    bucket = jnp.asarray(_bucket_tiles())
    return pl.pallas_call(
        _bias_kernel,
        grid=(DIFF_HEADS,),
        in_specs=[pl.BlockSpec(memory_space=pltpu.SMEM),
                  pl.BlockSpec(bucket.shape, lambda h: (0, 0, 0))],
        out_specs=pl.BlockSpec((None,) + bucket.shape, lambda h: (h, 0, 0, 0)),
        out_shape=jax.ShapeDtypeStruct((DIFF_HEADS,) + bucket.shape, F32),
        compiler_params=_cparams(("parallel",), 32),
        name="rel_bias_tiles",
    )(rel_table.reshape(-1), bucket)


def _attn_kernel(q_ref, k_ref, v_ref, bias_ref, lam_ref, gain_ref, o_ref, vt_ref, s_ref, acc_ref,
                 *, lambda_init, n_super):
    qi = pl.program_id(2)
    tks = KV_SUPER * TK
    heads = range(HEADS_PER_STEP)

    def block_diag(ref, hh):
        q = ref[:, hh * DIFF_V_DIM:(hh + 1) * DIFF_V_DIM]
        lane = lax.broadcasted_iota(jnp.int32, q.shape, 1)
        zero = jnp.zeros_like(q)
        return jnp.concatenate([jnp.where(lane < DIFF_HEAD_DIM, q, zero),
                                jnp.where(lane >= DIFF_HEAD_DIM, q, zero)], axis=0)

    def scores(t, hh, qd, tile):
        kb = k_ref[pl.ds(pl.multiple_of(t * tks, tks), tks), hh * DIFF_V_DIM:(hh + 1) * DIFF_V_DIM]
        s = lax.dot_general(kb, qd, (((1,), (1,)), ((), ())), preferred_element_type=F32)
        parts = []
        for u in range(KV_SUPER):
            tiles = []
            for c in range(TQ // TQ_BLOCK):
                rel = tile * (TQ // TQ_BLOCK) + c - (t * KV_SUPER + u)
                tiles.append(bias_ref[hh, jnp.where(rel < 0, N_BIAS_TILES - 1, jnp.minimum(rel, 2))])
            parts.append(s[u * TK:(u + 1) * TK] + jnp.concatenate(tiles + tiles, axis=1))
        s = jnp.concatenate(parts, axis=0)
        s_ref[hh] = s
        return jnp.max(s, axis=0, keepdims=True)

    @pl.when(qi == 0)
    def _():
        extra = (lax.broadcasted_iota(jnp.int32, (V_ROWS - DIFF_V_DIM, tks), 0) == 0).astype(BF16)
        for hh in heads:
            cols = slice(hh * DIFF_V_DIM, (hh + 1) * DIFF_V_DIM)
            for c in range(n_super):
                vt = v_ref[c * tks:(c + 1) * tks, cols].astype(F32).T.astype(BF16)
                vt_ref[hh, c] = jnp.concatenate([vt, extra], axis=0)

    qds = [block_diag(q_ref, hh) for hh in heads]

    def softmax_step(t, hh, m_prev, m_cur):
        alpha = jnp.exp2(m_prev - m_cur)
        p = jnp.exp2(s_ref[hh] - m_cur)
        pv = jnp.dot(vt_ref[hh, t], p.astype(BF16), preferred_element_type=F32)
        acc_ref[hh] = acc_ref[hh] * alpha + pv

    last = ((qi + 1) * TQ - 1) // tks
    acc_ref[...] = jnp.zeros(acc_ref.shape, F32)
    neg = jnp.full((1, 2 * TQ), NEG_INF, F32)
    init = tuple((neg, jnp.maximum(neg, scores(0, hh, qds[hh], qi))) for hh in heads)

    def body(t, carry):
        out = []
        for hh in heads:
            m_prev, m_cur = carry[hh]
            softmax_step(t, hh, m_prev, m_cur)
            m_next = jnp.maximum(m_cur, scores(t + 1, hh, qds[hh], qi))
            out.append((m_cur, m_next))
        return tuple(out)

    carry = lax.fori_loop(0, last, body, init)

    lam_p = lam_ref[...]
    lam = (jnp.exp(jnp.sum(lam_p[0:1] * lam_p[1:2], axis=1, keepdims=True))
           - jnp.exp(jnp.sum(lam_p[2:3] * lam_p[3:4], axis=1, keepdims=True)) + lambda_init)
    for hh in heads:
        m_prev, m_cur = carry[hh]
        softmax_step(last, hh, m_prev, m_cur)
        acc = acc_ref[hh, :DIFF_V_DIM, :]
        l = acc_ref[hh, DIFF_V_DIM:DIFF_V_DIM + 1, :]
        o = acc[:, :TQ] / l[:, :TQ] - lam * (acc[:, TQ:] / l[:, TQ:])
        ms = jnp.mean(o * o, axis=0, keepdims=True)
        y = o * lax.rsqrt(ms + EPS) * gain_ref[...] * (1.0 - lambda_init)
        o_ref[:, hh * DIFF_V_DIM:(hh + 1) * DIFF_V_DIM] = y.T.astype(o_ref.dtype)


def attn_call(qkv, bias_tiles, lam_params, subln_gain, lambda_init, batch, seq):
    tks = KV_SUPER * TK
    n_super = seq // tks
    hps = HEADS_PER_STEP
    width = hps * DIFF_V_DIM
    groups = DIFF_HEADS // hps
    n_q = seq // TQ
    return pl.pallas_call(
        functools.partial(_attn_kernel, lambda_init=lambda_init, n_super=n_super),
        grid=(batch, groups, n_q),
        in_specs=[pl.BlockSpec((None, TQ, width), lambda b, g, i: (b, i, g)),
                  pl.BlockSpec((None, seq, width), lambda b, g, i: (b, 0, groups + g), pipeline_mode=pl.Buffered(1)),
                  pl.BlockSpec((None, seq, width), lambda b, g, i: (b, 0, 2 * groups + g),
                               pipeline_mode=pl.Buffered(1)),
                  pl.BlockSpec((hps, N_BIAS_TILES, TK, TQ_BLOCK), lambda b, g, i: (g, 0, 0, 0),
                               pipeline_mode=pl.Buffered(1)),
                  pl.BlockSpec((4, DIFF_HEAD_DIM), lambda b, g, i: (0, 0)),
                  pl.BlockSpec((DIFF_V_DIM, 1), lambda b, g, i: (0, 0))],
        out_specs=pl.BlockSpec((None, TQ, width), lambda b, g, i: (b, i, g)),
        out_shape=jax.ShapeDtypeStruct((batch, seq, ATTN_WIDTH), BF16),
        scratch_shapes=[pltpu.VMEM((hps, n_super, V_ROWS, tks), BF16),
                        pltpu.VMEM((hps, tks, 2 * TQ), F32),
                        pltpu.VMEM((hps, V_ROWS, 2 * TQ), F32)],
        compiler_params=_cparams(("parallel", "parallel", "arbitrary"), 56),
        name="diff_attention",
    )(qkv, qkv, qkv, bias_tiles, lam_params, subln_gain.reshape(DIFF_V_DIM, 1))


def _sigmoid(x):
    return 1.0 / (1.0 + jnp.exp(-x))


def _merge_kernel(xn_ref, mixed_ref, attn_ref, wgp_ref, wga_ref, bgp_ref, bga_ref, wup_ref, wua_ref, z_ref,
                  wgp_s, wga_s, wup_s, wua_s):
    @pl.when(pl.program_id(1) == 0)
    def _():
        wgp_s[...] = wgp_ref[...].astype(BF16)
        wga_s[...] = wga_ref[...].astype(BF16)
        wup_s[...] = wup_ref[...].astype(BF16)
        wua_s[...] = wua_ref[...].astype(BF16)

    xn = xn_ref[...]
    g_pool = _sigmoid(jnp.dot(xn, wgp_s[...], preferred_element_type=F32) + bgp_ref[...])
    g_attn = _sigmoid(jnp.dot(xn, wga_s[...], preferred_element_type=F32) + bga_ref[...])
    y_pool = jnp.dot(mixed_ref[...], wup_s[...], preferred_element_type=F32)
    y_attn = jnp.dot(attn_ref[...], wua_s[...], preferred_element_type=F32)
    z_ref[...] = (g_pool * y_pool + g_attn * y_attn).astype(z_ref.dtype)


def merge_call(xn, mixed, attn, w_gate, b_gate, w_up_pool, w_up_attn, layer, tm=1024, tn=512):
    n, d = xn.shape
    nb = d // tn
    return pl.pallas_call(
        _merge_kernel,
        grid=(nb, n // tm),
        in_specs=[pl.BlockSpec((tm, d), lambda j, m: (m, 0)),
                  pl.BlockSpec((tm, POOL_WIDTH), lambda j, m: (m, 0)),
                  pl.BlockSpec((tm, ATTN_WIDTH), lambda j, m: (m, 0)),
                  pl.BlockSpec((None, d, tn), lambda j, m: (layer, 0, j)),
                  pl.BlockSpec((None, d, tn), lambda j, m: (layer, 0, nb + j)),
                  pl.BlockSpec((None, 1, tn), lambda j, m: (layer, 0, j)),
                  pl.BlockSpec((None, 1, tn), lambda j, m: (layer, 0, nb + j)),
                  pl.BlockSpec((None, POOL_WIDTH, tn), lambda j, m: (layer, 0, j)),
                  pl.BlockSpec((None, ATTN_WIDTH, tn), lambda j, m: (layer, 0, j))],
        out_specs=pl.BlockSpec((tm, tn), lambda j, m: (m, j)),
        out_shape=jax.ShapeDtypeStruct((n, d), BF16),
        scratch_shapes=[pltpu.VMEM((d, tn), BF16), pltpu.VMEM((d, tn), BF16),
                        pltpu.VMEM((POOL_WIDTH, tn), BF16), pltpu.VMEM((ATTN_WIDTH, tn), BF16)],
        compiler_params=_cparams(("arbitrary", "arbitrary"), 56),
        name="gated_merge",
    )(xn, mixed, attn, w_gate, w_gate, b_gate, b_gate, w_up_pool, w_up_attn)


HIGH_HALF = 0xFFFF0000


def _pack_halves(x):
    c = x.shape[1] // 2
    lo = lax.bitcast_convert_type(x[:, :c].astype(BF16).astype(F32), jnp.uint32)
    hi = lax.bitcast_convert_type(x[:, c:].astype(BF16).astype(F32), jnp.uint32)
    return (lo >> 16) | hi


def _unpack_halves(p):
    lo = lax.bitcast_convert_type(p << 16, F32)
    hi = lax.bitcast_convert_type(p & jnp.uint32(HIGH_HALF), F32)
    return lo, hi


ROW_TILE = 8


def _store_row_tiles(ref, packed):
    m = packed.shape[0]
    for s in range(ROW_TILE):
        ref[pl.ds(s, m, stride=ROW_TILE), :] = packed[:, s * LANES:(s + 1) * LANES]


def _load_row_tiles(ref):
    m = ref.shape[0] // ROW_TILE
    return jnp.concatenate([ref[pl.ds(s, m, stride=ROW_TILE), :] for s in range(ROW_TILE)], axis=1)


def _rows(ref, first, count=1):
    return ref.at[pl.ds(pl.multiple_of(first * ROW_TILE, ROW_TILE), count * ROW_TILE)]


def _outproj_kernel(z_ref, w_ref, h_ref, gain_ref, wr_ref, br_ref, h1_ref, hn_ref, logit_ref, w_s):
    @pl.when(pl.program_id(0) == 0)
    def _():
        w_s[...] = w_ref[...].astype(BF16)

    h1 = h_ref[...] + jnp.dot(z_ref[...], w_s[...], preferred_element_type=F32)
    h1_ref[...] = h1
    hn = _rms(h1, gain_ref[...])
    _store_row_tiles(hn_ref, _pack_halves(hn))
    logit_ref[...] = jnp.dot(hn.astype(BF16), wr_ref[...], preferred_element_type=F32) + br_ref[...]


def outproj_call(z, w_out, layer, h, gain, w_router, b_router, tm=256):
    n, d = h.shape
    return pl.pallas_call(
        _outproj_kernel,
        grid=(n // tm,),
        in_specs=[pl.BlockSpec((tm, d), lambda m: (m, 0)),
                  pl.BlockSpec((None, d, d), lambda m: (layer, 0, 0), pipeline_mode=pl.Buffered(1)),
                  pl.BlockSpec((tm, d), lambda m: (m, 0)),
                  pl.BlockSpec((1, d), lambda m: (0, 0)),
                  pl.BlockSpec((d, ROUTER_LANES), lambda m: (0, 0)),
                  pl.BlockSpec((1, ROUTER_LANES), lambda m: (0, 0))],
        out_specs=[pl.BlockSpec((tm, d), lambda m: (m, 0)),
                   pl.BlockSpec((tm * ROW_TILE, LANES), lambda m: (m, 0)),
                   pl.BlockSpec((tm, ROUTER_LANES), lambda m: (m, 0))],
        out_shape=[jax.ShapeDtypeStruct((n, d), F32),
                   jax.ShapeDtypeStruct((n * ROW_TILE, LANES), jnp.uint32),
                   jax.ShapeDtypeStruct((n, ROUTER_LANES), F32)],
        scratch_shapes=[pltpu.VMEM((d, d), BF16)],
        compiler_params=_cparams(("arbitrary",), 56),
        name="out_proj_norm_router",
    )(z, w_out, h, gain.reshape(1, d), w_router, b_router)


def _route(x, lane):
    big = float(ROUTER_LANES)

    def first_argmax(vals):
        top = jnp.max(vals, axis=1, keepdims=True)
        idx = jnp.min(jnp.where(vals == top, lane, big), axis=1, keepdims=True)
        return top, idx

    gmask = lane < N_GROUPS
    g_top, g_sel = first_argmax(jnp.where(gmask, x, -jnp.inf))
    g_weight = 1.0 / jnp.sum(jnp.where(gmask, jnp.exp(x - g_top), 0.0), axis=1, keepdims=True)
    lo = N_GROUPS + EXPERTS_PER_GROUP * g_sel
    e_vals = jnp.where((lane >= lo) & (lane < lo + EXPERTS_PER_GROUP), x, -jnp.inf)
    v1, i1 = first_argmax(e_vals)
    v2, i2 = first_argmax(jnp.where(lane == i1, -jnp.inf, e_vals))
    t = jnp.exp(v2 - v1)
    return i1 - N_GROUPS, i2 - N_GROUPS, g_weight / (1.0 + t), g_weight * t / (1.0 + t)


PLAN_BLOCK = 1024
PLAN_CHUNK = 256
META_TILE_EXPERT, META_NEXT_EXPERT, META_PAD_START, META_N_VALID = 0, 1, 2, 3


def _route_plan_kernel(logit_ref, w_ref, pos_ref, meta_ref, tri_s, cnt_s, base_s, off_s):
    phase = pl.program_id(0)
    blk = pl.program_id(1)
    tb = logit_ref.shape[0]
    lane = lax.broadcasted_iota(jnp.int32, (tb, ROUTER_LANES), 1).astype(F32)
    e0, e1, w0, w1 = _route(logit_ref[...], lane)
    onehot = jnp.where(lane == e0, 1.0, 0.0) + jnp.where(lane == e1, 1.0, 0.0)
    block_counts = jnp.sum(onehot, axis=0, keepdims=True)

    @pl.when(jnp.logical_and(phase == 0, blk == 0))
    def _():
        cnt_s[...] = jnp.zeros(cnt_s.shape, F32)

    @pl.when(phase == 0)
    def _():
        cnt_s[...] = cnt_s[...] + block_counts

    @pl.when(jnp.logical_and(phase == 1, blk == 0))
    def _():
        r = lax.broadcasted_iota(jnp.int32, (tb, tb), 0)
        c = lax.broadcasted_iota(jnp.int32, (tb, tb), 1)
        tri_s[...] = jnp.where(c < r, 1.0, 0.0).astype(BF16)
        lane1 = lane[0:1]
        cnt = cnt_s[...]
        tiles = jnp.floor((cnt + (TM_EXPERT - 1)) * (1.0 / TM_EXPERT))
        ri = lax.broadcasted_iota(jnp.int32, (ROUTER_LANES, ROUTER_LANES), 0)
        ci = lax.broadcasted_iota(jnp.int32, (ROUTER_LANES, ROUTER_LANES), 1)
        upper = jnp.where(ri <= ci, 1.0, 0.0).astype(BF16)
        ends = jnp.dot(jnp.broadcast_to(tiles, (8, ROUTER_LANES)).astype(BF16), upper,
                       preferred_element_type=F32)[0:1]
        off_s[...] = (ends - tiles) * TM_EXPERT
        base_s[...] = jnp.zeros(base_s.shape, F32)

        def pick(vec, e):
            return jnp.sum(jnp.where(lane1 == e, vec, 0.0), axis=1, keepdims=True)

        end_of = [pick(ends, e) for e in range(N_EXPERTS)]
        n_valid = end_of[N_EXPERTS - 1]

        def segment_of(tile):
            return sum(jnp.where(tile >= end_e, 1.0, 0.0) for end_e in end_of)

        tile_expert = jnp.where(lane1 < n_valid, segment_of(lane1), segment_of(n_valid - 1.0))
        following = sum(jnp.where(tile_expert == e, end_of[e], 0.0) for e in range(N_EXPERTS))
        next_expert = jnp.where(following < n_valid, segment_of(following), -1.0)
        rows = [tile_expert, next_expert, off_s[...] + cnt, jnp.broadcast_to(n_valid, (1, ROUTER_LANES))]
        rows += [jnp.zeros((1, ROUTER_LANES), F32)] * (meta_ref.shape[0] - len(rows))
        meta_ref[...] = jnp.concatenate(rows, axis=0).astype(jnp.int32)

    @pl.when(phase == 1)
    def _():
        before = jnp.dot(tri_s[...], onehot.astype(BF16), preferred_element_type=F32)
        row = before + base_s[...] + off_s[...]
        base_s[...] = base_s[...] + block_counts
        w_ref[...] = jnp.where(lane == 0, w0, jnp.where(lane == 1, w1, 0.0))
        eye = (lax.broadcasted_iota(jnp.int32, (LANES, LANES), 0)
               == lax.broadcasted_iota(jnp.int32, (LANES, LANES), 1))
        for j, e in enumerate((e0, e1)):
            col = jnp.sum(jnp.where(lane == e, row, 0.0), axis=1, keepdims=True)
            for q in range(tb // PLAN_CHUNK):
                parts = []
                for g in range(PLAN_CHUNK // LANES):
                    t0 = q * PLAN_CHUNK + g * LANES
                    square = jnp.broadcast_to(col[t0:t0 + LANES], (LANES, LANES))
                    parts.append(jnp.sum(jnp.where(eye, square, 0.0), axis=0, keepdims=True))
                pos_ref[j, q] = jnp.concatenate(parts, axis=1).astype(jnp.int32)


def route_plan_call(logits, n_tiles):
    n = logits.shape[0]
    tb = PLAN_BLOCK
    chunks = tb // PLAN_CHUNK
    assert n_tiles <= ROUTER_LANES and N_EXPERTS <= ROUTER_LANES
    return pl.pallas_call(
        _route_plan_kernel,
        grid=(2, n // tb),
        in_specs=[pl.BlockSpec((tb, ROUTER_LANES), lambda p, b: (b, 0))],
        out_specs=[pl.BlockSpec((tb, ROUTER_LANES), lambda p, b: (p * b, 0)),
                   pl.BlockSpec((2, chunks, 1, PLAN_CHUNK), lambda p, b: (0, p * b, 0, 0)),
                   pl.BlockSpec((8, ROUTER_LANES), lambda p, b: (0, 0))],
        out_shape=[jax.ShapeDtypeStruct((n, ROUTER_LANES), F32),
                   jax.ShapeDtypeStruct((2, n // PLAN_CHUNK, 1, PLAN_CHUNK), jnp.int32),
                   jax.ShapeDtypeStruct((8, ROUTER_LANES), jnp.int32)],
        scratch_shapes=[pltpu.VMEM((tb, tb), BF16), pltpu.VMEM((1, ROUTER_LANES), F32),
                        pltpu.VMEM((1, ROUTER_LANES), F32), pltpu.VMEM((1, ROUTER_LANES), F32)],
        compiler_params=_cparams(("arbitrary", "arbitrary"), 32),
        name="route_plan",
    )(logits)


DISPATCH_SLOTS = 3


def _dispatch_kernel(pad_ref, nv_ref, pos0_ref, pos1_ref, src_hbm, dst_ref, zero_buf, ring, sem, in_sems, row_sems,
                     *, chunk, n_tiles):
    i = pl.program_id(0)
    steps = pl.num_programs(0)
    slot = i % DISPATCH_SLOTS

    def load(block, into):
        return pltpu.make_async_copy(_rows(src_hbm, block * chunk, chunk), ring.at[into], in_sems.at[into])

    def wait_rows(of_slot):
        for j in range(2):
            pltpu.make_async_copy(ring.at[of_slot], _rows(dst_ref, 0, chunk), row_sems.at[of_slot]).wait()

    @pl.when(i == 0)
    def _():
        load(0, 0).start()
        zero_buf[...] = jnp.zeros(zero_buf.shape, zero_buf.dtype)

        def fill(e):
            return pltpu.make_async_copy(zero_buf, _rows(dst_ref, pad_ref[e], TM_EXPERT), sem)

        for e in range(N_EXPERTS):
            fill(e).start()
        for e in range(N_EXPERTS):
            fill(e).wait()

        def fill_tile(i):
            return pltpu.make_async_copy(zero_buf, _rows(dst_ref, i * TM_EXPERT, TM_EXPERT), sem)

        def start_tile(i, c):
            fill_tile(i).start()
            return c

        def wait_tile(i, c):
            fill_tile(i).wait()
            return c

        lax.fori_loop(nv_ref[0], n_tiles, start_tile, 0)
        lax.fori_loop(nv_ref[0], n_tiles, wait_tile, 0)

    nxt = (i + 1) % DISPATCH_SLOTS

    @pl.when(i >= DISPATCH_SLOTS - 1)
    def _():
        wait_rows(nxt)

    @pl.when(i + 1 < steps)
    def _():
        load(i + 1, nxt).start()

    load(i, slot).wait()
    src_ref = ring.at[slot]

    def start(t, c):
        for j, pos_ref in enumerate((pos0_ref, pos1_ref)):
            pltpu.make_async_copy(_rows(src_ref, t), _rows(dst_ref, pos_ref[0, 0, t]),
                                  row_sems.at[slot]).start(priority=j)
        return c

    lax.fori_loop(0, chunk, start, 0, unroll=ROW_DMA_UNROLL)

    @pl.when(i == steps - 1)
    def _():
        for back in range(DISPATCH_SLOTS - 1):
            @pl.when(i - back >= 0)
            def _():
                wait_rows((i - back) % DISPATCH_SLOTS)


def dispatch_call(hn, pos, pad_start, n_valid_tiles, n_tiles):
    n = hn.shape[0] // ROW_TILE
    chunk = PLAN_CHUNK
    grid_spec = pltpu.PrefetchScalarGridSpec(
        num_scalar_prefetch=2,
        grid=(n // chunk,),
        in_specs=[pl.BlockSpec((None, 1, 1, chunk), lambda i, pad, nv: (0, i, 0, 0), memory_space=pltpu.SMEM),
                  pl.BlockSpec((None, 1, 1, chunk), lambda i, pad, nv: (1, i, 0, 0), memory_space=pltpu.SMEM),
                  pl.BlockSpec(memory_space=pl.ANY)],
        out_specs=pl.BlockSpec(memory_space=pl.ANY),
        scratch_shapes=[pltpu.VMEM((TM_EXPERT * ROW_TILE, LANES), hn.dtype),
                        pltpu.VMEM((DISPATCH_SLOTS, chunk * ROW_TILE, LANES), hn.dtype),
                        pltpu.SemaphoreType.DMA(()), pltpu.SemaphoreType.DMA((DISPATCH_SLOTS,)),
                        pltpu.SemaphoreType.DMA((DISPATCH_SLOTS,))],
    )
    return pl.pallas_call(
        functools.partial(_dispatch_kernel, chunk=chunk, n_tiles=n_tiles),
        grid_spec=grid_spec,
        out_shape=jax.ShapeDtypeStruct((n_tiles * TM_EXPERT * ROW_TILE, LANES), hn.dtype),
        compiler_params=_cparams(("arbitrary",), 32),
        name="dispatch_rows",
    )(pad_start, n_valid_tiles, pos, pos, hn)


def _expert_kernel(te_ref, nv_ref, nx_ref, x_ref, wg_hbm, wu_hbm, wd_hbm, y_ref,
                   wg_f, wu_f, wd_f, wg_s, wu_s, wd_s, seg_ref, sems, *, layer):
    i = pl.program_id(0)
    valid = i < nv_ref[0]
    expert = te_ref[i]
    changed = jnp.logical_or(i == 0, expert != te_ref[jnp.maximum(i - 1, 0)])

    def fetch(e, slot):
        return [pltpu.make_async_copy(hbm.at[layer, e], buf.at[slot], sems.at[slot])
                for hbm, buf in ((wg_hbm, wg_f), (wu_hbm, wu_f), (wd_hbm, wd_f))]

    def swiglu_tile():
        lo, hi = _unpack_halves(_load_row_tiles(x_ref))
        x = jnp.concatenate([lo.astype(BF16), hi.astype(BF16)], axis=1)
        a = jnp.dot(x, wg_s[...], preferred_element_type=F32)
        b = jnp.dot(x, wu_s[...], preferred_element_type=F32)
        hmid = (a * _sigmoid(a) * b).astype(BF16)
        _store_row_tiles(y_ref, _pack_halves(jnp.dot(hmid, wd_s[...], preferred_element_type=F32)))

    @pl.when(i == 0)
    def _():
        seg_ref[0] = 0
        for cp in fetch(expert, 0):
            cp.start()

    @pl.when(jnp.logical_and(valid, changed))
    def _():
        slot = seg_ref[0] % 2
        for cp in fetch(expert, slot):
            cp.wait()
        nxt = nx_ref[i]

        @pl.when(nxt >= 0)
        def _():
            for cp in fetch(nxt, 1 - slot):
                cp.start()

        seg_ref[0] = seg_ref[0] + 1
        wg_s[...] = wg_f[slot].astype(BF16)
        wu_s[...] = wu_f[slot].astype(BF16)
        wd_s[...] = wd_f[slot].astype(BF16)
        swiglu_tile()

    @pl.when(jnp.logical_and(valid, jnp.logical_not(changed)))
    def _():
        swiglu_tile()

    @pl.when(jnp.logical_not(valid))
    def _():
        y_ref[...] = jnp.zeros(y_ref.shape, y_ref.dtype)


def expert_call(xs, tile_expert, n_valid_tiles, next_expert, w_gate, w_up, w_down, layer):
    p = xs.shape[0] // ROW_TILE
    d = 2 * ROW_TILE * LANES
    tm = TM_EXPERT
    f = w_gate.shape[-1]
    grid_spec = pltpu.PrefetchScalarGridSpec(
        num_scalar_prefetch=3,
        grid=(p // tm,),
        in_specs=[pl.BlockSpec((tm * ROW_TILE, LANES), lambda i, te, nv, nx: (jnp.minimum(i, nv[0] - 1), 0)),
                  pl.BlockSpec(memory_space=pl.ANY),
                  pl.BlockSpec(memory_space=pl.ANY),
                  pl.BlockSpec(memory_space=pl.ANY)],
        out_specs=pl.BlockSpec((tm * ROW_TILE, LANES), lambda i, te, nv, nx: (i, 0)),
        scratch_shapes=[pltpu.VMEM((2, d, f), F32), pltpu.VMEM((2, d, f), F32), pltpu.VMEM((2, f, d), F32),
                        pltpu.VMEM((d, f), BF16), pltpu.VMEM((d, f), BF16), pltpu.VMEM((f, d), BF16),
                        pltpu.SMEM((1,), jnp.int32), pltpu.SemaphoreType.DMA((2,))],
    )
    return pl.pallas_call(
        functools.partial(_expert_kernel, layer=layer),
        grid_spec=grid_spec,
        out_shape=jax.ShapeDtypeStruct(xs.shape, jnp.uint32),
        compiler_params=_cparams(("arbitrary",), 58),
        name="expert_swiglu",
    )(tile_expert, n_valid_tiles, next_expert, xs, w_gate, w_up, w_down)


def _combine_kernel(pos0_ref, pos1_ref, nxt0_ref, nxt1_ref, ys_ref, h_ref, w_ref, gain_ref, *rest, chunk):
    *out_refs, buf, sems = rest
    xn_ref, h2_ref = out_refs if len(out_refs) == 2 else (out_refs[0], None)
    i = pl.program_id(0)
    slot = i % 2

    def gather(p0_ref, p1_ref, into):
        def start(t, c):
            for j, pos_ref in enumerate((p0_ref, p1_ref)):
                pltpu.make_async_copy(_rows(ys_ref, pos_ref[0, 0, t]), _rows(buf.at[into, j], t),
                                      sems.at[into]).start(priority=j)
            return c

        lax.fori_loop(0, chunk, start, 0, unroll=ROW_DMA_UNROLL)

    @pl.when(i == 0)
    def _():
        gather(pos0_ref, pos1_ref, 0)

    @pl.when(i + 1 < pl.num_programs(0))
    def _():
        gather(nxt0_ref, nxt1_ref, 1 - slot)

    for j in range(2):
        pltpu.make_async_copy(_rows(ys_ref, 0, chunk), buf.at[slot, j], sems.at[slot]).wait()
    w = w_ref[...]
    lo0, hi0 = _unpack_halves(_load_row_tiles(buf.at[slot, 0]))
    lo1, hi1 = _unpack_halves(_load_row_tiles(buf.at[slot, 1]))
    y = jnp.concatenate([w[:, 0:1] * lo0 + w[:, 1:2] * lo1, w[:, 0:1] * hi0 + w[:, 1:2] * hi1], axis=1)
    h2 = h_ref[...] + y
    if h2_ref is not None:
        h2_ref[...] = h2
    xn_ref[...] = _rms(h2, gain_ref[...]).astype(xn_ref.dtype)


def combine_call(ys, pos, h1, weights, next_gain, xn_dtype, keep_residual, chunk=128):
    n, d = h1.shape
    row_spec = pl.BlockSpec((chunk, d), lambda c: (c, 0))
    out_specs = [row_spec, row_spec] if keep_residual else [row_spec]
    out_shape = [jax.ShapeDtypeStruct((n, d), xn_dtype)] + ([jax.ShapeDtypeStruct((n, d), F32)] if keep_residual else [])
    per_row = PLAN_CHUNK // chunk
    steps = n // chunk

    def pos_spec(j, ahead):
        def index(c):
            c = jnp.minimum(c + ahead, steps - 1)
            return (j, c // per_row, 0, c % per_row)
        return pl.BlockSpec((None, 1, 1, chunk), index, memory_space=pltpu.SMEM)

    outs = pl.pallas_call(
        functools.partial(_combine_kernel, chunk=chunk),
        grid=(steps,),
        in_specs=[pos_spec(0, 0), pos_spec(1, 0), pos_spec(0, 1), pos_spec(1, 1),
                  pl.BlockSpec(memory_space=pl.ANY),
                  row_spec,
                  pl.BlockSpec((chunk, ROUTER_LANES), lambda c: (c, 0)),
                  pl.BlockSpec((1, d), lambda c: (0, 0))],
        out_specs=out_specs,
        out_shape=out_shape,
        scratch_shapes=[pltpu.VMEM((2, 2, chunk * ROW_TILE, LANES), jnp.uint32), pltpu.SemaphoreType.DMA((2,))],
        compiler_params=_cparams(("arbitrary",), 32),
        name="combine_rows",
    )(pos, pos, pos, pos, ys, h1, weights, next_gain.reshape(1, d))
    return (outs[0], outs[1]) if keep_residual else (outs[0], None)


def kernel(x, rel_bias_table, norm_mix_gain, w_in, w_merge_gate, b_merge_gate, pool_mix, pool_scale, w_up_pool, lambda_q1, lambda_k1, lambda_q2, lambda_k2, subln_gain, w_up_attn, w_out, norm_ffn_gain, w_router_group, b_router_group, w_router_expert, b_router_expert, w_expert_gate, w_expert_up, w_expert_down, final_norm_gain):
    batch, seq, d = x.shape
    depth = w_in.shape[0]
    n = batch * seq
    n_tiles = (2 * n + N_EXPERTS * (TM_EXPERT - 1)) // TM_EXPERT + 1

    bias_tiles = bias_tiles_call(rel_bias_table)
    h = x.reshape(n, d)
    xn = rms_norm_call(h, norm_mix_gain[0], BF16)
    q_scale = LOG2E * DIFF_HEAD_DIM ** -0.5

    for l in range(depth):
        u = proj_call(xn, w_in, l, 0, POOL_WIDTH // 512, 512, F32, name="in_proj_pool")
        qkv = proj_call(xn, w_in, l, 1, 3, ATTN_WIDTH, BF16, first_block_scale=q_scale, name="in_proj_qkv")
        mixed = pool_call(u, pool_mix[l].astype(BF16), pool_scale[l], seq)
        lambda_init = 0.8 - 0.6 * math.exp(-0.3 * l)
        lam_params = jnp.stack([lambda_q1[l], lambda_k1[l], lambda_q2[l], lambda_k2[l]])
        attn = attn_call(qkv.reshape(batch, seq, 3 * ATTN_WIDTH), bias_tiles, lam_params, subln_gain[l],
                         lambda_init, batch, seq).reshape(n, ATTN_WIDTH)
        z = merge_call(xn, mixed, attn, w_merge_gate, b_merge_gate.reshape(depth, 1, -1), w_up_pool, w_up_attn, l)

        w_router = jnp.concatenate(
            [w_router_group[l], jnp.transpose(w_router_expert[l], (1, 0, 2)).reshape(d, N_EXPERTS),
             jnp.zeros((d, ROUTER_LANES - N_GROUPS - N_EXPERTS), F32)], axis=1).astype(BF16)
        b_router = jnp.concatenate(
            [b_router_group[l], b_router_expert[l].reshape(-1),
             jnp.zeros((ROUTER_LANES - N_GROUPS - N_EXPERTS,), F32)]).reshape(1, ROUTER_LANES)
        h1, hn, logits = outproj_call(z, w_out, l, h, norm_ffn_gain[l], w_router, b_router)

        weights, pos, meta = route_plan_call(logits, n_tiles)
        tile_expert = meta[META_TILE_EXPERT, :n_tiles]
        next_expert = meta[META_NEXT_EXPERT, :n_tiles]
        pad_start = meta[META_PAD_START, :N_EXPERTS]
        n_valid = meta[META_N_VALID, :1]
        xs = dispatch_call(hn, pos, pad_start, n_valid, n_tiles)
        ys = expert_call(xs, tile_expert, n_valid, next_expert, w_expert_gate, w_expert_up, w_expert_down, l)
        last = l == depth - 1
        next_gain = final_norm_gain if last else norm_mix_gain[l + 1]
        xn, h = combine_call(ys, pos, h1, weights, next_gain, F32 if last else BF16, keep_residual=not last)

    return xn.reshape(batch, seq, d)
```

```python
import functools
import math

import numpy as np
import jax
import jax.numpy as jnp
from jax import lax
from jax.experimental import pallas as pl
from jax.experimental.pallas import tpu as pltpu

F32 = jnp.float32
BF16 = jnp.bfloat16

D_MODEL = 2048
POOL_WIDTH = 1024
POOL_WINDOWS = (2, 4, 8, 16)
POOL_GROUP_DIM = 256
POOL_HALO = 16
DIFF_HEADS = 8
DIFF_HEAD_DIM = 64
DIFF_V_DIM = 128
ATTN_WIDTH = 1024
REL_BUCKETS = 32
REL_MAX_DISTANCE = 128
N_GROUPS = 4
EXPERTS_PER_GROUP = 8
N_EXPERTS = 32
D_EXPERT = 512
EPS = 1e-6
NEG_INF = -1e30
LOG2E = 1.4426950408889634

ROUTER_LANES = 128
TQ = 256
TQ_BLOCK = 256
TK = 256
KV_SUPER = 2
V_ROWS = DIFF_V_DIM + 16
HEADS_PER_STEP = 8
TM_EXPERT = 256
LANES = 128
ROW_DMA_UNROLL = 8
MIB = 1024 * 1024


def _cparams(sem, vmem_mib):
    return pltpu.CompilerParams(dimension_semantics=sem, vmem_limit_bytes=vmem_mib * MIB)


def _rms(xf, gain):
    ms = jnp.mean(xf * xf, axis=-1, keepdims=True)
    return xf * lax.rsqrt(ms + EPS) * gain


def _proj_kernel(x_ref, w_ref, o_ref, w_s, *, first_block_scale):
    @pl.when(pl.program_id(1) == 0)
    def _():
        w_s[...] = w_ref[...].astype(BF16)

    acc = jnp.dot(x_ref[...], w_s[...], preferred_element_type=F32)
    if first_block_scale is not None:
        acc = acc * jnp.where(pl.program_id(0) == 0, first_block_scale, 1.0).astype(F32)
    o_ref[...] = acc.astype(o_ref.dtype)


def proj_call(x, w, layer, col_block0, n_col_blocks, tn, out_dtype, first_block_scale=None, tm=1024, name="proj"):
    n, k = x.shape
    return pl.pallas_call(
        functools.partial(_proj_kernel, first_block_scale=first_block_scale),
        grid=(n_col_blocks, n // tm),
        in_specs=[pl.BlockSpec((tm, k), lambda j, m: (m, 0)),
                  pl.BlockSpec((None, k, tn), lambda j, m: (layer, 0, col_block0 + j))],
        out_specs=pl.BlockSpec((tm, tn), lambda j, m: (m, j)),
        out_shape=jax.ShapeDtypeStruct((n, n_col_blocks * tn), out_dtype),
        scratch_shapes=[pltpu.VMEM((k, tn), BF16)],
        compiler_params=_cparams(("arbitrary", "arbitrary"), 48),
        name=name,
    )(x, w)


def _pool_branch_kernel(x_ref, gain_ref, w_ref, mix_ref, scale_ref, *rest, tm, seq, normalize):
    if normalize:
        xn_ref, o_ref, w_s, carry = rest
    else:
        o_ref, w_s, carry = rest
    m = pl.program_id(0)

    @pl.when(m == 0)
    def _():
        w_s[...] = w_ref[...].astype(BF16)

    if normalize:
        xn = _rms(x_ref[...], gain_ref[...]).astype(BF16)
        xn_ref[...] = xn
    else:
        xn = x_ref[...]
    cur = jnp.dot(xn, w_s[...], preferred_element_type=F32)
    row0 = (m * tm) % seq
    prev = jnp.where(row0 == 0, 0.0, carry[...])
    carry[...] = cur[tm - POOL_HALO:]
    pos = row0 + lax.broadcasted_iota(jnp.int32, (tm, 1), 0)
    outs = []
    for g, w in enumerate(POOL_WINDOWS):
        sl = slice(g * POOL_GROUP_DIM, (g + 1) * POOL_GROUP_DIM)
        x = jnp.concatenate([prev[:, sl], cur[:, sl]], axis=0)
        s, d = x, 1
        while d < w:
            s = s[:-d] + s[d:]
            d *= 2
        start = POOL_HALO - w + 1
        wsum = s[start:start + tm]
        count = jnp.minimum(pos + 1, w).astype(F32)
        pooled = wsum / count - cur[:, sl]
        mixed = jnp.dot(pooled.astype(BF16), mix_ref[g], preferred_element_type=F32)
        outs.append(mixed * scale_ref[:, sl])
    o_ref[...] = jnp.concatenate(outs, axis=1).astype(o_ref.dtype)


def pool_branch_call(x, gain, w_in, layer, mix_bf16, scale, seq, normalize, tm=512):
    n, d = x.shape
    c = POOL_WIDTH
    row_spec = pl.BlockSpec((tm, c), lambda m: (m, 0))
    x_spec = pl.BlockSpec((tm, d), lambda m: (m, 0))
    out_specs = [x_spec, row_spec] if normalize else [row_spec]
    out_shape = ([jax.ShapeDtypeStruct((n, d), BF16)] if normalize else []) + [jax.ShapeDtypeStruct((n, c), BF16)]
    outs = pl.pallas_call(
        functools.partial(_pool_branch_kernel, tm=tm, seq=seq, normalize=normalize),
        grid=(n // tm,),
        in_specs=[x_spec,
                  pl.BlockSpec((1, d), lambda m: (0, 0)),
                  pl.BlockSpec((None, d, c), lambda m: (layer, 0, 0), pipeline_mode=pl.Buffered(1)),
                  pl.BlockSpec(mix_bf16.shape, lambda m: (0, 0, 0)),
                  pl.BlockSpec((1, c), lambda m: (0, 0))],
        out_specs=out_specs,
        out_shape=out_shape,
        scratch_shapes=[pltpu.VMEM((d, c), BF16), pltpu.VMEM((POOL_HALO, c), F32)],
        compiler_params=_cparams(("arbitrary",), 48),
        name="pool_branch",
    )(x, gain.reshape(1, d), w_in, mix_bf16, scale.reshape(1, c))
    return tuple(outs) if normalize else (None, outs[0])


N_BIAS_TILES = 4


def _bucket_tiles():
    kk = np.arange(TK)[:, None]
    qq = np.arange(TQ_BLOCK)[None, :]
    tiles = []
    for rel in (0, 1, 2, -1):
        n = rel * TK + qq - kk
        max_exact = REL_BUCKETS // 2
        nf = np.maximum(n, 1).astype(np.float64)
        large = max_exact + (np.log(nf / max_exact) / math.log(REL_MAX_DISTANCE / max_exact)
                             * (REL_BUCKETS - max_exact)).astype(np.int64)
        large = np.minimum(large, REL_BUCKETS - 1)
        bucket = np.where(n < max_exact, n, large)
        tiles.append(np.where(n < 0, -1, bucket))
    return np.stack(tiles).astype(np.int32)


def _bias_kernel(table_ref, bucket_ref, o_ref):
    h = pl.program_id(0)
    bucket = bucket_ref[...]
    acc = jnp.full(bucket.shape, NEG_INF, F32)
    for b in range(REL_BUCKETS):
        acc = jnp.where(bucket == b, table_ref[b * DIFF_HEADS + h] * LOG2E, acc)
    o_ref[...] = acc


def bias_tiles_call(rel_table):
    bucket = jnp.asarray(_bucket_tiles())
    return pl.pallas_call(
        _bias_kernel,
        grid=(DIFF_HEADS,),
        in_specs=[pl.BlockSpec(memory_space=pltpu.SMEM),
                  pl.BlockSpec(bucket.shape, lambda h: (0, 0, 0))],
        out_specs=pl.BlockSpec((None,) + bucket.shape, lambda h: (h, 0, 0, 0)),
        out_shape=jax.ShapeDtypeStruct((DIFF_HEADS,) + bucket.shape, F32),
        compiler_params=_cparams(("parallel",), 32),
        name="rel_bias_tiles",
    )(rel_table.reshape(-1), bucket)


def _attn_kernel(q_ref, k_ref, v_ref, bias_ref, lam_ref, gain_ref, o_ref, vt_ref, s_ref, acc_ref,
                 *, lambda_init, n_super):
    qi = pl.program_id(2)
    tks = KV_SUPER * TK
    heads = range(HEADS_PER_STEP)

    def block_diag(ref, hh):
        q = ref[:, hh * DIFF_V_DIM:(hh + 1) * DIFF_V_DIM]
        lane = lax.broadcasted_iota(jnp.int32, q.shape, 1)
        zero = jnp.zeros_like(q)
        return jnp.concatenate([jnp.where(lane < DIFF_HEAD_DIM, q, zero),
                                jnp.where(lane >= DIFF_HEAD_DIM, q, zero)], axis=0)

    def scores(t, hh, qd, tile):
        kb = k_ref[pl.ds(pl.multiple_of(t * tks, tks), tks), hh * DIFF_V_DIM:(hh + 1) * DIFF_V_DIM]
        s = lax.dot_general(kb, qd, (((1,), (1,)), ((), ())), preferred_element_type=F32)
        parts = []
        for u in range(KV_SUPER):
            tiles = []
            for c in range(TQ // TQ_BLOCK):
                rel = tile * (TQ // TQ_BLOCK) + c - (t * KV_SUPER + u)
                tiles.append(bias_ref[hh, jnp.where(rel < 0, N_BIAS_TILES - 1, jnp.minimum(rel, 2))])
            parts.append(s[u * TK:(u + 1) * TK] + jnp.concatenate(tiles + tiles, axis=1))
        s = jnp.concatenate(parts, axis=0)
        s_ref[hh] = s
        return jnp.max(s, axis=0, keepdims=True)

    @pl.when(qi == 0)
    def _():
        extra = (lax.broadcasted_iota(jnp.int32, (V_ROWS - DIFF_V_DIM, tks), 0) == 0).astype(BF16)
        for hh in heads:
            cols = slice(hh * DIFF_V_DIM, (hh + 1) * DIFF_V_DIM)
            for c in range(n_super):
                vt = v_ref[c * tks:(c + 1) * tks, cols].astype(F32).T.astype(BF16)
                vt_ref[hh, c] = jnp.concatenate([vt, extra], axis=0)

    qds = [block_diag(q_ref, hh) for hh in heads]

    def softmax_step(t, hh, m_prev, m_cur):
        alpha = jnp.exp2(m_prev - m_cur)
        p = jnp.exp2(s_ref[hh] - m_cur)
        pv = jnp.dot(vt_ref[hh, t], p.astype(BF16), preferred_element_type=F32)
        acc_ref[hh] = acc_ref[hh] * alpha + pv

    last = ((qi + 1) * TQ - 1) // tks
    acc_ref[...] = jnp.zeros(acc_ref.shape, F32)
    neg = jnp.full((1, 2 * TQ), NEG_INF, F32)
    init = tuple((neg, jnp.maximum(neg, scores(0, hh, qds[hh], qi))) for hh in heads)

    def body(t, carry):
        out = []
        for hh in heads:
            m_prev, m_cur = carry[hh]
            softmax_step(t, hh, m_prev, m_cur)
            m_next = jnp.maximum(m_cur, scores(t + 1, hh, qds[hh], qi))
            out.append((m_cur, m_next))
        return tuple(out)

    carry = lax.fori_loop(0, last, body, init)

    lam_p = lam_ref[...]
    lam = (jnp.exp(jnp.sum(lam_p[0:1] * lam_p[1:2], axis=1, keepdims=True))
           - jnp.exp(jnp.sum(lam_p[2:3] * lam_p[3:4], axis=1, keepdims=True)) + lambda_init)
    for hh in heads:
        m_prev, m_cur = carry[hh]
        softmax_step(last, hh, m_prev, m_cur)
        acc = acc_ref[hh, :DIFF_V_DIM, :]
        l = acc_ref[hh, DIFF_V_DIM:DIFF_V_DIM + 1, :]
        o = acc[:, :TQ] / l[:, :TQ] - lam * (acc[:, TQ:] / l[:, TQ:])
        ms = jnp.mean(o * o, axis=0, keepdims=True)
        y = o * lax.rsqrt(ms + EPS) * gain_ref[...] * (1.0 - lambda_init)
        o_ref[:, hh * DIFF_V_DIM:(hh + 1) * DIFF_V_DIM] = y.T.astype(o_ref.dtype)


def attn_call(qkv, bias_tiles, lam_params, subln_gain, lambda_init, batch, seq):
    tks = KV_SUPER * TK
    n_super = seq // tks
    hps = HEADS_PER_STEP
    width = hps * DIFF_V_DIM
    groups = DIFF_HEADS // hps
    n_q = seq // TQ
    return pl.pallas_call(
        functools.partial(_attn_kernel, lambda_init=lambda_init, n_super=n_super),
        grid=(batch, groups, n_q),
        in_specs=[pl.BlockSpec((None, TQ, width), lambda b, g, i: (b, i, g)),
                  pl.BlockSpec((None, seq, width), lambda b, g, i: (b, 0, groups + g), pipeline_mode=pl.Buffered(1)),
                  pl.BlockSpec((None, seq, width), lambda b, g, i: (b, 0, 2 * groups + g),
                               pipeline_mode=pl.Buffered(1)),
                  pl.BlockSpec((hps, N_BIAS_TILES, TK, TQ_BLOCK), lambda b, g, i: (g, 0, 0, 0),
                               pipeline_mode=pl.Buffered(1)),
                  pl.BlockSpec((4, DIFF_HEAD_DIM), lambda b, g, i: (0, 0)),
                  pl.BlockSpec((DIFF_V_DIM, 1), lambda b, g, i: (0, 0))],
        out_specs=pl.BlockSpec((None, TQ, width), lambda b, g, i: (b, i, g)),
        out_shape=jax.ShapeDtypeStruct((batch, seq, ATTN_WIDTH), BF16),
        scratch_shapes=[pltpu.VMEM((hps, n_super, V_ROWS, tks), BF16),
                        pltpu.VMEM((hps, tks, 2 * TQ), F32),
                        pltpu.VMEM((hps, V_ROWS, 2 * TQ), F32)],
        compiler_params=_cparams(("parallel", "parallel", "arbitrary"), 56),
        name="diff_attention",
    )(qkv, qkv, qkv, bias_tiles, lam_params, subln_gain.reshape(DIFF_V_DIM, 1))


def _sigmoid(x):
    return 1.0 / (1.0 + jnp.exp(-x))


def _merge_kernel(xn_ref, mixed_ref, attn_ref, wgp_ref, wga_ref, bgp_ref, bga_ref, wup_ref, wua_ref, z_ref,
                  wgp_s, wga_s, wup_s, wua_s):
    @pl.when(pl.program_id(1) == 0)
    def _():
        wgp_s[...] = wgp_ref[...].astype(BF16)
        wga_s[...] = wga_ref[...].astype(BF16)
        wup_s[...] = wup_ref[...].astype(BF16)
        wua_s[...] = wua_ref[...].astype(BF16)

    xn = xn_ref[...]
    g_pool = _sigmoid(jnp.dot(xn, wgp_s[...], preferred_element_type=F32) + bgp_ref[...])
    g_attn = _sigmoid(jnp.dot(xn, wga_s[...], preferred_element_type=F32) + bga_ref[...])
    y_pool = jnp.dot(mixed_ref[...], wup_s[...], preferred_element_type=F32)
    y_attn = jnp.dot(attn_ref[...], wua_s[...], preferred_element_type=F32)
    z_ref[...] = (g_pool * y_pool + g_attn * y_attn).astype(z_ref.dtype)


def merge_call(xn, mixed, attn, w_gate, b_gate, w_up_pool, w_up_attn, layer, tm=1024, tn=512):
    n, d = xn.shape
    nb = d // tn
    return pl.pallas_call(
        _merge_kernel,
        grid=(nb, n // tm),
        in_specs=[pl.BlockSpec((tm, d), lambda j, m: (m, 0)),
                  pl.BlockSpec((tm, POOL_WIDTH), lambda j, m: (m, 0)),
                  pl.BlockSpec((tm, ATTN_WIDTH), lambda j, m: (m, 0)),
                  pl.BlockSpec((None, d, tn), lambda j, m: (layer, 0, j)),
                  pl.BlockSpec((None, d, tn), lambda j, m: (layer, 0, nb + j)),
                  pl.BlockSpec((None, 1, tn), lambda j, m: (layer, 0, j)),
                  pl.BlockSpec((None, 1, tn), lambda j, m: (layer, 0, nb + j)),
                  pl.BlockSpec((None, POOL_WIDTH, tn), lambda j, m: (layer, 0, j)),
                  pl.BlockSpec((None, ATTN_WIDTH, tn), lambda j, m: (layer, 0, j))],
        out_specs=pl.BlockSpec((tm, tn), lambda j, m: (m, j)),
        out_shape=jax.ShapeDtypeStruct((n, d), BF16),
        scratch_shapes=[pltpu.VMEM((d, tn), BF16), pltpu.VMEM((d, tn), BF16),
                        pltpu.VMEM((POOL_WIDTH, tn), BF16), pltpu.VMEM((ATTN_WIDTH, tn), BF16)],
        compiler_params=_cparams(("arbitrary", "arbitrary"), 56),
        name="gated_merge",
    )(xn, mixed, attn, w_gate, w_gate, b_gate, b_gate, w_up_pool, w_up_attn)


HIGH_HALF = 0xFFFF0000


def _pack_halves(x):
    c = x.shape[1] // 2
    lo = lax.bitcast_convert_type(x[:, :c].astype(BF16).astype(F32), jnp.uint32)
    hi = lax.bitcast_convert_type(x[:, c:].astype(BF16).astype(F32), jnp.uint32)
    return (lo >> 16) | hi


def _unpack_halves(p):
    lo = lax.bitcast_convert_type(p << 16, F32)
    hi = lax.bitcast_convert_type(p & jnp.uint32(HIGH_HALF), F32)
    return lo, hi


ROW_TILE = 8


def _store_row_tiles(ref, packed):
    m = packed.shape[0]
    for s in range(ROW_TILE):
        ref[pl.ds(s, m, stride=ROW_TILE), :] = packed[:, s * LANES:(s + 1) * LANES]


def _load_row_tiles(ref):
    m = ref.shape[0] // ROW_TILE
    return jnp.concatenate([ref[pl.ds(s, m, stride=ROW_TILE), :] for s in range(ROW_TILE)], axis=1)


def _rows(ref, first, count=1):
    return ref.at[pl.ds(pl.multiple_of(first * ROW_TILE, ROW_TILE), count * ROW_TILE)]


def _outproj_kernel(z_ref, w_ref, h_ref, gain_ref, wr_ref, br_ref, h1_ref, hn_ref, logit_ref, w_s):
    @pl.when(pl.program_id(0) == 0)
    def _():
        w_s[...] = w_ref[...].astype(BF16)

    h1 = h_ref[...] + jnp.dot(z_ref[...], w_s[...], preferred_element_type=F32)
    h1_ref[...] = h1
    hn = _rms(h1, gain_ref[...])
    _store_row_tiles(hn_ref, _pack_halves(hn))
    logit_ref[...] = jnp.dot(hn.astype(BF16), wr_ref[...], preferred_element_type=F32) + br_ref[...]


def outproj_call(z, w_out, layer, h, gain, w_router, b_router, tm=256):
    n, d = h.shape
    return pl.pallas_call(
        _outproj_kernel,
        grid=(n // tm,),
        in_specs=[pl.BlockSpec((tm, d), lambda m: (m, 0)),
                  pl.BlockSpec((None, d, d), lambda m: (layer, 0, 0), pipeline_mode=pl.Buffered(1)),
                  pl.BlockSpec((tm, d), lambda m: (m, 0)),
                  pl.BlockSpec((1, d), lambda m: (0, 0)),
                  pl.BlockSpec((d, ROUTER_LANES), lambda m: (0, 0)),
                  pl.BlockSpec((1, ROUTER_LANES), lambda m: (0, 0))],
        out_specs=[pl.BlockSpec((tm, d), lambda m: (m, 0)),
                   pl.BlockSpec((tm * ROW_TILE, LANES), lambda m: (m, 0)),
                   pl.BlockSpec((tm, ROUTER_LANES), lambda m: (m, 0))],
        out_shape=[jax.ShapeDtypeStruct((n, d), F32),
                   jax.ShapeDtypeStruct((n * ROW_TILE, LANES), jnp.uint32),
                   jax.ShapeDtypeStruct((n, ROUTER_LANES), F32)],
        scratch_shapes=[pltpu.VMEM((d, d), BF16)],
        compiler_params=_cparams(("arbitrary",), 56),
        name="out_proj_norm_router",
    )(z, w_out, h, gain.reshape(1, d), w_router, b_router)


def _route(x, lane):
    big = float(ROUTER_LANES)

    def first_argmax(vals):
        top = jnp.max(vals, axis=1, keepdims=True)
        idx = jnp.min(jnp.where(vals == top, lane, big), axis=1, keepdims=True)
        return top, idx

    gmask = lane < N_GROUPS
    g_top, g_sel = first_argmax(jnp.where(gmask, x, -jnp.inf))
    g_weight = 1.0 / jnp.sum(jnp.where(gmask, jnp.exp(x - g_top), 0.0), axis=1, keepdims=True)
    lo = N_GROUPS + EXPERTS_PER_GROUP * g_sel
    e_vals = jnp.where((lane >= lo) & (lane < lo + EXPERTS_PER_GROUP), x, -jnp.inf)
    v1, i1 = first_argmax(e_vals)
    v2, i2 = first_argmax(jnp.where(lane == i1, -jnp.inf, e_vals))
    t = jnp.exp(v2 - v1)
    return i1 - N_GROUPS, i2 - N_GROUPS, g_weight / (1.0 + t), g_weight * t / (1.0 + t)


PLAN_BLOCK = 1024
PLAN_CHUNK = 256
META_TILE_EXPERT, META_NEXT_EXPERT, META_PAD_START, META_N_VALID = 0, 1, 2, 3


def _route_plan_kernel(logit_ref, w_ref, pos_ref, meta_ref, tri_s, cnt_s, base_s, off_s):
    phase = pl.program_id(0)
    blk = pl.program_id(1)
    tb = logit_ref.shape[0]
    lane = lax.broadcasted_iota(jnp.int32, (tb, ROUTER_LANES), 1).astype(F32)
    e0, e1, w0, w1 = _route(logit_ref[...], lane)
    onehot = jnp.where(lane == e0, 1.0, 0.0) + jnp.where(lane == e1, 1.0, 0.0)
    block_counts = jnp.sum(onehot, axis=0, keepdims=True)

    @pl.when(jnp.logical_and(phase == 0, blk == 0))
    def _():
        cnt_s[...] = jnp.zeros(cnt_s.shape, F32)

    @pl.when(phase == 0)
    def _():
        cnt_s[...] = cnt_s[...] + block_counts

    @pl.when(jnp.logical_and(phase == 1, blk == 0))
    def _():
        r = lax.broadcasted_iota(jnp.int32, (tb, tb), 0)
        c = lax.broadcasted_iota(jnp.int32, (tb, tb), 1)
        tri_s[...] = jnp.where(c < r, 1.0, 0.0).astype(BF16)
        lane1 = lane[0:1]
        cnt = cnt_s[...]
        tiles = jnp.floor((cnt + (TM_EXPERT - 1)) * (1.0 / TM_EXPERT))
        ri = lax.broadcasted_iota(jnp.int32, (ROUTER_LANES, ROUTER_LANES), 0)
        ci = lax.broadcasted_iota(jnp.int32, (ROUTER_LANES, ROUTER_LANES), 1)
        upper = jnp.where(ri <= ci, 1.0, 0.0).astype(BF16)
        ends = jnp.dot(jnp.broadcast_to(tiles, (8, ROUTER_LANES)).astype(BF16), upper,
                       preferred_element_type=F32)[0:1]
        off_s[...] = (ends - tiles) * TM_EXPERT
        base_s[...] = jnp.zeros(base_s.shape, F32)

        def pick(vec, e):
            return jnp.sum(jnp.where(lane1 == e, vec, 0.0), axis=1, keepdims=True)

        end_of = [pick(ends, e) for e in range(N_EXPERTS)]
        n_valid = end_of[N_EXPERTS - 1]

        def segment_of(tile):
            return sum(jnp.where(tile >= end_e, 1.0, 0.0) for end_e in end_of)

        tile_expert = jnp.where(lane1 < n_valid, segment_of(lane1), segment_of(n_valid - 1.0))
        following = sum(jnp.where(tile_expert == e, end_of[e], 0.0) for e in range(N_EXPERTS))
        next_expert = jnp.where(following < n_valid, segment_of(following), -1.0)
        rows = [tile_expert, next_expert, off_s[...] + cnt, jnp.broadcast_to(n_valid, (1, ROUTER_LANES))]
        rows += [jnp.zeros((1, ROUTER_LANES), F32)] * (meta_ref.shape[0] - len(rows))
        meta_ref[...] = jnp.concatenate(rows, axis=0).astype(jnp.int32)

    @pl.when(phase == 1)
    def _():
        before = jnp.dot(tri_s[...], onehot.astype(BF16), preferred_element_type=F32)
        row = before + base_s[...] + off_s[...]
        base_s[...] = base_s[...] + block_counts
        w_ref[...] = jnp.where(lane == 0, w0, jnp.where(lane == 1, w1, 0.0))
        eye = (lax.broadcasted_iota(jnp.int32, (LANES, LANES), 0)
               == lax.broadcasted_iota(jnp.int32, (LANES, LANES), 1))
        for j, e in enumerate((e0, e1)):
            col = jnp.sum(jnp.where(lane == e, row, 0.0), axis=1, keepdims=True)
            for q in range(tb // PLAN_CHUNK):
                parts = []
                for g in range(PLAN_CHUNK // LANES):
                    t0 = q * PLAN_CHUNK + g * LANES
                    square = jnp.broadcast_to(col[t0:t0 + LANES], (LANES, LANES))
                    parts.append(jnp.sum(jnp.where(eye, square, 0.0), axis=0, keepdims=True))
                pos_ref[j, q] = jnp.concatenate(parts, axis=1).astype(jnp.int32)


def route_plan_call(logits, n_tiles):
    n = logits.shape[0]
    tb = PLAN_BLOCK
    chunks = tb // PLAN_CHUNK
    assert n_tiles <= ROUTER_LANES and N_EXPERTS <= ROUTER_LANES
    return pl.pallas_call(
        _route_plan_kernel,
        grid=(2, n // tb),
        in_specs=[pl.BlockSpec((tb, ROUTER_LANES), lambda p, b: (b, 0))],
        out_specs=[pl.BlockSpec((tb, ROUTER_LANES), lambda p, b: (p * b, 0)),
                   pl.BlockSpec((2, chunks, 1, PLAN_CHUNK), lambda p, b: (0, p * b, 0, 0)),
                   pl.BlockSpec((8, ROUTER_LANES), lambda p, b: (0, 0))],
        out_shape=[jax.ShapeDtypeStruct((n, ROUTER_LANES), F32),
                   jax.ShapeDtypeStruct((2, n // PLAN_CHUNK, 1, PLAN_CHUNK), jnp.int32),
                   jax.ShapeDtypeStruct((8, ROUTER_LANES), jnp.int32)],
        scratch_shapes=[pltpu.VMEM((tb, tb), BF16), pltpu.VMEM((1, ROUTER_LANES), F32),
                        pltpu.VMEM((1, ROUTER_LANES), F32), pltpu.VMEM((1, ROUTER_LANES), F32)],
        compiler_params=_cparams(("arbitrary", "arbitrary"), 32),
        name="route_plan",
    )(logits)


DISPATCH_SLOTS = 3


def _dispatch_kernel(pad_ref, nv_ref, pos0_ref, pos1_ref, src_hbm, dst_ref, zero_buf, ring, sem, in_sems, row_sems,
                     *, chunk, n_tiles):
    i = pl.program_id(0)
    steps = pl.num_programs(0)
    slot = i % DISPATCH_SLOTS

    def load(block, into):
        return pltpu.make_async_copy(_rows(src_hbm, block * chunk, chunk), ring.at[into], in_sems.at[into])

    def wait_rows(of_slot):
        for j in range(2):
            pltpu.make_async_copy(ring.at[of_slot], _rows(dst_ref, 0, chunk), row_sems.at[of_slot]).wait()

    @pl.when(i == 0)
    def _():
        load(0, 0).start()
        zero_buf[...] = jnp.zeros(zero_buf.shape, zero_buf.dtype)

        def fill(e):
            return pltpu.make_async_copy(zero_buf, _rows(dst_ref, pad_ref[e], TM_EXPERT), sem)

        for e in range(N_EXPERTS):
            fill(e).start()
        for e in range(N_EXPERTS):
            fill(e).wait()

        def fill_tile(i):
            return pltpu.make_async_copy(zero_buf, _rows(dst_ref, i * TM_EXPERT, TM_EXPERT), sem)

        def start_tile(i, c):
            fill_tile(i).start()
            return c

        def wait_tile(i, c):
            fill_tile(i).wait()
            return c

        lax.fori_loop(nv_ref[0], n_tiles, start_tile, 0)
        lax.fori_loop(nv_ref[0], n_tiles, wait_tile, 0)

    nxt = (i + 1) % DISPATCH_SLOTS

    @pl.when(i >= DISPATCH_SLOTS - 1)
    def _():
        wait_rows(nxt)

    @pl.when(i + 1 < steps)
    def _():
        load(i + 1, nxt).start()

    load(i, slot).wait()
    src_ref = ring.at[slot]

    def start(t, c):
        for j, pos_ref in enumerate((pos0_ref, pos1_ref)):
            pltpu.make_async_copy(_rows(src_ref, t), _rows(dst_ref, pos_ref[0, 0, t]),
                                  row_sems.at[slot]).start(priority=j)
        return c

    lax.fori_loop(0, chunk, start, 0, unroll=ROW_DMA_UNROLL)

    @pl.when(i == steps - 1)
    def _():
        for back in range(DISPATCH_SLOTS - 1):
            @pl.when(i - back >= 0)
            def _():
                wait_rows((i - back) % DISPATCH_SLOTS)


def dispatch_call(hn, pos, pad_start, n_valid_tiles, n_tiles):
    n = hn.shape[0] // ROW_TILE
    chunk = PLAN_CHUNK
    grid_spec = pltpu.PrefetchScalarGridSpec(
        num_scalar_prefetch=2,
        grid=(n // chunk,),
        in_specs=[pl.BlockSpec((None, 1, 1, chunk), lambda i, pad, nv: (0, i, 0, 0), memory_space=pltpu.SMEM),
                  pl.BlockSpec((None, 1, 1, chunk), lambda i, pad, nv: (1, i, 0, 0), memory_space=pltpu.SMEM),
                  pl.BlockSpec(memory_space=pl.ANY)],
        out_specs=pl.BlockSpec(memory_space=pl.ANY),
        scratch_shapes=[pltpu.VMEM((TM_EXPERT * ROW_TILE, LANES), hn.dtype),
                        pltpu.VMEM((DISPATCH_SLOTS, chunk * ROW_TILE, LANES), hn.dtype),
                        pltpu.SemaphoreType.DMA(()), pltpu.SemaphoreType.DMA((DISPATCH_SLOTS,)),
                        pltpu.SemaphoreType.DMA((DISPATCH_SLOTS,))],
    )
    return pl.pallas_call(
        functools.partial(_dispatch_kernel, chunk=chunk, n_tiles=n_tiles),
        grid_spec=grid_spec,
        out_shape=jax.ShapeDtypeStruct((n_tiles * TM_EXPERT * ROW_TILE, LANES), hn.dtype),
        compiler_params=_cparams(("arbitrary",), 32),
        name="dispatch_rows",
    )(pad_start, n_valid_tiles, pos, pos, hn)


def _expert_kernel(te_ref, nv_ref, nx_ref, x_ref, wg_hbm, wu_hbm, wd_hbm, y_ref,
                   wg_f, wu_f, wd_f, wg_s, wu_s, wd_s, seg_ref, sems, *, layer):
    i = pl.program_id(0)
    valid = i < nv_ref[0]
    expert = te_ref[i]
    changed = jnp.logical_or(i == 0, expert != te_ref[jnp.maximum(i - 1, 0)])

    def fetch(e, slot):
        return [pltpu.make_async_copy(hbm.at[layer, e], buf.at[slot], sems.at[slot])
                for hbm, buf in ((wg_hbm, wg_f), (wu_hbm, wu_f), (wd_hbm, wd_f))]

    def swiglu_tile():
        lo, hi = _unpack_halves(_load_row_tiles(x_ref))
        x = jnp.concatenate([lo.astype(BF16), hi.astype(BF16)], axis=1)
        a = jnp.dot(x, wg_s[...], preferred_element_type=F32)
        b = jnp.dot(x, wu_s[...], preferred_element_type=F32)
        hmid = (a * _sigmoid(a) * b).astype(BF16)
        _store_row_tiles(y_ref, _pack_halves(jnp.dot(hmid, wd_s[...], preferred_element_type=F32)))

    @pl.when(i == 0)
    def _():
        seg_ref[0] = 0
        for cp in fetch(expert, 0):
            cp.start()

    @pl.when(jnp.logical_and(valid, changed))
    def _():
        slot = seg_ref[0] % 2
        for cp in fetch(expert, slot):
            cp.wait()
        nxt = nx_ref[i]

        @pl.when(nxt >= 0)
        def _():
            for cp in fetch(nxt, 1 - slot):
                cp.start()

        seg_ref[0] = seg_ref[0] + 1
        wg_s[...] = wg_f[slot].astype(BF16)
        wu_s[...] = wu_f[slot].astype(BF16)
        wd_s[...] = wd_f[slot].astype(BF16)
        swiglu_tile()

    @pl.when(jnp.logical_and(valid, jnp.logical_not(changed)))
    def _():
        swiglu_tile()

    @pl.when(jnp.logical_not(valid))
    def _():
        y_ref[...] = jnp.zeros(y_ref.shape, y_ref.dtype)


def expert_call(xs, tile_expert, n_valid_tiles, next_expert, w_gate, w_up, w_down, layer):
    p = xs.shape[0] // ROW_TILE
    d = 2 * ROW_TILE * LANES
    tm = TM_EXPERT
    f = w_gate.shape[-1]
    grid_spec = pltpu.PrefetchScalarGridSpec(
        num_scalar_prefetch=3,
        grid=(p // tm,),
        in_specs=[pl.BlockSpec((tm * ROW_TILE, LANES), lambda i, te, nv, nx: (jnp.minimum(i, nv[0] - 1), 0)),
                  pl.BlockSpec(memory_space=pl.ANY),
                  pl.BlockSpec(memory_space=pl.ANY),
                  pl.BlockSpec(memory_space=pl.ANY)],
        out_specs=pl.BlockSpec((tm * ROW_TILE, LANES), lambda i, te, nv, nx: (i, 0)),
        scratch_shapes=[pltpu.VMEM((2, d, f), F32), pltpu.VMEM((2, d, f), F32), pltpu.VMEM((2, f, d), F32),
                        pltpu.VMEM((d, f), BF16), pltpu.VMEM((d, f), BF16), pltpu.VMEM((f, d), BF16),
                        pltpu.SMEM((1,), jnp.int32), pltpu.SemaphoreType.DMA((2,))],
    )
    return pl.pallas_call(
        functools.partial(_expert_kernel, layer=layer),
        grid_spec=grid_spec,
        out_shape=jax.ShapeDtypeStruct(xs.shape, jnp.uint32),
        compiler_params=_cparams(("arbitrary",), 58),
        name="expert_swiglu",
    )(tile_expert, n_valid_tiles, next_expert, xs, w_gate, w_up, w_down)


def _combine_kernel(pos0_ref, pos1_ref, nxt0_ref, nxt1_ref, ys_ref, h_ref, w_ref, gain_ref, *rest, chunk):
    *out_refs, buf, sems = rest
    xn_ref, h2_ref = out_refs if len(out_refs) == 2 else (out_refs[0], None)
    i = pl.program_id(0)
    slot = i % 2

    def gather(p0_ref, p1_ref, into):
        def start(t, c):
            for j, pos_ref in enumerate((p0_ref, p1_ref)):
                pltpu.make_async_copy(_rows(ys_ref, pos_ref[0, 0, t]), _rows(buf.at[into, j], t),
                                      sems.at[into]).start(priority=j)
            return c

        lax.fori_loop(0, chunk, start, 0, unroll=ROW_DMA_UNROLL)

    @pl.when(i == 0)
    def _():
        gather(pos0_ref, pos1_ref, 0)

    @pl.when(i + 1 < pl.num_programs(0))
    def _():
        gather(nxt0_ref, nxt1_ref, 1 - slot)

    for j in range(2):
        pltpu.make_async_copy(_rows(ys_ref, 0, chunk), buf.at[slot, j], sems.at[slot]).wait()
    w = w_ref[...]
    lo0, hi0 = _unpack_halves(_load_row_tiles(buf.at[slot, 0]))
    lo1, hi1 = _unpack_halves(_load_row_tiles(buf.at[slot, 1]))
    y = jnp.concatenate([w[:, 0:1] * lo0 + w[:, 1:2] * lo1, w[:, 0:1] * hi0 + w[:, 1:2] * hi1], axis=1)
    h2 = h_ref[...] + y
    if h2_ref is not None:
        h2_ref[...] = h2
    xn_ref[...] = _rms(h2, gain_ref[...]).astype(xn_ref.dtype)


def combine_call(ys, pos, h1, weights, next_gain, xn_dtype, keep_residual, chunk=128):
    n, d = h1.shape
    row_spec = pl.BlockSpec((chunk, d), lambda c: (c, 0))
    out_specs = [row_spec, row_spec] if keep_residual else [row_spec]
    out_shape = [jax.ShapeDtypeStruct((n, d), xn_dtype)] + ([jax.ShapeDtypeStruct((n, d), F32)] if keep_residual else [])
    per_row = PLAN_CHUNK // chunk
    steps = n // chunk

    def pos_spec(j, ahead):
        def index(c):
            c = jnp.minimum(c + ahead, steps - 1)
            return (j, c // per_row, 0, c % per_row)
        return pl.BlockSpec((None, 1, 1, chunk), index, memory_space=pltpu.SMEM)

    outs = pl.pallas_call(
        functools.partial(_combine_kernel, chunk=chunk),
        grid=(steps,),
        in_specs=[pos_spec(0, 0), pos_spec(1, 0), pos_spec(0, 1), pos_spec(1, 1),
                  pl.BlockSpec(memory_space=pl.ANY),
                  row_spec,
                  pl.BlockSpec((chunk, ROUTER_LANES), lambda c: (c, 0)),
                  pl.BlockSpec((1, d), lambda c: (0, 0))],
        out_specs=out_specs,
        out_shape=out_shape,
        scratch_shapes=[pltpu.VMEM((2, 2, chunk * ROW_TILE, LANES), jnp.uint32), pltpu.SemaphoreType.DMA((2,))],
        compiler_params=_cparams(("arbitrary",), 32),
        name="combine_rows",
    )(pos, pos, pos, pos, ys, h1, weights, next_gain.reshape(1, d))
    return (outs[0], outs[1]) if keep_residual else (outs[0], None)


def kernel(x, rel_bias_table, norm_mix_gain, w_in, w_merge_gate, b_merge_gate, pool_mix, pool_scale, w_up_pool, lambda_q1, lambda_k1, lambda_q2, lambda_k2, subln_gain, w_up_attn, w_out, norm_ffn_gain, w_router_group, b_router_group, w_router_expert, b_router_expert, w_expert_gate, w_expert_up, w_expert_down, final_norm_gain):
    batch, seq, d = x.shape
    depth = w_in.shape[0]
    n = batch * seq
    n_tiles = (2 * n + N_EXPERTS * (TM_EXPERT - 1)) // TM_EXPERT + 1

    bias_tiles = bias_tiles_call(rel_bias_table)
    h = x.reshape(n, d)
    xn = None
    q_scale = LOG2E * DIFF_HEAD_DIM ** -0.5

    for l in range(depth):
        first = l == 0
        xn_first, mixed = pool_branch_call(h if first else xn, norm_mix_gain[l], w_in, l, pool_mix[l].astype(BF16),
                                           pool_scale[l], seq, normalize=first)
        xn = xn_first if first else xn
        qkv = proj_call(xn, w_in, l, 1, 3, ATTN_WIDTH, BF16, first_block_scale=q_scale, name="in_proj_qkv")
        lambda_init = 0.8 - 0.6 * math.exp(-0.3 * l)
        lam_params = jnp.stack([lambda_q1[l], lambda_k1[l], lambda_q2[l], lambda_k2[l]])
        attn = attn_call(qkv.reshape(batch, seq, 3 * ATTN_WIDTH), bias_tiles, lam_params, subln_gain[l],
                         lambda_init, batch, seq).reshape(n, ATTN_WIDTH)
        z = merge_call(xn, mixed, attn, w_merge_gate, b_merge_gate.reshape(depth, 1, -1), w_up_pool, w_up_attn, l)

        w_router = jnp.concatenate(
            [w_router_group[l], jnp.transpose(w_router_expert[l], (1, 0, 2)).reshape(d, N_EXPERTS),
             jnp.zeros((d, ROUTER_LANES - N_GROUPS - N_EXPERTS), F32)], axis=1).astype(BF16)
        b_router = jnp.concatenate(
            [b_router_group[l], b_router_expert[l].reshape(-1),
             jnp.zeros((ROUTER_LANES - N_GROUPS - N_EXPERTS,), F32)]).reshape(1, ROUTER_LANES)
        h1, hn, logits = outproj_call(z, w_out, l, h, norm_ffn_gain[l], w_router, b_router)

        weights, pos, meta = route_plan_call(logits, n_tiles)
        tile_expert = meta[META_TILE_EXPERT, :n_tiles]
        next_expert = meta[META_NEXT_EXPERT, :n_tiles]
        pad_start = meta[META_PAD_START, :N_EXPERTS]
        n_valid = meta[META_N_VALID, :1]
        xs = dispatch_call(hn, pos, pad_start, n_valid, n_tiles)
        ys = expert_call(xs, tile_expert, n_valid, next_expert, w_expert_gate, w_expert_up, w_expert_down, l)
        last = l == depth - 1
        next_gain = final_norm_gain if last else norm_mix_gain[l + 1]
        xn, h = combine_call(ys, pos, h1, weights, next_gain, F32 if last else BF16, keep_residual=not last)

    return xn.reshape(batch, seq, d)
```

```python
import functools
import math

import numpy as np
import jax
import jax.numpy as jnp
from jax import lax
from jax.experimental import pallas as pl
from jax.experimental.pallas import tpu as pltpu

F32 = jnp.float32
BF16 = jnp.bfloat16

D_MODEL = 2048
POOL_WIDTH = 1024
POOL_WINDOWS = (2, 4, 8, 16)
POOL_GROUP_DIM = 256
POOL_HALO = 16
DIFF_HEADS = 8
DIFF_HEAD_DIM = 64
DIFF_V_DIM = 128
ATTN_WIDTH = 1024
REL_BUCKETS = 32
REL_MAX_DISTANCE = 128
N_GROUPS = 4
EXPERTS_PER_GROUP = 8
N_EXPERTS = 32
D_EXPERT = 512
EPS = 1e-6
NEG_INF = -1e30
LOG2E = 1.4426950408889634

ROUTER_LANES = 128
TQ = 256
TQ_BLOCK = 256
TK = 256
KV_SUPER = 2
V_ROWS = DIFF_V_DIM + 16
HEADS_PER_STEP = 8
TM_EXPERT = 256
EXPERT_FETCH_SPLIT = 4
LANES = 128
ROW_DMA_UNROLL = 8
MIB = 1024 * 1024


def _cparams(sem, vmem_mib):
    return pltpu.CompilerParams(dimension_semantics=sem, vmem_limit_bytes=vmem_mib * MIB)


def _rms(xf, gain):
    ms = jnp.mean(xf * xf, axis=-1, keepdims=True)
    return xf * lax.rsqrt(ms + EPS) * gain


def _proj_kernel(x_ref, w_ref, o_ref, w_s, *, first_block_scale):
    @pl.when(pl.program_id(1) == 0)
    def _():
        w_s[...] = w_ref[...].astype(BF16)

    acc = jnp.dot(x_ref[...], w_s[...], preferred_element_type=F32)
    if first_block_scale is not None:
        acc = acc * jnp.where(pl.program_id(0) == 0, first_block_scale, 1.0).astype(F32)
    o_ref[...] = acc.astype(o_ref.dtype)


def proj_call(x, w, layer, col_block0, n_col_blocks, tn, out_dtype, first_block_scale=None, tm=1024, name="proj"):
    n, k = x.shape
    return pl.pallas_call(
        functools.partial(_proj_kernel, first_block_scale=first_block_scale),
        grid=(n_col_blocks, n // tm),
        in_specs=[pl.BlockSpec((tm, k), lambda j, m: (m, 0)),
                  pl.BlockSpec((None, k, tn), lambda j, m: (layer, 0, col_block0 + j))],
        out_specs=pl.BlockSpec((tm, tn), lambda j, m: (m, j)),
        out_shape=jax.ShapeDtypeStruct((n, n_col_blocks * tn), out_dtype),
        scratch_shapes=[pltpu.VMEM((k, tn), BF16)],
        compiler_params=_cparams(("arbitrary", "arbitrary"), 48),
        name=name,
    )(x, w)


def _pool_branch_kernel(x_ref, gain_ref, w_ref, mix_ref, scale_ref, *rest, tm, seq, normalize):
    if normalize:
        xn_ref, o_ref, w_s, carry = rest
    else:
        o_ref, w_s, carry = rest
    m = pl.program_id(0)

    @pl.when(m == 0)
    def _():
        w_s[...] = w_ref[...].astype(BF16)

    if normalize:
        xn = _rms(x_ref[...], gain_ref[...]).astype(BF16)
        xn_ref[...] = xn
    else:
        xn = x_ref[...]
    cur = jnp.dot(xn, w_s[...], preferred_element_type=F32)
    row0 = (m * tm) % seq
    prev = jnp.where(row0 == 0, 0.0, carry[...])
    carry[...] = cur[tm - POOL_HALO:]
    pos = row0 + lax.broadcasted_iota(jnp.int32, (tm, 1), 0)
    outs = []
    for g, w in enumerate(POOL_WINDOWS):
        sl = slice(g * POOL_GROUP_DIM, (g + 1) * POOL_GROUP_DIM)
        x = jnp.concatenate([prev[:, sl], cur[:, sl]], axis=0)
        s, d = x, 1
        while d < w:
            s = s[:-d] + s[d:]
            d *= 2
        start = POOL_HALO - w + 1
        wsum = s[start:start + tm]
        count = jnp.minimum(pos + 1, w).astype(F32)
        pooled = wsum / count - cur[:, sl]
        mixed = jnp.dot(pooled.astype(BF16), mix_ref[g], preferred_element_type=F32)
        outs.append(mixed * scale_ref[:, sl])
    o_ref[...] = jnp.concatenate(outs, axis=1).astype(o_ref.dtype)


def pool_branch_call(x, gain, w_in, layer, mix_bf16, scale, seq, normalize, tm=512):
    n, d = x.shape
    c = POOL_WIDTH
    row_spec = pl.BlockSpec((tm, c), lambda m: (m, 0))
    x_spec = pl.BlockSpec((tm, d), lambda m: (m, 0))
    out_specs = [x_spec, row_spec] if normalize else [row_spec]
    out_shape = ([jax.ShapeDtypeStruct((n, d), BF16)] if normalize else []) + [jax.ShapeDtypeStruct((n, c), BF16)]
    outs = pl.pallas_call(
        functools.partial(_pool_branch_kernel, tm=tm, seq=seq, normalize=normalize),
        grid=(n // tm,),
        in_specs=[x_spec,
                  pl.BlockSpec((1, d), lambda m: (0, 0)),
                  pl.BlockSpec((None, d, c), lambda m: (layer, 0, 0), pipeline_mode=pl.Buffered(1)),
                  pl.BlockSpec(mix_bf16.shape, lambda m: (0, 0, 0)),
                  pl.BlockSpec((1, c), lambda m: (0, 0))],
        out_specs=out_specs,
        out_shape=out_shape,
        scratch_shapes=[pltpu.VMEM((d, c), BF16), pltpu.VMEM((POOL_HALO, c), F32)],
        compiler_params=_cparams(("arbitrary",), 48),
        name="pool_branch",
    )(x, gain.reshape(1, d), w_in, mix_bf16, scale.reshape(1, c))
    return tuple(outs) if normalize else (None, outs[0])


N_BIAS_TILES = 4


def _bucket_tiles():
    kk = np.arange(TK)[:, None]
    qq = np.arange(TQ_BLOCK)[None, :]
    tiles = []
    for rel in (0, 1, 2, -1):
        n = rel * TK + qq - kk
        max_exact = REL_BUCKETS // 2
        nf = np.maximum(n, 1).astype(np.float64)
        large = max_exact + (np.log(nf / max_exact) / math.log(REL_MAX_DISTANCE / max_exact)
                             * (REL_BUCKETS - max_exact)).astype(np.int64)
        large = np.minimum(large, REL_BUCKETS - 1)
        bucket = np.where(n < max_exact, n, large)
        tiles.append(np.where(n < 0, -1, bucket))
    return np.stack(tiles).astype(np.int32)


def _bias_kernel(table_ref, bucket_ref, o_ref):
    h = pl.program_id(0)
    bucket = bucket_ref[...]
    acc = jnp.full(bucket.shape, NEG_INF, F32)
    for b in range(REL_BUCKETS):
        acc = jnp.where(bucket == b, table_ref[b * DIFF_HEADS + h] * LOG2E, acc)
    o_ref[...] = acc


def bias_tiles_call(rel_table):
    bucket = jnp.asarray(_bucket_tiles())
    return pl.pallas_call(
        _bias_kernel,
        grid=(DIFF_HEADS,),
        in_specs=[pl.BlockSpec(memory_space=pltpu.SMEM),
                  pl.BlockSpec(bucket.shape, lambda h: (0, 0, 0))],
        out_specs=pl.BlockSpec((None,) + bucket.shape, lambda h: (h, 0, 0, 0)),
        out_shape=jax.ShapeDtypeStruct((DIFF_HEADS,) + bucket.shape, F32),
        compiler_params=_cparams(("parallel",), 32),
        name="rel_bias_tiles",
    )(rel_table.reshape(-1), bucket)


def _attn_kernel(q_ref, k_ref, v_ref, bias_ref, lam_ref, gain_ref, o_ref, vt_ref, s_ref, acc_ref,
                 *, lambda_init, n_super):
    qi = pl.program_id(2)
    tks = KV_SUPER * TK
    heads = range(HEADS_PER_STEP)

    def block_diag(ref, hh):
        q = ref[:, hh * DIFF_V_DIM:(hh + 1) * DIFF_V_DIM]
        lane = lax.broadcasted_iota(jnp.int32, q.shape, 1)
        zero = jnp.zeros_like(q)
        return jnp.concatenate([jnp.where(lane < DIFF_HEAD_DIM, q, zero),
                                jnp.where(lane >= DIFF_HEAD_DIM, q, zero)], axis=0)

    def scores(t, hh, qd, tile):
        kb = k_ref[pl.ds(pl.multiple_of(t * tks, tks), tks), hh * DIFF_V_DIM:(hh + 1) * DIFF_V_DIM]
        s = lax.dot_general(kb, qd, (((1,), (1,)), ((), ())), preferred_element_type=F32)
        parts = []
        for u in range(KV_SUPER):
            tiles = []
            for c in range(TQ // TQ_BLOCK):
                rel = tile * (TQ // TQ_BLOCK) + c - (t * KV_SUPER + u)
                tiles.append(bias_ref[hh, jnp.where(rel < 0, N_BIAS_TILES - 1, jnp.minimum(rel, 2))])
            parts.append(s[u * TK:(u + 1) * TK] + jnp.concatenate(tiles + tiles, axis=1))
        s = jnp.concatenate(parts, axis=0)
        s_ref[hh] = s
        return jnp.max(s, axis=0, keepdims=True)

    @pl.when(qi == 0)
    def _():
        extra = (lax.broadcasted_iota(jnp.int32, (V_ROWS - DIFF_V_DIM, tks), 0) == 0).astype(BF16)
        for hh in heads:
            cols = slice(hh * DIFF_V_DIM, (hh + 1) * DIFF_V_DIM)
            for c in range(n_super):
                vt = v_ref[c * tks:(c + 1) * tks, cols].astype(F32).T.astype(BF16)
                vt_ref[hh, c] = jnp.concatenate([vt, extra], axis=0)

    qds = [block_diag(q_ref, hh) for hh in heads]

    def softmax_step(t, hh, m_prev, m_cur):
        alpha = jnp.exp2(m_prev - m_cur)
        p = jnp.exp2(s_ref[hh] - m_cur)
        pv = jnp.dot(vt_ref[hh, t], p.astype(BF16), preferred_element_type=F32)
        acc_ref[hh] = acc_ref[hh] * alpha + pv

    last = ((qi + 1) * TQ - 1) // tks
    acc_ref[...] = jnp.zeros(acc_ref.shape, F32)
    neg = jnp.full((1, 2 * TQ), NEG_INF, F32)
    init = tuple((neg, jnp.maximum(neg, scores(0, hh, qds[hh], qi))) for hh in heads)

    def body(t, carry):
        out = []
        for hh in heads:
            m_prev, m_cur = carry[hh]
            softmax_step(t, hh, m_prev, m_cur)
            m_next = jnp.maximum(m_cur, scores(t + 1, hh, qds[hh], qi))
            out.append((m_cur, m_next))
        return tuple(out)

    carry = lax.fori_loop(0, last, body, init)

    lam_p = lam_ref[...]
    lam = (jnp.exp(jnp.sum(lam_p[0:1] * lam_p[1:2], axis=1, keepdims=True))
           - jnp.exp(jnp.sum(lam_p[2:3] * lam_p[3:4], axis=1, keepdims=True)) + lambda_init)
    for hh in heads:
        m_prev, m_cur = carry[hh]
        softmax_step(last, hh, m_prev, m_cur)
        acc = acc_ref[hh, :DIFF_V_DIM, :]
        l = acc_ref[hh, DIFF_V_DIM:DIFF_V_DIM + 1, :]
        o = acc[:, :TQ] / l[:, :TQ] - lam * (acc[:, TQ:] / l[:, TQ:])
        ms = jnp.mean(o * o, axis=0, keepdims=True)
        y = o * lax.rsqrt(ms + EPS) * gain_ref[...] * (1.0 - lambda_init)
        o_ref[:, hh * DIFF_V_DIM:(hh + 1) * DIFF_V_DIM] = y.T.astype(o_ref.dtype)


def attn_call(qkv, bias_tiles, lam_params, subln_gain, lambda_init, batch, seq):
    tks = KV_SUPER * TK
    n_super = seq // tks
    hps = HEADS_PER_STEP
    width = hps * DIFF_V_DIM
    groups = DIFF_HEADS // hps
    n_q = seq // TQ
    return pl.pallas_call(
        functools.partial(_attn_kernel, lambda_init=lambda_init, n_super=n_super),
        grid=(batch, groups, n_q),
        in_specs=[pl.BlockSpec((None, TQ, width), lambda b, g, i: (b, i, g)),
                  pl.BlockSpec((None, seq, width), lambda b, g, i: (b, 0, groups + g), pipeline_mode=pl.Buffered(1)),
                  pl.BlockSpec((None, seq, width), lambda b, g, i: (b, 0, 2 * groups + g),
                               pipeline_mode=pl.Buffered(1)),
                  pl.BlockSpec((hps, N_BIAS_TILES, TK, TQ_BLOCK), lambda b, g, i: (g, 0, 0, 0),
                               pipeline_mode=pl.Buffered(1)),
                  pl.BlockSpec((4, DIFF_HEAD_DIM), lambda b, g, i: (0, 0)),
                  pl.BlockSpec((DIFF_V_DIM, 1), lambda b, g, i: (0, 0))],
        out_specs=pl.BlockSpec((None, TQ, width), lambda b, g, i: (b, i, g)),
        out_shape=jax.ShapeDtypeStruct((batch, seq, ATTN_WIDTH), BF16),
        scratch_shapes=[pltpu.VMEM((hps, n_super, V_ROWS, tks), BF16),
                        pltpu.VMEM((hps, tks, 2 * TQ), F32),
                        pltpu.VMEM((hps, V_ROWS, 2 * TQ), F32)],
        compiler_params=_cparams(("parallel", "parallel", "arbitrary"), 56),
        name="diff_attention",
    )(qkv, qkv, qkv, bias_tiles, lam_params, subln_gain.reshape(DIFF_V_DIM, 1))


def _sigmoid(x):
    return 1.0 / (1.0 + jnp.exp(-x))


def _merge_kernel(xn_ref, mixed_ref, attn_ref, wgp_ref, wga_ref, bgp_ref, bga_ref, wup_ref, wua_ref, z_ref,
                  wgp_s, wga_s, wup_s, wua_s):
    @pl.when(pl.program_id(1) == 0)
    def _():
        wgp_s[...] = wgp_ref[...].astype(BF16)
        wga_s[...] = wga_ref[...].astype(BF16)
        wup_s[...] = wup_ref[...].astype(BF16)
        wua_s[...] = wua_ref[...].astype(BF16)

    xn = xn_ref[...]
    g_pool = _sigmoid(jnp.dot(xn, wgp_s[...], preferred_element_type=F32) + bgp_ref[...])
    g_attn = _sigmoid(jnp.dot(xn, wga_s[...], preferred_element_type=F32) + bga_ref[...])
    y_pool = jnp.dot(mixed_ref[...], wup_s[...], preferred_element_type=F32)
    y_attn = jnp.dot(attn_ref[...], wua_s[...], preferred_element_type=F32)
    z_ref[...] = (g_pool * y_pool + g_attn * y_attn).astype(z_ref.dtype)


def merge_call(xn, mixed, attn, w_gate, b_gate, w_up_pool, w_up_attn, layer, tm=1024, tn=512):
    n, d = xn.shape
    nb = d // tn
    return pl.pallas_call(
        _merge_kernel,
        grid=(nb, n // tm),
        in_specs=[pl.BlockSpec((tm, d), lambda j, m: (m, 0)),
                  pl.BlockSpec((tm, POOL_WIDTH), lambda j, m: (m, 0)),
                  pl.BlockSpec((tm, ATTN_WIDTH), lambda j, m: (m, 0)),
                  pl.BlockSpec((None, d, tn), lambda j, m: (layer, 0, j)),
                  pl.BlockSpec((None, d, tn), lambda j, m: (layer, 0, nb + j)),
                  pl.BlockSpec((None, 1, tn), lambda j, m: (layer, 0, j)),
                  pl.BlockSpec((None, 1, tn), lambda j, m: (layer, 0, nb + j)),
                  pl.BlockSpec((None, POOL_WIDTH, tn), lambda j, m: (layer, 0, j)),
                  pl.BlockSpec((None, ATTN_WIDTH, tn), lambda j, m: (layer, 0, j))],
        out_specs=pl.BlockSpec((tm, tn), lambda j, m: (m, j)),
        out_shape=jax.ShapeDtypeStruct((n, d), BF16),
        scratch_shapes=[pltpu.VMEM((d, tn), BF16), pltpu.VMEM((d, tn), BF16),
                        pltpu.VMEM((POOL_WIDTH, tn), BF16), pltpu.VMEM((ATTN_WIDTH, tn), BF16)],
        compiler_params=_cparams(("arbitrary", "arbitrary"), 56),
        name="gated_merge",
    )(xn, mixed, attn, w_gate, w_gate, b_gate, b_gate, w_up_pool, w_up_attn)


HIGH_HALF = 0xFFFF0000


def _pack_halves(x):
    c = x.shape[1] // 2
    lo = lax.bitcast_convert_type(x[:, :c].astype(BF16).astype(F32), jnp.uint32)
    hi = lax.bitcast_convert_type(x[:, c:].astype(BF16).astype(F32), jnp.uint32)
    return (lo >> 16) | hi


def _unpack_halves(p):
    lo = lax.bitcast_convert_type(p << 16, F32)
    hi = lax.bitcast_convert_type(p & jnp.uint32(HIGH_HALF), F32)
    return lo, hi


ROW_TILE = 8


def _store_row_tiles(ref, packed):
    m = packed.shape[0]
    for s in range(ROW_TILE):
        ref[pl.ds(s, m, stride=ROW_TILE), :] = packed[:, s * LANES:(s + 1) * LANES]


def _load_row_tiles(ref):
    m = ref.shape[0] // ROW_TILE
    return jnp.concatenate([ref[pl.ds(s, m, stride=ROW_TILE), :] for s in range(ROW_TILE)], axis=1)


def _rows(ref, first, count=1):
    return ref.at[pl.ds(pl.multiple_of(first * ROW_TILE, ROW_TILE), count * ROW_TILE)]


def _outproj_kernel(z_ref, w_ref, h_ref, gain_ref, wr_ref, br_ref, h1_ref, hn_ref, logit_ref, w_s):
    @pl.when(pl.program_id(0) == 0)
    def _():
        w_s[...] = w_ref[...].astype(BF16)

    h1 = h_ref[...] + jnp.dot(z_ref[...], w_s[...], preferred_element_type=F32)
    h1_ref[...] = h1
    hn = _rms(h1, gain_ref[...])
    _store_row_tiles(hn_ref, _pack_halves(hn))
    logit_ref[...] = jnp.dot(hn.astype(BF16), wr_ref[...], preferred_element_type=F32) + br_ref[...]


def outproj_call(z, w_out, layer, h, gain, w_router, b_router, tm=256):
    n, d = h.shape
    return pl.pallas_call(
        _outproj_kernel,
        grid=(n // tm,),
        in_specs=[pl.BlockSpec((tm, d), lambda m: (m, 0)),
                  pl.BlockSpec((None, d, d), lambda m: (layer, 0, 0), pipeline_mode=pl.Buffered(1)),
                  pl.BlockSpec((tm, d), lambda m: (m, 0)),
                  pl.BlockSpec((1, d), lambda m: (0, 0)),
                  pl.BlockSpec((d, ROUTER_LANES), lambda m: (0, 0)),
                  pl.BlockSpec((1, ROUTER_LANES), lambda m: (0, 0))],
        out_specs=[pl.BlockSpec((tm, d), lambda m: (m, 0)),
                   pl.BlockSpec((tm * ROW_TILE, LANES), lambda m: (m, 0)),
                   pl.BlockSpec((tm, ROUTER_LANES), lambda m: (m, 0))],
        out_shape=[jax.ShapeDtypeStruct((n, d), F32),
                   jax.ShapeDtypeStruct((n * ROW_TILE, LANES), jnp.uint32),
                   jax.ShapeDtypeStruct((n, ROUTER_LANES), F32)],
        scratch_shapes=[pltpu.VMEM((d, d), BF16)],
        compiler_params=_cparams(("arbitrary",), 56),
        name="out_proj_norm_router",
    )(z, w_out, h, gain.reshape(1, d), w_router, b_router)


def _route(x, lane):
    big = float(ROUTER_LANES)

    def first_argmax(vals):
        top = jnp.max(vals, axis=1, keepdims=True)
        idx = jnp.min(jnp.where(vals == top, lane, big), axis=1, keepdims=True)
        return top, idx

    gmask = lane < N_GROUPS
    g_top, g_sel = first_argmax(jnp.where(gmask, x, -jnp.inf))
    g_weight = 1.0 / jnp.sum(jnp.where(gmask, jnp.exp(x - g_top), 0.0), axis=1, keepdims=True)
    lo = N_GROUPS + EXPERTS_PER_GROUP * g_sel
    e_vals = jnp.where((lane >= lo) & (lane < lo + EXPERTS_PER_GROUP), x, -jnp.inf)
    v1, i1 = first_argmax(e_vals)
    v2, i2 = first_argmax(jnp.where(lane == i1, -jnp.inf, e_vals))
    t = jnp.exp(v2 - v1)
    return i1 - N_GROUPS, i2 - N_GROUPS, g_weight / (1.0 + t), g_weight * t / (1.0 + t)


PLAN_BLOCK = 1024
PLAN_CHUNK = 256
META_TILE_EXPERT, META_NEXT_EXPERT, META_PAD_START, META_N_VALID = 0, 1, 2, 3


def _route_plan_kernel(logit_ref, w_ref, pos_ref, meta_ref, tri_s, cnt_s, base_s, off_s):
    phase = pl.program_id(0)
    blk = pl.program_id(1)
    tb = logit_ref.shape[0]
    lane = lax.broadcasted_iota(jnp.int32, (tb, ROUTER_LANES), 1).astype(F32)
    e0, e1, w0, w1 = _route(logit_ref[...], lane)
    onehot = jnp.where(lane == e0, 1.0, 0.0) + jnp.where(lane == e1, 1.0, 0.0)
    block_counts = jnp.sum(onehot, axis=0, keepdims=True)

    @pl.when(jnp.logical_and(phase == 0, blk == 0))
    def _():
        cnt_s[...] = jnp.zeros(cnt_s.shape, F32)

    @pl.when(phase == 0)
    def _():
        cnt_s[...] = cnt_s[...] + block_counts

    @pl.when(jnp.logical_and(phase == 1, blk == 0))
    def _():
        r = lax.broadcasted_iota(jnp.int32, (tb, tb), 0)
        c = lax.broadcasted_iota(jnp.int32, (tb, tb), 1)
        tri_s[...] = jnp.where(c < r, 1.0, 0.0).astype(BF16)
        lane1 = lane[0:1]
        cnt = cnt_s[...]
        tiles = jnp.floor((cnt + (TM_EXPERT - 1)) * (1.0 / TM_EXPERT))
        ri = lax.broadcasted_iota(jnp.int32, (ROUTER_LANES, ROUTER_LANES), 0)
        ci = lax.broadcasted_iota(jnp.int32, (ROUTER_LANES, ROUTER_LANES), 1)
        upper = jnp.where(ri <= ci, 1.0, 0.0).astype(BF16)
        ends = jnp.dot(jnp.broadcast_to(tiles, (8, ROUTER_LANES)).astype(BF16), upper,
                       preferred_element_type=F32)[0:1]
        off_s[...] = (ends - tiles) * TM_EXPERT
        base_s[...] = jnp.zeros(base_s.shape, F32)

        def pick(vec, e):
            return jnp.sum(jnp.where(lane1 == e, vec, 0.0), axis=1, keepdims=True)

        end_of = [pick(ends, e) for e in range(N_EXPERTS)]
        n_valid = end_of[N_EXPERTS - 1]

        def segment_of(tile):
            return sum(jnp.where(tile >= end_e, 1.0, 0.0) for end_e in end_of)

        tile_expert = jnp.where(lane1 < n_valid, segment_of(lane1), segment_of(n_valid - 1.0))
        following = sum(jnp.where(tile_expert == e, end_of[e], 0.0) for e in range(N_EXPERTS))
        next_expert = jnp.where(following < n_valid, segment_of(following), -1.0)
        rows = [tile_expert, next_expert, off_s[...] + cnt, jnp.broadcast_to(n_valid, (1, ROUTER_LANES))]
        rows += [jnp.zeros((1, ROUTER_LANES), F32)] * (meta_ref.shape[0] - len(rows))
        meta_ref[...] = jnp.concatenate(rows, axis=0).astype(jnp.int32)

    @pl.when(phase == 1)
    def _():
        before = jnp.dot(tri_s[...], onehot.astype(BF16), preferred_element_type=F32)
        row = before + base_s[...] + off_s[...]
        base_s[...] = base_s[...] + block_counts
        w_ref[...] = jnp.where(lane == 0, w0, jnp.where(lane == 1, w1, 0.0))
        eye = (lax.broadcasted_iota(jnp.int32, (LANES, LANES), 0)
               == lax.broadcasted_iota(jnp.int32, (LANES, LANES), 1))
        for j, e in enumerate((e0, e1)):
            col = jnp.sum(jnp.where(lane == e, row, 0.0), axis=1, keepdims=True)
            for q in range(tb // PLAN_CHUNK):
                parts = []
                for g in range(PLAN_CHUNK // LANES):
                    t0 = q * PLAN_CHUNK + g * LANES
                    square = jnp.broadcast_to(col[t0:t0 + LANES], (LANES, LANES))
                    parts.append(jnp.sum(jnp.where(eye, square, 0.0), axis=0, keepdims=True))
                pos_ref[j, q] = jnp.concatenate(parts, axis=1).astype(jnp.int32)


def route_plan_call(logits, n_tiles):
    n = logits.shape[0]
    tb = PLAN_BLOCK
    chunks = tb // PLAN_CHUNK
    assert n_tiles <= ROUTER_LANES and N_EXPERTS <= ROUTER_LANES
    return pl.pallas_call(
        _route_plan_kernel,
        grid=(2, n // tb),
        in_specs=[pl.BlockSpec((tb, ROUTER_LANES), lambda p, b: (b, 0))],
        out_specs=[pl.BlockSpec((tb, ROUTER_LANES), lambda p, b: (p * b, 0)),
                   pl.BlockSpec((2, chunks, 1, PLAN_CHUNK), lambda p, b: (0, p * b, 0, 0)),
                   pl.BlockSpec((8, ROUTER_LANES), lambda p, b: (0, 0))],
        out_shape=[jax.ShapeDtypeStruct((n, ROUTER_LANES), F32),
                   jax.ShapeDtypeStruct((2, n // PLAN_CHUNK, 1, PLAN_CHUNK), jnp.int32),
                   jax.ShapeDtypeStruct((8, ROUTER_LANES), jnp.int32)],
        scratch_shapes=[pltpu.VMEM((tb, tb), BF16), pltpu.VMEM((1, ROUTER_LANES), F32),
                        pltpu.VMEM((1, ROUTER_LANES), F32), pltpu.VMEM((1, ROUTER_LANES), F32)],
        compiler_params=_cparams(("arbitrary", "arbitrary"), 32),
        name="route_plan",
    )(logits)


DISPATCH_SLOTS = 3


def _dispatch_kernel(pad_ref, nv_ref, pos0_ref, pos1_ref, src_hbm, dst_ref, zero_buf, ring, sem, in_sems, row_sems,
                     *, chunk, n_tiles):
    i = pl.program_id(0)
    steps = pl.num_programs(0)
    slot = i % DISPATCH_SLOTS

    def load(block, into):
        return pltpu.make_async_copy(_rows(src_hbm, block * chunk, chunk), ring.at[into], in_sems.at[into])

    def wait_rows(of_slot):
        for j in range(2):
            pltpu.make_async_copy(ring.at[of_slot], _rows(dst_ref, 0, chunk), row_sems.at[of_slot]).wait()

    @pl.when(i == 0)
    def _():
        load(0, 0).start()
        zero_buf[...] = jnp.zeros(zero_buf.shape, zero_buf.dtype)

        def fill(e):
            return pltpu.make_async_copy(zero_buf, _rows(dst_ref, pad_ref[e], TM_EXPERT), sem)

        for e in range(N_EXPERTS):
            fill(e).start()
        for e in range(N_EXPERTS):
            fill(e).wait()

        def fill_tile(i):
            return pltpu.make_async_copy(zero_buf, _rows(dst_ref, i * TM_EXPERT, TM_EXPERT), sem)

        def start_tile(i, c):
            fill_tile(i).start()
            return c

        def wait_tile(i, c):
            fill_tile(i).wait()
            return c

        lax.fori_loop(nv_ref[0], n_tiles, start_tile, 0)
        lax.fori_loop(nv_ref[0], n_tiles, wait_tile, 0)

    nxt = (i + 1) % DISPATCH_SLOTS

    @pl.when(i >= DISPATCH_SLOTS - 1)
    def _():
        wait_rows(nxt)

    @pl.when(i + 1 < steps)
    def _():
        load(i + 1, nxt).start()

    load(i, slot).wait()
    src_ref = ring.at[slot]

    def start(t, c):
        for j, pos_ref in enumerate((pos0_ref, pos1_ref)):
            pltpu.make_async_copy(_rows(src_ref, t), _rows(dst_ref, pos_ref[0, 0, t]),
                                  row_sems.at[slot]).start(priority=j)
        return c

    lax.fori_loop(0, chunk, start, 0, unroll=ROW_DMA_UNROLL)

    @pl.when(i == steps - 1)
    def _():
        for back in range(DISPATCH_SLOTS - 1):
            @pl.when(i - back >= 0)
            def _():
                wait_rows((i - back) % DISPATCH_SLOTS)


def dispatch_call(hn, pos, pad_start, n_valid_tiles, n_tiles):
    n = hn.shape[0] // ROW_TILE
    chunk = PLAN_CHUNK
    grid_spec = pltpu.PrefetchScalarGridSpec(
        num_scalar_prefetch=2,
        grid=(n // chunk,),
        in_specs=[pl.BlockSpec((None, 1, 1, chunk), lambda i, pad, nv: (0, i, 0, 0), memory_space=pltpu.SMEM),
                  pl.BlockSpec((None, 1, 1, chunk), lambda i, pad, nv: (1, i, 0, 0), memory_space=pltpu.SMEM),
                  pl.BlockSpec(memory_space=pl.ANY)],
        out_specs=pl.BlockSpec(memory_space=pl.ANY),
        scratch_shapes=[pltpu.VMEM((TM_EXPERT * ROW_TILE, LANES), hn.dtype),
                        pltpu.VMEM((DISPATCH_SLOTS, chunk * ROW_TILE, LANES), hn.dtype),
                        pltpu.SemaphoreType.DMA(()), pltpu.SemaphoreType.DMA((DISPATCH_SLOTS,)),
                        pltpu.SemaphoreType.DMA((DISPATCH_SLOTS,))],
    )
    return pl.pallas_call(
        functools.partial(_dispatch_kernel, chunk=chunk, n_tiles=n_tiles),
        grid_spec=grid_spec,
        out_shape=jax.ShapeDtypeStruct((n_tiles * TM_EXPERT * ROW_TILE, LANES), hn.dtype),
        compiler_params=_cparams(("arbitrary",), 32),
        name="dispatch_rows",
    )(pad_start, n_valid_tiles, pos, pos, hn)


def _expert_kernel(te_ref, nv_ref, nx_ref, x_ref, wg_hbm, wu_hbm, wd_hbm, y_ref,
                   wg_f, wu_f, wd_f, wg_s, wu_s, wd_s, seg_ref, sems, *, layer):
    i = pl.program_id(0)
    valid = i < nv_ref[0]
    expert = te_ref[i]
    changed = jnp.logical_or(i == 0, expert != te_ref[jnp.maximum(i - 1, 0)])

    def fetch(e, slot):
        copies = []
        for hbm, buf in ((wg_hbm, wg_f), (wu_hbm, wu_f), (wd_hbm, wd_f)):
            rows = buf.shape[1] // EXPERT_FETCH_SPLIT
            for part in range(EXPERT_FETCH_SPLIT):
                sl = pl.ds(part * rows, rows)
                copies.append(pltpu.make_async_copy(hbm.at[layer, e, sl], buf.at[slot, sl], sems.at[slot]))
        return copies

    def swiglu_tile():
        lo, hi = _unpack_halves(_load_row_tiles(x_ref))
        x = jnp.concatenate([lo.astype(BF16), hi.astype(BF16)], axis=1)
        a = jnp.dot(x, wg_s[...], preferred_element_type=F32)
        b = jnp.dot(x, wu_s[...], preferred_element_type=F32)
        hmid = (a * _sigmoid(a) * b).astype(BF16)
        _store_row_tiles(y_ref, _pack_halves(jnp.dot(hmid, wd_s[...], preferred_element_type=F32)))

    @pl.when(i == 0)
    def _():
        seg_ref[0] = 0
        for cp in fetch(expert, 0):
            cp.start()

    @pl.when(jnp.logical_and(valid, changed))
    def _():
        slot = seg_ref[0] % 2
        for cp in fetch(expert, slot):
            cp.wait()
        nxt = nx_ref[i]

        @pl.when(nxt >= 0)
        def _():
            for cp in fetch(nxt, 1 - slot):
                cp.start()

        seg_ref[0] = seg_ref[0] + 1
        wg_s[...] = wg_f[slot].astype(BF16)
        wu_s[...] = wu_f[slot].astype(BF16)
        wd_s[...] = wd_f[slot].astype(BF16)
        swiglu_tile()

    @pl.when(jnp.logical_and(valid, jnp.logical_not(changed)))
    def _():
        swiglu_tile()

    @pl.when(jnp.logical_not(valid))
    def _():
        y_ref[...] = jnp.zeros(y_ref.shape, y_ref.dtype)


def expert_call(xs, tile_expert, n_valid_tiles, next_expert, w_gate, w_up, w_down, layer):
    p = xs.shape[0] // ROW_TILE
    d = 2 * ROW_TILE * LANES
    tm = TM_EXPERT
    f = w_gate.shape[-1]
    grid_spec = pltpu.PrefetchScalarGridSpec(
        num_scalar_prefetch=3,
        grid=(p // tm,),
        in_specs=[pl.BlockSpec((tm * ROW_TILE, LANES), lambda i, te, nv, nx: (jnp.minimum(i, nv[0] - 1), 0)),
                  pl.BlockSpec(memory_space=pl.ANY),
                  pl.BlockSpec(memory_space=pl.ANY),
                  pl.BlockSpec(memory_space=pl.ANY)],
        out_specs=pl.BlockSpec((tm * ROW_TILE, LANES), lambda i, te, nv, nx: (i, 0)),
        scratch_shapes=[pltpu.VMEM((2, d, f), F32), pltpu.VMEM((2, d, f), F32), pltpu.VMEM((2, f, d), F32),
                        pltpu.VMEM((d, f), BF16), pltpu.VMEM((d, f), BF16), pltpu.VMEM((f, d), BF16),
                        pltpu.SMEM((1,), jnp.int32), pltpu.SemaphoreType.DMA((2,))],
    )
    return pl.pallas_call(
        functools.partial(_expert_kernel, layer=layer),
        grid_spec=grid_spec,
        out_shape=jax.ShapeDtypeStruct(xs.shape, jnp.uint32),
        compiler_params=_cparams(("arbitrary",), 58),
        name="expert_swiglu",
    )(tile_expert, n_valid_tiles, next_expert, xs, w_gate, w_up, w_down)


def _combine_kernel(pos0_ref, pos1_ref, nxt0_ref, nxt1_ref, ys_ref, h_ref, w_ref, gain_ref, *rest, chunk):
    *out_refs, buf, sems = rest
    xn_ref, h2_ref = out_refs if len(out_refs) == 2 else (out_refs[0], None)
    i = pl.program_id(0)
    slot = i % 2

    def gather(p0_ref, p1_ref, into):
        def start(t, c):
            for j, pos_ref in enumerate((p0_ref, p1_ref)):
                pltpu.make_async_copy(_rows(ys_ref, pos_ref[0, 0, t]), _rows(buf.at[into, j], t),
                                      sems.at[into]).start(priority=j)
            return c

        lax.fori_loop(0, chunk, start, 0, unroll=ROW_DMA_UNROLL)

    @pl.when(i == 0)
    def _():
        gather(pos0_ref, pos1_ref, 0)

    @pl.when(i + 1 < pl.num_programs(0))
    def _():
        gather(nxt0_ref, nxt1_ref, 1 - slot)

    for j in range(2):
        pltpu.make_async_copy(_rows(ys_ref, 0, chunk), buf.at[slot, j], sems.at[slot]).wait()
    w = w_ref[...]
    lo0, hi0 = _unpack_halves(_load_row_tiles(buf.at[slot, 0]))
    lo1, hi1 = _unpack_halves(_load_row_tiles(buf.at[slot, 1]))
    y = jnp.concatenate([w[:, 0:1] * lo0 + w[:, 1:2] * lo1, w[:, 0:1] * hi0 + w[:, 1:2] * hi1], axis=1)
    h2 = h_ref[...] + y
    if h2_ref is not None:
        h2_ref[...] = h2
    xn_ref[...] = _rms(h2, gain_ref[...]).astype(xn_ref.dtype)


def combine_call(ys, pos, h1, weights, next_gain, xn_dtype, keep_residual, chunk=256):
    n, d = h1.shape
    row_spec = pl.BlockSpec((chunk, d), lambda c: (c, 0))
    out_specs = [row_spec, row_spec] if keep_residual else [row_spec]
    out_shape = [jax.ShapeDtypeStruct((n, d), xn_dtype)] + ([jax.ShapeDtypeStruct((n, d), F32)] if keep_residual else [])
    per_row = PLAN_CHUNK // chunk
    steps = n // chunk

    def pos_spec(j, ahead):
        def index(c):
            c = jnp.minimum(c + ahead, steps - 1)
            return (j, c // per_row, 0, c % per_row)
        return pl.BlockSpec((None, 1, 1, chunk), index, memory_space=pltpu.SMEM)

    outs = pl.pallas_call(
        functools.partial(_combine_kernel, chunk=chunk),
        grid=(steps,),
        in_specs=[pos_spec(0, 0), pos_spec(1, 0), pos_spec(0, 1), pos_spec(1, 1),
                  pl.BlockSpec(memory_space=pl.ANY),
                  row_spec,
                  pl.BlockSpec((chunk, ROUTER_LANES), lambda c: (c, 0)),
                  pl.BlockSpec((1, d), lambda c: (0, 0))],
        out_specs=out_specs,
        out_shape=out_shape,
        scratch_shapes=[pltpu.VMEM((2, 2, chunk * ROW_TILE, LANES), jnp.uint32), pltpu.SemaphoreType.DMA((2,))],
        compiler_params=_cparams(("arbitrary",), 32),
        name="combine_rows",
    )(pos, pos, pos, pos, ys, h1, weights, next_gain.reshape(1, d))
    return (outs[0], outs[1]) if keep_residual else (outs[0], None)


def kernel(x, rel_bias_table, norm_mix_gain, w_in, w_merge_gate, b_merge_gate, pool_mix, pool_scale, w_up_pool, lambda_q1, lambda_k1, lambda_q2, lambda_k2, subln_gain, w_up_attn, w_out, norm_ffn_gain, w_router_group, b_router_group, w_router_expert, b_router_expert, w_expert_gate, w_expert_up, w_expert_down, final_norm_gain):
    batch, seq, d = x.shape
    depth = w_in.shape[0]
    n = batch * seq
    n_tiles = (2 * n + N_EXPERTS * (TM_EXPERT - 1)) // TM_EXPERT + 1

    bias_tiles = bias_tiles_call(rel_bias_table)
    h = x.reshape(n, d)
    xn = None
    q_scale = LOG2E * DIFF_HEAD_DIM ** -0.5

    for l in range(depth):
        first = l == 0
        xn_first, mixed = pool_branch_call(h if first else xn, norm_mix_gain[l], w_in, l, pool_mix[l].astype(BF16),
                                           pool_scale[l], seq, normalize=first)
        xn = xn_first if first else xn
        qkv = proj_call(xn, w_in, l, 1, 3, ATTN_WIDTH, BF16, first_block_scale=q_scale, name="in_proj_qkv")
        lambda_init = 0.8 - 0.6 * math.exp(-0.3 * l)
        lam_params = jnp.stack([lambda_q1[l], lambda_k1[l], lambda_q2[l], lambda_k2[l]])
        attn = attn_call(qkv.reshape(batch, seq, 3 * ATTN_WIDTH), bias_tiles, lam_params, subln_gain[l],
                         lambda_init, batch, seq).reshape(n, ATTN_WIDTH)
        z = merge_call(xn, mixed, attn, w_merge_gate, b_merge_gate.reshape(depth, 1, -1), w_up_pool, w_up_attn, l)

        w_router = jnp.concatenate(
            [w_router_group[l], jnp.transpose(w_router_expert[l], (1, 0, 2)).reshape(d, N_EXPERTS),
             jnp.zeros((d, ROUTER_LANES - N_GROUPS - N_EXPERTS), F32)], axis=1).astype(BF16)
        b_router = jnp.concatenate(
            [b_router_group[l], b_router_expert[l].reshape(-1),
             jnp.zeros((ROUTER_LANES - N_GROUPS - N_EXPERTS,), F32)]).reshape(1, ROUTER_LANES)
        h1, hn, logits = outproj_call(z, w_out, l, h, norm_ffn_gain[l], w_router, b_router)

        weights, pos, meta = route_plan_call(logits, n_tiles)
        tile_expert = meta[META_TILE_EXPERT, :n_tiles]
        next_expert = meta[META_NEXT_EXPERT, :n_tiles]
        pad_start = meta[META_PAD_START, :N_EXPERTS]
        n_valid = meta[META_N_VALID, :1]
        xs = dispatch_call(hn, pos, pad_start, n_valid, n_tiles)
        ys = expert_call(xs, tile_expert, n_valid, next_expert, w_expert_gate, w_expert_up, w_expert_down, l)
        last = l == depth - 1
        next_gain = final_norm_gain if last else norm_mix_gain[l + 1]
        xn, h = combine_call(ys, pos, h1, weights, next_gain, F32 if last else BF16, keep_residual=not last)

    return xn.reshape(batch, seq, d)
```

```python
import functools
import math

import numpy as np
import jax
import jax.numpy as jnp
from jax import lax
from jax.experimental import pallas as pl
from jax.experimental.pallas import tpu as pltpu

F32 = jnp.float32
BF16 = jnp.bfloat16

D_MODEL = 2048
POOL_WIDTH = 1024
POOL_WINDOWS = (2, 4, 8, 16)
POOL_GROUP_DIM = 256
POOL_HALO = 16
DIFF_HEADS = 8
DIFF_HEAD_DIM = 64
DIFF_V_DIM = 128
ATTN_WIDTH = 1024
REL_BUCKETS = 32
REL_MAX_DISTANCE = 128
N_GROUPS = 4
EXPERTS_PER_GROUP = 8
N_EXPERTS = 32
D_EXPERT = 512
EPS = 1e-6
NEG_INF = -1e30
LOG2E = 1.4426950408889634

ROUTER_LANES = 128
TQ = 256
TQ_BLOCK = 256
TK = 256
KV_SUPER = 2
V_ROWS = DIFF_V_DIM + 16
HEADS_PER_STEP = 8
TM_EXPERT = 256
LANES = 128
ROW_DMA_UNROLL = 8
MIB = 1024 * 1024


def _cparams(sem, vmem_mib):
    return pltpu.CompilerParams(dimension_semantics=sem, vmem_limit_bytes=vmem_mib * MIB)


def _rms(xf, gain):
    ms = jnp.mean(xf * xf, axis=-1, keepdims=True)
    return xf * lax.rsqrt(ms + EPS) * gain


def _proj_kernel(x_ref, w_ref, o_ref, w_s, *, first_block_scale):
    @pl.when(pl.program_id(1) == 0)
    def _():
        w_s[...] = w_ref[...].astype(BF16)

    acc = jnp.dot(x_ref[...], w_s[...], preferred_element_type=F32)
    if first_block_scale is not None:
        acc = acc * jnp.where(pl.program_id(0) == 0, first_block_scale, 1.0).astype(F32)
    o_ref[...] = acc.astype(o_ref.dtype)


def proj_call(x, w, layer, col_block0, n_col_blocks, tn, out_dtype, first_block_scale=None, tm=1024, name="proj"):
    n, k = x.shape
    return pl.pallas_call(
        functools.partial(_proj_kernel, first_block_scale=first_block_scale),
        grid=(n_col_blocks, n // tm),
        in_specs=[pl.BlockSpec((tm, k), lambda j, m: (m, 0)),
                  pl.BlockSpec((None, k, tn), lambda j, m: (layer, 0, col_block0 + j))],
        out_specs=pl.BlockSpec((tm, tn), lambda j, m: (m, j)),
        out_shape=jax.ShapeDtypeStruct((n, n_col_blocks * tn), out_dtype),
        scratch_shapes=[pltpu.VMEM((k, tn), BF16)],
        compiler_params=_cparams(("arbitrary", "arbitrary"), 48),
        name=name,
    )(x, w)


def _pool_branch_kernel(x_ref, gain_ref, w_ref, mix_ref, scale_ref, *rest, tm, seq, normalize):
    if normalize:
        xn_ref, o_ref, w_s, carry = rest
    else:
        o_ref, w_s, carry = rest
    m = pl.program_id(0)

    @pl.when(m == 0)
    def _():
        w_s[...] = w_ref[...].astype(BF16)

    if normalize:
        xn = _rms(x_ref[...], gain_ref[...]).astype(BF16)
        xn_ref[...] = xn
    else:
        xn = x_ref[...]
    cur = jnp.dot(xn, w_s[...], preferred_element_type=F32)
    row0 = (m * tm) % seq
    prev = jnp.where(row0 == 0, 0.0, carry[...])
    carry[...] = cur[tm - POOL_HALO:]
    pos = row0 + lax.broadcasted_iota(jnp.int32, (tm, 1), 0)
    outs = []
    for g, w in enumerate(POOL_WINDOWS):
        sl = slice(g * POOL_GROUP_DIM, (g + 1) * POOL_GROUP_DIM)
        x = jnp.concatenate([prev[:, sl], cur[:, sl]], axis=0)
        s, d = x, 1
        while d < w:
            s = s[:-d] + s[d:]
            d *= 2
        start = POOL_HALO - w + 1
        wsum = s[start:start + tm]
        count = jnp.minimum(pos + 1, w).astype(F32)
        pooled = wsum / count - cur[:, sl]
        mixed = jnp.dot(pooled.astype(BF16), mix_ref[g], preferred_element_type=F32)
        outs.append(mixed * scale_ref[:, sl])
    o_ref[...] = jnp.concatenate(outs, axis=1).astype(o_ref.dtype)


def pool_branch_call(x, gain, w_in, layer, mix_bf16, scale, seq, normalize, tm=512):
    n, d = x.shape
    c = POOL_WIDTH
    row_spec = pl.BlockSpec((tm, c), lambda m: (m, 0))
    x_spec = pl.BlockSpec((tm, d), lambda m: (m, 0))
    out_specs = [x_spec, row_spec] if normalize else [row_spec]
    out_shape = ([jax.ShapeDtypeStruct((n, d), BF16)] if normalize else []) + [jax.ShapeDtypeStruct((n, c), BF16)]
    outs = pl.pallas_call(
        functools.partial(_pool_branch_kernel, tm=tm, seq=seq, normalize=normalize),
        grid=(n // tm,),
        in_specs=[x_spec,
                  pl.BlockSpec((1, d), lambda m: (0, 0)),
                  pl.BlockSpec((None, d, c), lambda m: (layer, 0, 0), pipeline_mode=pl.Buffered(1)),
                  pl.BlockSpec(mix_bf16.shape, lambda m: (0, 0, 0)),
                  pl.BlockSpec((1, c), lambda m: (0, 0))],
        out_specs=out_specs,
        out_shape=out_shape,
        scratch_shapes=[pltpu.VMEM((d, c), BF16), pltpu.VMEM((POOL_HALO, c), F32)],
        compiler_params=_cparams(("arbitrary",), 48),
        name="pool_branch",
    )(x, gain.reshape(1, d), w_in, mix_bf16, scale.reshape(1, c))
    return tuple(outs) if normalize else (None, outs[0])


N_BIAS_TILES = 4


def _bucket_tiles():
    kk = np.arange(TK)[:, None]
    qq = np.arange(TQ_BLOCK)[None, :]
    tiles = []
    for rel in (0, 1, 2, -1):
        n = rel * TK + qq - kk
        max_exact = REL_BUCKETS // 2
        nf = np.maximum(n, 1).astype(np.float64)
        large = max_exact + (np.log(nf / max_exact) / math.log(REL_MAX_DISTANCE / max_exact)
                             * (REL_BUCKETS - max_exact)).astype(np.int64)
        large = np.minimum(large, REL_BUCKETS - 1)
        bucket = np.where(n < max_exact, n, large)
        tiles.append(np.where(n < 0, -1, bucket))
    return np.stack(tiles).astype(np.int32)


def _bias_kernel(table_ref, bucket_ref, o_ref):
    h = pl.program_id(0)
    bucket = bucket_ref[...]
    acc = jnp.full(bucket.shape, NEG_INF, F32)
    for b in range(REL_BUCKETS):
        acc = jnp.where(bucket == b, table_ref[b * DIFF_HEADS + h] * LOG2E, acc)
    o_ref[...] = acc


def bias_tiles_call(rel_table):
    bucket = jnp.asarray(_bucket_tiles())
    return pl.pallas_call(
        _bias_kernel,
        grid=(DIFF_HEADS,),
        in_specs=[pl.BlockSpec(memory_space=pltpu.SMEM),
                  pl.BlockSpec(bucket.shape, lambda h: (0, 0, 0))],
        out_specs=pl.BlockSpec((None,) + bucket.shape, lambda h: (h, 0, 0, 0)),
        out_shape=jax.ShapeDtypeStruct((DIFF_HEADS,) + bucket.shape, F32),
        compiler_params=_cparams(("parallel",), 32),
        name="rel_bias_tiles",
    )(rel_table.reshape(-1), bucket)


def _attn_kernel(q_ref, k_ref, v_ref, bias_ref, lam_ref, gain_ref, o_ref, vt_ref, s_ref, acc_ref,
                 *, lambda_init, n_super):
    qi = pl.program_id(2)
    tks = KV_SUPER * TK
    heads = range(HEADS_PER_STEP)

    def block_diag(ref, hh):
        q = ref[:, hh * DIFF_V_DIM:(hh + 1) * DIFF_V_DIM]
        lane = lax.broadcasted_iota(jnp.int32, q.shape, 1)
        zero = jnp.zeros_like(q)
        return jnp.concatenate([jnp.where(lane < DIFF_HEAD_DIM, q, zero),
                                jnp.where(lane >= DIFF_HEAD_DIM, q, zero)], axis=0)

    def scores(t, hh, qd, tile):
        kb = k_ref[pl.ds(pl.multiple_of(t * tks, tks), tks), hh * DIFF_V_DIM:(hh + 1) * DIFF_V_DIM]
        s = lax.dot_general(kb, qd, (((1,), (1,)), ((), ())), preferred_element_type=F32)
        parts = []
        for u in range(KV_SUPER):
            tiles = []
            for c in range(TQ // TQ_BLOCK):
                rel = tile * (TQ // TQ_BLOCK) + c - (t * KV_SUPER + u)
                tiles.append(bias_ref[hh, jnp.where(rel < 0, N_BIAS_TILES - 1, jnp.minimum(rel, 2))])
            parts.append(s[u * TK:(u + 1) * TK] + jnp.concatenate(tiles + tiles, axis=1))
        s = jnp.concatenate(parts, axis=0)
        s_ref[hh] = s
        return jnp.max(s, axis=0, keepdims=True)

    @pl.when(qi == 0)
    def _():
        extra = (lax.broadcasted_iota(jnp.int32, (V_ROWS - DIFF_V_DIM, tks), 0) == 0).astype(BF16)
        for hh in heads:
            cols = slice(hh * DIFF_V_DIM, (hh + 1) * DIFF_V_DIM)
            for c in range(n_super):
                vt = v_ref[c * tks:(c + 1) * tks, cols].astype(F32).T.astype(BF16)
                vt_ref[hh, c] = jnp.concatenate([vt, extra], axis=0)

    qds = [block_diag(q_ref, hh) for hh in heads]

    def softmax_step(t, hh, m_prev, m_cur):
        alpha = jnp.exp2(m_prev - m_cur)
        p = jnp.exp2(s_ref[hh] - m_cur)
        pv = jnp.dot(vt_ref[hh, t], p.astype(BF16), preferred_element_type=F32)
        acc_ref[hh] = acc_ref[hh] * alpha + pv

    last = ((qi + 1) * TQ - 1) // tks
    acc_ref[...] = jnp.zeros(acc_ref.shape, F32)
    neg = jnp.full((1, 2 * TQ), NEG_INF, F32)
    init = tuple((neg, jnp.maximum(neg, scores(0, hh, qds[hh], qi))) for hh in heads)

    def body(t, carry):
        out = []
        for hh in heads:
            m_prev, m_cur = carry[hh]
            softmax_step(t, hh, m_prev, m_cur)
            m_next = jnp.maximum(m_cur, scores(t + 1, hh, qds[hh], qi))
            out.append((m_cur, m_next))
        return tuple(out)

    carry = lax.fori_loop(0, last, body, init)

    lam_p = lam_ref[...]
    lam = (jnp.exp(jnp.sum(lam_p[0:1] * lam_p[1:2], axis=1, keepdims=True))
           - jnp.exp(jnp.sum(lam_p[2:3] * lam_p[3:4], axis=1, keepdims=True)) + lambda_init)
    for hh in heads:
        m_prev, m_cur = carry[hh]
        softmax_step(last, hh, m_prev, m_cur)
        acc = acc_ref[hh, :DIFF_V_DIM, :]
        l = acc_ref[hh, DIFF_V_DIM:DIFF_V_DIM + 1, :]
        o = acc[:, :TQ] / l[:, :TQ] - lam * (acc[:, TQ:] / l[:, TQ:])
        ms = jnp.mean(o * o, axis=0, keepdims=True)
        y = o * lax.rsqrt(ms + EPS) * gain_ref[...] * (1.0 - lambda_init)
        o_ref[:, hh * DIFF_V_DIM:(hh + 1) * DIFF_V_DIM] = y.T.astype(o_ref.dtype)


def attn_call(qkv, bias_tiles, lam_params, subln_gain, lambda_init, batch, seq):
    tks = KV_SUPER * TK
    n_super = seq // tks
    hps = HEADS_PER_STEP
    width = hps * DIFF_V_DIM
    groups = DIFF_HEADS // hps
    n_q = seq // TQ
    return pl.pallas_call(
        functools.partial(_attn_kernel, lambda_init=lambda_init, n_super=n_super),
        grid=(batch, groups, n_q),
        in_specs=[pl.BlockSpec((None, TQ, width), lambda b, g, i: (b, i, g)),
                  pl.BlockSpec((None, seq, width), lambda b, g, i: (b, 0, groups + g), pipeline_mode=pl.Buffered(1)),
                  pl.BlockSpec((None, seq, width), lambda b, g, i: (b, 0, 2 * groups + g),
                               pipeline_mode=pl.Buffered(1)),
                  pl.BlockSpec((hps, N_BIAS_TILES, TK, TQ_BLOCK), lambda b, g, i: (g, 0, 0, 0),
                               pipeline_mode=pl.Buffered(1)),
                  pl.BlockSpec((4, DIFF_HEAD_DIM), lambda b, g, i: (0, 0)),
                  pl.BlockSpec((DIFF_V_DIM, 1), lambda b, g, i: (0, 0))],
        out_specs=pl.BlockSpec((None, TQ, width), lambda b, g, i: (b, i, g)),
        out_shape=jax.ShapeDtypeStruct((batch, seq, ATTN_WIDTH), BF16),
        scratch_shapes=[pltpu.VMEM((hps, n_super, V_ROWS, tks), BF16),
                        pltpu.VMEM((hps, tks, 2 * TQ), F32),
                        pltpu.VMEM((hps, V_ROWS, 2 * TQ), F32)],
        compiler_params=_cparams(("parallel", "parallel", "arbitrary"), 56),
        name="diff_attention",
    )(qkv, qkv, qkv, bias_tiles, lam_params, subln_gain.reshape(DIFF_V_DIM, 1))


def _sigmoid(x):
    return 1.0 / (1.0 + jnp.exp(-x))


def _merge_kernel(xn_ref, mixed_ref, attn_ref, wgp_ref, wga_ref, bgp_ref, bga_ref, wup_ref, wua_ref, z_ref,
                  wgp_s, wga_s, wup_s, wua_s):
    @pl.when(pl.program_id(1) == 0)
    def _():
        wgp_s[...] = wgp_ref[...].astype(BF16)
        wga_s[...] = wga_ref[...].astype(BF16)
        wup_s[...] = wup_ref[...].astype(BF16)
        wua_s[...] = wua_ref[...].astype(BF16)

    xn = xn_ref[...]
    g_pool = _sigmoid(jnp.dot(xn, wgp_s[...], preferred_element_type=F32) + bgp_ref[...])
    g_attn = _sigmoid(jnp.dot(xn, wga_s[...], preferred_element_type=F32) + bga_ref[...])
    y_pool = jnp.dot(mixed_ref[...], wup_s[...], preferred_element_type=F32)
    y_attn = jnp.dot(attn_ref[...], wua_s[...], preferred_element_type=F32)
    z_ref[...] = (g_pool * y_pool + g_attn * y_attn).astype(z_ref.dtype)


def merge_call(xn, mixed, attn, w_gate, b_gate, w_up_pool, w_up_attn, layer, tm=1024, tn=512):
    n, d = xn.shape
    nb = d // tn
    return pl.pallas_call(
        _merge_kernel,
        grid=(nb, n // tm),
        in_specs=[pl.BlockSpec((tm, d), lambda j, m: (m, 0)),
                  pl.BlockSpec((tm, POOL_WIDTH), lambda j, m: (m, 0)),
                  pl.BlockSpec((tm, ATTN_WIDTH), lambda j, m: (m, 0)),
                  pl.BlockSpec((None, d, tn), lambda j, m: (layer, 0, j)),
                  pl.BlockSpec((None, d, tn), lambda j, m: (layer, 0, nb + j)),
                  pl.BlockSpec((None, 1, tn), lambda j, m: (layer, 0, j)),
                  pl.BlockSpec((None, 1, tn), lambda j, m: (layer, 0, nb + j)),
                  pl.BlockSpec((None, POOL_WIDTH, tn), lambda j, m: (layer, 0, j)),
                  pl.BlockSpec((None, ATTN_WIDTH, tn), lambda j, m: (layer, 0, j))],
        out_specs=pl.BlockSpec((tm, tn), lambda j, m: (m, j)),
        out_shape=jax.ShapeDtypeStruct((n, d), BF16),
        scratch_shapes=[pltpu.VMEM((d, tn), BF16), pltpu.VMEM((d, tn), BF16),
                        pltpu.VMEM((POOL_WIDTH, tn), BF16), pltpu.VMEM((ATTN_WIDTH, tn), BF16)],
        compiler_params=_cparams(("arbitrary", "arbitrary"), 56),
        name="gated_merge",
    )(xn, mixed, attn, w_gate, w_gate, b_gate, b_gate, w_up_pool, w_up_attn)


HIGH_HALF = 0xFFFF0000


def _pack_halves(x):
    c = x.shape[1] // 2
    lo = lax.bitcast_convert_type(x[:, :c].astype(BF16).astype(F32), jnp.uint32)
    hi = lax.bitcast_convert_type(x[:, c:].astype(BF16).astype(F32), jnp.uint32)
    return (lo >> 16) | hi


def _unpack_halves(p):
    lo = lax.bitcast_convert_type(p << 16, F32)
    hi = lax.bitcast_convert_type(p & jnp.uint32(HIGH_HALF), F32)
    return lo, hi


ROW_TILE = 8


def _store_row_tiles(ref, packed):
    m = packed.shape[0]
    for s in range(ROW_TILE):
        ref[pl.ds(s, m, stride=ROW_TILE), :] = packed[:, s * LANES:(s + 1) * LANES]


def _load_row_tiles(ref):
    m = ref.shape[0] // ROW_TILE
    return jnp.concatenate([ref[pl.ds(s, m, stride=ROW_TILE), :] for s in range(ROW_TILE)], axis=1)


def _rows(ref, first, count=1):
    return ref.at[pl.ds(pl.multiple_of(first * ROW_TILE, ROW_TILE), count * ROW_TILE)]


def _outproj_kernel(z_ref, w_ref, h_ref, gain_ref, wr_ref, br_ref, h1_ref, hn_ref, logit_ref, w_s):
    @pl.when(pl.program_id(0) == 0)
    def _():
        w_s[...] = w_ref[...].astype(BF16)

    h1 = h_ref[...] + jnp.dot(z_ref[...], w_s[...], preferred_element_type=F32)
    h1_ref[...] = h1
    hn = _rms(h1, gain_ref[...])
    _store_row_tiles(hn_ref, _pack_halves(hn))
    logit_ref[...] = jnp.dot(hn.astype(BF16), wr_ref[...], preferred_element_type=F32) + br_ref[...]


def outproj_call(z, w_out, layer, h, gain, w_router, b_router, tm=256):
    n, d = h.shape
    return pl.pallas_call(
        _outproj_kernel,
        grid=(n // tm,),
        in_specs=[pl.BlockSpec((tm, d), lambda m: (m, 0)),
                  pl.BlockSpec((None, d, d), lambda m: (layer, 0, 0), pipeline_mode=pl.Buffered(1)),
                  pl.BlockSpec((tm, d), lambda m: (m, 0)),
                  pl.BlockSpec((1, d), lambda m: (0, 0)),
                  pl.BlockSpec((d, ROUTER_LANES), lambda m: (0, 0)),
                  pl.BlockSpec((1, ROUTER_LANES), lambda m: (0, 0))],
        out_specs=[pl.BlockSpec((tm, d), lambda m: (m, 0)),
                   pl.BlockSpec((tm * ROW_TILE, LANES), lambda m: (m, 0)),
                   pl.BlockSpec((tm, ROUTER_LANES), lambda m: (m, 0))],
        out_shape=[jax.ShapeDtypeStruct((n, d), F32),
                   jax.ShapeDtypeStruct((n * ROW_TILE, LANES), jnp.uint32),
                   jax.ShapeDtypeStruct((n, ROUTER_LANES), F32)],
        scratch_shapes=[pltpu.VMEM((d, d), BF16)],
        compiler_params=_cparams(("arbitrary",), 56),
        name="out_proj_norm_router",
    )(z, w_out, h, gain.reshape(1, d), w_router, b_router)


def _route(x, lane):
    big = float(ROUTER_LANES)

    def first_argmax(vals):
        top = jnp.max(vals, axis=1, keepdims=True)
        idx = jnp.min(jnp.where(vals == top, lane, big), axis=1, keepdims=True)
        return top, idx

    gmask = lane < N_GROUPS
    g_top, g_sel = first_argmax(jnp.where(gmask, x, -jnp.inf))
    g_weight = 1.0 / jnp.sum(jnp.where(gmask, jnp.exp(x - g_top), 0.0), axis=1, keepdims=True)
    lo = N_GROUPS + EXPERTS_PER_GROUP * g_sel
    e_vals = jnp.where((lane >= lo) & (lane < lo + EXPERTS_PER_GROUP), x, -jnp.inf)
    v1, i1 = first_argmax(e_vals)
    v2, i2 = first_argmax(jnp.where(lane == i1, -jnp.inf, e_vals))
    t = jnp.exp(v2 - v1)
    return i1 - N_GROUPS, i2 - N_GROUPS, g_weight / (1.0 + t), g_weight * t / (1.0 + t)


PLAN_BLOCK = 1024
PLAN_CHUNK = 512
META_TILE_EXPERT, META_NEXT_EXPERT, META_PAD_START, META_N_VALID = 0, 1, 2, 3


def _route_plan_kernel(logit_ref, w_ref, pos_ref, meta_ref, tri_s, cnt_s, base_s, off_s):
    phase = pl.program_id(0)
    blk = pl.program_id(1)
    tb = logit_ref.shape[0]
    lane = lax.broadcasted_iota(jnp.int32, (tb, ROUTER_LANES), 1).astype(F32)
    e0, e1, w0, w1 = _route(logit_ref[...], lane)
    onehot = jnp.where(lane == e0, 1.0, 0.0) + jnp.where(lane == e1, 1.0, 0.0)
    block_counts = jnp.sum(onehot, axis=0, keepdims=True)

    @pl.when(jnp.logical_and(phase == 0, blk == 0))
    def _():
        cnt_s[...] = jnp.zeros(cnt_s.shape, F32)

    @pl.when(phase == 0)
    def _():
        cnt_s[...] = cnt_s[...] + block_counts

    @pl.when(jnp.logical_and(phase == 1, blk == 0))
    def _():
        r = lax.broadcasted_iota(jnp.int32, (tb, tb), 0)
        c = lax.broadcasted_iota(jnp.int32, (tb, tb), 1)
        tri_s[...] = jnp.where(c < r, 1.0, 0.0).astype(BF16)
        lane1 = lane[0:1]
        cnt = cnt_s[...]
        tiles = jnp.floor((cnt + (TM_EXPERT - 1)) * (1.0 / TM_EXPERT))
        ri = lax.broadcasted_iota(jnp.int32, (ROUTER_LANES, ROUTER_LANES), 0)
        ci = lax.broadcasted_iota(jnp.int32, (ROUTER_LANES, ROUTER_LANES), 1)
        upper = jnp.where(ri <= ci, 1.0, 0.0).astype(BF16)
        ends = jnp.dot(jnp.broadcast_to(tiles, (8, ROUTER_LANES)).astype(BF16), upper,
                       preferred_element_type=F32)[0:1]
        off_s[...] = (ends - tiles) * TM_EXPERT
        base_s[...] = jnp.zeros(base_s.shape, F32)

        def pick(vec, e):
            return jnp.sum(jnp.where(lane1 == e, vec, 0.0), axis=1, keepdims=True)

        end_of = [pick(ends, e) for e in range(N_EXPERTS)]
        n_valid = end_of[N_EXPERTS - 1]

        def segment_of(tile):
            return sum(jnp.where(tile >= end_e, 1.0, 0.0) for end_e in end_of)

        tile_expert = jnp.where(lane1 < n_valid, segment_of(lane1), segment_of(n_valid - 1.0))
        following = sum(jnp.where(tile_expert == e, end_of[e], 0.0) for e in range(N_EXPERTS))
        next_expert = jnp.where(following < n_valid, segment_of(following), -1.0)
        rows = [tile_expert, next_expert, off_s[...] + cnt, jnp.broadcast_to(n_valid, (1, ROUTER_LANES))]
        rows += [jnp.zeros((1, ROUTER_LANES), F32)] * (meta_ref.shape[0] - len(rows))
        meta_ref[...] = jnp.concatenate(rows, axis=0).astype(jnp.int32)

    @pl.when(phase == 1)
    def _():
        before = jnp.dot(tri_s[...], onehot.astype(BF16), preferred_element_type=F32)
        row = before + base_s[...] + off_s[...]
        base_s[...] = base_s[...] + block_counts
        w_ref[...] = jnp.where(lane == 0, w0, jnp.where(lane == 1, w1, 0.0))
        eye = (lax.broadcasted_iota(jnp.int32, (LANES, LANES), 0)
               == lax.broadcasted_iota(jnp.int32, (LANES, LANES), 1))
        for j, e in enumerate((e0, e1)):
            col = jnp.sum(jnp.where(lane == e, row, 0.0), axis=1, keepdims=True)
            for q in range(tb // PLAN_CHUNK):
                parts = []
                for g in range(PLAN_CHUNK // LANES):
                    t0 = q * PLAN_CHUNK + g * LANES
                    square = jnp.broadcast_to(col[t0:t0 + LANES], (LANES, LANES))
                    parts.append(jnp.sum(jnp.where(eye, square, 0.0), axis=0, keepdims=True))
                pos_ref[j, q] = jnp.concatenate(parts, axis=1).astype(jnp.int32)


def route_plan_call(logits, n_tiles):
    n = logits.shape[0]
    tb = PLAN_BLOCK
    chunks = tb // PLAN_CHUNK
    assert n_tiles <= ROUTER_LANES and N_EXPERTS <= ROUTER_LANES
    return pl.pallas_call(
        _route_plan_kernel,
        grid=(2, n // tb),
        in_specs=[pl.BlockSpec((tb, ROUTER_LANES), lambda p, b: (b, 0))],
        out_specs=[pl.BlockSpec((tb, ROUTER_LANES), lambda p, b: (p * b, 0)),
                   pl.BlockSpec((2, chunks, 1, PLAN_CHUNK), lambda p, b: (0, p * b, 0, 0)),
                   pl.BlockSpec((8, ROUTER_LANES), lambda p, b: (0, 0))],
        out_shape=[jax.ShapeDtypeStruct((n, ROUTER_LANES), F32),
                   jax.ShapeDtypeStruct((2, n // PLAN_CHUNK, 1, PLAN_CHUNK), jnp.int32),
                   jax.ShapeDtypeStruct((8, ROUTER_LANES), jnp.int32)],
        scratch_shapes=[pltpu.VMEM((tb, tb), BF16), pltpu.VMEM((1, ROUTER_LANES), F32),
                        pltpu.VMEM((1, ROUTER_LANES), F32), pltpu.VMEM((1, ROUTER_LANES), F32)],
        compiler_params=_cparams(("arbitrary", "arbitrary"), 32),
        name="route_plan",
    )(logits)


DISPATCH_SLOTS = 3


def _dispatch_kernel(pad_ref, nv_ref, pos0_ref, pos1_ref, src_hbm, dst_ref, zero_buf, ring, sem, in_sems, row_sems,
                     *, chunk, n_tiles):
    i = pl.program_id(0)
    steps = pl.num_programs(0)
    slot = i % DISPATCH_SLOTS

    def load(block, into):
        return pltpu.make_async_copy(_rows(src_hbm, block * chunk, chunk), ring.at[into], in_sems.at[into])

    def wait_rows(of_slot):
        for j in range(2):
            pltpu.make_async_copy(ring.at[of_slot], _rows(dst_ref, 0, chunk), row_sems.at[of_slot]).wait()

    @pl.when(i == 0)
    def _():
        load(0, 0).start()
        zero_buf[...] = jnp.zeros(zero_buf.shape, zero_buf.dtype)

        def fill(e):
            return pltpu.make_async_copy(zero_buf, _rows(dst_ref, pad_ref[e], TM_EXPERT), sem)

        for e in range(N_EXPERTS):
            fill(e).start()
        for e in range(N_EXPERTS):
            fill(e).wait()

        def fill_tile(i):
            return pltpu.make_async_copy(zero_buf, _rows(dst_ref, i * TM_EXPERT, TM_EXPERT), sem)

        def start_tile(i, c):
            fill_tile(i).start()
            return c

        def wait_tile(i, c):
            fill_tile(i).wait()
            return c

        lax.fori_loop(nv_ref[0], n_tiles, start_tile, 0)
        lax.fori_loop(nv_ref[0], n_tiles, wait_tile, 0)

    nxt = (i + 1) % DISPATCH_SLOTS

    @pl.when(i >= DISPATCH_SLOTS - 1)
    def _():
        wait_rows(nxt)

    @pl.when(i + 1 < steps)
    def _():
        load(i + 1, nxt).start()

    load(i, slot).wait()
    src_ref = ring.at[slot]

    def start(t, c):
        for j, pos_ref in enumerate((pos0_ref, pos1_ref)):
            pltpu.make_async_copy(_rows(src_ref, t), _rows(dst_ref, pos_ref[0, 0, t]),
                                  row_sems.at[slot]).start(priority=j)
        return c

    lax.fori_loop(0, chunk, start, 0, unroll=ROW_DMA_UNROLL)

    @pl.when(i == steps - 1)
    def _():
        for back in range(DISPATCH_SLOTS - 1):
            @pl.when(i - back >= 0)
            def _():
                wait_rows((i - back) % DISPATCH_SLOTS)


def dispatch_call(hn, pos, pad_start, n_valid_tiles, n_tiles):
    n = hn.shape[0] // ROW_TILE
    chunk = PLAN_CHUNK
    grid_spec = pltpu.PrefetchScalarGridSpec(
        num_scalar_prefetch=2,
        grid=(n // chunk,),
        in_specs=[pl.BlockSpec((None, 1, 1, chunk), lambda i, pad, nv: (0, i, 0, 0), memory_space=pltpu.SMEM),
                  pl.BlockSpec((None, 1, 1, chunk), lambda i, pad, nv: (1, i, 0, 0), memory_space=pltpu.SMEM),
                  pl.BlockSpec(memory_space=pl.ANY)],
        out_specs=pl.BlockSpec(memory_space=pl.ANY),
        scratch_shapes=[pltpu.VMEM((TM_EXPERT * ROW_TILE, LANES), hn.dtype),
                        pltpu.VMEM((DISPATCH_SLOTS, chunk * ROW_TILE, LANES), hn.dtype),
                        pltpu.SemaphoreType.DMA(()), pltpu.SemaphoreType.DMA((DISPATCH_SLOTS,)),
                        pltpu.SemaphoreType.DMA((DISPATCH_SLOTS,))],
    )
    return pl.pallas_call(
        functools.partial(_dispatch_kernel, chunk=chunk, n_tiles=n_tiles),
        grid_spec=grid_spec,
        out_shape=jax.ShapeDtypeStruct((n_tiles * TM_EXPERT * ROW_TILE, LANES), hn.dtype),
        compiler_params=_cparams(("arbitrary",), 32),
        name="dispatch_rows",
    )(pad_start, n_valid_tiles, pos, pos, hn)


def _expert_kernel(te_ref, nv_ref, nx_ref, x_ref, wg_hbm, wu_hbm, wd_hbm, y_ref,
                   wg_f, wu_f, wd_f, wg_s, wu_s, wd_s, seg_ref, sems, *, layer):
    i = pl.program_id(0)
    valid = i < nv_ref[0]
    expert = te_ref[i]
    changed = jnp.logical_or(i == 0, expert != te_ref[jnp.maximum(i - 1, 0)])

    def fetch(e, slot):
        return [pltpu.make_async_copy(hbm.at[layer, e], buf.at[slot], sems.at[slot])
                for hbm, buf in ((wg_hbm, wg_f), (wu_hbm, wu_f), (wd_hbm, wd_f))]

    def swiglu_tile():
        lo, hi = _unpack_halves(_load_row_tiles(x_ref))
        x = jnp.concatenate([lo.astype(BF16), hi.astype(BF16)], axis=1)
        a = jnp.dot(x, wg_s[...], preferred_element_type=F32)
        b = jnp.dot(x, wu_s[...], preferred_element_type=F32)
        hmid = (a * _sigmoid(a) * b).astype(BF16)
        _store_row_tiles(y_ref, _pack_halves(jnp.dot(hmid, wd_s[...], preferred_element_type=F32)))

    @pl.when(i == 0)
    def _():
        seg_ref[0] = 0
        for cp in fetch(expert, 0):
            cp.start()

    @pl.when(jnp.logical_and(valid, changed))
    def _():
        slot = seg_ref[0] % 2
        for cp in fetch(expert, slot):
            cp.wait()
        nxt = nx_ref[i]

        @pl.when(nxt >= 0)
        def _():
            for cp in fetch(nxt, 1 - slot):
                cp.start()

        seg_ref[0] = seg_ref[0] + 1
        wg_s[...] = wg_f[slot].astype(BF16)
        wu_s[...] = wu_f[slot].astype(BF16)
        wd_s[...] = wd_f[slot].astype(BF16)
        swiglu_tile()

    @pl.when(jnp.logical_and(valid, jnp.logical_not(changed)))
    def _():
        swiglu_tile()

    @pl.when(jnp.logical_not(valid))
    def _():
        y_ref[...] = jnp.zeros(y_ref.shape, y_ref.dtype)


def expert_call(xs, tile_expert, n_valid_tiles, next_expert, w_gate, w_up, w_down, layer):
    p = xs.shape[0] // ROW_TILE
    d = 2 * ROW_TILE * LANES
    tm = TM_EXPERT
    f = w_gate.shape[-1]
    grid_spec = pltpu.PrefetchScalarGridSpec(
        num_scalar_prefetch=3,
        grid=(p // tm,),
        in_specs=[pl.BlockSpec((tm * ROW_TILE, LANES), lambda i, te, nv, nx: (jnp.minimum(i, nv[0] - 1), 0)),
                  pl.BlockSpec(memory_space=pl.ANY),
                  pl.BlockSpec(memory_space=pl.ANY),
                  pl.BlockSpec(memory_space=pl.ANY)],
        out_specs=pl.BlockSpec((tm * ROW_TILE, LANES), lambda i, te, nv, nx: (i, 0)),
        scratch_shapes=[pltpu.VMEM((2, d, f), F32), pltpu.VMEM((2, d, f), F32), pltpu.VMEM((2, f, d), F32),
                        pltpu.VMEM((d, f), BF16), pltpu.VMEM((d, f), BF16), pltpu.VMEM((f, d), BF16),
                        pltpu.SMEM((1,), jnp.int32), pltpu.SemaphoreType.DMA((2,))],
    )
    return pl.pallas_call(
        functools.partial(_expert_kernel, layer=layer),
        grid_spec=grid_spec,
        out_shape=jax.ShapeDtypeStruct(xs.shape, jnp.uint32),
        compiler_params=_cparams(("arbitrary",), 58),
        name="expert_swiglu",
    )(tile_expert, n_valid_tiles, next_expert, xs, w_gate, w_up, w_down)


def _combine_kernel(pos0_ref, pos1_ref, nxt0_ref, nxt1_ref, ys_ref, h_ref, w_ref, gain_ref, *rest, chunk):
    *out_refs, buf, sems = rest
    xn_ref, h2_ref = out_refs if len(out_refs) == 2 else (out_refs[0], None)
    i = pl.program_id(0)
    slot = i % 2

    def gather(p0_ref, p1_ref, into):
        def start(t, c):
            for j, pos_ref in enumerate((p0_ref, p1_ref)):
                pltpu.make_async_copy(_rows(ys_ref, pos_ref[0, 0, t]), _rows(buf.at[into, j], t),
                                      sems.at[into]).start(priority=j)
            return c

        lax.fori_loop(0, chunk, start, 0, unroll=ROW_DMA_UNROLL)

    @pl.when(i == 0)
    def _():
        gather(pos0_ref, pos1_ref, 0)

    @pl.when(i + 1 < pl.num_programs(0))
    def _():
        gather(nxt0_ref, nxt1_ref, 1 - slot)

    for j in range(2):
        pltpu.make_async_copy(_rows(ys_ref, 0, chunk), buf.at[slot, j], sems.at[slot]).wait()
    w = w_ref[...]
    lo0, hi0 = _unpack_halves(_load_row_tiles(buf.at[slot, 0]))
    lo1, hi1 = _unpack_halves(_load_row_tiles(buf.at[slot, 1]))
    y = jnp.concatenate([w[:, 0:1] * lo0 + w[:, 1:2] * lo1, w[:, 0:1] * hi0 + w[:, 1:2] * hi1], axis=1)
    h2 = h_ref[...] + y
    if h2_ref is not None:
        h2_ref[...] = h2
    xn_ref[...] = _rms(h2, gain_ref[...]).astype(xn_ref.dtype)


def combine_call(ys, pos, h1, weights, next_gain, xn_dtype, keep_residual, chunk=512):
    n, d = h1.shape
    row_spec = pl.BlockSpec((chunk, d), lambda c: (c, 0))
    out_specs = [row_spec, row_spec] if keep_residual else [row_spec]
    out_shape = [jax.ShapeDtypeStruct((n, d), xn_dtype)] + ([jax.ShapeDtypeStruct((n, d), F32)] if keep_residual else [])
    per_row = PLAN_CHUNK // chunk
    steps = n // chunk

    def pos_spec(j, ahead):
        def index(c):
            c = jnp.minimum(c + ahead, steps - 1)
            return (j, c // per_row, 0, c % per_row)
        return pl.BlockSpec((None, 1, 1, chunk), index, memory_space=pltpu.SMEM)

    outs = pl.pallas_call(
        functools.partial(_combine_kernel, chunk=chunk),
        grid=(steps,),
        in_specs=[pos_spec(0, 0), pos_spec(1, 0), pos_spec(0, 1), pos_spec(1, 1),
                  pl.BlockSpec(memory_space=pl.ANY),
                  row_spec,
                  pl.BlockSpec((chunk, ROUTER_LANES), lambda c: (c, 0)),
                  pl.BlockSpec((1, d), lambda c: (0, 0))],
        out_specs=out_specs,
        out_shape=out_shape,
        scratch_shapes=[pltpu.VMEM((2, 2, chunk * ROW_TILE, LANES), jnp.uint32), pltpu.SemaphoreType.DMA((2,))],
        compiler_params=_cparams(("arbitrary",), 56),
        name="combine_rows",
    )(pos, pos, pos, pos, ys, h1, weights, next_gain.reshape(1, d))
    return (outs[0], outs[1]) if keep_residual else (outs[0], None)


def kernel(x, rel_bias_table, norm_mix_gain, w_in, w_merge_gate, b_merge_gate, pool_mix, pool_scale, w_up_pool, lambda_q1, lambda_k1, lambda_q2, lambda_k2, subln_gain, w_up_attn, w_out, norm_ffn_gain, w_router_group, b_router_group, w_router_expert, b_router_expert, w_expert_gate, w_expert_up, w_expert_down, final_norm_gain):
    batch, seq, d = x.shape
    depth = w_in.shape[0]
    n = batch * seq
    n_tiles = (2 * n + N_EXPERTS * (TM_EXPERT - 1)) // TM_EXPERT + 1

    bias_tiles = bias_tiles_call(rel_bias_table)
    h = x.reshape(n, d)
    xn = None
    q_scale = LOG2E * DIFF_HEAD_DIM ** -0.5

    for l in range(depth):
        first = l == 0
        xn_first, mixed = pool_branch_call(h if first else xn, norm_mix_gain[l], w_in, l, pool_mix[l].astype(BF16),
                                           pool_scale[l], seq, normalize=first)
        xn = xn_first if first else xn
        qkv = proj_call(xn, w_in, l, 1, 3, ATTN_WIDTH, BF16, first_block_scale=q_scale, name="in_proj_qkv")
        lambda_init = 0.8 - 0.6 * math.exp(-0.3 * l)
        lam_params = jnp.stack([lambda_q1[l], lambda_k1[l], lambda_q2[l], lambda_k2[l]])
        attn = attn_call(qkv.reshape(batch, seq, 3 * ATTN_WIDTH), bias_tiles, lam_params, subln_gain[l],
                         lambda_init, batch, seq).reshape(n, ATTN_WIDTH)
        z = merge_call(xn, mixed, attn, w_merge_gate, b_merge_gate.reshape(depth, 1, -1), w_up_pool, w_up_attn, l)

        w_router = jnp.concatenate(
            [w_router_group[l], jnp.transpose(w_router_expert[l], (1, 0, 2)).reshape(d, N_EXPERTS),
             jnp.zeros((d, ROUTER_LANES - N_GROUPS - N_EXPERTS), F32)], axis=1).astype(BF16)
        b_router = jnp.concatenate(
            [b_router_group[l], b_router_expert[l].reshape(-1),
             jnp.zeros((ROUTER_LANES - N_GROUPS - N_EXPERTS,), F32)]).reshape(1, ROUTER_LANES)
        h1, hn, logits = outproj_call(z, w_out, l, h, norm_ffn_gain[l], w_router, b_router)

        weights, pos, meta = route_plan_call(logits, n_tiles)
        tile_expert = meta[META_TILE_EXPERT, :n_tiles]
        next_expert = meta[META_NEXT_EXPERT, :n_tiles]
        pad_start = meta[META_PAD_START, :N_EXPERTS]
        n_valid = meta[META_N_VALID, :1]
        xs = dispatch_call(hn, pos, pad_start, n_valid, n_tiles)
        ys = expert_call(xs, tile_expert, n_valid, next_expert, w_expert_gate, w_expert_up, w_expert_down, l)
        last = l == depth - 1
        next_gain = final_norm_gain if last else norm_mix_gain[l + 1]
        xn, h = combine_call(ys, pos, h1, weights, next_gain, F32 if last else BF16, keep_residual=not last)

    return xn.reshape(batch, seq, d)
```

```python
import functools
import math

import numpy as np
import jax
import jax.numpy as jnp
from jax import lax
from jax.experimental import pallas as pl
from jax.experimental.pallas import tpu as pltpu

F32 = jnp.float32
BF16 = jnp.bfloat16

D_MODEL = 2048
POOL_WIDTH = 1024
POOL_WINDOWS = (2, 4, 8, 16)
POOL_GROUP_DIM = 256
POOL_HALO = 16
DIFF_HEADS = 8
DIFF_HEAD_DIM = 64
DIFF_V_DIM = 128
ATTN_WIDTH = 1024
REL_BUCKETS = 32
REL_MAX_DISTANCE = 128
N_GROUPS = 4
EXPERTS_PER_GROUP = 8
N_EXPERTS = 32
D_EXPERT = 512
EPS = 1e-6
NEG_INF = -1e30
LOG2E = 1.4426950408889634

ROUTER_LANES = 128
TQ = 256
TQ_BLOCK = 256
TK = 256
KV_SUPER = 2
V_ROWS = DIFF_V_DIM + 16
HEADS_PER_STEP = 8
TM_EXPERT = 256
LANES = 128
ROW_DMA_UNROLL = 8
MIB = 1024 * 1024


def _cparams(sem, vmem_mib):
    return pltpu.CompilerParams(dimension_semantics=sem, vmem_limit_bytes=vmem_mib * MIB)


def _rms(xf, gain):
    ms = jnp.mean(xf * xf, axis=-1, keepdims=True)
    return xf * lax.rsqrt(ms + EPS) * gain


def _proj_kernel(x_ref, w_ref, o_ref, w_s, *, first_block_scale):
    @pl.when(pl.program_id(1) == 0)
    def _():
        w_s[...] = w_ref[...].astype(BF16)

    acc = jnp.dot(x_ref[...], w_s[...], preferred_element_type=F32)
    if first_block_scale is not None:
        acc = acc * jnp.where(pl.program_id(0) == 0, first_block_scale, 1.0).astype(F32)
    o_ref[...] = acc.astype(o_ref.dtype)


def proj_call(x, w, layer, col_block0, n_col_blocks, tn, out_dtype, first_block_scale=None, tm=1024, name="proj"):
    n, k = x.shape
    return pl.pallas_call(
        functools.partial(_proj_kernel, first_block_scale=first_block_scale),
        grid=(n_col_blocks, n // tm),
        in_specs=[pl.BlockSpec((tm, k), lambda j, m: (m, 0)),
                  pl.BlockSpec((None, k, tn), lambda j, m: (layer, 0, col_block0 + j))],
        out_specs=pl.BlockSpec((tm, tn), lambda j, m: (m, j)),
        out_shape=jax.ShapeDtypeStruct((n, n_col_blocks * tn), out_dtype),
        scratch_shapes=[pltpu.VMEM((k, tn), BF16)],
        compiler_params=_cparams(("arbitrary", "arbitrary"), 48),
        name=name,
    )(x, w)


def _first_pool_kernel(x_ref, gain_ref, w_ref, mix_ref, scale_ref, xn_ref, o_ref, w_s, carry, *, tm, seq):
    m = pl.program_id(0)

    @pl.when(m == 0)
    def _():
        w_s[...] = w_ref[...].astype(BF16)

    xn = _rms(x_ref[...], gain_ref[...]).astype(BF16)
    xn_ref[...] = xn
    o_ref[...] = _pool_tile(xn, w_s, carry, (m * tm) % seq, mix_ref, scale_ref).astype(o_ref.dtype)


def _pool_tile(xn, w_s, carry, row0, mix_ref, scale_ref):
    tm = xn.shape[0]
    cur = jnp.dot(xn, w_s[...], preferred_element_type=F32)
    prev = jnp.where(row0 == 0, 0.0, carry[...])
    carry[...] = cur[tm - POOL_HALO:]
    pos = row0 + lax.broadcasted_iota(jnp.int32, (tm, 1), 0)
    outs = []
    for g, w in enumerate(POOL_WINDOWS):
        sl = slice(g * POOL_GROUP_DIM, (g + 1) * POOL_GROUP_DIM)
        x = jnp.concatenate([prev[:, sl], cur[:, sl]], axis=0)
        s, d = x, 1
        while d < w:
            s = s[:-d] + s[d:]
            d *= 2
        start = POOL_HALO - w + 1
        wsum = s[start:start + tm]
        count = jnp.minimum(pos + 1, w).astype(F32)
        pooled = wsum / count - cur[:, sl]
        mixed = jnp.dot(pooled.astype(BF16), mix_ref[g], preferred_element_type=F32)
        outs.append(mixed * scale_ref[:, sl])
    return jnp.concatenate(outs, axis=1)


def first_pool_call(x, gain, w_in, mix_bf16, scale, seq, tm=512):
    n, d = x.shape
    c = POOL_WIDTH
    row_spec = pl.BlockSpec((tm, c), lambda m: (m, 0))
    x_spec = pl.BlockSpec((tm, d), lambda m: (m, 0))
    return pl.pallas_call(
        functools.partial(_first_pool_kernel, tm=tm, seq=seq),
        grid=(n // tm,),
        in_specs=[x_spec,
                  pl.BlockSpec((1, d), lambda m: (0, 0)),
                  pl.BlockSpec((None, d, c), lambda m: (0, 0, 0), pipeline_mode=pl.Buffered(1)),
                  pl.BlockSpec(mix_bf16.shape, lambda m: (0, 0, 0)),
                  pl.BlockSpec((1, c), lambda m: (0, 0))],
        out_specs=[x_spec, row_spec],
        out_shape=[jax.ShapeDtypeStruct((n, d), BF16), jax.ShapeDtypeStruct((n, c), BF16)],
        scratch_shapes=[pltpu.VMEM((d, c), BF16), pltpu.VMEM((POOL_HALO, c), F32)],
        compiler_params=_cparams(("arbitrary",), 48),
        name="first_norm_pool",
    )(x, gain.reshape(1, d), w_in, mix_bf16, scale.reshape(1, c))


N_BIAS_TILES = 4


def _bucket_tiles():
    kk = np.arange(TK)[:, None]
    qq = np.arange(TQ_BLOCK)[None, :]
    tiles = []
    for rel in (0, 1, 2, -1):
        n = rel * TK + qq - kk
        max_exact = REL_BUCKETS // 2
        nf = np.maximum(n, 1).astype(np.float64)
        large = max_exact + (np.log(nf / max_exact) / math.log(REL_MAX_DISTANCE / max_exact)
                             * (REL_BUCKETS - max_exact)).astype(np.int64)
        large = np.minimum(large, REL_BUCKETS - 1)
        bucket = np.where(n < max_exact, n, large)
        tiles.append(np.where(n < 0, -1, bucket))
    return np.stack(tiles).astype(np.int32)


def _bias_kernel(table_ref, bucket_ref, o_ref):
    h = pl.program_id(0)
    bucket = bucket_ref[...]
    acc = jnp.full(bucket.shape, NEG_INF, F32)
    for b in range(REL_BUCKETS):
        acc = jnp.where(bucket == b, table_ref[b * DIFF_HEADS + h] * LOG2E, acc)
    o_ref[...] = acc


def bias_tiles_call(rel_table):
    bucket = jnp.asarray(_bucket_tiles())
    return pl.pallas_call(
        _bias_kernel,
        grid=(DIFF_HEADS,),
        in_specs=[pl.BlockSpec(memory_space=pltpu.SMEM),
                  pl.BlockSpec(bucket.shape, lambda h: (0, 0, 0))],
        out_specs=pl.BlockSpec((None,) + bucket.shape, lambda h: (h, 0, 0, 0)),
        out_shape=jax.ShapeDtypeStruct((DIFF_HEADS,) + bucket.shape, F32),
        compiler_params=_cparams(("parallel",), 32),
        name="rel_bias_tiles",
    )(rel_table.reshape(-1), bucket)


def _attn_kernel(q_ref, k_ref, v_ref, bias_ref, lam_ref, gain_ref, o_ref, vt_ref, s_ref, acc_ref,
                 *, lambda_init, n_super):
    qi = pl.program_id(2)
    tks = KV_SUPER * TK
    heads = range(HEADS_PER_STEP)

    def block_diag(ref, hh):
        q = ref[:, hh * DIFF_V_DIM:(hh + 1) * DIFF_V_DIM]
        lane = lax.broadcasted_iota(jnp.int32, q.shape, 1)
        zero = jnp.zeros_like(q)
        return jnp.concatenate([jnp.where(lane < DIFF_HEAD_DIM, q, zero),
                                jnp.where(lane >= DIFF_HEAD_DIM, q, zero)], axis=0)

    def scores(t, hh, qd, tile):
        kb = k_ref[pl.ds(pl.multiple_of(t * tks, tks), tks), hh * DIFF_V_DIM:(hh + 1) * DIFF_V_DIM]
        s = lax.dot_general(kb, qd, (((1,), (1,)), ((), ())), preferred_element_type=F32)
        parts = []
        for u in range(KV_SUPER):
            tiles = []
            for c in range(TQ // TQ_BLOCK):
                rel = tile * (TQ // TQ_BLOCK) + c - (t * KV_SUPER + u)
                tiles.append(bias_ref[hh, jnp.where(rel < 0, N_BIAS_TILES - 1, jnp.minimum(rel, 2))])
            parts.append(s[u * TK:(u + 1) * TK] + jnp.concatenate(tiles + tiles, axis=1))
        s = jnp.concatenate(parts, axis=0)
        s_ref[hh] = s
        return jnp.max(s, axis=0, keepdims=True)

    @pl.when(qi == 0)
    def _():
        extra = (lax.broadcasted_iota(jnp.int32, (V_ROWS - DIFF_V_DIM, tks), 0) == 0).astype(BF16)
        for hh in heads:
            cols = slice(hh * DIFF_V_DIM, (hh + 1) * DIFF_V_DIM)
            for c in range(n_super):
                vt = v_ref[c * tks:(c + 1) * tks, cols].astype(F32).T.astype(BF16)
                vt_ref[hh, c] = jnp.concatenate([vt, extra], axis=0)

    qds = [block_diag(q_ref, hh) for hh in heads]

    def softmax_step(t, hh, m_prev, m_cur):
        alpha = jnp.exp2(m_prev - m_cur)
        p = jnp.exp2(s_ref[hh] - m_cur)
        pv = jnp.dot(vt_ref[hh, t], p.astype(BF16), preferred_element_type=F32)
        acc_ref[hh] = acc_ref[hh] * alpha + pv

    last = ((qi + 1) * TQ - 1) // tks
    acc_ref[...] = jnp.zeros(acc_ref.shape, F32)
    neg = jnp.full((1, 2 * TQ), NEG_INF, F32)
    init = tuple((neg, jnp.maximum(neg, scores(0, hh, qds[hh], qi))) for hh in heads)

    def body(t, carry):
        out = []
        for hh in heads:
            m_prev, m_cur = carry[hh]
            softmax_step(t, hh, m_prev, m_cur)
            m_next = jnp.maximum(m_cur, scores(t + 1, hh, qds[hh], qi))
            out.append((m_cur, m_next))
        return tuple(out)

    carry = lax.fori_loop(0, last, body, init)

    lam_p = lam_ref[...]
    lam = (jnp.exp(jnp.sum(lam_p[0:1] * lam_p[1:2], axis=1, keepdims=True))
           - jnp.exp(jnp.sum(lam_p[2:3] * lam_p[3:4], axis=1, keepdims=True)) + lambda_init)
    for hh in heads:
        m_prev, m_cur = carry[hh]
        softmax_step(last, hh, m_prev, m_cur)
        acc = acc_ref[hh, :DIFF_V_DIM, :]
        l = acc_ref[hh, DIFF_V_DIM:DIFF_V_DIM + 1, :]
        o = acc[:, :TQ] / l[:, :TQ] - lam * (acc[:, TQ:] / l[:, TQ:])
        ms = jnp.mean(o * o, axis=0, keepdims=True)
        y = o * lax.rsqrt(ms + EPS) * gain_ref[...] * (1.0 - lambda_init)
        o_ref[:, hh * DIFF_V_DIM:(hh + 1) * DIFF_V_DIM] = y.T.astype(o_ref.dtype)


def attn_call(qkv, bias_tiles, lam_params, subln_gain, lambda_init, batch, seq):
    tks = KV_SUPER * TK
    n_super = seq // tks
    hps = HEADS_PER_STEP
    width = hps * DIFF_V_DIM
    groups = DIFF_HEADS // hps
    n_q = seq // TQ
    return pl.pallas_call(
        functools.partial(_attn_kernel, lambda_init=lambda_init, n_super=n_super),
        grid=(batch, groups, n_q),
        in_specs=[pl.BlockSpec((None, TQ, width), lambda b, g, i: (b, i, g)),
                  pl.BlockSpec((None, seq, width), lambda b, g, i: (b, 0, groups + g), pipeline_mode=pl.Buffered(1)),
                  pl.BlockSpec((None, seq, width), lambda b, g, i: (b, 0, 2 * groups + g),
                               pipeline_mode=pl.Buffered(1)),
                  pl.BlockSpec((hps, N_BIAS_TILES, TK, TQ_BLOCK), lambda b, g, i: (g, 0, 0, 0),
                               pipeline_mode=pl.Buffered(1)),
                  pl.BlockSpec((4, DIFF_HEAD_DIM), lambda b, g, i: (0, 0)),
                  pl.BlockSpec((DIFF_V_DIM, 1), lambda b, g, i: (0, 0))],
        out_specs=pl.BlockSpec((None, TQ, width), lambda b, g, i: (b, i, g)),
        out_shape=jax.ShapeDtypeStruct((batch, seq, ATTN_WIDTH), BF16),
        scratch_shapes=[pltpu.VMEM((hps, n_super, V_ROWS, tks), BF16),
                        pltpu.VMEM((hps, tks, 2 * TQ), F32),
                        pltpu.VMEM((hps, V_ROWS, 2 * TQ), F32)],
        compiler_params=_cparams(("parallel", "parallel", "arbitrary"), 56),
        name="diff_attention",
    )(qkv, qkv, qkv, bias_tiles, lam_params, subln_gain.reshape(DIFF_V_DIM, 1))


def _sigmoid(x):
    return 1.0 / (1.0 + jnp.exp(-x))


def _merge_kernel(xn_ref, mixed_ref, attn_ref, wgp_ref, wga_ref, bgp_ref, bga_ref, wup_ref, wua_ref, z_ref,
                  wgp_s, wga_s, wup_s, wua_s):
    @pl.when(pl.program_id(1) == 0)
    def _():
        wgp_s[...] = wgp_ref[...].astype(BF16)
        wga_s[...] = wga_ref[...].astype(BF16)
        wup_s[...] = wup_ref[...].astype(BF16)
        wua_s[...] = wua_ref[...].astype(BF16)

    xn = xn_ref[...]
    g_pool = _sigmoid(jnp.dot(xn, wgp_s[...], preferred_element_type=F32) + bgp_ref[...])
    g_attn = _sigmoid(jnp.dot(xn, wga_s[...], preferred_element_type=F32) + bga_ref[...])
    y_pool = jnp.dot(mixed_ref[...], wup_s[...], preferred_element_type=F32)
    y_attn = jnp.dot(attn_ref[...], wua_s[...], preferred_element_type=F32)
    z_ref[...] = (g_pool * y_pool + g_attn * y_attn).astype(z_ref.dtype)


def merge_call(xn, mixed, attn, w_gate, b_gate, w_up_pool, w_up_attn, layer, tm=1024, tn=512):
    n, d = xn.shape
    nb = d // tn
    return pl.pallas_call(
        _merge_kernel,
        grid=(nb, n // tm),
        in_specs=[pl.BlockSpec((tm, d), lambda j, m: (m, 0)),
                  pl.BlockSpec((tm, POOL_WIDTH), lambda j, m: (m, 0)),
                  pl.BlockSpec((tm, ATTN_WIDTH), lambda j, m: (m, 0)),
                  pl.BlockSpec((None, d, tn), lambda j, m: (layer, 0, j)),
                  pl.BlockSpec((None, d, tn), lambda j, m: (layer, 0, nb + j)),
                  pl.BlockSpec((None, 1, tn), lambda j, m: (layer, 0, j)),
                  pl.BlockSpec((None, 1, tn), lambda j, m: (layer, 0, nb + j)),
                  pl.BlockSpec((None, POOL_WIDTH, tn), lambda j, m: (layer, 0, j)),
                  pl.BlockSpec((None, ATTN_WIDTH, tn), lambda j, m: (layer, 0, j))],
        out_specs=pl.BlockSpec((tm, tn), lambda j, m: (m, j)),
        out_shape=jax.ShapeDtypeStruct((n, d), BF16),
        scratch_shapes=[pltpu.VMEM((d, tn), BF16), pltpu.VMEM((d, tn), BF16),
                        pltpu.VMEM((POOL_WIDTH, tn), BF16), pltpu.VMEM((ATTN_WIDTH, tn), BF16)],
        compiler_params=_cparams(("arbitrary", "arbitrary"), 56),
        name="gated_merge",
    )(xn, mixed, attn, w_gate, w_gate, b_gate, b_gate, w_up_pool, w_up_attn)


HIGH_HALF = 0xFFFF0000


def _pack_halves(x):
    c = x.shape[1] // 2
    lo = lax.bitcast_convert_type(x[:, :c].astype(BF16).astype(F32), jnp.uint32)
    hi = lax.bitcast_convert_type(x[:, c:].astype(BF16).astype(F32), jnp.uint32)
    return (lo >> 16) | hi


def _unpack_halves(p):
    lo = lax.bitcast_convert_type(p << 16, F32)
    hi = lax.bitcast_convert_type(p & jnp.uint32(HIGH_HALF), F32)
    return lo, hi


ROW_TILE = 8


def _store_row_tiles(ref, packed):
    m = packed.shape[0]
    for s in range(ROW_TILE):
        ref[pl.ds(s, m, stride=ROW_TILE), :] = packed[:, s * LANES:(s + 1) * LANES]


def _load_row_tiles(ref):
    m = ref.shape[0] // ROW_TILE
    return jnp.concatenate([ref[pl.ds(s, m, stride=ROW_TILE), :] for s in range(ROW_TILE)], axis=1)


def _rows(ref, first, count=1):
    return ref.at[pl.ds(pl.multiple_of(first * ROW_TILE, ROW_TILE), count * ROW_TILE)]


def _outproj_kernel(z_ref, w_ref, h_ref, gain_ref, wr_ref, br_ref, h1_ref, hn_ref, logit_ref, w_s):
    @pl.when(pl.program_id(0) == 0)
    def _():
        w_s[...] = w_ref[...].astype(BF16)

    h1 = h_ref[...] + jnp.dot(z_ref[...], w_s[...], preferred_element_type=F32)
    h1_ref[...] = h1
    hn = _rms(h1, gain_ref[...])
    _store_row_tiles(hn_ref, _pack_halves(hn))
    logit_ref[...] = jnp.dot(hn.astype(BF16), wr_ref[...], preferred_element_type=F32) + br_ref[...]


def outproj_call(z, w_out, layer, h, gain, w_router, b_router, tm=256):
    n, d = h.shape
    return pl.pallas_call(
        _outproj_kernel,
        grid=(n // tm,),
        in_specs=[pl.BlockSpec((tm, d), lambda m: (m, 0)),
                  pl.BlockSpec((None, d, d), lambda m: (layer, 0, 0), pipeline_mode=pl.Buffered(1)),
                  pl.BlockSpec((tm, d), lambda m: (m, 0)),
                  pl.BlockSpec((1, d), lambda m: (0, 0)),
                  pl.BlockSpec((d, ROUTER_LANES), lambda m: (0, 0)),
                  pl.BlockSpec((1, ROUTER_LANES), lambda m: (0, 0))],
        out_specs=[pl.BlockSpec((tm, d), lambda m: (m, 0)),
                   pl.BlockSpec((tm * ROW_TILE, LANES), lambda m: (m, 0)),
                   pl.BlockSpec((tm, ROUTER_LANES), lambda m: (m, 0))],
        out_shape=[jax.ShapeDtypeStruct((n, d), F32),
                   jax.ShapeDtypeStruct((n * ROW_TILE, LANES), jnp.uint32),
                   jax.ShapeDtypeStruct((n, ROUTER_LANES), F32)],
        scratch_shapes=[pltpu.VMEM((d, d), BF16)],
        compiler_params=_cparams(("arbitrary",), 56),
        name="out_proj_norm_router",
    )(z, w_out, h, gain.reshape(1, d), w_router, b_router)


def _route(x, lane):
    big = float(ROUTER_LANES)

    def first_argmax(vals):
        top = jnp.max(vals, axis=1, keepdims=True)
        idx = jnp.min(jnp.where(vals == top, lane, big), axis=1, keepdims=True)
        return top, idx

    gmask = lane < N_GROUPS
    g_top, g_sel = first_argmax(jnp.where(gmask, x, -jnp.inf))
    g_weight = 1.0 / jnp.sum(jnp.where(gmask, jnp.exp(x - g_top), 0.0), axis=1, keepdims=True)
    lo = N_GROUPS + EXPERTS_PER_GROUP * g_sel
    e_vals = jnp.where((lane >= lo) & (lane < lo + EXPERTS_PER_GROUP), x, -jnp.inf)
    v1, i1 = first_argmax(e_vals)
    v2, i2 = first_argmax(jnp.where(lane == i1, -jnp.inf, e_vals))
    t = jnp.exp(v2 - v1)
    return i1 - N_GROUPS, i2 - N_GROUPS, g_weight / (1.0 + t), g_weight * t / (1.0 + t)


PLAN_BLOCK = 1024
PLAN_CHUNK = 512
META_TILE_EXPERT, META_NEXT_EXPERT, META_PAD_START, META_N_VALID = 0, 1, 2, 3


def _route_plan_kernel(logit_ref, w_ref, pos_ref, meta_ref, tri_s, cnt_s, base_s, off_s):
    phase = pl.program_id(0)
    blk = pl.program_id(1)
    tb = logit_ref.shape[0]
    lane = lax.broadcasted_iota(jnp.int32, (tb, ROUTER_LANES), 1).astype(F32)
    e0, e1, w0, w1 = _route(logit_ref[...], lane)
    onehot = jnp.where(lane == e0, 1.0, 0.0) + jnp.where(lane == e1, 1.0, 0.0)
    block_counts = jnp.sum(onehot, axis=0, keepdims=True)

    @pl.when(jnp.logical_and(phase == 0, blk == 0))
    def _():
        cnt_s[...] = jnp.zeros(cnt_s.shape, F32)

    @pl.when(phase == 0)
    def _():
        cnt_s[...] = cnt_s[...] + block_counts

    @pl.when(jnp.logical_and(phase == 1, blk == 0))
    def _():
        r = lax.broadcasted_iota(jnp.int32, (tb, tb), 0)
        c = lax.broadcasted_iota(jnp.int32, (tb, tb), 1)
        tri_s[...] = jnp.where(c < r, 1.0, 0.0).astype(BF16)
        lane1 = lane[0:1]
        cnt = cnt_s[...]
        tiles = jnp.floor((cnt + (TM_EXPERT - 1)) * (1.0 / TM_EXPERT))
        ri = lax.broadcasted_iota(jnp.int32, (ROUTER_LANES, ROUTER_LANES), 0)
        ci = lax.broadcasted_iota(jnp.int32, (ROUTER_LANES, ROUTER_LANES), 1)
        upper = jnp.where(ri <= ci, 1.0, 0.0).astype(BF16)
        ends = jnp.dot(jnp.broadcast_to(tiles, (8, ROUTER_LANES)).astype(BF16), upper,
                       preferred_element_type=F32)[0:1]
        off_s[...] = (ends - tiles) * TM_EXPERT
        base_s[...] = jnp.zeros(base_s.shape, F32)

        def pick(vec, e):
            return jnp.sum(jnp.where(lane1 == e, vec, 0.0), axis=1, keepdims=True)

        end_of = [pick(ends, e) for e in range(N_EXPERTS)]
        n_valid = end_of[N_EXPERTS - 1]

        def segment_of(tile):
            return sum(jnp.where(tile >= end_e, 1.0, 0.0) for end_e in end_of)

        tile_expert = jnp.where(lane1 < n_valid, segment_of(lane1), segment_of(n_valid - 1.0))
        following = sum(jnp.where(tile_expert == e, end_of[e], 0.0) for e in range(N_EXPERTS))
        next_expert = jnp.where(following < n_valid, segment_of(following), -1.0)
        rows = [tile_expert, next_expert, off_s[...] + cnt, jnp.broadcast_to(n_valid, (1, ROUTER_LANES))]
        rows += [jnp.zeros((1, ROUTER_LANES), F32)] * (meta_ref.shape[0] - len(rows))
        meta_ref[...] = jnp.concatenate(rows, axis=0).astype(jnp.int32)

    @pl.when(phase == 1)
    def _():
        before = jnp.dot(tri_s[...], onehot.astype(BF16), preferred_element_type=F32)
        row = before + base_s[...] + off_s[...]
        base_s[...] = base_s[...] + block_counts
        w_ref[...] = jnp.where(lane == 0, w0, jnp.where(lane == 1, w1, 0.0))
        eye = (lax.broadcasted_iota(jnp.int32, (LANES, LANES), 0)
               == lax.broadcasted_iota(jnp.int32, (LANES, LANES), 1))
        for j, e in enumerate((e0, e1)):
            col = jnp.sum(jnp.where(lane == e, row, 0.0), axis=1, keepdims=True)
            for q in range(tb // PLAN_CHUNK):
                parts = []
                for g in range(PLAN_CHUNK // LANES):
                    t0 = q * PLAN_CHUNK + g * LANES
                    square = jnp.broadcast_to(col[t0:t0 + LANES], (LANES, LANES))
                    parts.append(jnp.sum(jnp.where(eye, square, 0.0), axis=0, keepdims=True))
                pos_ref[j, q] = jnp.concatenate(parts, axis=1).astype(jnp.int32)


def route_plan_call(logits, n_tiles):
    n = logits.shape[0]
    tb = PLAN_BLOCK
    chunks = tb // PLAN_CHUNK
    assert n_tiles <= ROUTER_LANES and N_EXPERTS <= ROUTER_LANES
    return pl.pallas_call(
        _route_plan_kernel,
        grid=(2, n // tb),
        in_specs=[pl.BlockSpec((tb, ROUTER_LANES), lambda p, b: (b, 0))],
        out_specs=[pl.BlockSpec((tb, ROUTER_LANES), lambda p, b: (p * b, 0)),
                   pl.BlockSpec((2, chunks, 1, PLAN_CHUNK), lambda p, b: (0, p * b, 0, 0)),
                   pl.BlockSpec((8, ROUTER_LANES), lambda p, b: (0, 0))],
        out_shape=[jax.ShapeDtypeStruct((n, ROUTER_LANES), F32),
                   jax.ShapeDtypeStruct((2, n // PLAN_CHUNK, 1, PLAN_CHUNK), jnp.int32),
                   jax.ShapeDtypeStruct((8, ROUTER_LANES), jnp.int32)],
        scratch_shapes=[pltpu.VMEM((tb, tb), BF16), pltpu.VMEM((1, ROUTER_LANES), F32),
                        pltpu.VMEM((1, ROUTER_LANES), F32), pltpu.VMEM((1, ROUTER_LANES), F32)],
        compiler_params=_cparams(("arbitrary", "arbitrary"), 32),
        name="route_plan",
    )(logits)


DISPATCH_SLOTS = 3


def _dispatch_kernel(pad_ref, nv_ref, pos0_ref, pos1_ref, src_hbm, dst_ref, zero_buf, ring, sem, in_sems, row_sems,
                     *, chunk, n_tiles):
    i = pl.program_id(0)
    steps = pl.num_programs(0)
    slot = i % DISPATCH_SLOTS

    def load(block, into):
        return pltpu.make_async_copy(_rows(src_hbm, block * chunk, chunk), ring.at[into], in_sems.at[into])

    def wait_rows(of_slot):
        for j in range(2):
            pltpu.make_async_copy(ring.at[of_slot], _rows(dst_ref, 0, chunk), row_sems.at[of_slot]).wait()

    @pl.when(i == 0)
    def _():
        load(0, 0).start()
        zero_buf[...] = jnp.zeros(zero_buf.shape, zero_buf.dtype)

        def fill(e):
            return pltpu.make_async_copy(zero_buf, _rows(dst_ref, pad_ref[e], TM_EXPERT), sem)

        for e in range(N_EXPERTS):
            fill(e).start()
        for e in range(N_EXPERTS):
            fill(e).wait()

        def fill_tile(i):
            return pltpu.make_async_copy(zero_buf, _rows(dst_ref, i * TM_EXPERT, TM_EXPERT), sem)

        def start_tile(i, c):
            fill_tile(i).start()
            return c

        def wait_tile(i, c):
            fill_tile(i).wait()
            return c

        lax.fori_loop(nv_ref[0], n_tiles, start_tile, 0)
        lax.fori_loop(nv_ref[0], n_tiles, wait_tile, 0)

    nxt = (i + 1) % DISPATCH_SLOTS

    @pl.when(i >= DISPATCH_SLOTS - 1)
    def _():
        wait_rows(nxt)

    @pl.when(i + 1 < steps)
    def _():
        load(i + 1, nxt).start()

    load(i, slot).wait()
    src_ref = ring.at[slot]

    def start(t, c):
        for j, pos_ref in enumerate((pos0_ref, pos1_ref)):
            pltpu.make_async_copy(_rows(src_ref, t), _rows(dst_ref, pos_ref[0, 0, t]),
                                  row_sems.at[slot]).start(priority=j)
        return c

    lax.fori_loop(0, chunk, start, 0, unroll=ROW_DMA_UNROLL)

    @pl.when(i == steps - 1)
    def _():
        for back in range(DISPATCH_SLOTS - 1):
            @pl.when(i - back >= 0)
            def _():
                wait_rows((i - back) % DISPATCH_SLOTS)


def dispatch_call(hn, pos, pad_start, n_valid_tiles, n_tiles):
    n = hn.shape[0] // ROW_TILE
    chunk = PLAN_CHUNK
    grid_spec = pltpu.PrefetchScalarGridSpec(
        num_scalar_prefetch=2,
        grid=(n // chunk,),
        in_specs=[pl.BlockSpec((None, 1, 1, chunk), lambda i, pad, nv: (0, i, 0, 0), memory_space=pltpu.SMEM),
                  pl.BlockSpec((None, 1, 1, chunk), lambda i, pad, nv: (1, i, 0, 0), memory_space=pltpu.SMEM),
                  pl.BlockSpec(memory_space=pl.ANY)],
        out_specs=pl.BlockSpec(memory_space=pl.ANY),
        scratch_shapes=[pltpu.VMEM((TM_EXPERT * ROW_TILE, LANES), hn.dtype),
                        pltpu.VMEM((DISPATCH_SLOTS, chunk * ROW_TILE, LANES), hn.dtype),
                        pltpu.SemaphoreType.DMA(()), pltpu.SemaphoreType.DMA((DISPATCH_SLOTS,)),
                        pltpu.SemaphoreType.DMA((DISPATCH_SLOTS,))],
    )
    return pl.pallas_call(
        functools.partial(_dispatch_kernel, chunk=chunk, n_tiles=n_tiles),
        grid_spec=grid_spec,
        out_shape=jax.ShapeDtypeStruct((n_tiles * TM_EXPERT * ROW_TILE, LANES), hn.dtype),
        compiler_params=_cparams(("arbitrary",), 32),
        name="dispatch_rows",
    )(pad_start, n_valid_tiles, pos, pos, hn)


def _expert_kernel(te_ref, nv_ref, nx_ref, x_ref, wg_hbm, wu_hbm, wd_hbm, y_ref,
                   wg_f, wu_f, wd_f, wg_s, wu_s, wd_s, seg_ref, sems, *, layer):
    i = pl.program_id(0)
    valid = i < nv_ref[0]
    expert = te_ref[i]
    changed = jnp.logical_or(i == 0, expert != te_ref[jnp.maximum(i - 1, 0)])

    def fetch(e, slot):
        return [pltpu.make_async_copy(hbm.at[layer, e], buf.at[slot], sems.at[slot])
                for hbm, buf in ((wg_hbm, wg_f), (wu_hbm, wu_f), (wd_hbm, wd_f))]

    def swiglu_tile():
        lo, hi = _unpack_halves(_load_row_tiles(x_ref))
        x = jnp.concatenate([lo.astype(BF16), hi.astype(BF16)], axis=1)
        a = jnp.dot(x, wg_s[...], preferred_element_type=F32)
        b = jnp.dot(x, wu_s[...], preferred_element_type=F32)
        hmid = (a * _sigmoid(a) * b).astype(BF16)
        _store_row_tiles(y_ref, _pack_halves(jnp.dot(hmid, wd_s[...], preferred_element_type=F32)))

    @pl.when(i == 0)
    def _():
        seg_ref[0] = 0
        for cp in fetch(expert, 0):
            cp.start()

    @pl.when(jnp.logical_and(valid, changed))
    def _():
        slot = seg_ref[0] % 2
        for cp in fetch(expert, slot):
            cp.wait()
        nxt = nx_ref[i]

        @pl.when(nxt >= 0)
        def _():
            for cp in fetch(nxt, 1 - slot):
                cp.start()

        seg_ref[0] = seg_ref[0] + 1
        wg_s[...] = wg_f[slot].astype(BF16)
        wu_s[...] = wu_f[slot].astype(BF16)
        wd_s[...] = wd_f[slot].astype(BF16)
        swiglu_tile()

    @pl.when(jnp.logical_and(valid, jnp.logical_not(changed)))
    def _():
        swiglu_tile()

    @pl.when(jnp.logical_not(valid))
    def _():
        y_ref[...] = jnp.zeros(y_ref.shape, y_ref.dtype)


def expert_call(xs, tile_expert, n_valid_tiles, next_expert, w_gate, w_up, w_down, layer):
    p = xs.shape[0] // ROW_TILE
    d = 2 * ROW_TILE * LANES
    tm = TM_EXPERT
    f = w_gate.shape[-1]
    grid_spec = pltpu.PrefetchScalarGridSpec(
        num_scalar_prefetch=3,
        grid=(p // tm,),
        in_specs=[pl.BlockSpec((tm * ROW_TILE, LANES), lambda i, te, nv, nx: (jnp.minimum(i, nv[0] - 1), 0)),
                  pl.BlockSpec(memory_space=pl.ANY),
                  pl.BlockSpec(memory_space=pl.ANY),
                  pl.BlockSpec(memory_space=pl.ANY)],
        out_specs=pl.BlockSpec((tm * ROW_TILE, LANES), lambda i, te, nv, nx: (i, 0)),
        scratch_shapes=[pltpu.VMEM((2, d, f), F32), pltpu.VMEM((2, d, f), F32), pltpu.VMEM((2, f, d), F32),
                        pltpu.VMEM((d, f), BF16), pltpu.VMEM((d, f), BF16), pltpu.VMEM((f, d), BF16),
                        pltpu.SMEM((1,), jnp.int32), pltpu.SemaphoreType.DMA((2,))],
    )
    return pl.pallas_call(
        functools.partial(_expert_kernel, layer=layer),
        grid_spec=grid_spec,
        out_shape=jax.ShapeDtypeStruct(xs.shape, jnp.uint32),
        compiler_params=_cparams(("arbitrary",), 58),
        name="expert_swiglu",
    )(tile_expert, n_valid_tiles, next_expert, xs, w_gate, w_up, w_down)


def _combine_kernel(pos0_ref, pos1_ref, nxt0_ref, nxt1_ref, ys_ref, h_ref, w_ref, gain_ref, *rest, chunk, seq):
    if seq is None:
        xn_ref, buf, sems = rest
        h2_ref = None
    else:
        w_in_ref, mix_ref, scale_ref, xn_ref, h2_ref, mixed_ref, buf, sems, w_s, carry = rest
    i = pl.program_id(0)
    slot = i % 2

    def gather(p0_ref, p1_ref, into):
        def start(t, c):
            for j, pos_ref in enumerate((p0_ref, p1_ref)):
                pltpu.make_async_copy(_rows(ys_ref, pos_ref[0, 0, t]), _rows(buf.at[into, j], t),
                                      sems.at[into]).start(priority=j)
            return c

        lax.fori_loop(0, chunk, start, 0, unroll=ROW_DMA_UNROLL)

    @pl.when(i == 0)
    def _():
        gather(pos0_ref, pos1_ref, 0)

    @pl.when(i + 1 < pl.num_programs(0))
    def _():
        gather(nxt0_ref, nxt1_ref, 1 - slot)

    for j in range(2):
        pltpu.make_async_copy(_rows(ys_ref, 0, chunk), buf.at[slot, j], sems.at[slot]).wait()
    w = w_ref[...]
    lo0, hi0 = _unpack_halves(_load_row_tiles(buf.at[slot, 0]))
    lo1, hi1 = _unpack_halves(_load_row_tiles(buf.at[slot, 1]))
    y = jnp.concatenate([w[:, 0:1] * lo0 + w[:, 1:2] * lo1, w[:, 0:1] * hi0 + w[:, 1:2] * hi1], axis=1)
    h2 = h_ref[...] + y
    xn = _rms(h2, gain_ref[...]).astype(xn_ref.dtype)
    xn_ref[...] = xn
    if seq is not None:
        h2_ref[...] = h2

        @pl.when(i == 0)
        def _():
            w_s[...] = w_in_ref[...].astype(BF16)

        mixed_ref[...] = _pool_tile(xn, w_s, carry, (i * chunk) % seq, mix_ref, scale_ref).astype(mixed_ref.dtype)


def combine_call(ys, pos, h1, weights, next_gain, next_pool=None, chunk=256):
    n, d = h1.shape
    row_spec = pl.BlockSpec((chunk, d), lambda c: (c, 0))
    per_row = PLAN_CHUNK // chunk
    steps = n // chunk
    in_specs_pool, args_pool, scratch_pool = [], [], []
    if next_pool is None:
        seq = None
        out_specs = [row_spec]
        out_shape = [jax.ShapeDtypeStruct((n, d), F32)]
    else:
        w_in, layer, mix_bf16, scale, seq = next_pool
        c = POOL_WIDTH
        pool_spec = pl.BlockSpec((chunk, c), lambda s: (s, 0))
        in_specs_pool = [pl.BlockSpec((None, d, c), lambda s: (layer, 0, 0), pipeline_mode=pl.Buffered(1)),
                         pl.BlockSpec(mix_bf16.shape, lambda s: (0, 0, 0)),
                         pl.BlockSpec((1, c), lambda s: (0, 0))]
        args_pool = [w_in, mix_bf16, scale.reshape(1, c)]
        scratch_pool = [pltpu.VMEM((d, c), BF16), pltpu.VMEM((POOL_HALO, c), F32)]
        out_specs = [row_spec, row_spec, pool_spec]
        out_shape = [jax.ShapeDtypeStruct((n, d), BF16), jax.ShapeDtypeStruct((n, d), F32),
                     jax.ShapeDtypeStruct((n, c), BF16)]

    def pos_spec(j, ahead):
        def index(c):
            c = jnp.minimum(c + ahead, steps - 1)
            return (j, c // per_row, 0, c % per_row)
        return pl.BlockSpec((None, 1, 1, chunk), index, memory_space=pltpu.SMEM)

    outs = pl.pallas_call(
        functools.partial(_combine_kernel, chunk=chunk, seq=seq),
        grid=(steps,),
        in_specs=[pos_spec(0, 0), pos_spec(1, 0), pos_spec(0, 1), pos_spec(1, 1),
                  pl.BlockSpec(memory_space=pl.ANY),
                  row_spec,
                  pl.BlockSpec((chunk, ROUTER_LANES), lambda c: (c, 0)),
                  pl.BlockSpec((1, d), lambda c: (0, 0))] + in_specs_pool,
        out_specs=out_specs,
        out_shape=out_shape,
        scratch_shapes=[pltpu.VMEM((2, 2, chunk * ROW_TILE, LANES), jnp.uint32),
                        pltpu.SemaphoreType.DMA((2,))] + scratch_pool,
        compiler_params=_cparams(("arbitrary",), 56),
        name="combine_rows",
    )(pos, pos, pos, pos, ys, h1, weights, next_gain.reshape(1, d), *args_pool)
    return outs[0] if next_pool is None else tuple(outs)


def kernel(x, rel_bias_table, norm_mix_gain, w_in, w_merge_gate, b_merge_gate, pool_mix, pool_scale, w_up_pool, lambda_q1, lambda_k1, lambda_q2, lambda_k2, subln_gain, w_up_attn, w_out, norm_ffn_gain, w_router_group, b_router_group, w_router_expert, b_router_expert, w_expert_gate, w_expert_up, w_expert_down, final_norm_gain):
    batch, seq, d = x.shape
    depth = w_in.shape[0]
    n = batch * seq
    n_tiles = (2 * n + N_EXPERTS * (TM_EXPERT - 1)) // TM_EXPERT + 1

    bias_tiles = bias_tiles_call(rel_bias_table)
    h = x.reshape(n, d)
    q_scale = LOG2E * DIFF_HEAD_DIM ** -0.5
    xn, mixed = first_pool_call(h, norm_mix_gain[0], w_in, pool_mix[0].astype(BF16), pool_scale[0], seq)

    for l in range(depth):
        qkv = proj_call(xn, w_in, l, 1, 3, ATTN_WIDTH, BF16, first_block_scale=q_scale, name="in_proj_qkv")
        lambda_init = 0.8 - 0.6 * math.exp(-0.3 * l)
        lam_params = jnp.stack([lambda_q1[l], lambda_k1[l], lambda_q2[l], lambda_k2[l]])
        attn = attn_call(qkv.reshape(batch, seq, 3 * ATTN_WIDTH), bias_tiles, lam_params, subln_gain[l],
                         lambda_init, batch, seq).reshape(n, ATTN_WIDTH)
        z = merge_call(xn, mixed, attn, w_merge_gate, b_merge_gate.reshape(depth, 1, -1), w_up_pool, w_up_attn, l)

        w_router = jnp.concatenate(
            [w_router_group[l], jnp.transpose(w_router_expert[l], (1, 0, 2)).reshape(d, N_EXPERTS),
             jnp.zeros((d, ROUTER_LANES - N_GROUPS - N_EXPERTS), F32)], axis=1).astype(BF16)
        b_router = jnp.concatenate(
            [b_router_group[l], b_router_expert[l].reshape(-1),
             jnp.zeros((ROUTER_LANES - N_GROUPS - N_EXPERTS,), F32)]).reshape(1, ROUTER_LANES)
        h1, hn, logits = outproj_call(z, w_out, l, h, norm_ffn_gain[l], w_router, b_router)

        weights, pos, meta = route_plan_call(logits, n_tiles)
        tile_expert = meta[META_TILE_EXPERT, :n_tiles]
        next_expert = meta[META_NEXT_EXPERT, :n_tiles]
        pad_start = meta[META_PAD_START, :N_EXPERTS]
        n_valid = meta[META_N_VALID, :1]
        xs = dispatch_call(hn, pos, pad_start, n_valid, n_tiles)
        ys = expert_call(xs, tile_expert, n_valid, next_expert, w_expert_gate, w_expert_up, w_expert_down, l)
        if l == depth - 1:
            out = combine_call(ys, pos, h1, weights, final_norm_gain)
        else:
            next_pool = (w_in, l + 1, pool_mix[l + 1].astype(BF16), pool_scale[l + 1], seq)
            xn, h, mixed = combine_call(ys, pos, h1, weights, norm_mix_gain[l + 1], next_pool)

    return out.reshape(batch, seq, d)
```

```python
import functools
import math

import numpy as np
import jax
import jax.numpy as jnp
from jax import lax
from jax.experimental import pallas as pl
from jax.experimental.pallas import tpu as pltpu

F32 = jnp.float32
BF16 = jnp.bfloat16

POOL_WIDTH = 1024
POOL_WINDOWS = (2, 4, 8, 16)
POOL_GROUP_DIM = 256
POOL_HALO = 16
DIFF_HEADS = 8
DIFF_HEAD_DIM = 64
DIFF_V_DIM = 128
ATTN_WIDTH = 1024
REL_BUCKETS = 32
REL_MAX_DISTANCE = 128
N_GROUPS = 4
EXPERTS_PER_GROUP = 8
N_EXPERTS = 32
EPS = 1e-6
NEG_INF = -1e30
LOG2E = 1.4426950408889634

ROUTER_LANES = 128
TQ = 256
TQ_BLOCK = 256
TK = 256
KV_SUPER = 2
V_ROWS = DIFF_V_DIM + 16
HEADS_PER_STEP = 8
TM_EXPERT = 256
LANES = 128
ROW_DMA_UNROLL = 8
MIB = 1024 * 1024


def _cparams(sem, vmem_mib):
    return pltpu.CompilerParams(dimension_semantics=sem, vmem_limit_bytes=vmem_mib * MIB)


def _rms(xf, gain):
    ms = jnp.mean(xf * xf, axis=-1, keepdims=True)
    return xf * lax.rsqrt(ms + EPS) * gain


def _proj_kernel(x_ref, w_ref, o_ref, w_s, *, first_block_scale):
    @pl.when(pl.program_id(1) == 0)
    def _():
        w_s[...] = w_ref[...].astype(BF16)

    acc = jnp.dot(x_ref[...], w_s[...], preferred_element_type=F32)
    if first_block_scale is not None:
        acc = acc * jnp.where(pl.program_id(0) == 0, first_block_scale, 1.0).astype(F32)
    o_ref[...] = acc.astype(o_ref.dtype)


def proj_call(x, w, layer, col_block0, n_col_blocks, tn, out_dtype, first_block_scale=None, tm=1024, name="proj"):
    n, k = x.shape
    return pl.pallas_call(
        functools.partial(_proj_kernel, first_block_scale=first_block_scale),
        grid=(n_col_blocks, n // tm),
        in_specs=[pl.BlockSpec((tm, k), lambda j, m: (m, 0)),
                  pl.BlockSpec((None, k, tn), lambda j, m: (layer, 0, col_block0 + j))],
        out_specs=pl.BlockSpec((tm, tn), lambda j, m: (m, j)),
        out_shape=jax.ShapeDtypeStruct((n, n_col_blocks * tn), out_dtype),
        scratch_shapes=[pltpu.VMEM((k, tn), BF16)],
        compiler_params=_cparams(("arbitrary", "arbitrary"), 48),
        name=name,
    )(x, w)


def _first_pool_kernel(x_ref, gain_ref, w_ref, mix_ref, scale_ref, xn_ref, o_ref, w_s, carry, *, tm, seq):
    m = pl.program_id(0)

    @pl.when(m == 0)
    def _():
        w_s[...] = w_ref[...].astype(BF16)

    xn = _rms(x_ref[...], gain_ref[...]).astype(BF16)
    xn_ref[...] = xn
    o_ref[...] = _pool_tile(xn, w_s, carry, (m * tm) % seq, mix_ref, scale_ref).astype(o_ref.dtype)


def _pool_tile(xn, w_s, carry, row0, mix_ref, scale_ref):
    tm = xn.shape[0]
    cur = jnp.dot(xn, w_s[...], preferred_element_type=F32)
    prev = jnp.where(row0 == 0, 0.0, carry[...])
    carry[...] = cur[tm - POOL_HALO:]
    pos = row0 + lax.broadcasted_iota(jnp.int32, (tm, 1), 0)
    outs = []
    for g, w in enumerate(POOL_WINDOWS):
        sl = slice(g * POOL_GROUP_DIM, (g + 1) * POOL_GROUP_DIM)
        x = jnp.concatenate([prev[:, sl], cur[:, sl]], axis=0)
        s, d = x, 1
        while d < w:
            s = s[:-d] + s[d:]
            d *= 2
        start = POOL_HALO - w + 1
        wsum = s[start:start + tm]
        count = jnp.minimum(pos + 1, w).astype(F32)
        pooled = wsum / count - cur[:, sl]
        mixed = jnp.dot(pooled.astype(BF16), mix_ref[g], preferred_element_type=F32)
        outs.append(mixed * scale_ref[:, sl])
    return jnp.concatenate(outs, axis=1)


def first_pool_call(x, gain, w_in, mix_bf16, scale, seq, tm=512):
    n, d = x.shape
    c = POOL_WIDTH
    row_spec = pl.BlockSpec((tm, c), lambda m: (m, 0))
    x_spec = pl.BlockSpec((tm, d), lambda m: (m, 0))
    return pl.pallas_call(
        functools.partial(_first_pool_kernel, tm=tm, seq=seq),
        grid=(n // tm,),
        in_specs=[x_spec,
                  pl.BlockSpec((1, d), lambda m: (0, 0)),
                  pl.BlockSpec((None, d, c), lambda m: (0, 0, 0), pipeline_mode=pl.Buffered(1)),
                  pl.BlockSpec(mix_bf16.shape, lambda m: (0, 0, 0)),
                  pl.BlockSpec((1, c), lambda m: (0, 0))],
        out_specs=[x_spec, row_spec],
        out_shape=[jax.ShapeDtypeStruct((n, d), BF16), jax.ShapeDtypeStruct((n, c), BF16)],
        scratch_shapes=[pltpu.VMEM((d, c), BF16), pltpu.VMEM((POOL_HALO, c), F32)],
        compiler_params=_cparams(("arbitrary",), 48),
        name="first_norm_pool",
    )(x, gain.reshape(1, d), w_in, mix_bf16, scale.reshape(1, c))


N_BIAS_TILES = 4


def _bucket_tiles():
    kk = np.arange(TK)[:, None]
    qq = np.arange(TQ_BLOCK)[None, :]
    tiles = []
    for rel in (0, 1, 2, -1):
        n = rel * TK + qq - kk
        max_exact = REL_BUCKETS // 2
        nf = np.maximum(n, 1).astype(np.float64)
        large = max_exact + (np.log(nf / max_exact) / math.log(REL_MAX_DISTANCE / max_exact)
                             * (REL_BUCKETS - max_exact)).astype(np.int64)
        large = np.minimum(large, REL_BUCKETS - 1)
        bucket = np.where(n < max_exact, n, large)
        tiles.append(np.where(n < 0, -1, bucket))
    return np.stack(tiles).astype(np.int32)


N_NEAR_TILES = 2


def _bias_kernel(table_ref, bucket_ref, o_ref):
    h = pl.program_id(0)
    near = bucket_ref[0:N_NEAR_TILES]
    acc = jnp.full(near.shape, NEG_INF, F32)
    for b in range(REL_BUCKETS):
        acc = jnp.where(near == b, table_ref[b * DIFF_HEADS + h] * LOG2E, acc)
    o_ref[0:N_NEAR_TILES] = acc
    last = REL_BUCKETS - 1
    far_bias = table_ref[last * DIFF_HEADS + h] * LOG2E
    o_ref[N_NEAR_TILES:] = jnp.where(bucket_ref[N_NEAR_TILES:] == last, far_bias, NEG_INF)


def bias_tiles_call(rel_table):
    tiles = _bucket_tiles()
    assert set(np.unique(tiles[N_NEAR_TILES:]).tolist()) <= {-1, REL_BUCKETS - 1}
    bucket = jnp.asarray(tiles)
    return pl.pallas_call(
        _bias_kernel,
        grid=(DIFF_HEADS,),
        in_specs=[pl.BlockSpec(memory_space=pltpu.SMEM),
                  pl.BlockSpec(bucket.shape, lambda h: (0, 0, 0))],
        out_specs=pl.BlockSpec((None,) + bucket.shape, lambda h: (h, 0, 0, 0)),
        out_shape=jax.ShapeDtypeStruct((DIFF_HEADS,) + bucket.shape, F32),
        compiler_params=_cparams(("parallel",), 32),
        name="rel_bias_tiles",
    )(rel_table.reshape(-1), bucket)


def _attn_kernel(q_ref, k_ref, v_ref, bias_ref, lam_ref, gain_ref, o_ref, vt_ref, s_ref, acc_ref,
                 *, lambda_init, n_super):
    qi = pl.program_id(2)
    tks = KV_SUPER * TK
    heads = range(HEADS_PER_STEP)

    def block_diag(ref, hh):
        q = ref[:, hh * DIFF_V_DIM:(hh + 1) * DIFF_V_DIM]
        lane = lax.broadcasted_iota(jnp.int32, q.shape, 1)
        zero = jnp.zeros_like(q)
        return jnp.concatenate([jnp.where(lane < DIFF_HEAD_DIM, q, zero),
                                jnp.where(lane >= DIFF_HEAD_DIM, q, zero)], axis=0)

    def scores(t, hh, qd, tile):
        kb = k_ref[pl.ds(pl.multiple_of(t * tks, tks), tks), hh * DIFF_V_DIM:(hh + 1) * DIFF_V_DIM]
        s = lax.dot_general(kb, qd, (((1,), (1,)), ((), ())), preferred_element_type=F32)
        parts = []
        for u in range(KV_SUPER):
            tiles = []
            for c in range(TQ // TQ_BLOCK):
                rel = tile * (TQ // TQ_BLOCK) + c - (t * KV_SUPER + u)
                tiles.append(bias_ref[hh, jnp.where(rel < 0, N_BIAS_TILES - 1, jnp.minimum(rel, 2))])
            parts.append(s[u * TK:(u + 1) * TK] + jnp.concatenate(tiles + tiles, axis=1))
        s = jnp.concatenate(parts, axis=0)
        s_ref[hh] = s
        return jnp.max(s, axis=0, keepdims=True)

    @pl.when(qi == 0)
    def _():
        extra = (lax.broadcasted_iota(jnp.int32, (V_ROWS - DIFF_V_DIM, tks), 0) == 0).astype(BF16)
        for hh in heads:
            cols = slice(hh * DIFF_V_DIM, (hh + 1) * DIFF_V_DIM)
            for c in range(n_super):
                vt = v_ref[c * tks:(c + 1) * tks, cols].astype(F32).T.astype(BF16)
                vt_ref[hh, c] = jnp.concatenate([vt, extra], axis=0)

    qds = [block_diag(q_ref, hh) for hh in heads]

    def softmax_step(t, hh, m_prev, m_cur):
        alpha = jnp.exp2(m_prev - m_cur)
        p = jnp.exp2(s_ref[hh] - m_cur)
        pv = jnp.dot(vt_ref[hh, t], p.astype(BF16), preferred_element_type=F32)
        acc_ref[hh] = acc_ref[hh] * alpha + pv

    last = ((qi + 1) * TQ - 1) // tks
    acc_ref[...] = jnp.zeros(acc_ref.shape, F32)
    neg = jnp.full((1, 2 * TQ), NEG_INF, F32)
    init = tuple((neg, jnp.maximum(neg, scores(0, hh, qds[hh], qi))) for hh in heads)

    def body(t, carry):
        out = []
        for hh in heads:
            m_prev, m_cur = carry[hh]
            softmax_step(t, hh, m_prev, m_cur)
            m_next = jnp.maximum(m_cur, scores(t + 1, hh, qds[hh], qi))
            out.append((m_cur, m_next))
        return tuple(out)

    carry = lax.fori_loop(0, last, body, init)

    lam_p = lam_ref[...]
    lam = (jnp.exp(jnp.sum(lam_p[0:1] * lam_p[1:2], axis=1, keepdims=True))
           - jnp.exp(jnp.sum(lam_p[2:3] * lam_p[3:4], axis=1, keepdims=True)) + lambda_init)
    for hh in heads:
        m_prev, m_cur = carry[hh]
        softmax_step(last, hh, m_prev, m_cur)
        acc = acc_ref[hh, :DIFF_V_DIM, :]
        l = acc_ref[hh, DIFF_V_DIM:DIFF_V_DIM + 1, :]
        o = acc[:, :TQ] / l[:, :TQ] - lam * (acc[:, TQ:] / l[:, TQ:])
        ms = jnp.mean(o * o, axis=0, keepdims=True)
        y = o * lax.rsqrt(ms + EPS) * gain_ref[...] * (1.0 - lambda_init)
        o_ref[:, hh * DIFF_V_DIM:(hh + 1) * DIFF_V_DIM] = y.T.astype(o_ref.dtype)


def attn_call(qkv, bias_tiles, lam_params, subln_gain, lambda_init, batch, seq):
    tks = KV_SUPER * TK
    n_super = seq // tks
    hps = HEADS_PER_STEP
    width = hps * DIFF_V_DIM
    groups = DIFF_HEADS // hps
    n_q = seq // TQ
    return pl.pallas_call(
        functools.partial(_attn_kernel, lambda_init=lambda_init, n_super=n_super),
        grid=(batch, groups, n_q),
        in_specs=[pl.BlockSpec((None, TQ, width), lambda b, g, i: (b, i, g)),
                  pl.BlockSpec((None, seq, width), lambda b, g, i: (b, 0, groups + g), pipeline_mode=pl.Buffered(1)),
                  pl.BlockSpec((None, seq, width), lambda b, g, i: (b, 0, 2 * groups + g),
                               pipeline_mode=pl.Buffered(1)),
                  pl.BlockSpec((hps, N_BIAS_TILES, TK, TQ_BLOCK), lambda b, g, i: (g, 0, 0, 0),
                               pipeline_mode=pl.Buffered(1)),
                  pl.BlockSpec((4, DIFF_HEAD_DIM), lambda b, g, i: (0, 0)),
                  pl.BlockSpec((DIFF_V_DIM, 1), lambda b, g, i: (0, 0))],
        out_specs=pl.BlockSpec((None, TQ, width), lambda b, g, i: (b, i, g)),
        out_shape=jax.ShapeDtypeStruct((batch, seq, ATTN_WIDTH), BF16),
        scratch_shapes=[pltpu.VMEM((hps, n_super, V_ROWS, tks), BF16),
                        pltpu.VMEM((hps, tks, 2 * TQ), F32),
                        pltpu.VMEM((hps, V_ROWS, 2 * TQ), F32)],
        compiler_params=_cparams(("parallel", "parallel", "arbitrary"), 56),
        name="diff_attention",
    )(qkv, qkv, qkv, bias_tiles, lam_params, subln_gain.reshape(DIFF_V_DIM, 1))


def _sigmoid(x):
    return 1.0 / (1.0 + jnp.exp(-x))


def _merge_kernel(xn_ref, mixed_ref, attn_ref, wgp_ref, wga_ref, bgp_ref, bga_ref, wup_ref, wua_ref, z_ref,
                  wgp_s, wga_s, wup_s, wua_s):
    @pl.when(pl.program_id(1) == 0)
    def _():
        wgp_s[...] = wgp_ref[...].astype(BF16)
        wga_s[...] = wga_ref[...].astype(BF16)
        wup_s[...] = wup_ref[...].astype(BF16)
        wua_s[...] = wua_ref[...].astype(BF16)

    xn = xn_ref[...]
    g_pool = _sigmoid(jnp.dot(xn, wgp_s[...], preferred_element_type=F32) + bgp_ref[...])
    g_attn = _sigmoid(jnp.dot(xn, wga_s[...], preferred_element_type=F32) + bga_ref[...])
    y_pool = jnp.dot(mixed_ref[...], wup_s[...], preferred_element_type=F32)
    y_attn = jnp.dot(attn_ref[...], wua_s[...], preferred_element_type=F32)
    z_ref[...] = (g_pool * y_pool + g_attn * y_attn).astype(z_ref.dtype)


def merge_call(xn, mixed, attn, w_gate, b_gate, w_up_pool, w_up_attn, layer, tm=1024, tn=512):
    n, d = xn.shape
    nb = d // tn
    return pl.pallas_call(
        _merge_kernel,
        grid=(nb, n // tm),
        in_specs=[pl.BlockSpec((tm, d), lambda j, m: (m, 0)),
                  pl.BlockSpec((tm, POOL_WIDTH), lambda j, m: (m, 0)),
                  pl.BlockSpec((tm, ATTN_WIDTH), lambda j, m: (m, 0)),
                  pl.BlockSpec((None, d, tn), lambda j, m: (layer, 0, j)),
                  pl.BlockSpec((None, d, tn), lambda j, m: (layer, 0, nb + j)),
                  pl.BlockSpec((None, 1, tn), lambda j, m: (layer, 0, j)),
                  pl.BlockSpec((None, 1, tn), lambda j, m: (layer, 0, nb + j)),
                  pl.BlockSpec((None, POOL_WIDTH, tn), lambda j, m: (layer, 0, j)),
                  pl.BlockSpec((None, ATTN_WIDTH, tn), lambda j, m: (layer, 0, j))],
        out_specs=pl.BlockSpec((tm, tn), lambda j, m: (m, j)),
        out_shape=jax.ShapeDtypeStruct((n, d), BF16),
        scratch_shapes=[pltpu.VMEM((d, tn), BF16), pltpu.VMEM((d, tn), BF16),
                        pltpu.VMEM((POOL_WIDTH, tn), BF16), pltpu.VMEM((ATTN_WIDTH, tn), BF16)],
        compiler_params=_cparams(("arbitrary", "arbitrary"), 56),
        name="gated_merge",
    )(xn, mixed, attn, w_gate, w_gate, b_gate, b_gate, w_up_pool, w_up_attn)


HIGH_HALF = 0xFFFF0000


def _pack_halves(x):
    c = x.shape[1] // 2
    lo = lax.bitcast_convert_type(x[:, :c].astype(BF16).astype(F32), jnp.uint32)
    hi = lax.bitcast_convert_type(x[:, c:].astype(BF16).astype(F32), jnp.uint32)
    return (lo >> 16) | hi


def _unpack_halves(p):
    lo = lax.bitcast_convert_type(p << 16, F32)
    hi = lax.bitcast_convert_type(p & jnp.uint32(HIGH_HALF), F32)
    return lo, hi


ROW_TILE = 8


def _store_row_tiles(ref, packed):
    m = packed.shape[0]
    for s in range(ROW_TILE):
        ref[pl.ds(s, m, stride=ROW_TILE), :] = packed[:, s * LANES:(s + 1) * LANES]


def _load_row_tiles(ref):
    m = ref.shape[0] // ROW_TILE
    return jnp.concatenate([ref[pl.ds(s, m, stride=ROW_TILE), :] for s in range(ROW_TILE)], axis=1)


def _rows(ref, first, count=1):
    return ref.at[pl.ds(pl.multiple_of(first * ROW_TILE, ROW_TILE), count * ROW_TILE)]


def _outproj_kernel(z_ref, w_ref, h_ref, gain_ref, wr_ref, br_ref, h1_ref, hn_ref, logit_ref, w_s):
    @pl.when(pl.program_id(0) == 0)
    def _():
        w_s[...] = w_ref[...].astype(BF16)

    h1 = h_ref[...] + jnp.dot(z_ref[...], w_s[...], preferred_element_type=F32)
    h1_ref[...] = h1
    hn = _rms(h1, gain_ref[...])
    _store_row_tiles(hn_ref, _pack_halves(hn))
    logit_ref[...] = jnp.dot(hn.astype(BF16), wr_ref[...], preferred_element_type=F32) + br_ref[...]


def outproj_call(z, w_out, layer, h, gain, w_router, b_router, tm=256):
    n, d = h.shape
    return pl.pallas_call(
        _outproj_kernel,
        grid=(n // tm,),
        in_specs=[pl.BlockSpec((tm, d), lambda m: (m, 0)),
                  pl.BlockSpec((None, d, d), lambda m: (layer, 0, 0), pipeline_mode=pl.Buffered(1)),
                  pl.BlockSpec((tm, d), lambda m: (m, 0)),
                  pl.BlockSpec((1, d), lambda m: (0, 0)),
                  pl.BlockSpec((d, ROUTER_LANES), lambda m: (0, 0)),
                  pl.BlockSpec((1, ROUTER_LANES), lambda m: (0, 0))],
        out_specs=[pl.BlockSpec((tm, d), lambda m: (m, 0)),
                   pl.BlockSpec((tm * ROW_TILE, LANES), lambda m: (m, 0)),
                   pl.BlockSpec((tm, ROUTER_LANES), lambda m: (m, 0))],
        out_shape=[jax.ShapeDtypeStruct((n, d), F32),
                   jax.ShapeDtypeStruct((n * ROW_TILE, LANES), jnp.uint32),
                   jax.ShapeDtypeStruct((n, ROUTER_LANES), F32)],
        scratch_shapes=[pltpu.VMEM((d, d), BF16)],
        compiler_params=_cparams(("arbitrary",), 56),
        name="out_proj_norm_router",
    )(z, w_out, h, gain.reshape(1, d), w_router, b_router)


def _route(x, lane):
    big = float(ROUTER_LANES)

    def first_argmax(vals):
        top = jnp.max(vals, axis=1, keepdims=True)
        idx = jnp.min(jnp.where(vals == top, lane, big), axis=1, keepdims=True)
        return top, idx

    gmask = lane < N_GROUPS
    g_top, g_sel = first_argmax(jnp.where(gmask, x, -jnp.inf))
    g_weight = 1.0 / jnp.sum(jnp.where(gmask, jnp.exp(x - g_top), 0.0), axis=1, keepdims=True)
    lo = N_GROUPS + EXPERTS_PER_GROUP * g_sel
    e_vals = jnp.where((lane >= lo) & (lane < lo + EXPERTS_PER_GROUP), x, -jnp.inf)
    v1, i1 = first_argmax(e_vals)
    v2, i2 = first_argmax(jnp.where(lane == i1, -jnp.inf, e_vals))
    t = jnp.exp(v2 - v1)
    return i1 - N_GROUPS, i2 - N_GROUPS, g_weight / (1.0 + t), g_weight * t / (1.0 + t)


PLAN_BLOCK = 1024
PLAN_CHUNK = 512
META_TILE_EXPERT, META_NEXT_EXPERT, META_PAD_START, META_N_VALID = 0, 1, 2, 3


def _route_plan_kernel(logit_ref, w_ref, pos_ref, meta_ref, tri_s, cnt_s, base_s, off_s):
    phase = pl.program_id(0)
    blk = pl.program_id(1)
    tb = logit_ref.shape[0]
    lane = lax.broadcasted_iota(jnp.int32, (tb, ROUTER_LANES), 1).astype(F32)
    e0, e1, w0, w1 = _route(logit_ref[...], lane)
    onehot = jnp.where(lane == e0, 1.0, 0.0) + jnp.where(lane == e1, 1.0, 0.0)
    block_counts = jnp.sum(onehot, axis=0, keepdims=True)

    @pl.when(jnp.logical_and(phase == 0, blk == 0))
    def _():
        cnt_s[...] = jnp.zeros(cnt_s.shape, F32)

    @pl.when(phase == 0)
    def _():
        cnt_s[...] = cnt_s[...] + block_counts

    @pl.when(jnp.logical_and(phase == 1, blk == 0))
    def _():
        r = lax.broadcasted_iota(jnp.int32, (tb, tb), 0)
        c = lax.broadcasted_iota(jnp.int32, (tb, tb), 1)
        tri_s[...] = jnp.where(c < r, 1.0, 0.0).astype(BF16)
        lane1 = lane[0:1]
        cnt = cnt_s[...]
        tiles = jnp.floor((cnt + (TM_EXPERT - 1)) * (1.0 / TM_EXPERT))
        ri = lax.broadcasted_iota(jnp.int32, (ROUTER_LANES, ROUTER_LANES), 0)
        ci = lax.broadcasted_iota(jnp.int32, (ROUTER_LANES, ROUTER_LANES), 1)
        upper = jnp.where(ri <= ci, 1.0, 0.0).astype(BF16)
        ends = jnp.dot(jnp.broadcast_to(tiles, (8, ROUTER_LANES)).astype(BF16), upper,
                       preferred_element_type=F32)[0:1]
        off_s[...] = (ends - tiles) * TM_EXPERT
        base_s[...] = jnp.zeros(base_s.shape, F32)

        def pick(vec, e):
            return jnp.sum(jnp.where(lane1 == e, vec, 0.0), axis=1, keepdims=True)

        end_of = [pick(ends, e) for e in range(N_EXPERTS)]
        n_valid = end_of[N_EXPERTS - 1]

        def segment_of(tile):
            return sum(jnp.where(tile >= end_e, 1.0, 0.0) for end_e in end_of)

        tile_expert = jnp.where(lane1 < n_valid, segment_of(lane1), segment_of(n_valid - 1.0))
        following = sum(jnp.where(tile_expert == e, end_of[e], 0.0) for e in range(N_EXPERTS))
        next_expert = jnp.where(following < n_valid, segment_of(following), -1.0)
        rows = [tile_expert, next_expert, off_s[...] + cnt, jnp.broadcast_to(n_valid, (1, ROUTER_LANES))]
        rows += [jnp.zeros((1, ROUTER_LANES), F32)] * (meta_ref.shape[0] - len(rows))
        meta_ref[...] = jnp.concatenate(rows, axis=0).astype(jnp.int32)

    @pl.when(phase == 1)
    def _():
        before = jnp.dot(tri_s[...], onehot.astype(BF16), preferred_element_type=F32)
        row = before + base_s[...] + off_s[...]
        base_s[...] = base_s[...] + block_counts
        w_ref[...] = jnp.where(lane == 0, w0, jnp.where(lane == 1, w1, 0.0))
        eye = (lax.broadcasted_iota(jnp.int32, (LANES, LANES), 0)
               == lax.broadcasted_iota(jnp.int32, (LANES, LANES), 1))
        for j, e in enumerate((e0, e1)):
            col = jnp.sum(jnp.where(lane == e, row, 0.0), axis=1, keepdims=True)
            for q in range(tb // PLAN_CHUNK):
                parts = []
                for g in range(PLAN_CHUNK // LANES):
                    t0 = q * PLAN_CHUNK + g * LANES
                    square = jnp.broadcast_to(col[t0:t0 + LANES], (LANES, LANES))
                    parts.append(jnp.sum(jnp.where(eye, square, 0.0), axis=0, keepdims=True))
                pos_ref[j, q] = jnp.concatenate(parts, axis=1).astype(jnp.int32)


def route_plan_call(logits, n_tiles):
    n = logits.shape[0]
    tb = PLAN_BLOCK
    chunks = tb // PLAN_CHUNK
    assert n_tiles <= ROUTER_LANES and N_EXPERTS <= ROUTER_LANES
    return pl.pallas_call(
        _route_plan_kernel,
        grid=(2, n // tb),
        in_specs=[pl.BlockSpec((tb, ROUTER_LANES), lambda p, b: (b, 0))],
        out_specs=[pl.BlockSpec((tb, ROUTER_LANES), lambda p, b: (p * b, 0)),
                   pl.BlockSpec((2, chunks, 1, PLAN_CHUNK), lambda p, b: (0, p * b, 0, 0)),
                   pl.BlockSpec((8, ROUTER_LANES), lambda p, b: (0, 0))],
        out_shape=[jax.ShapeDtypeStruct((n, ROUTER_LANES), F32),
                   jax.ShapeDtypeStruct((2, n // PLAN_CHUNK, 1, PLAN_CHUNK), jnp.int32),
                   jax.ShapeDtypeStruct((8, ROUTER_LANES), jnp.int32)],
        scratch_shapes=[pltpu.VMEM((tb, tb), BF16), pltpu.VMEM((1, ROUTER_LANES), F32),
                        pltpu.VMEM((1, ROUTER_LANES), F32), pltpu.VMEM((1, ROUTER_LANES), F32)],
        compiler_params=_cparams(("arbitrary", "arbitrary"), 32),
        name="route_plan",
    )(logits)


DISPATCH_SLOTS = 3


def _dispatch_kernel(pad_ref, nv_ref, pos0_ref, pos1_ref, src_hbm, dst_ref, zero_buf, ring, sem, in_sems, row_sems,
                     *, chunk, n_tiles):
    i = pl.program_id(0)
    steps = pl.num_programs(0)
    slot = i % DISPATCH_SLOTS

    def load(block, into):
        return pltpu.make_async_copy(_rows(src_hbm, block * chunk, chunk), ring.at[into], in_sems.at[into])

    def wait_rows(of_slot):
        for j in range(2):
            pltpu.make_async_copy(ring.at[of_slot], _rows(dst_ref, 0, chunk), row_sems.at[of_slot]).wait()

    @pl.when(i == 0)
    def _():
        load(0, 0).start()
        zero_buf[...] = jnp.zeros(zero_buf.shape, zero_buf.dtype)

        def fill(e):
            return pltpu.make_async_copy(zero_buf, _rows(dst_ref, pad_ref[e], TM_EXPERT), sem)

        for e in range(N_EXPERTS):
            fill(e).start()
        for e in range(N_EXPERTS):
            fill(e).wait()

        def fill_tile(i):
            return pltpu.make_async_copy(zero_buf, _rows(dst_ref, i * TM_EXPERT, TM_EXPERT), sem)

        def start_tile(i, c):
            fill_tile(i).start()
            return c

        def wait_tile(i, c):
            fill_tile(i).wait()
            return c

        lax.fori_loop(nv_ref[0], n_tiles, start_tile, 0)
        lax.fori_loop(nv_ref[0], n_tiles, wait_tile, 0)

    nxt = (i + 1) % DISPATCH_SLOTS

    @pl.when(i >= DISPATCH_SLOTS - 1)
    def _():
        wait_rows(nxt)

    @pl.when(i + 1 < steps)
    def _():
        load(i + 1, nxt).start()

    load(i, slot).wait()
    src_ref = ring.at[slot]

    def start(t, c):
        for j, pos_ref in enumerate((pos0_ref, pos1_ref)):
            pltpu.make_async_copy(_rows(src_ref, t), _rows(dst_ref, pos_ref[0, 0, t]),
                                  row_sems.at[slot]).start(priority=j)
        return c

    lax.fori_loop(0, chunk, start, 0, unroll=ROW_DMA_UNROLL)

    @pl.when(i == steps - 1)
    def _():
        for back in range(DISPATCH_SLOTS - 1):
            @pl.when(i - back >= 0)
            def _():
                wait_rows((i - back) % DISPATCH_SLOTS)


def dispatch_call(hn, pos, pad_start, n_valid_tiles, n_tiles):
    n = hn.shape[0] // ROW_TILE
    chunk = PLAN_CHUNK
    grid_spec = pltpu.PrefetchScalarGridSpec(
        num_scalar_prefetch=2,
        grid=(n // chunk,),
        in_specs=[pl.BlockSpec((None, 1, 1, chunk), lambda i, pad, nv: (0, i, 0, 0), memory_space=pltpu.SMEM),
                  pl.BlockSpec((None, 1, 1, chunk), lambda i, pad, nv: (1, i, 0, 0), memory_space=pltpu.SMEM),
                  pl.BlockSpec(memory_space=pl.ANY)],
        out_specs=pl.BlockSpec(memory_space=pl.ANY),
        scratch_shapes=[pltpu.VMEM((TM_EXPERT * ROW_TILE, LANES), hn.dtype),
                        pltpu.VMEM((DISPATCH_SLOTS, chunk * ROW_TILE, LANES), hn.dtype),
                        pltpu.SemaphoreType.DMA(()), pltpu.SemaphoreType.DMA((DISPATCH_SLOTS,)),
                        pltpu.SemaphoreType.DMA((DISPATCH_SLOTS,))],
    )
    return pl.pallas_call(
        functools.partial(_dispatch_kernel, chunk=chunk, n_tiles=n_tiles),
        grid_spec=grid_spec,
        out_shape=jax.ShapeDtypeStruct((n_tiles * TM_EXPERT * ROW_TILE, LANES), hn.dtype),
        compiler_params=_cparams(("arbitrary",), 32),
        name="dispatch_rows",
    )(pad_start, n_valid_tiles, pos, pos, hn)


def _expert_kernel(te_ref, nv_ref, nx_ref, x_ref, wg_hbm, wu_hbm, wd_hbm, y_ref,
                   wg_f, wu_f, wd_f, wg_s, wu_s, wd_s, seg_ref, sems, *, layer):
    i = pl.program_id(0)
    valid = i < nv_ref[0]
    expert = te_ref[i]
    changed = jnp.logical_or(i == 0, expert != te_ref[jnp.maximum(i - 1, 0)])

    def fetch(e, slot):
        return [pltpu.make_async_copy(hbm.at[layer, e], buf.at[slot], sems.at[slot])
                for hbm, buf in ((wg_hbm, wg_f), (wu_hbm, wu_f), (wd_hbm, wd_f))]

    def swiglu_tile():
        lo, hi = _unpack_halves(_load_row_tiles(x_ref))
        x = jnp.concatenate([lo.astype(BF16), hi.astype(BF16)], axis=1)
        a = jnp.dot(x, wg_s[...], preferred_element_type=F32)
        b = jnp.dot(x, wu_s[...], preferred_element_type=F32)
        hmid = (a * _sigmoid(a) * b).astype(BF16)
        _store_row_tiles(y_ref, _pack_halves(jnp.dot(hmid, wd_s[...], preferred_element_type=F32)))

    @pl.when(i == 0)
    def _():
        seg_ref[0] = 0
        for cp in fetch(expert, 0):
            cp.start()

    @pl.when(jnp.logical_and(valid, changed))
    def _():
        slot = seg_ref[0] % 2
        for cp in fetch(expert, slot):
            cp.wait()
        nxt = nx_ref[i]

        @pl.when(nxt >= 0)
        def _():
            for cp in fetch(nxt, 1 - slot):
                cp.start()

        seg_ref[0] = seg_ref[0] + 1
        wg_s[...] = wg_f[slot].astype(BF16)
        wu_s[...] = wu_f[slot].astype(BF16)
        wd_s[...] = wd_f[slot].astype(BF16)
        swiglu_tile()

    @pl.when(jnp.logical_and(valid, jnp.logical_not(changed)))
    def _():
        swiglu_tile()

    @pl.when(jnp.logical_not(valid))
    def _():
        y_ref[...] = jnp.zeros(y_ref.shape, y_ref.dtype)


def expert_call(xs, tile_expert, n_valid_tiles, next_expert, w_gate, w_up, w_down, layer):
    p = xs.shape[0] // ROW_TILE
    d = 2 * ROW_TILE * LANES
    tm = TM_EXPERT
    f = w_gate.shape[-1]
    grid_spec = pltpu.PrefetchScalarGridSpec(
        num_scalar_prefetch=3,
        grid=(p // tm,),
        in_specs=[pl.BlockSpec((tm * ROW_TILE, LANES), lambda i, te, nv, nx: (jnp.minimum(i, nv[0] - 1), 0)),
                  pl.BlockSpec(memory_space=pl.ANY),
                  pl.BlockSpec(memory_space=pl.ANY),
                  pl.BlockSpec(memory_space=pl.ANY)],
        out_specs=pl.BlockSpec((tm * ROW_TILE, LANES), lambda i, te, nv, nx: (i, 0)),
        scratch_shapes=[pltpu.VMEM((2, d, f), F32), pltpu.VMEM((2, d, f), F32), pltpu.VMEM((2, f, d), F32),
                        pltpu.VMEM((d, f), BF16), pltpu.VMEM((d, f), BF16), pltpu.VMEM((f, d), BF16),
                        pltpu.SMEM((1,), jnp.int32), pltpu.SemaphoreType.DMA((2,))],
    )
    return pl.pallas_call(
        functools.partial(_expert_kernel, layer=layer),
        grid_spec=grid_spec,
        out_shape=jax.ShapeDtypeStruct(xs.shape, jnp.uint32),
        compiler_params=_cparams(("arbitrary",), 58),
        name="expert_swiglu",
    )(tile_expert, n_valid_tiles, next_expert, xs, w_gate, w_up, w_down)


def _combine_kernel(pos0_ref, pos1_ref, nxt0_ref, nxt1_ref, ys_ref, h_ref, w_ref, gain_ref, *rest, chunk, seq):
    if seq is None:
        xn_ref, buf, sems = rest
        h2_ref = None
    else:
        w_in_ref, mix_ref, scale_ref, xn_ref, h2_ref, mixed_ref, buf, sems, w_s, carry = rest
    i = pl.program_id(0)
    slot = i % 2

    def gather(p0_ref, p1_ref, into):
        def start(t, c):
            for j, pos_ref in enumerate((p0_ref, p1_ref)):
                pltpu.make_async_copy(_rows(ys_ref, pos_ref[0, 0, t]), _rows(buf.at[into, j], t),
                                      sems.at[into]).start(priority=j)
            return c

        lax.fori_loop(0, chunk, start, 0, unroll=ROW_DMA_UNROLL)

    @pl.when(i == 0)
    def _():
        gather(pos0_ref, pos1_ref, 0)

    @pl.when(i + 1 < pl.num_programs(0))
    def _():
        gather(nxt0_ref, nxt1_ref, 1 - slot)

    for j in range(2):
        pltpu.make_async_copy(_rows(ys_ref, 0, chunk), buf.at[slot, j], sems.at[slot]).wait()
    w = w_ref[...]
    lo0, hi0 = _unpack_halves(_load_row_tiles(buf.at[slot, 0]))
    lo1, hi1 = _unpack_halves(_load_row_tiles(buf.at[slot, 1]))
    y = jnp.concatenate([w[:, 0:1] * lo0 + w[:, 1:2] * lo1, w[:, 0:1] * hi0 + w[:, 1:2] * hi1], axis=1)
    h2 = h_ref[...] + y
    xn = _rms(h2, gain_ref[...]).astype(xn_ref.dtype)
    xn_ref[...] = xn
    if seq is not None:
        h2_ref[...] = h2

        @pl.when(i == 0)
        def _():
            w_s[...] = w_in_ref[...].astype(BF16)

        mixed_ref[...] = _pool_tile(xn, w_s, carry, (i * chunk) % seq, mix_ref, scale_ref).astype(mixed_ref.dtype)


def combine_call(ys, pos, h1, weights, next_gain, next_pool=None, chunk=256):
    n, d = h1.shape
    row_spec = pl.BlockSpec((chunk, d), lambda c: (c, 0))
    per_row = PLAN_CHUNK // chunk
    steps = n // chunk
    in_specs_pool, args_pool, scratch_pool = [], [], []
    if next_pool is None:
        seq = None
        out_specs = [row_spec]
        out_shape = [jax.ShapeDtypeStruct((n, d), F32)]
    else:
        w_in, layer, mix_bf16, scale, seq = next_pool
        c = POOL_WIDTH
        pool_spec = pl.BlockSpec((chunk, c), lambda s: (s, 0))
        in_specs_pool = [pl.BlockSpec((None, d, c), lambda s: (layer, 0, 0), pipeline_mode=pl.Buffered(1)),
                         pl.BlockSpec(mix_bf16.shape, lambda s: (0, 0, 0)),
                         pl.BlockSpec((1, c), lambda s: (0, 0))]
        args_pool = [w_in, mix_bf16, scale.reshape(1, c)]
        scratch_pool = [pltpu.VMEM((d, c), BF16), pltpu.VMEM((POOL_HALO, c), F32)]
        out_specs = [row_spec, row_spec, pool_spec]
        out_shape = [jax.ShapeDtypeStruct((n, d), BF16), jax.ShapeDtypeStruct((n, d), F32),
                     jax.ShapeDtypeStruct((n, c), BF16)]

    def pos_spec(j, ahead):
        def index(c):
            c = jnp.minimum(c + ahead, steps - 1)
            return (j, c // per_row, 0, c % per_row)
        return pl.BlockSpec((None, 1, 1, chunk), index, memory_space=pltpu.SMEM)

    outs = pl.pallas_call(
        functools.partial(_combine_kernel, chunk=chunk, seq=seq),
        grid=(steps,),
        in_specs=[pos_spec(0, 0), pos_spec(1, 0), pos_spec(0, 1), pos_spec(1, 1),
                  pl.BlockSpec(memory_space=pl.ANY),
                  row_spec,
                  pl.BlockSpec((chunk, ROUTER_LANES), lambda c: (c, 0)),
                  pl.BlockSpec((1, d), lambda c: (0, 0))] + in_specs_pool,
        out_specs=out_specs,
        out_shape=out_shape,
        scratch_shapes=[pltpu.VMEM((2, 2, chunk * ROW_TILE, LANES), jnp.uint32),
                        pltpu.SemaphoreType.DMA((2,))] + scratch_pool,
        compiler_params=_cparams(("arbitrary",), 56),
        name="combine_rows",
    )(pos, pos, pos, pos, ys, h1, weights, next_gain.reshape(1, d), *args_pool)
    return outs[0] if next_pool is None else tuple(outs)


def kernel(x, rel_bias_table, norm_mix_gain, w_in, w_merge_gate, b_merge_gate, pool_mix, pool_scale, w_up_pool, lambda_q1, lambda_k1, lambda_q2, lambda_k2, subln_gain, w_up_attn, w_out, norm_ffn_gain, w_router_group, b_router_group, w_router_expert, b_router_expert, w_expert_gate, w_expert_up, w_expert_down, final_norm_gain):
    batch, seq, d = x.shape
    depth = w_in.shape[0]
    n = batch * seq
    n_tiles = (2 * n + N_EXPERTS * (TM_EXPERT - 1)) // TM_EXPERT + 1

    bias_tiles = bias_tiles_call(rel_bias_table)
    h = x.reshape(n, d)
    q_scale = LOG2E * DIFF_HEAD_DIM ** -0.5
    xn, mixed = first_pool_call(h, norm_mix_gain[0], w_in, pool_mix[0].astype(BF16), pool_scale[0], seq)

    for l in range(depth):
        qkv = proj_call(xn, w_in, l, 1, 3, ATTN_WIDTH, BF16, first_block_scale=q_scale, name="in_proj_qkv")
        lambda_init = 0.8 - 0.6 * math.exp(-0.3 * l)
        lam_params = jnp.stack([lambda_q1[l], lambda_k1[l], lambda_q2[l], lambda_k2[l]])
        attn = attn_call(qkv.reshape(batch, seq, 3 * ATTN_WIDTH), bias_tiles, lam_params, subln_gain[l],
                         lambda_init, batch, seq).reshape(n, ATTN_WIDTH)
        z = merge_call(xn, mixed, attn, w_merge_gate, b_merge_gate.reshape(depth, 1, -1), w_up_pool, w_up_attn, l)

        w_router = jnp.concatenate(
            [w_router_group[l], jnp.transpose(w_router_expert[l], (1, 0, 2)).reshape(d, N_EXPERTS),
             jnp.zeros((d, ROUTER_LANES - N_GROUPS - N_EXPERTS), F32)], axis=1).astype(BF16)
        b_router = jnp.concatenate(
            [b_router_group[l], b_router_expert[l].reshape(-1),
             jnp.zeros((ROUTER_LANES - N_GROUPS - N_EXPERTS,), F32)]).reshape(1, ROUTER_LANES)
        h1, hn, logits = outproj_call(z, w_out, l, h, norm_ffn_gain[l], w_router, b_router)

        weights, pos, meta = route_plan_call(logits, n_tiles)
        tile_expert = meta[META_TILE_EXPERT, :n_tiles]
        next_expert = meta[META_NEXT_EXPERT, :n_tiles]
        pad_start = meta[META_PAD_START, :N_EXPERTS]
        n_valid = meta[META_N_VALID, :1]
        xs = dispatch_call(hn, pos, pad_start, n_valid, n_tiles)
        ys = expert_call(xs, tile_expert, n_valid, next_expert, w_expert_gate, w_expert_up, w_expert_down, l)
        if l == depth - 1:
            out = combine_call(ys, pos, h1, weights, final_norm_gain)
        else:
            next_pool = (w_in, l + 1, pool_mix[l + 1].astype(BF16), pool_scale[l + 1], seq)
            xn, h, mixed = combine_call(ys, pos, h1, weights, norm_mix_gain[l + 1], next_pool)

    return out.reshape(batch, seq, d)
```

```python
import functools
import math

import numpy as np
import jax
import jax.numpy as jnp
from jax import lax
from jax.experimental import pallas as pl
from jax.experimental.pallas import tpu as pltpu

F32 = jnp.float32
BF16 = jnp.bfloat16

POOL_WIDTH = 1024
POOL_WINDOWS = (2, 4, 8, 16)
POOL_GROUP_DIM = 256
POOL_HALO = 16
DIFF_HEADS = 8
DIFF_HEAD_DIM = 64
DIFF_V_DIM = 128
ATTN_WIDTH = 1024
REL_BUCKETS = 32
REL_MAX_DISTANCE = 128
N_GROUPS = 4
EXPERTS_PER_GROUP = 8
N_EXPERTS = 32
EPS = 1e-6
NEG_INF = -1e30
LOG2E = 1.4426950408889634

ROUTER_LANES = 128
TQ = 256
TQ_BLOCK = 256
TK = 256
KV_SUPER = 2
V_ROWS = DIFF_V_DIM + 16
HEADS_PER_STEP = 8
TM_EXPERT = 256
LANES = 128
ROW_DMA_UNROLL = 8
MIB = 1024 * 1024


def _cparams(sem, vmem_mib):
    return pltpu.CompilerParams(dimension_semantics=sem, vmem_limit_bytes=vmem_mib * MIB)


def _rms(xf, gain):
    ms = jnp.mean(xf * xf, axis=-1, keepdims=True)
    return xf * lax.rsqrt(ms + EPS) * gain


def _proj_kernel(x_ref, w_ref, o_ref, w_s, *, first_block_scale):
    @pl.when(pl.program_id(1) == 0)
    def _():
        w_s[...] = w_ref[...].astype(BF16)

    acc = jnp.dot(x_ref[...], w_s[...], preferred_element_type=F32)
    if first_block_scale is not None:
        acc = acc * jnp.where(pl.program_id(0) == 0, first_block_scale, 1.0).astype(F32)
    o_ref[...] = acc.astype(o_ref.dtype)


def proj_call(x, w, layer, col_block0, n_col_blocks, tn, out_dtype, first_block_scale=None, tm=1024, name="proj"):
    n, k = x.shape
    return pl.pallas_call(
        functools.partial(_proj_kernel, first_block_scale=first_block_scale),
        grid=(n_col_blocks, n // tm),
        in_specs=[pl.BlockSpec((tm, k), lambda j, m: (m, 0)),
                  pl.BlockSpec((None, k, tn), lambda j, m: (layer, 0, col_block0 + j))],
        out_specs=pl.BlockSpec((tm, tn), lambda j, m: (m, j)),
        out_shape=jax.ShapeDtypeStruct((n, n_col_blocks * tn), out_dtype),
        scratch_shapes=[pltpu.VMEM((k, tn), BF16)],
        compiler_params=_cparams(("arbitrary", "arbitrary"), 48),
        name=name,
    )(x, w)


def _first_pool_kernel(x_ref, gain_ref, w_ref, mix_ref, scale_ref, xn_ref, o_ref, w_s, carry, *, tm, seq):
    m = pl.program_id(0)

    @pl.when(m == 0)
    def _():
        w_s[...] = w_ref[...].astype(BF16)

    xn = _rms(x_ref[...], gain_ref[...]).astype(BF16)
    xn_ref[...] = xn
    o_ref[...] = _pool_tile(xn, w_s, carry, (m * tm) % seq, mix_ref, scale_ref).astype(o_ref.dtype)


def _pool_tile(xn, w_s, carry, row0, mix_ref, scale_ref):
    tm = xn.shape[0]
    cur = jnp.dot(xn, w_s[...], preferred_element_type=F32)
    prev = jnp.where(row0 == 0, 0.0, carry[...])
    carry[...] = cur[tm - POOL_HALO:]
    pos = row0 + lax.broadcasted_iota(jnp.int32, (tm, 1), 0)
    outs = []
    for g, w in enumerate(POOL_WINDOWS):
        sl = slice(g * POOL_GROUP_DIM, (g + 1) * POOL_GROUP_DIM)
        x = jnp.concatenate([prev[:, sl], cur[:, sl]], axis=0)
        s, d = x, 1
        while d < w:
            s = s[:-d] + s[d:]
            d *= 2
        start = POOL_HALO - w + 1
        wsum = s[start:start + tm]
        count = jnp.minimum(pos + 1, w).astype(F32)
        pooled = wsum / count - cur[:, sl]
        mixed = jnp.dot(pooled.astype(BF16), mix_ref[g], preferred_element_type=F32)
        outs.append(mixed * scale_ref[:, sl])
    return jnp.concatenate(outs, axis=1)


def first_pool_call(x, gain, w_in, mix_bf16, scale, seq, tm=512):
    n, d = x.shape
    c = POOL_WIDTH
    row_spec = pl.BlockSpec((tm, c), lambda m: (m, 0))
    x_spec = pl.BlockSpec((tm, d), lambda m: (m, 0))
    return pl.pallas_call(
        functools.partial(_first_pool_kernel, tm=tm, seq=seq),
        grid=(n // tm,),
        in_specs=[x_spec,
                  pl.BlockSpec((1, d), lambda m: (0, 0)),
                  pl.BlockSpec((None, d, c), lambda m: (0, 0, 0), pipeline_mode=pl.Buffered(1)),
                  pl.BlockSpec(mix_bf16.shape, lambda m: (0, 0, 0)),
                  pl.BlockSpec((1, c), lambda m: (0, 0))],
        out_specs=[x_spec, row_spec],
        out_shape=[jax.ShapeDtypeStruct((n, d), BF16), jax.ShapeDtypeStruct((n, c), BF16)],
        scratch_shapes=[pltpu.VMEM((d, c), BF16), pltpu.VMEM((POOL_HALO, c), F32)],
        compiler_params=_cparams(("arbitrary",), 48),
        name="first_norm_pool",
    )(x, gain.reshape(1, d), w_in, mix_bf16, scale.reshape(1, c))


N_BIAS_TILES = 4


def _bucket_tiles():
    kk = np.arange(TK)[:, None]
    qq = np.arange(TQ_BLOCK)[None, :]
    tiles = []
    for rel in (0, 1, 2, -1):
        n = rel * TK + qq - kk
        max_exact = REL_BUCKETS // 2
        nf = np.maximum(n, 1).astype(np.float64)
        large = max_exact + (np.log(nf / max_exact) / math.log(REL_MAX_DISTANCE / max_exact)
                             * (REL_BUCKETS - max_exact)).astype(np.int64)
        large = np.minimum(large, REL_BUCKETS - 1)
        bucket = np.where(n < max_exact, n, large)
        tiles.append(np.where(n < 0, -1, bucket))
    return np.stack(tiles).astype(np.int32)


N_NEAR_TILES = 2


def _bias_kernel(table_ref, bucket_ref, o_ref):
    h = pl.program_id(0)
    near = bucket_ref[0:N_NEAR_TILES]
    acc = jnp.full(near.shape, NEG_INF, F32)
    for b in range(REL_BUCKETS):
        acc = jnp.where(near == b, table_ref[b * DIFF_HEADS + h] * LOG2E, acc)
    o_ref[0:N_NEAR_TILES] = acc
    last = REL_BUCKETS - 1
    far_bias = table_ref[last * DIFF_HEADS + h] * LOG2E
    o_ref[N_NEAR_TILES:] = jnp.where(bucket_ref[N_NEAR_TILES:] == last, far_bias, NEG_INF)


def bias_tiles_call(rel_table):
    tiles = _bucket_tiles()
    assert set(np.unique(tiles[N_NEAR_TILES:]).tolist()) <= {-1, REL_BUCKETS - 1}
    bucket = jnp.asarray(tiles)
    return pl.pallas_call(
        _bias_kernel,
        grid=(DIFF_HEADS,),
        in_specs=[pl.BlockSpec(memory_space=pltpu.SMEM),
                  pl.BlockSpec(bucket.shape, lambda h: (0, 0, 0))],
        out_specs=pl.BlockSpec((None,) + bucket.shape, lambda h: (h, 0, 0, 0)),
        out_shape=jax.ShapeDtypeStruct((DIFF_HEADS,) + bucket.shape, F32),
        compiler_params=_cparams(("parallel",), 32),
        name="rel_bias_tiles",
    )(rel_table.reshape(-1), bucket)


def _attn_kernel(q_ref, k_ref, v_ref, bias_ref, lam_ref, gain_ref, o_ref, vt_ref, s_ref, acc_ref,
                 *, lambda_init, n_super):
    qi = pl.program_id(2)
    tks = KV_SUPER * TK
    heads = range(HEADS_PER_STEP)

    def block_diag(ref, hh):
        q = ref[:, hh * DIFF_V_DIM:(hh + 1) * DIFF_V_DIM]
        lane = lax.broadcasted_iota(jnp.int32, q.shape, 1)
        zero = jnp.zeros_like(q)
        return jnp.concatenate([jnp.where(lane < DIFF_HEAD_DIM, q, zero),
                                jnp.where(lane >= DIFF_HEAD_DIM, q, zero)], axis=0)

    def scores(t, hh, qd, tile):
        kb = k_ref[pl.ds(pl.multiple_of(t * tks, tks), tks), hh * DIFF_V_DIM:(hh + 1) * DIFF_V_DIM]
        s = lax.dot_general(kb, qd, (((1,), (1,)), ((), ())), preferred_element_type=F32)
        parts = []
        for u in range(KV_SUPER):
            tiles = []
            for c in range(TQ // TQ_BLOCK):
                rel = tile * (TQ // TQ_BLOCK) + c - (t * KV_SUPER + u)
                tiles.append(bias_ref[hh, jnp.where(rel < 0, N_BIAS_TILES - 1, jnp.minimum(rel, 2))])
            parts.append(s[u * TK:(u + 1) * TK] + jnp.concatenate(tiles + tiles, axis=1))
        s = jnp.concatenate(parts, axis=0)
        s_ref[hh] = s
        return jnp.max(s, axis=0, keepdims=True)

    @pl.when(qi == 0)
    def _():
        extra = (lax.broadcasted_iota(jnp.int32, (V_ROWS - DIFF_V_DIM, tks), 0) == 0).astype(BF16)
        for hh in heads:
            cols = slice(hh * DIFF_V_DIM, (hh + 1) * DIFF_V_DIM)
            for c in range(n_super):
                vt = v_ref[c * tks:(c + 1) * tks, cols].astype(F32).T.astype(BF16)
                vt_ref[hh, c] = jnp.concatenate([vt, extra], axis=0)

    qds = [block_diag(q_ref, hh) for hh in heads]

    def softmax_step(t, hh, m_prev, m_cur):
        alpha = jnp.exp2(m_prev - m_cur)
        p = jnp.exp2(s_ref[hh] - m_cur)
        pv = jnp.dot(vt_ref[hh, t], p.astype(BF16), preferred_element_type=F32)
        acc_ref[hh] = acc_ref[hh] * alpha + pv

    last = ((qi + 1) * TQ - 1) // tks
    acc_ref[...] = jnp.zeros(acc_ref.shape, F32)
    neg = jnp.full((1, 2 * TQ), NEG_INF, F32)
    init = tuple((neg, jnp.maximum(neg, scores(0, hh, qds[hh], qi))) for hh in heads)

    def body(t, carry):
        out = []
        for hh in heads:
            m_prev, m_cur = carry[hh]
            softmax_step(t, hh, m_prev, m_cur)
            m_next = jnp.maximum(m_cur, scores(t + 1, hh, qds[hh], qi))
            out.append((m_cur, m_next))
        return tuple(out)

    carry = lax.fori_loop(0, last, body, init)

    lam_p = lam_ref[...]
    lam = (jnp.exp(jnp.sum(lam_p[0:1] * lam_p[1:2], axis=1, keepdims=True))
           - jnp.exp(jnp.sum(lam_p[2:3] * lam_p[3:4], axis=1, keepdims=True)) + lambda_init)
    for hh in heads:
        m_prev, m_cur = carry[hh]
        softmax_step(last, hh, m_prev, m_cur)
        acc = acc_ref[hh, :DIFF_V_DIM, :]
        l = acc_ref[hh, DIFF_V_DIM:DIFF_V_DIM + 1, :]
        o = acc[:, :TQ] / l[:, :TQ] - lam * (acc[:, TQ:] / l[:, TQ:])
        ms = jnp.mean(o * o, axis=0, keepdims=True)
        y = o * lax.rsqrt(ms + EPS) * gain_ref[...] * (1.0 - lambda_init)
        o_ref[:, hh * DIFF_V_DIM:(hh + 1) * DIFF_V_DIM] = y.T.astype(o_ref.dtype)


def attn_call(qkv, bias_tiles, lam_params, subln_gain, lambda_init, batch, seq):
    tks = KV_SUPER * TK
    n_super = seq // tks
    hps = HEADS_PER_STEP
    width = hps * DIFF_V_DIM
    groups = DIFF_HEADS // hps
    n_q = seq // TQ
    return pl.pallas_call(
        functools.partial(_attn_kernel, lambda_init=lambda_init, n_super=n_super),
        grid=(batch, groups, n_q),
        in_specs=[pl.BlockSpec((None, TQ, width), lambda b, g, i: (b, i, g)),
                  pl.BlockSpec((None, seq, width), lambda b, g, i: (b, 0, groups + g), pipeline_mode=pl.Buffered(1)),
                  pl.BlockSpec((None, seq, width), lambda b, g, i: (b, 0, 2 * groups + g),
                               pipeline_mode=pl.Buffered(1)),
                  pl.BlockSpec((hps, N_BIAS_TILES, TK, TQ_BLOCK), lambda b, g, i: (g, 0, 0, 0),
                               pipeline_mode=pl.Buffered(1)),
                  pl.BlockSpec((4, DIFF_HEAD_DIM), lambda b, g, i: (0, 0)),
                  pl.BlockSpec((DIFF_V_DIM, 1), lambda b, g, i: (0, 0))],
        out_specs=pl.BlockSpec((None, TQ, width), lambda b, g, i: (b, i, g)),
        out_shape=jax.ShapeDtypeStruct((batch, seq, ATTN_WIDTH), BF16),
        scratch_shapes=[pltpu.VMEM((hps, n_super, V_ROWS, tks), BF16),
                        pltpu.VMEM((hps, tks, 2 * TQ), F32),
                        pltpu.VMEM((hps, V_ROWS, 2 * TQ), F32)],
        compiler_params=_cparams(("parallel", "parallel", "arbitrary"), 56),
        name="diff_attention",
    )(qkv, qkv, qkv, bias_tiles, lam_params, subln_gain.reshape(DIFF_V_DIM, 1))


def _sigmoid(x):
    return 1.0 / (1.0 + jnp.exp(-x))


def _merge_kernel(xn_ref, mixed_ref, attn_ref, wgp_ref, wga_ref, bgp_ref, bga_ref, wup_ref, wua_ref, z_ref,
                  wgp_s, wga_s, wup_s, wua_s):
    @pl.when(pl.program_id(1) == 0)
    def _():
        wgp_s[...] = wgp_ref[...].astype(BF16)
        wga_s[...] = wga_ref[...].astype(BF16)
        wup_s[...] = wup_ref[...].astype(BF16)
        wua_s[...] = wua_ref[...].astype(BF16)

    xn = xn_ref[...]
    g_pool = _sigmoid(jnp.dot(xn, wgp_s[...], preferred_element_type=F32) + bgp_ref[...])
    g_attn = _sigmoid(jnp.dot(xn, wga_s[...], preferred_element_type=F32) + bga_ref[...])
    y_pool = jnp.dot(mixed_ref[...], wup_s[...], preferred_element_type=F32)
    y_attn = jnp.dot(attn_ref[...], wua_s[...], preferred_element_type=F32)
    z_ref[...] = (g_pool * y_pool + g_attn * y_attn).astype(z_ref.dtype)


def merge_call(xn, mixed, attn, w_gate, b_gate, w_up_pool, w_up_attn, layer, tm=1024, tn=512):
    n, d = xn.shape
    nb = d // tn
    return pl.pallas_call(
        _merge_kernel,
        grid=(nb, n // tm),
        in_specs=[pl.BlockSpec((tm, d), lambda j, m: (m, 0)),
                  pl.BlockSpec((tm, POOL_WIDTH), lambda j, m: (m, 0)),
                  pl.BlockSpec((tm, ATTN_WIDTH), lambda j, m: (m, 0)),
                  pl.BlockSpec((None, d, tn), lambda j, m: (layer, 0, j)),
                  pl.BlockSpec((None, d, tn), lambda j, m: (layer, 0, nb + j)),
                  pl.BlockSpec((None, 1, tn), lambda j, m: (layer, 0, j)),
                  pl.BlockSpec((None, 1, tn), lambda j, m: (layer, 0, nb + j)),
                  pl.BlockSpec((None, POOL_WIDTH, tn), lambda j, m: (layer, 0, j)),
                  pl.BlockSpec((None, ATTN_WIDTH, tn), lambda j, m: (layer, 0, j))],
        out_specs=pl.BlockSpec((tm, tn), lambda j, m: (m, j)),
        out_shape=jax.ShapeDtypeStruct((n, d), BF16),
        scratch_shapes=[pltpu.VMEM((d, tn), BF16), pltpu.VMEM((d, tn), BF16),
                        pltpu.VMEM((POOL_WIDTH, tn), BF16), pltpu.VMEM((ATTN_WIDTH, tn), BF16)],
        compiler_params=_cparams(("arbitrary", "arbitrary"), 56),
        name="gated_merge",
    )(xn, mixed, attn, w_gate, w_gate, b_gate, b_gate, w_up_pool, w_up_attn)


HIGH_HALF = 0xFFFF0000


def _pack_halves(x):
    c = x.shape[1] // 2
    lo = lax.bitcast_convert_type(x[:, :c].astype(BF16).astype(F32), jnp.uint32)
    hi = lax.bitcast_convert_type(x[:, c:].astype(BF16).astype(F32), jnp.uint32)
    return (lo >> 16) | hi


def _unpack_halves(p):
    lo = lax.bitcast_convert_type(p << 16, F32)
    hi = lax.bitcast_convert_type(p & jnp.uint32(HIGH_HALF), F32)
    return lo, hi


ROW_TILE = 8


def _store_row_tiles(ref, packed):
    m = packed.shape[0]
    for s in range(ROW_TILE):
        ref[pl.ds(s, m, stride=ROW_TILE), :] = packed[:, s * LANES:(s + 1) * LANES]


def _load_row_tiles(ref):
    m = ref.shape[0] // ROW_TILE
    return jnp.concatenate([ref[pl.ds(s, m, stride=ROW_TILE), :] for s in range(ROW_TILE)], axis=1)


def _rows(ref, first, count=1):
    return ref.at[pl.ds(pl.multiple_of(first * ROW_TILE, ROW_TILE), count * ROW_TILE)]


def _outproj_kernel(z_ref, w_ref, h_ref, gain_ref, wr_ref, br_ref, h1_ref, hn_ref, logit_ref, w_s):
    @pl.when(pl.program_id(0) == 0)
    def _():
        w_s[...] = w_ref[...].astype(BF16)

    h1 = h_ref[...] + jnp.dot(z_ref[...], w_s[...], preferred_element_type=F32)
    h1_ref[...] = h1
    hn = _rms(h1, gain_ref[...])
    _store_row_tiles(hn_ref, _pack_halves(hn))
    logit_ref[...] = jnp.dot(hn.astype(BF16), wr_ref[...], preferred_element_type=F32) + br_ref[...]


def outproj_call(z, w_out, layer, h, gain, w_router, b_router, tm=256):
    n, d = h.shape
    return pl.pallas_call(
        _outproj_kernel,
        grid=(n // tm,),
        in_specs=[pl.BlockSpec((tm, d), lambda m: (m, 0)),
                  pl.BlockSpec((None, d, d), lambda m: (layer, 0, 0), pipeline_mode=pl.Buffered(1)),
                  pl.BlockSpec((tm, d), lambda m: (m, 0)),
                  pl.BlockSpec((1, d), lambda m: (0, 0)),
                  pl.BlockSpec((d, ROUTER_LANES), lambda m: (0, 0)),
                  pl.BlockSpec((1, ROUTER_LANES), lambda m: (0, 0))],
        out_specs=[pl.BlockSpec((tm, d), lambda m: (m, 0)),
                   pl.BlockSpec((tm * ROW_TILE, LANES), lambda m: (m, 0)),
                   pl.BlockSpec((tm, ROUTER_LANES), lambda m: (m, 0))],
        out_shape=[jax.ShapeDtypeStruct((n, d), F32),
                   jax.ShapeDtypeStruct((n * ROW_TILE, LANES), jnp.uint32),
                   jax.ShapeDtypeStruct((n, ROUTER_LANES), F32)],
        scratch_shapes=[pltpu.VMEM((d, d), BF16)],
        compiler_params=_cparams(("arbitrary",), 56),
        name="out_proj_norm_router",
    )(z, w_out, h, gain.reshape(1, d), w_router, b_router)


def _route(x, lane):
    big = float(ROUTER_LANES)

    def first_argmax(vals):
        top = jnp.max(vals, axis=1, keepdims=True)
        idx = jnp.min(jnp.where(vals == top, lane, big), axis=1, keepdims=True)
        return top, idx

    gmask = lane < N_GROUPS
    g_top, g_sel = first_argmax(jnp.where(gmask, x, -jnp.inf))
    g_weight = 1.0 / jnp.sum(jnp.where(gmask, jnp.exp(x - g_top), 0.0), axis=1, keepdims=True)
    lo = N_GROUPS + EXPERTS_PER_GROUP * g_sel
    e_vals = jnp.where((lane >= lo) & (lane < lo + EXPERTS_PER_GROUP), x, -jnp.inf)
    v1, i1 = first_argmax(e_vals)
    v2, i2 = first_argmax(jnp.where(lane == i1, -jnp.inf, e_vals))
    t = jnp.exp(v2 - v1)
    return i1 - N_GROUPS, i2 - N_GROUPS, g_weight / (1.0 + t), g_weight * t / (1.0 + t)


PLAN_BLOCK = 1024
PLAN_CHUNK = 512
META_TILE_EXPERT, META_NEXT_EXPERT, META_PAD_START, META_N_VALID = 0, 1, 2, 3


def _route_plan_kernel(logit_ref, w_ref, pos_ref, meta_ref, tri_s, cnt_s, base_s, off_s):
    phase = pl.program_id(0)
    blk = pl.program_id(1)
    tb = logit_ref.shape[0]
    lane = lax.broadcasted_iota(jnp.int32, (tb, ROUTER_LANES), 1).astype(F32)
    e0, e1, w0, w1 = _route(logit_ref[...], lane)
    onehot = jnp.where(lane == e0, 1.0, 0.0) + jnp.where(lane == e1, 1.0, 0.0)
    block_counts = jnp.sum(onehot, axis=0, keepdims=True)

    @pl.when(jnp.logical_and(phase == 0, blk == 0))
    def _():
        cnt_s[...] = jnp.zeros(cnt_s.shape, F32)

    @pl.when(phase == 0)
    def _():
        cnt_s[...] = cnt_s[...] + block_counts

    @pl.when(jnp.logical_and(phase == 1, blk == 0))
    def _():
        r = lax.broadcasted_iota(jnp.int32, (tb, tb), 0)
        c = lax.broadcasted_iota(jnp.int32, (tb, tb), 1)
        tri_s[...] = jnp.where(c < r, 1.0, 0.0).astype(BF16)
        lane1 = lane[0:1]
        cnt = cnt_s[...]
        tiles = jnp.floor((cnt + (TM_EXPERT - 1)) * (1.0 / TM_EXPERT))
        ri = lax.broadcasted_iota(jnp.int32, (ROUTER_LANES, ROUTER_LANES), 0)
        ci = lax.broadcasted_iota(jnp.int32, (ROUTER_LANES, ROUTER_LANES), 1)
        upper = jnp.where(ri <= ci, 1.0, 0.0).astype(BF16)
        ends = jnp.dot(jnp.broadcast_to(tiles, (8, ROUTER_LANES)).astype(BF16), upper,
                       preferred_element_type=F32)[0:1]
        off_s[...] = (ends - tiles) * TM_EXPERT
        base_s[...] = jnp.zeros(base_s.shape, F32)

        def pick(vec, e):
            return jnp.sum(jnp.where(lane1 == e, vec, 0.0), axis=1, keepdims=True)

        end_of = [pick(ends, e) for e in range(N_EXPERTS)]
        n_valid = end_of[N_EXPERTS - 1]

        def segment_of(tile):
            return sum(jnp.where(tile >= end_e, 1.0, 0.0) for end_e in end_of)

        tile_expert = jnp.where(lane1 < n_valid, segment_of(lane1), segment_of(n_valid - 1.0))
        following = sum(jnp.where(tile_expert == e, end_of[e], 0.0) for e in range(N_EXPERTS))
        next_expert = jnp.where(following < n_valid, segment_of(following), -1.0)
        rows = [tile_expert, next_expert, off_s[...] + cnt, jnp.broadcast_to(n_valid, (1, ROUTER_LANES))]
        rows += [jnp.zeros((1, ROUTER_LANES), F32)] * (meta_ref.shape[0] - len(rows))
        meta_ref[...] = jnp.concatenate(rows, axis=0).astype(jnp.int32)

    @pl.when(phase == 1)
    def _():
        before = jnp.dot(tri_s[...], onehot.astype(BF16), preferred_element_type=F32)
        row = before + base_s[...] + off_s[...]
        base_s[...] = base_s[...] + block_counts
        w_ref[...] = jnp.where(lane == 0, w0, jnp.where(lane == 1, w1, 0.0))
        eye = (lax.broadcasted_iota(jnp.int32, (LANES, LANES), 0)
               == lax.broadcasted_iota(jnp.int32, (LANES, LANES), 1))
        for j, e in enumerate((e0, e1)):
            col = jnp.sum(jnp.where(lane == e, row, 0.0), axis=1, keepdims=True)
            for q in range(tb // PLAN_CHUNK):
                parts = []
                for g in range(PLAN_CHUNK // LANES):
                    t0 = q * PLAN_CHUNK + g * LANES
                    square = jnp.broadcast_to(col[t0:t0 + LANES], (LANES, LANES))
                    parts.append(jnp.sum(jnp.where(eye, square, 0.0), axis=0, keepdims=True))
                pos_ref[j, q] = jnp.concatenate(parts, axis=1).astype(jnp.int32)


def route_plan_call(logits, n_tiles):
    n = logits.shape[0]
    tb = PLAN_BLOCK
    chunks = tb // PLAN_CHUNK
    assert n_tiles <= ROUTER_LANES and N_EXPERTS <= ROUTER_LANES
    return pl.pallas_call(
        _route_plan_kernel,
        grid=(2, n // tb),
        in_specs=[pl.BlockSpec((tb, ROUTER_LANES), lambda p, b: (b, 0))],
        out_specs=[pl.BlockSpec((tb, ROUTER_LANES), lambda p, b: (p * b, 0)),
                   pl.BlockSpec((2, chunks, 1, PLAN_CHUNK), lambda p, b: (0, p * b, 0, 0)),
                   pl.BlockSpec((8, ROUTER_LANES), lambda p, b: (0, 0))],
        out_shape=[jax.ShapeDtypeStruct((n, ROUTER_LANES), F32),
                   jax.ShapeDtypeStruct((2, n // PLAN_CHUNK, 1, PLAN_CHUNK), jnp.int32),
                   jax.ShapeDtypeStruct((8, ROUTER_LANES), jnp.int32)],
        scratch_shapes=[pltpu.VMEM((tb, tb), BF16), pltpu.VMEM((1, ROUTER_LANES), F32),
                        pltpu.VMEM((1, ROUTER_LANES), F32), pltpu.VMEM((1, ROUTER_LANES), F32)],
        compiler_params=_cparams(("arbitrary", "arbitrary"), 32),
        name="route_plan",
    )(logits)


DISPATCH_SLOTS = 3


def _dispatch_kernel(pad_ref, nv_ref, pos0_ref, pos1_ref, src_hbm, dst_ref, zero_buf, ring, sem, in_sems, row_sems,
                     *, chunk, n_tiles):
    i = pl.program_id(0)
    steps = pl.num_programs(0)
    slot = i % DISPATCH_SLOTS

    def load(block, into):
        return pltpu.make_async_copy(_rows(src_hbm, block * chunk, chunk), ring.at[into], in_sems.at[into])

    def wait_rows(of_slot):
        for j in range(2):
            pltpu.make_async_copy(ring.at[of_slot], _rows(dst_ref, 0, chunk), row_sems.at[of_slot]).wait()

    @pl.when(i == 0)
    def _():
        load(0, 0).start()
        zero_buf[...] = jnp.zeros(zero_buf.shape, zero_buf.dtype)

        def fill(e):
            return pltpu.make_async_copy(zero_buf, _rows(dst_ref, pad_ref[e], TM_EXPERT), sem)

        for e in range(N_EXPERTS):
            fill(e).start()
        for e in range(N_EXPERTS):
            fill(e).wait()

        def fill_tile(i):
            return pltpu.make_async_copy(zero_buf, _rows(dst_ref, i * TM_EXPERT, TM_EXPERT), sem)

        def start_tile(i, c):
            fill_tile(i).start()
            return c

        def wait_tile(i, c):
            fill_tile(i).wait()
            return c

        lax.fori_loop(nv_ref[0], n_tiles, start_tile, 0)
        lax.fori_loop(nv_ref[0], n_tiles, wait_tile, 0)

    nxt = (i + 1) % DISPATCH_SLOTS

    @pl.when(i >= DISPATCH_SLOTS - 1)
    def _():
        wait_rows(nxt)

    @pl.when(i + 1 < steps)
    def _():
        load(i + 1, nxt).start()

    load(i, slot).wait()
    src_ref = ring.at[slot]

    def start(t, c):
        for j, pos_ref in enumerate((pos0_ref, pos1_ref)):
            pltpu.make_async_copy(_rows(src_ref, t), _rows(dst_ref, pos_ref[0, 0, t]),
                                  row_sems.at[slot]).start(priority=j)
        return c

    lax.fori_loop(0, chunk, start, 0, unroll=ROW_DMA_UNROLL)

    @pl.when(i == steps - 1)
    def _():
        for back in range(DISPATCH_SLOTS - 1):
            @pl.when(i - back >= 0)
            def _():
                wait_rows((i - back) % DISPATCH_SLOTS)


def dispatch_call(hn, pos, pad_start, n_valid_tiles, n_tiles):
    n = hn.shape[0] // ROW_TILE
    chunk = PLAN_CHUNK
    grid_spec = pltpu.PrefetchScalarGridSpec(
        num_scalar_prefetch=2,
        grid=(n // chunk,),
        in_specs=[pl.BlockSpec((None, 1, 1, chunk), lambda i, pad, nv: (0, i, 0, 0), memory_space=pltpu.SMEM),
                  pl.BlockSpec((None, 1, 1, chunk), lambda i, pad, nv: (1, i, 0, 0), memory_space=pltpu.SMEM),
                  pl.BlockSpec(memory_space=pl.ANY)],
        out_specs=pl.BlockSpec(memory_space=pl.ANY),
        scratch_shapes=[pltpu.VMEM((TM_EXPERT * ROW_TILE, LANES), hn.dtype),
                        pltpu.VMEM((DISPATCH_SLOTS, chunk * ROW_TILE, LANES), hn.dtype),
                        pltpu.SemaphoreType.DMA(()), pltpu.SemaphoreType.DMA((DISPATCH_SLOTS,)),
                        pltpu.SemaphoreType.DMA((DISPATCH_SLOTS,))],
    )
    return pl.pallas_call(
        functools.partial(_dispatch_kernel, chunk=chunk, n_tiles=n_tiles),
        grid_spec=grid_spec,
        out_shape=jax.ShapeDtypeStruct((n_tiles * TM_EXPERT * ROW_TILE, LANES), hn.dtype),
        compiler_params=_cparams(("arbitrary",), 32),
        name="dispatch_rows",
    )(pad_start, n_valid_tiles, pos, pos, hn)


def _expert_kernel(te_ref, nv_ref, nx_ref, x_ref, wg_hbm, wu_hbm, wd_hbm, y_ref,
                   wg_f, wu_f, wd_f, wg_s, wu_s, wd_s, seg_ref, sems, *, layer):
    i = pl.program_id(0)
    valid = i < nv_ref[0]
    expert = te_ref[i]
    changed = jnp.logical_or(i == 0, expert != te_ref[jnp.maximum(i - 1, 0)])

    def fetch(e, slot):
        return [pltpu.make_async_copy(hbm.at[layer, e], buf.at[slot], sems.at[slot])
                for hbm, buf in ((wg_hbm, wg_f), (wu_hbm, wu_f), (wd_hbm, wd_f))]

    def swiglu_tile():
        lo, hi = _unpack_halves(_load_row_tiles(x_ref))
        x = jnp.concatenate([lo.astype(BF16), hi.astype(BF16)], axis=1)
        a = jnp.dot(x, wg_s[...], preferred_element_type=F32)
        b = jnp.dot(x, wu_s[...], preferred_element_type=F32)
        hmid = (a * _sigmoid(a) * b).astype(BF16)
        _store_row_tiles(y_ref, _pack_halves(jnp.dot(hmid, wd_s[...], preferred_element_type=F32)))

    @pl.when(i == 0)
    def _():
        seg_ref[0] = 0
        for cp in fetch(expert, 0):
            cp.start()

    @pl.when(jnp.logical_and(valid, changed))
    def _():
        slot = seg_ref[0] % 2
        for cp in fetch(expert, slot):
            cp.wait()
        nxt = nx_ref[i]

        @pl.when(nxt >= 0)
        def _():
            for cp in fetch(nxt, 1 - slot):
                cp.start()

        seg_ref[0] = seg_ref[0] + 1
        wg_s[...] = wg_f[slot].astype(BF16)
        wu_s[...] = wu_f[slot].astype(BF16)
        wd_s[...] = wd_f[slot].astype(BF16)
        swiglu_tile()

    @pl.when(jnp.logical_and(valid, jnp.logical_not(changed)))
    def _():
        swiglu_tile()

    @pl.when(jnp.logical_not(valid))
    def _():
        y_ref[...] = jnp.zeros(y_ref.shape, y_ref.dtype)


def expert_call(xs, tile_expert, n_valid_tiles, next_expert, w_gate, w_up, w_down, layer):
    p = xs.shape[0] // ROW_TILE
    d = 2 * ROW_TILE * LANES
    tm = TM_EXPERT
    f = w_gate.shape[-1]
    grid_spec = pltpu.PrefetchScalarGridSpec(
        num_scalar_prefetch=3,
        grid=(p // tm,),
        in_specs=[pl.BlockSpec((tm * ROW_TILE, LANES), lambda i, te, nv, nx: (jnp.minimum(i, nv[0] - 1), 0)),
                  pl.BlockSpec(memory_space=pl.ANY),
                  pl.BlockSpec(memory_space=pl.ANY),
                  pl.BlockSpec(memory_space=pl.ANY)],
        out_specs=pl.BlockSpec((tm * ROW_TILE, LANES), lambda i, te, nv, nx: (i, 0)),
        scratch_shapes=[pltpu.VMEM((2, d, f), F32), pltpu.VMEM((2, d, f), F32), pltpu.VMEM((2, f, d), F32),
                        pltpu.VMEM((d, f), BF16), pltpu.VMEM((d, f), BF16), pltpu.VMEM((f, d), BF16),
                        pltpu.SMEM((1,), jnp.int32), pltpu.SemaphoreType.DMA((2,))],
    )
    return pl.pallas_call(
        functools.partial(_expert_kernel, layer=layer),
        grid_spec=grid_spec,
        out_shape=jax.ShapeDtypeStruct(xs.shape, jnp.uint32),
        compiler_params=_cparams(("arbitrary",), 58),
        name="expert_swiglu",
    )(tile_expert, n_valid_tiles, next_expert, xs, w_gate, w_up, w_down)


def _combine_kernel(pos0_ref, pos1_ref, nxt0_ref, nxt1_ref, ys_ref, h_ref, w_ref, gain_ref, *rest, chunk, seq,
                    layer, q_scale):
    if seq is None:
        xn_ref, buf, sems = rest
        h2_ref = None
    else:
        (w_in_hbm, mix_ref, scale_ref, xn_ref, h2_ref, mixed_ref, qkv_ref, buf, sems, w_all, stage, stage_sems,
         carry) = rest
    i = pl.program_id(0)
    slot = i % 2

    def gather(p0_ref, p1_ref, into):
        def start(t, c):
            for j, pos_ref in enumerate((p0_ref, p1_ref)):
                pltpu.make_async_copy(_rows(ys_ref, pos_ref[0, 0, t]), _rows(buf.at[into, j], t),
                                      sems.at[into]).start(priority=j)
            return c

        lax.fori_loop(0, chunk, start, 0, unroll=ROW_DMA_UNROLL)

    @pl.when(i == 0)
    def _():
        gather(pos0_ref, pos1_ref, 0)

    @pl.when(i + 1 < pl.num_programs(0))
    def _():
        gather(nxt0_ref, nxt1_ref, 1 - slot)

    if seq is not None:
        @pl.when(i == 0)
        def _():
            cols = stage.shape[2]
            n_parts = w_all.shape[1] // cols

            def part(p):
                return pltpu.make_async_copy(w_in_hbm.at[layer, :, pl.ds(p * cols, cols)], stage.at[p % 2],
                                             stage_sems.at[p % 2])

            part(0).start()
            part(1).start()
            for p in range(n_parts):
                part(p).wait()
                w_all[:, p * cols:(p + 1) * cols] = stage[p % 2].astype(BF16)
                if p + 2 < n_parts:
                    part(p + 2).start()

    for j in range(2):
        pltpu.make_async_copy(_rows(ys_ref, 0, chunk), buf.at[slot, j], sems.at[slot]).wait()
    w = w_ref[...]
    lo0, hi0 = _unpack_halves(_load_row_tiles(buf.at[slot, 0]))
    lo1, hi1 = _unpack_halves(_load_row_tiles(buf.at[slot, 1]))
    y = jnp.concatenate([w[:, 0:1] * lo0 + w[:, 1:2] * lo1, w[:, 0:1] * hi0 + w[:, 1:2] * hi1], axis=1)
    h2 = h_ref[...] + y
    xn = _rms(h2, gain_ref[...]).astype(xn_ref.dtype)
    xn_ref[...] = xn
    if seq is not None:
        h2_ref[...] = h2
        w_pool = w_all.at[:, pl.ds(0, POOL_WIDTH)]
        mixed_ref[...] = _pool_tile(xn, w_pool, carry, (i * chunk) % seq, mix_ref, scale_ref).astype(mixed_ref.dtype)
        qkv = jnp.dot(xn, w_all[:, POOL_WIDTH:], preferred_element_type=F32)
        q_cols = lax.broadcasted_iota(jnp.int32, (1, qkv.shape[1]), 1) < ATTN_WIDTH
        qkv_ref[...] = (qkv * jnp.where(q_cols, q_scale, 1.0)).astype(qkv_ref.dtype)


IN_PROJ_STAGE_COLS = 512


def combine_call(ys, pos, h1, weights, next_gain, next_proj=None, chunk=256):
    n, d = h1.shape
    row_spec = pl.BlockSpec((chunk, d), lambda c: (c, 0))
    per_row = PLAN_CHUNK // chunk
    steps = n // chunk
    in_specs_pool, args_pool, scratch_pool = [], [], []
    layer = q_scale = None
    if next_proj is None:
        seq = None
        out_specs = [row_spec]
        out_shape = [jax.ShapeDtypeStruct((n, d), F32)]
    else:
        w_in, layer, mix_bf16, scale, seq, q_scale = next_proj
        c = POOL_WIDTH
        width = w_in.shape[-1]
        in_specs_pool = [pl.BlockSpec(memory_space=pl.ANY),
                         pl.BlockSpec(mix_bf16.shape, lambda s: (0, 0, 0)),
                         pl.BlockSpec((1, c), lambda s: (0, 0))]
        args_pool = [w_in, mix_bf16, scale.reshape(1, c)]
        scratch_pool = [pltpu.VMEM((d, width), BF16), pltpu.VMEM((2, d, IN_PROJ_STAGE_COLS), F32),
                        pltpu.SemaphoreType.DMA((2,)), pltpu.VMEM((POOL_HALO, c), F32)]
        out_specs = [row_spec, row_spec, pl.BlockSpec((chunk, c), lambda s: (s, 0)),
                     pl.BlockSpec((chunk, width - c), lambda s: (s, 0))]
        out_shape = [jax.ShapeDtypeStruct((n, d), BF16), jax.ShapeDtypeStruct((n, d), F32),
                     jax.ShapeDtypeStruct((n, c), BF16), jax.ShapeDtypeStruct((n, width - c), BF16)]

    def pos_spec(j, ahead):
        def index(c):
            c = jnp.minimum(c + ahead, steps - 1)
            return (j, c // per_row, 0, c % per_row)
        return pl.BlockSpec((None, 1, 1, chunk), index, memory_space=pltpu.SMEM)

    outs = pl.pallas_call(
        functools.partial(_combine_kernel, chunk=chunk, seq=seq, layer=layer, q_scale=q_scale),
        grid=(steps,),
        in_specs=[pos_spec(0, 0), pos_spec(1, 0), pos_spec(0, 1), pos_spec(1, 1),
                  pl.BlockSpec(memory_space=pl.ANY),
                  row_spec,
                  pl.BlockSpec((chunk, ROUTER_LANES), lambda c: (c, 0)),
                  pl.BlockSpec((1, d), lambda c: (0, 0))] + in_specs_pool,
        out_specs=out_specs,
        out_shape=out_shape,
        scratch_shapes=[pltpu.VMEM((2, 2, chunk * ROW_TILE, LANES), jnp.uint32),
                        pltpu.SemaphoreType.DMA((2,))] + scratch_pool,
        compiler_params=_cparams(("arbitrary",), 58),
        name="combine_rows",
    )(pos, pos, pos, pos, ys, h1, weights, next_gain.reshape(1, d), *args_pool)
    return outs[0] if next_proj is None else tuple(outs)


def kernel(x, rel_bias_table, norm_mix_gain, w_in, w_merge_gate, b_merge_gate, pool_mix, pool_scale, w_up_pool, lambda_q1, lambda_k1, lambda_q2, lambda_k2, subln_gain, w_up_attn, w_out, norm_ffn_gain, w_router_group, b_router_group, w_router_expert, b_router_expert, w_expert_gate, w_expert_up, w_expert_down, final_norm_gain):
    batch, seq, d = x.shape
    depth = w_in.shape[0]
    n = batch * seq
    n_tiles = (2 * n + N_EXPERTS * (TM_EXPERT - 1)) // TM_EXPERT + 1

    bias_tiles = bias_tiles_call(rel_bias_table)
    h = x.reshape(n, d)
    q_scale = LOG2E * DIFF_HEAD_DIM ** -0.5
    xn, mixed = first_pool_call(h, norm_mix_gain[0], w_in, pool_mix[0].astype(BF16), pool_scale[0], seq)
    qkv = proj_call(xn, w_in, 0, 1, 3, ATTN_WIDTH, BF16, first_block_scale=q_scale, name="in_proj_qkv")

    for l in range(depth):
        lambda_init = 0.8 - 0.6 * math.exp(-0.3 * l)
        lam_params = jnp.stack([lambda_q1[l], lambda_k1[l], lambda_q2[l], lambda_k2[l]])
        attn = attn_call(qkv.reshape(batch, seq, 3 * ATTN_WIDTH), bias_tiles, lam_params, subln_gain[l],
                         lambda_init, batch, seq).reshape(n, ATTN_WIDTH)
        z = merge_call(xn, mixed, attn, w_merge_gate, b_merge_gate.reshape(depth, 1, -1), w_up_pool, w_up_attn, l)

        w_router = jnp.concatenate(
            [w_router_group[l], jnp.transpose(w_router_expert[l], (1, 0, 2)).reshape(d, N_EXPERTS),
             jnp.zeros((d, ROUTER_LANES - N_GROUPS - N_EXPERTS), F32)], axis=1).astype(BF16)
        b_router = jnp.concatenate(
            [b_router_group[l], b_router_expert[l].reshape(-1),
             jnp.zeros((ROUTER_LANES - N_GROUPS - N_EXPERTS,), F32)]).reshape(1, ROUTER_LANES)
        h1, hn, logits = outproj_call(z, w_out, l, h, norm_ffn_gain[l], w_router, b_router)

        weights, pos, meta = route_plan_call(logits, n_tiles)
        tile_expert = meta[META_TILE_EXPERT, :n_tiles]
        next_expert = meta[META_NEXT_EXPERT, :n_tiles]
        pad_start = meta[META_PAD_START, :N_EXPERTS]
        n_valid = meta[META_N_VALID, :1]
        xs = dispatch_call(hn, pos, pad_start, n_valid, n_tiles)
        ys = expert_call(xs, tile_expert, n_valid, next_expert, w_expert_gate, w_expert_up, w_expert_down, l)
        if l == depth - 1:
            out = combine_call(ys, pos, h1, weights, final_norm_gain)
        else:
            next_proj = (w_in, l + 1, pool_mix[l + 1].astype(BF16), pool_scale[l + 1], seq, q_scale)
            xn, h, mixed, qkv = combine_call(ys, pos, h1, weights, norm_mix_gain[l + 1], next_proj)

    return out.reshape(batch, seq, d)
```

```python
import functools
import math

import numpy as np
import jax
import jax.numpy as jnp
from jax import lax
from jax.experimental import pallas as pl
from jax.experimental.pallas import tpu as pltpu

F32 = jnp.float32
BF16 = jnp.bfloat16

POOL_WIDTH = 1024
POOL_WINDOWS = (2, 4, 8, 16)
POOL_GROUP_DIM = 256
POOL_HALO = 16
DIFF_HEADS = 8
DIFF_HEAD_DIM = 64
DIFF_V_DIM = 128
ATTN_WIDTH = 1024
REL_BUCKETS = 32
REL_MAX_DISTANCE = 128
N_GROUPS = 4
EXPERTS_PER_GROUP = 8
N_EXPERTS = 32
EPS = 1e-6
NEG_INF = -1e30
LOG2E = 1.4426950408889634

ROUTER_LANES = 128
TQ = 256
TQ_BLOCK = 256
TK = 256
KV_SUPER = 2
V_ROWS = DIFF_V_DIM + 16
HEADS_PER_STEP = 8
TM_EXPERT = 256
LANES = 128
ROW_DMA_UNROLL = 8
MIB = 1024 * 1024


def _cparams(sem, vmem_mib):
    return pltpu.CompilerParams(dimension_semantics=sem, vmem_limit_bytes=vmem_mib * MIB)


def _rms(xf, gain):
    ms = jnp.mean(xf * xf, axis=-1, keepdims=True)
    return xf * lax.rsqrt(ms + EPS) * gain


def _proj_kernel(x_ref, w_ref, o_ref, w_s, *, first_block_scale):
    @pl.when(pl.program_id(1) == 0)
    def _():
        w_s[...] = w_ref[...].astype(BF16)

    acc = jnp.dot(x_ref[...], w_s[...], preferred_element_type=F32)
    if first_block_scale is not None:
        acc = acc * jnp.where(pl.program_id(0) == 0, first_block_scale, 1.0).astype(F32)
    o_ref[...] = acc.astype(o_ref.dtype)


def proj_call(x, w, layer, col_block0, n_col_blocks, tn, out_dtype, first_block_scale=None, tm=1024, name="proj"):
    n, k = x.shape
    return pl.pallas_call(
        functools.partial(_proj_kernel, first_block_scale=first_block_scale),
        grid=(n_col_blocks, n // tm),
        in_specs=[pl.BlockSpec((tm, k), lambda j, m: (m, 0)),
                  pl.BlockSpec((None, k, tn), lambda j, m: (layer, 0, col_block0 + j))],
        out_specs=pl.BlockSpec((tm, tn), lambda j, m: (m, j)),
        out_shape=jax.ShapeDtypeStruct((n, n_col_blocks * tn), out_dtype),
        scratch_shapes=[pltpu.VMEM((k, tn), BF16)],
        compiler_params=_cparams(("arbitrary", "arbitrary"), 48),
        name=name,
    )(x, w)


def _first_pool_kernel(x_ref, gain_ref, w_ref, mix_ref, scale_ref, xn_ref, o_ref, w_s, carry, *, tm, seq):
    m = pl.program_id(0)

    @pl.when(m == 0)
    def _():
        w_s[...] = w_ref[...].astype(BF16)

    xn = _rms(x_ref[...], gain_ref[...]).astype(BF16)
    xn_ref[...] = xn
    o_ref[...] = _pool_tile(xn, w_s, carry, (m * tm) % seq, mix_ref, scale_ref).astype(o_ref.dtype)


def _pool_tile(xn, w_s, carry, row0, mix_ref, scale_ref):
    tm = xn.shape[0]
    cur = jnp.dot(xn, w_s[...], preferred_element_type=F32)
    prev = jnp.where(row0 == 0, 0.0, carry[...])
    carry[...] = cur[tm - POOL_HALO:]
    pos = row0 + lax.broadcasted_iota(jnp.int32, (tm, 1), 0)
    outs = []
    for g, w in enumerate(POOL_WINDOWS):
        sl = slice(g * POOL_GROUP_DIM, (g + 1) * POOL_GROUP_DIM)
        x = jnp.concatenate([prev[:, sl], cur[:, sl]], axis=0)
        s, d = x, 1
        while d < w:
            s = s[:-d] + s[d:]
            d *= 2
        start = POOL_HALO - w + 1
        wsum = s[start:start + tm]
        count = jnp.minimum(pos + 1, w).astype(F32)
        pooled = wsum / count - cur[:, sl]
        mixed = jnp.dot(pooled.astype(BF16), mix_ref[g], preferred_element_type=F32)
        outs.append(mixed * scale_ref[:, sl])
    return jnp.concatenate(outs, axis=1)


def first_pool_call(x, gain, w_in, mix_bf16, scale, seq, tm=512):
    n, d = x.shape
    c = POOL_WIDTH
    row_spec = pl.BlockSpec((tm, c), lambda m: (m, 0))
    x_spec = pl.BlockSpec((tm, d), lambda m: (m, 0))
    return pl.pallas_call(
        functools.partial(_first_pool_kernel, tm=tm, seq=seq),
        grid=(n // tm,),
        in_specs=[x_spec,
                  pl.BlockSpec((1, d), lambda m: (0, 0)),
                  pl.BlockSpec((None, d, c), lambda m: (0, 0, 0), pipeline_mode=pl.Buffered(1)),
                  pl.BlockSpec(mix_bf16.shape, lambda m: (0, 0, 0)),
                  pl.BlockSpec((1, c), lambda m: (0, 0))],
        out_specs=[x_spec, row_spec],
        out_shape=[jax.ShapeDtypeStruct((n, d), BF16), jax.ShapeDtypeStruct((n, c), BF16)],
        scratch_shapes=[pltpu.VMEM((d, c), BF16), pltpu.VMEM((POOL_HALO, c), F32)],
        compiler_params=_cparams(("arbitrary",), 48),
        name="first_norm_pool",
    )(x, gain.reshape(1, d), w_in, mix_bf16, scale.reshape(1, c))


N_BIAS_TILES = 4


def _bucket_tiles():
    kk = np.arange(TK)[:, None]
    qq = np.arange(TQ_BLOCK)[None, :]
    tiles = []
    for rel in (0, 1, 2, -1):
        n = rel * TK + qq - kk
        max_exact = REL_BUCKETS // 2
        nf = np.maximum(n, 1).astype(np.float64)
        large = max_exact + (np.log(nf / max_exact) / math.log(REL_MAX_DISTANCE / max_exact)
                             * (REL_BUCKETS - max_exact)).astype(np.int64)
        large = np.minimum(large, REL_BUCKETS - 1)
        bucket = np.where(n < max_exact, n, large)
        tiles.append(np.where(n < 0, -1, bucket))
    return np.stack(tiles).astype(np.int32)


N_NEAR_TILES = 2


def _bias_kernel(table_ref, bucket_ref, o_ref):
    h = pl.program_id(0)
    near = bucket_ref[0:N_NEAR_TILES]
    acc = jnp.full(near.shape, NEG_INF, F32)
    for b in range(REL_BUCKETS):
        acc = jnp.where(near == b, table_ref[b * DIFF_HEADS + h] * LOG2E, acc)
    o_ref[0:N_NEAR_TILES] = acc
    last = REL_BUCKETS - 1
    far_bias = table_ref[last * DIFF_HEADS + h] * LOG2E
    o_ref[N_NEAR_TILES:] = jnp.where(bucket_ref[N_NEAR_TILES:] == last, far_bias, NEG_INF)


def bias_tiles_call(rel_table):
    tiles = _bucket_tiles()
    assert set(np.unique(tiles[N_NEAR_TILES:]).tolist()) <= {-1, REL_BUCKETS - 1}
    bucket = jnp.asarray(tiles)
    return pl.pallas_call(
        _bias_kernel,
        grid=(DIFF_HEADS,),
        in_specs=[pl.BlockSpec(memory_space=pltpu.SMEM),
                  pl.BlockSpec(bucket.shape, lambda h: (0, 0, 0))],
        out_specs=pl.BlockSpec((None,) + bucket.shape, lambda h: (h, 0, 0, 0)),
        out_shape=jax.ShapeDtypeStruct((DIFF_HEADS,) + bucket.shape, F32),
        compiler_params=_cparams(("parallel",), 32),
        name="rel_bias_tiles",
    )(rel_table.reshape(-1), bucket)


def _attn_kernel(q_ref, k_ref, v_ref, bias_ref, lam_ref, gain_ref, o_ref, vt_ref, s_ref, acc_ref,
                 *, lambda_init, n_super):
    qi = pl.program_id(2)
    tks = KV_SUPER * TK
    heads = range(HEADS_PER_STEP)

    def block_diag(ref, hh):
        q = ref[:, hh * DIFF_V_DIM:(hh + 1) * DIFF_V_DIM]
        lane = lax.broadcasted_iota(jnp.int32, q.shape, 1)
        zero = jnp.zeros_like(q)
        return jnp.concatenate([jnp.where(lane < DIFF_HEAD_DIM, q, zero),
                                jnp.where(lane >= DIFF_HEAD_DIM, q, zero)], axis=0)

    def scores(t, hh, qd, tile):
        kb = k_ref[pl.ds(pl.multiple_of(t * tks, tks), tks), hh * DIFF_V_DIM:(hh + 1) * DIFF_V_DIM]
        s = lax.dot_general(kb, qd, (((1,), (1,)), ((), ())), preferred_element_type=F32)
        parts = []
        for u in range(KV_SUPER):
            tiles = []
            for c in range(TQ // TQ_BLOCK):
                rel = tile * (TQ // TQ_BLOCK) + c - (t * KV_SUPER + u)
                tiles.append(bias_ref[hh, jnp.where(rel < 0, N_BIAS_TILES - 1, jnp.minimum(rel, 2))])
            parts.append(s[u * TK:(u + 1) * TK] + jnp.concatenate(tiles + tiles, axis=1))
        s = jnp.concatenate(parts, axis=0)
        s_ref[hh] = s
        return jnp.max(s, axis=0, keepdims=True)

    @pl.when(qi == 0)
    def _():
        extra = (lax.broadcasted_iota(jnp.int32, (V_ROWS - DIFF_V_DIM, tks), 0) == 0).astype(BF16)
        for hh in heads:
            cols = slice(hh * DIFF_V_DIM, (hh + 1) * DIFF_V_DIM)
            for c in range(n_super):
                vt = v_ref[c * tks:(c + 1) * tks, cols].astype(F32).T.astype(BF16)
                vt_ref[hh, c] = jnp.concatenate([vt, extra], axis=0)

    qds = [block_diag(q_ref, hh) for hh in heads]

    def softmax_step(t, hh, m_prev, m_cur):
        alpha = jnp.exp2(m_prev - m_cur)
        p = jnp.exp2(s_ref[hh] - m_cur)
        pv = jnp.dot(vt_ref[hh, t], p.astype(BF16), preferred_element_type=F32)
        acc_ref[hh] = acc_ref[hh] * alpha + pv

    last = ((qi + 1) * TQ - 1) // tks
    acc_ref[...] = jnp.zeros(acc_ref.shape, F32)
    neg = jnp.full((1, 2 * TQ), NEG_INF, F32)
    init = tuple((neg, jnp.maximum(neg, scores(0, hh, qds[hh], qi))) for hh in heads)

    def body(t, carry):
        out = []
        for hh in heads:
            m_prev, m_cur = carry[hh]
            softmax_step(t, hh, m_prev, m_cur)
            m_next = jnp.maximum(m_cur, scores(t + 1, hh, qds[hh], qi))
            out.append((m_cur, m_next))
        return tuple(out)

    carry = lax.fori_loop(0, last, body, init)

    lam_p = lam_ref[...]
    lam = (jnp.exp(jnp.sum(lam_p[0:1] * lam_p[1:2], axis=1, keepdims=True))
           - jnp.exp(jnp.sum(lam_p[2:3] * lam_p[3:4], axis=1, keepdims=True)) + lambda_init)
    for hh in heads:
        m_prev, m_cur = carry[hh]
        softmax_step(last, hh, m_prev, m_cur)
        acc = acc_ref[hh, :DIFF_V_DIM, :]
        l = acc_ref[hh, DIFF_V_DIM:DIFF_V_DIM + 1, :]
        o = acc[:, :TQ] / l[:, :TQ] - lam * (acc[:, TQ:] / l[:, TQ:])
        ms = jnp.mean(o * o, axis=0, keepdims=True)
        y = o * lax.rsqrt(ms + EPS) * gain_ref[...] * (1.0 - lambda_init)
        o_ref[:, hh * DIFF_V_DIM:(hh + 1) * DIFF_V_DIM] = y.T.astype(o_ref.dtype)


def attn_call(qkv, bias_tiles, lam_params, subln_gain, lambda_init, batch, seq):
    tks = KV_SUPER * TK
    n_super = seq // tks
    hps = HEADS_PER_STEP
    width = hps * DIFF_V_DIM
    groups = DIFF_HEADS // hps
    n_q = seq // TQ
    return pl.pallas_call(
        functools.partial(_attn_kernel, lambda_init=lambda_init, n_super=n_super),
        grid=(batch, groups, n_q),
        in_specs=[pl.BlockSpec((None, TQ, width), lambda b, g, i: (b, i, g)),
                  pl.BlockSpec((None, seq, width), lambda b, g, i: (b, 0, groups + g), pipeline_mode=pl.Buffered(1)),
                  pl.BlockSpec((None, seq, width), lambda b, g, i: (b, 0, 2 * groups + g),
                               pipeline_mode=pl.Buffered(1)),
                  pl.BlockSpec((hps, N_BIAS_TILES, TK, TQ_BLOCK), lambda b, g, i: (g, 0, 0, 0),
                               pipeline_mode=pl.Buffered(1)),
                  pl.BlockSpec((4, DIFF_HEAD_DIM), lambda b, g, i: (0, 0)),
                  pl.BlockSpec((DIFF_V_DIM, 1), lambda b, g, i: (0, 0))],
        out_specs=pl.BlockSpec((None, TQ, width), lambda b, g, i: (b, i, g)),
        out_shape=jax.ShapeDtypeStruct((batch, seq, ATTN_WIDTH), BF16),
        scratch_shapes=[pltpu.VMEM((hps, n_super, V_ROWS, tks), BF16),
                        pltpu.VMEM((hps, tks, 2 * TQ), F32),
                        pltpu.VMEM((hps, V_ROWS, 2 * TQ), F32)],
        compiler_params=_cparams(("parallel", "parallel", "arbitrary"), 56),
        name="diff_attention",
    )(qkv, qkv, qkv, bias_tiles, lam_params, subln_gain.reshape(DIFF_V_DIM, 1))


def _sigmoid(x):
    return 1.0 / (1.0 + jnp.exp(-x))


def _merge_kernel(xn_ref, mixed_ref, attn_ref, wgp_ref, wga_ref, bgp_ref, bga_ref, wup_ref, wua_ref, z_ref,
                  wgp_s, wga_s, wup_s, wua_s):
    @pl.when(pl.program_id(1) == 0)
    def _():
        wgp_s[...] = wgp_ref[...].astype(BF16)
        wga_s[...] = wga_ref[...].astype(BF16)
        wup_s[...] = wup_ref[...].astype(BF16)
        wua_s[...] = wua_ref[...].astype(BF16)

    xn = xn_ref[...]
    g_pool = _sigmoid(jnp.dot(xn, wgp_s[...], preferred_element_type=F32) + bgp_ref[...])
    g_attn = _sigmoid(jnp.dot(xn, wga_s[...], preferred_element_type=F32) + bga_ref[...])
    y_pool = jnp.dot(mixed_ref[...], wup_s[...], preferred_element_type=F32)
    y_attn = jnp.dot(attn_ref[...], wua_s[...], preferred_element_type=F32)
    z_ref[...] = (g_pool * y_pool + g_attn * y_attn).astype(z_ref.dtype)


def merge_call(xn, mixed, attn, w_gate, b_gate, w_up_pool, w_up_attn, layer, tm=1024, tn=512):
    n, d = xn.shape
    nb = d // tn
    return pl.pallas_call(
        _merge_kernel,
        grid=(nb, n // tm),
        in_specs=[pl.BlockSpec((tm, d), lambda j, m: (m, 0)),
                  pl.BlockSpec((tm, POOL_WIDTH), lambda j, m: (m, 0)),
                  pl.BlockSpec((tm, ATTN_WIDTH), lambda j, m: (m, 0)),
                  pl.BlockSpec((None, d, tn), lambda j, m: (layer, 0, j)),
                  pl.BlockSpec((None, d, tn), lambda j, m: (layer, 0, nb + j)),
                  pl.BlockSpec((None, 1, tn), lambda j, m: (layer, 0, j)),
                  pl.BlockSpec((None, 1, tn), lambda j, m: (layer, 0, nb + j)),
                  pl.BlockSpec((None, POOL_WIDTH, tn), lambda j, m: (layer, 0, j)),
                  pl.BlockSpec((None, ATTN_WIDTH, tn), lambda j, m: (layer, 0, j))],
        out_specs=pl.BlockSpec((tm, tn), lambda j, m: (m, j)),
        out_shape=jax.ShapeDtypeStruct((n, d), BF16),
        scratch_shapes=[pltpu.VMEM((d, tn), BF16), pltpu.VMEM((d, tn), BF16),
                        pltpu.VMEM((POOL_WIDTH, tn), BF16), pltpu.VMEM((ATTN_WIDTH, tn), BF16)],
        compiler_params=_cparams(("arbitrary", "arbitrary"), 56),
        name="gated_merge",
    )(xn, mixed, attn, w_gate, w_gate, b_gate, b_gate, w_up_pool, w_up_attn)


HIGH_HALF = 0xFFFF0000


def _pack_halves(x):
    c = x.shape[1] // 2
    lo = lax.bitcast_convert_type(x[:, :c].astype(BF16).astype(F32), jnp.uint32)
    hi = lax.bitcast_convert_type(x[:, c:].astype(BF16).astype(F32), jnp.uint32)
    return (lo >> 16) | hi


def _unpack_halves(p):
    lo = lax.bitcast_convert_type(p << 16, F32)
    hi = lax.bitcast_convert_type(p & jnp.uint32(HIGH_HALF), F32)
    return lo, hi


ROW_TILE = 8


def _store_row_tiles(ref, packed):
    m = packed.shape[0]
    for s in range(ROW_TILE):
        ref[pl.ds(s, m, stride=ROW_TILE), :] = packed[:, s * LANES:(s + 1) * LANES]


def _load_row_tiles(ref):
    m = ref.shape[0] // ROW_TILE
    return jnp.concatenate([ref[pl.ds(s, m, stride=ROW_TILE), :] for s in range(ROW_TILE)], axis=1)


def _rows(ref, first, count=1):
    return ref.at[pl.ds(pl.multiple_of(first * ROW_TILE, ROW_TILE), count * ROW_TILE)]


ROUTE_E0, ROUTE_E1, ROUTE_W0, ROUTE_W1 = 0, 1, 2, 3


def _outproj_kernel(z_ref, w_ref, h_ref, gain_ref, wr_ref, br_ref, h1_ref, hn_ref, route_ref, cnt_ref, w_s, logit_s):
    m = pl.program_id(0)
    tiles = pl.num_programs(0) - 1

    def project():
        h1 = h_ref[...] + jnp.dot(z_ref[...], w_s[...], preferred_element_type=F32)
        h1_ref[...] = h1
        hn = _rms(h1, gain_ref[...])
        _store_row_tiles(hn_ref, _pack_halves(hn))
        logit_s[...] = jnp.dot(hn.astype(BF16), wr_ref[...], preferred_element_type=F32) + br_ref[...]

    def route_previous():
        logits = logit_s[...]
        lane = lax.broadcasted_iota(jnp.int32, logits.shape, 1).astype(F32)
        e0, e1, w0, w1 = _route(logits, lane)
        route_ref[...] = jnp.where(lane == ROUTE_E0, e0, jnp.where(lane == ROUTE_E1, e1,
                                   jnp.where(lane == ROUTE_W0, w0, jnp.where(lane == ROUTE_W1, w1, 0.0))))
        onehot = jnp.where(lane == e0, 1.0, 0.0) + jnp.where(lane == e1, 1.0, 0.0)
        cnt_ref[0:1] = cnt_ref[0:1] + jnp.sum(onehot, axis=0, keepdims=True)

    @pl.when(m == 0)
    def _():
        w_s[...] = w_ref[...].astype(BF16)
        cnt_ref[...] = jnp.zeros(cnt_ref.shape, F32)
        project()

    @pl.when(jnp.logical_and(m > 0, m < tiles))
    def _():
        route_previous()
        project()

    @pl.when(m == tiles)
    def _():
        route_previous()


def outproj_call(z, w_out, layer, h, gain, w_router, b_router, tm=256):
    n, d = h.shape
    tiles = n // tm

    def tile(m):
        return (jnp.minimum(m, tiles - 1), 0)

    def routed(m):
        return (jnp.maximum(m - 1, 0), 0)

    return pl.pallas_call(
        _outproj_kernel,
        grid=(tiles + 1,),
        in_specs=[pl.BlockSpec((tm, d), tile),
                  pl.BlockSpec((None, d, d), lambda m: (layer, 0, 0), pipeline_mode=pl.Buffered(1)),
                  pl.BlockSpec((tm, d), tile),
                  pl.BlockSpec((1, d), lambda m: (0, 0)),
                  pl.BlockSpec((d, ROUTER_LANES), lambda m: (0, 0)),
                  pl.BlockSpec((1, ROUTER_LANES), lambda m: (0, 0))],
        out_specs=[pl.BlockSpec((tm, d), tile),
                   pl.BlockSpec((tm * ROW_TILE, LANES), tile),
                   pl.BlockSpec((tm, ROUTER_LANES), routed),
                   pl.BlockSpec((8, ROUTER_LANES), lambda m: (0, 0))],
        out_shape=[jax.ShapeDtypeStruct((n, d), F32),
                   jax.ShapeDtypeStruct((n * ROW_TILE, LANES), jnp.uint32),
                   jax.ShapeDtypeStruct((n, ROUTER_LANES), F32),
                   jax.ShapeDtypeStruct((8, ROUTER_LANES), F32)],
        scratch_shapes=[pltpu.VMEM((d, d), BF16), pltpu.VMEM((tm, ROUTER_LANES), F32)],
        compiler_params=_cparams(("arbitrary",), 56),
        name="out_proj_norm_router",
    )(z, w_out, h, gain.reshape(1, d), w_router, b_router)


def _route(x, lane):
    big = float(ROUTER_LANES)

    def first_argmax(vals):
        top = jnp.max(vals, axis=1, keepdims=True)
        idx = jnp.min(jnp.where(vals == top, lane, big), axis=1, keepdims=True)
        return top, idx

    gmask = lane < N_GROUPS
    g_top, g_sel = first_argmax(jnp.where(gmask, x, -jnp.inf))
    g_weight = 1.0 / jnp.sum(jnp.where(gmask, jnp.exp(x - g_top), 0.0), axis=1, keepdims=True)
    lo = N_GROUPS + EXPERTS_PER_GROUP * g_sel
    e_vals = jnp.where((lane >= lo) & (lane < lo + EXPERTS_PER_GROUP), x, -jnp.inf)
    v1, i1 = first_argmax(e_vals)
    v2, i2 = first_argmax(jnp.where(lane == i1, -jnp.inf, e_vals))
    t = jnp.exp(v2 - v1)
    return i1 - N_GROUPS, i2 - N_GROUPS, g_weight / (1.0 + t), g_weight * t / (1.0 + t)


PLAN_BLOCK = 1024
PLAN_CHUNK = 512
META_TILE_EXPERT, META_NEXT_EXPERT, META_PAD_START, META_N_VALID = 0, 1, 2, 3


def _plan_kernel(route_ref, cnt_ref, pos_ref, meta_ref, tri_s, base_s, off_s):
    blk = pl.program_id(0)
    tb = route_ref.shape[0]
    lane = lax.broadcasted_iota(jnp.int32, (tb, ROUTER_LANES), 1).astype(F32)
    record = route_ref[...]
    e0 = jnp.sum(jnp.where(lane == ROUTE_E0, record, 0.0), axis=1, keepdims=True)
    e1 = jnp.sum(jnp.where(lane == ROUTE_E1, record, 0.0), axis=1, keepdims=True)
    onehot = jnp.where(lane == e0, 1.0, 0.0) + jnp.where(lane == e1, 1.0, 0.0)
    block_counts = jnp.sum(onehot, axis=0, keepdims=True)

    @pl.when(blk == 0)
    def _():
        r = lax.broadcasted_iota(jnp.int32, (tb, tb), 0)
        c = lax.broadcasted_iota(jnp.int32, (tb, tb), 1)
        tri_s[...] = jnp.where(c < r, 1.0, 0.0).astype(BF16)
        lane1 = lane[0:1]
        cnt = cnt_ref[0:1]
        tiles = jnp.floor((cnt + (TM_EXPERT - 1)) * (1.0 / TM_EXPERT))
        ri = lax.broadcasted_iota(jnp.int32, (ROUTER_LANES, ROUTER_LANES), 0)
        ci = lax.broadcasted_iota(jnp.int32, (ROUTER_LANES, ROUTER_LANES), 1)
        upper = jnp.where(ri <= ci, 1.0, 0.0).astype(BF16)
        ends = jnp.dot(jnp.broadcast_to(tiles, (8, ROUTER_LANES)).astype(BF16), upper,
                       preferred_element_type=F32)[0:1]
        off_s[...] = (ends - tiles) * TM_EXPERT
        base_s[...] = jnp.zeros(base_s.shape, F32)

        def pick(vec, e):
            return jnp.sum(jnp.where(lane1 == e, vec, 0.0), axis=1, keepdims=True)

        end_of = [pick(ends, e) for e in range(N_EXPERTS)]
        n_valid = end_of[N_EXPERTS - 1]

        def segment_of(tile):
            return sum(jnp.where(tile >= end_e, 1.0, 0.0) for end_e in end_of)

        tile_expert = jnp.where(lane1 < n_valid, segment_of(lane1), segment_of(n_valid - 1.0))
        following = sum(jnp.where(tile_expert == e, end_of[e], 0.0) for e in range(N_EXPERTS))
        next_expert = jnp.where(following < n_valid, segment_of(following), -1.0)
        rows = [tile_expert, next_expert, off_s[...] + cnt, jnp.broadcast_to(n_valid, (1, ROUTER_LANES))]
        rows += [jnp.zeros((1, ROUTER_LANES), F32)] * (meta_ref.shape[0] - len(rows))
        meta_ref[...] = jnp.concatenate(rows, axis=0).astype(jnp.int32)

    before = jnp.dot(tri_s[...], onehot.astype(BF16), preferred_element_type=F32)
    row = before + base_s[...] + off_s[...]
    base_s[...] = base_s[...] + block_counts
    eye = (lax.broadcasted_iota(jnp.int32, (LANES, LANES), 0)
           == lax.broadcasted_iota(jnp.int32, (LANES, LANES), 1))
    for j, e in enumerate((e0, e1)):
        col = jnp.sum(jnp.where(lane == e, row, 0.0), axis=1, keepdims=True)
        for q in range(tb // PLAN_CHUNK):
            parts = []
            for g in range(PLAN_CHUNK // LANES):
                t0 = q * PLAN_CHUNK + g * LANES
                square = jnp.broadcast_to(col[t0:t0 + LANES], (LANES, LANES))
                parts.append(jnp.sum(jnp.where(eye, square, 0.0), axis=0, keepdims=True))
            pos_ref[j, q] = jnp.concatenate(parts, axis=1).astype(jnp.int32)


def plan_call(route, counts, n_tiles):
    n = route.shape[0]
    tb = PLAN_BLOCK
    chunks = tb // PLAN_CHUNK
    assert n_tiles <= ROUTER_LANES and N_EXPERTS <= ROUTER_LANES
    return pl.pallas_call(
        _plan_kernel,
        grid=(n // tb,),
        in_specs=[pl.BlockSpec((tb, ROUTER_LANES), lambda b: (b, 0)),
                  pl.BlockSpec((8, ROUTER_LANES), lambda b: (0, 0))],
        out_specs=[pl.BlockSpec((2, chunks, 1, PLAN_CHUNK), lambda b: (0, b, 0, 0)),
                   pl.BlockSpec((8, ROUTER_LANES), lambda b: (0, 0))],
        out_shape=[jax.ShapeDtypeStruct((2, n // PLAN_CHUNK, 1, PLAN_CHUNK), jnp.int32),
                   jax.ShapeDtypeStruct((8, ROUTER_LANES), jnp.int32)],
        scratch_shapes=[pltpu.VMEM((tb, tb), BF16), pltpu.VMEM((1, ROUTER_LANES), F32),
                        pltpu.VMEM((1, ROUTER_LANES), F32)],
        compiler_params=_cparams(("arbitrary",), 32),
        name="dispatch_plan",
    )(route, counts)


DISPATCH_SLOTS = 3


def _dispatch_kernel(pad_ref, nv_ref, pos0_ref, pos1_ref, src_hbm, dst_ref, zero_buf, ring, sem, in_sems, row_sems,
                     *, chunk, n_tiles):
    i = pl.program_id(0)
    steps = pl.num_programs(0)
    slot = i % DISPATCH_SLOTS

    def load(block, into):
        return pltpu.make_async_copy(_rows(src_hbm, block * chunk, chunk), ring.at[into], in_sems.at[into])

    def wait_rows(of_slot):
        for j in range(2):
            pltpu.make_async_copy(ring.at[of_slot], _rows(dst_ref, 0, chunk), row_sems.at[of_slot]).wait()

    @pl.when(i == 0)
    def _():
        load(0, 0).start()
        zero_buf[...] = jnp.zeros(zero_buf.shape, zero_buf.dtype)

        def fill(e):
            return pltpu.make_async_copy(zero_buf, _rows(dst_ref, pad_ref[e], TM_EXPERT), sem)

        for e in range(N_EXPERTS):
            fill(e).start()
        for e in range(N_EXPERTS):
            fill(e).wait()

        def fill_tile(i):
            return pltpu.make_async_copy(zero_buf, _rows(dst_ref, i * TM_EXPERT, TM_EXPERT), sem)

        def start_tile(i, c):
            fill_tile(i).start()
            return c

        def wait_tile(i, c):
            fill_tile(i).wait()
            return c

        lax.fori_loop(nv_ref[0], n_tiles, start_tile, 0)
        lax.fori_loop(nv_ref[0], n_tiles, wait_tile, 0)

    nxt = (i + 1) % DISPATCH_SLOTS

    @pl.when(i >= DISPATCH_SLOTS - 1)
    def _():
        wait_rows(nxt)

    @pl.when(i + 1 < steps)
    def _():
        load(i + 1, nxt).start()

    load(i, slot).wait()
    src_ref = ring.at[slot]

    def start(t, c):
        for j, pos_ref in enumerate((pos0_ref, pos1_ref)):
            pltpu.make_async_copy(_rows(src_ref, t), _rows(dst_ref, pos_ref[0, 0, t]),
                                  row_sems.at[slot]).start(priority=j)
        return c

    lax.fori_loop(0, chunk, start, 0, unroll=ROW_DMA_UNROLL)

    @pl.when(i == steps - 1)
    def _():
        for back in range(DISPATCH_SLOTS - 1):
            @pl.when(i - back >= 0)
            def _():
                wait_rows((i - back) % DISPATCH_SLOTS)


def dispatch_call(hn, pos, pad_start, n_valid_tiles, n_tiles):
    n = hn.shape[0] // ROW_TILE
    chunk = PLAN_CHUNK
    grid_spec = pltpu.PrefetchScalarGridSpec(
        num_scalar_prefetch=2,
        grid=(n // chunk,),
        in_specs=[pl.BlockSpec((None, 1, 1, chunk), lambda i, pad, nv: (0, i, 0, 0), memory_space=pltpu.SMEM),
                  pl.BlockSpec((None, 1, 1, chunk), lambda i, pad, nv: (1, i, 0, 0), memory_space=pltpu.SMEM),
                  pl.BlockSpec(memory_space=pl.ANY)],
        out_specs=pl.BlockSpec(memory_space=pl.ANY),
        scratch_shapes=[pltpu.VMEM((TM_EXPERT * ROW_TILE, LANES), hn.dtype),
                        pltpu.VMEM((DISPATCH_SLOTS, chunk * ROW_TILE, LANES), hn.dtype),
                        pltpu.SemaphoreType.DMA(()), pltpu.SemaphoreType.DMA((DISPATCH_SLOTS,)),
                        pltpu.SemaphoreType.DMA((DISPATCH_SLOTS,))],
    )
    return pl.pallas_call(
        functools.partial(_dispatch_kernel, chunk=chunk, n_tiles=n_tiles),
        grid_spec=grid_spec,
        out_shape=jax.ShapeDtypeStruct((n_tiles * TM_EXPERT * ROW_TILE, LANES), hn.dtype),
        compiler_params=_cparams(("arbitrary",), 32),
        name="dispatch_rows",
    )(pad_start, n_valid_tiles, pos, pos, hn)


def _expert_kernel(te_ref, nv_ref, nx_ref, x_ref, wg_hbm, wu_hbm, wd_hbm, y_ref,
                   wg_f, wu_f, wd_f, wg_s, wu_s, wd_s, seg_ref, sems, *, layer):
    i = pl.program_id(0)
    valid = i < nv_ref[0]
    expert = te_ref[i]
    changed = jnp.logical_or(i == 0, expert != te_ref[jnp.maximum(i - 1, 0)])

    def fetch(e, slot):
        return [pltpu.make_async_copy(hbm.at[layer, e], buf.at[slot], sems.at[slot])
                for hbm, buf in ((wg_hbm, wg_f), (wu_hbm, wu_f), (wd_hbm, wd_f))]

    def swiglu_tile():
        lo, hi = _unpack_halves(_load_row_tiles(x_ref))
        x = jnp.concatenate([lo.astype(BF16), hi.astype(BF16)], axis=1)
        a = jnp.dot(x, wg_s[...], preferred_element_type=F32)
        b = jnp.dot(x, wu_s[...], preferred_element_type=F32)
        hmid = (a * _sigmoid(a) * b).astype(BF16)
        _store_row_tiles(y_ref, _pack_halves(jnp.dot(hmid, wd_s[...], preferred_element_type=F32)))

    @pl.when(i == 0)
    def _():
        seg_ref[0] = 0
        for cp in fetch(expert, 0):
            cp.start()

    @pl.when(jnp.logical_and(valid, changed))
    def _():
        slot = seg_ref[0] % 2
        for cp in fetch(expert, slot):
            cp.wait()
        nxt = nx_ref[i]

        @pl.when(nxt >= 0)
        def _():
            for cp in fetch(nxt, 1 - slot):
                cp.start()

        seg_ref[0] = seg_ref[0] + 1
        wg_s[...] = wg_f[slot].astype(BF16)
        wu_s[...] = wu_f[slot].astype(BF16)
        wd_s[...] = wd_f[slot].astype(BF16)
        swiglu_tile()

    @pl.when(jnp.logical_and(valid, jnp.logical_not(changed)))
    def _():
        swiglu_tile()

    @pl.when(jnp.logical_not(valid))
    def _():
        y_ref[...] = jnp.zeros(y_ref.shape, y_ref.dtype)


def expert_call(xs, tile_expert, n_valid_tiles, next_expert, w_gate, w_up, w_down, layer):
    p = xs.shape[0] // ROW_TILE
    d = 2 * ROW_TILE * LANES
    tm = TM_EXPERT
    f = w_gate.shape[-1]
    grid_spec = pltpu.PrefetchScalarGridSpec(
        num_scalar_prefetch=3,
        grid=(p // tm,),
        in_specs=[pl.BlockSpec((tm * ROW_TILE, LANES), lambda i, te, nv, nx: (jnp.minimum(i, nv[0] - 1), 0)),
                  pl.BlockSpec(memory_space=pl.ANY),
                  pl.BlockSpec(memory_space=pl.ANY),
                  pl.BlockSpec(memory_space=pl.ANY)],
        out_specs=pl.BlockSpec((tm * ROW_TILE, LANES), lambda i, te, nv, nx: (i, 0)),
        scratch_shapes=[pltpu.VMEM((2, d, f), F32), pltpu.VMEM((2, d, f), F32), pltpu.VMEM((2, f, d), F32),
                        pltpu.VMEM((d, f), BF16), pltpu.VMEM((d, f), BF16), pltpu.VMEM((f, d), BF16),
                        pltpu.SMEM((1,), jnp.int32), pltpu.SemaphoreType.DMA((2,))],
    )
    return pl.pallas_call(
        functools.partial(_expert_kernel, layer=layer),
        grid_spec=grid_spec,
        out_shape=jax.ShapeDtypeStruct(xs.shape, jnp.uint32),
        compiler_params=_cparams(("arbitrary",), 58),
        name="expert_swiglu",
    )(tile_expert, n_valid_tiles, next_expert, xs, w_gate, w_up, w_down)


def _combine_kernel(pos0_ref, pos1_ref, nxt0_ref, nxt1_ref, ys_ref, h_ref, w_ref, gain_ref, *rest, chunk, seq,
                    layer, q_scale):
    if seq is None:
        xn_ref, buf, sems = rest
        h2_ref = None
    else:
        (w_in_hbm, mix_ref, scale_ref, xn_ref, h2_ref, mixed_ref, qkv_ref, buf, sems, w_all, stage, stage_sems,
         carry) = rest
    i = pl.program_id(0)
    slot = i % 2

    def gather(p0_ref, p1_ref, into):
        def start(t, c):
            for j, pos_ref in enumerate((p0_ref, p1_ref)):
                pltpu.make_async_copy(_rows(ys_ref, pos_ref[0, 0, t]), _rows(buf.at[into, j], t),
                                      sems.at[into]).start(priority=j)
            return c

        lax.fori_loop(0, chunk, start, 0, unroll=ROW_DMA_UNROLL)

    @pl.when(i == 0)
    def _():
        gather(pos0_ref, pos1_ref, 0)

    @pl.when(i + 1 < pl.num_programs(0))
    def _():
        gather(nxt0_ref, nxt1_ref, 1 - slot)

    if seq is not None:
        @pl.when(i == 0)
        def _():
            cols = stage.shape[2]
            n_parts = w_all.shape[1] // cols

            def part(p):
                return pltpu.make_async_copy(w_in_hbm.at[layer, :, pl.ds(p * cols, cols)], stage.at[p % 2],
                                             stage_sems.at[p % 2])

            part(0).start()
            part(1).start()
            for p in range(n_parts):
                part(p).wait()
                w_all[:, p * cols:(p + 1) * cols] = stage[p % 2].astype(BF16)
                if p + 2 < n_parts:
                    part(p + 2).start()

    for j in range(2):
        pltpu.make_async_copy(_rows(ys_ref, 0, chunk), buf.at[slot, j], sems.at[slot]).wait()
    w = w_ref[...]
    lo0, hi0 = _unpack_halves(_load_row_tiles(buf.at[slot, 0]))
    lo1, hi1 = _unpack_halves(_load_row_tiles(buf.at[slot, 1]))
    w0 = w[:, ROUTE_W0:ROUTE_W0 + 1]
    w1 = w[:, ROUTE_W1:ROUTE_W1 + 1]
    y = jnp.concatenate([w0 * lo0 + w1 * lo1, w0 * hi0 + w1 * hi1], axis=1)
    h2 = h_ref[...] + y
    xn = _rms(h2, gain_ref[...]).astype(xn_ref.dtype)
    xn_ref[...] = xn
    if seq is not None:
        h2_ref[...] = h2
        w_pool = w_all.at[:, pl.ds(0, POOL_WIDTH)]
        mixed_ref[...] = _pool_tile(xn, w_pool, carry, (i * chunk) % seq, mix_ref, scale_ref).astype(mixed_ref.dtype)
        qkv = jnp.dot(xn, w_all[:, POOL_WIDTH:], preferred_element_type=F32)
        q_cols = lax.broadcasted_iota(jnp.int32, (1, qkv.shape[1]), 1) < ATTN_WIDTH
        qkv_ref[...] = (qkv * jnp.where(q_cols, q_scale, 1.0)).astype(qkv_ref.dtype)


IN_PROJ_STAGE_COLS = 512


def combine_call(ys, pos, h1, weights, next_gain, next_proj=None, chunk=256):
    n, d = h1.shape
    row_spec = pl.BlockSpec((chunk, d), lambda c: (c, 0))
    per_row = PLAN_CHUNK // chunk
    steps = n // chunk
    in_specs_pool, args_pool, scratch_pool = [], [], []
    layer = q_scale = None
    if next_proj is None:
        seq = None
        out_specs = [row_spec]
        out_shape = [jax.ShapeDtypeStruct((n, d), F32)]
    else:
        w_in, layer, mix_bf16, scale, seq, q_scale = next_proj
        c = POOL_WIDTH
        width = w_in.shape[-1]
        in_specs_pool = [pl.BlockSpec(memory_space=pl.ANY),
                         pl.BlockSpec(mix_bf16.shape, lambda s: (0, 0, 0)),
                         pl.BlockSpec((1, c), lambda s: (0, 0))]
        args_pool = [w_in, mix_bf16, scale.reshape(1, c)]
        scratch_pool = [pltpu.VMEM((d, width), BF16), pltpu.VMEM((2, d, IN_PROJ_STAGE_COLS), F32),
                        pltpu.SemaphoreType.DMA((2,)), pltpu.VMEM((POOL_HALO, c), F32)]
        out_specs = [row_spec, row_spec, pl.BlockSpec((chunk, c), lambda s: (s, 0)),
                     pl.BlockSpec((chunk, width - c), lambda s: (s, 0))]
        out_shape = [jax.ShapeDtypeStruct((n, d), BF16), jax.ShapeDtypeStruct((n, d), F32),
                     jax.ShapeDtypeStruct((n, c), BF16), jax.ShapeDtypeStruct((n, width - c), BF16)]

    def pos_spec(j, ahead):
        def index(c):
            c = jnp.minimum(c + ahead, steps - 1)
            return (j, c // per_row, 0, c % per_row)
        return pl.BlockSpec((None, 1, 1, chunk), index, memory_space=pltpu.SMEM)

    outs = pl.pallas_call(
        functools.partial(_combine_kernel, chunk=chunk, seq=seq, layer=layer, q_scale=q_scale),
        grid=(steps,),
        in_specs=[pos_spec(0, 0), pos_spec(1, 0), pos_spec(0, 1), pos_spec(1, 1),
                  pl.BlockSpec(memory_space=pl.ANY),
                  row_spec,
                  pl.BlockSpec((chunk, ROUTER_LANES), lambda c: (c, 0)),
                  pl.BlockSpec((1, d), lambda c: (0, 0))] + in_specs_pool,
        out_specs=out_specs,
        out_shape=out_shape,
        scratch_shapes=[pltpu.VMEM((2, 2, chunk * ROW_TILE, LANES), jnp.uint32),
                        pltpu.SemaphoreType.DMA((2,))] + scratch_pool,
        compiler_params=_cparams(("arbitrary",), 58),
        name="combine_rows",
    )(pos, pos, pos, pos, ys, h1, weights, next_gain.reshape(1, d), *args_pool)
    return outs[0] if next_proj is None else tuple(outs)


def kernel(x, rel_bias_table, norm_mix_gain, w_in, w_merge_gate, b_merge_gate, pool_mix, pool_scale, w_up_pool, lambda_q1, lambda_k1, lambda_q2, lambda_k2, subln_gain, w_up_attn, w_out, norm_ffn_gain, w_router_group, b_router_group, w_router_expert, b_router_expert, w_expert_gate, w_expert_up, w_expert_down, final_norm_gain):
    batch, seq, d = x.shape
    depth = w_in.shape[0]
    n = batch * seq
    n_tiles = (2 * n + N_EXPERTS * (TM_EXPERT - 1)) // TM_EXPERT + 1

    bias_tiles = bias_tiles_call(rel_bias_table)
    h = x.reshape(n, d)
    q_scale = LOG2E * DIFF_HEAD_DIM ** -0.5
    xn, mixed = first_pool_call(h, norm_mix_gain[0], w_in, pool_mix[0].astype(BF16), pool_scale[0], seq)
    qkv = proj_call(xn, w_in, 0, 1, 3, ATTN_WIDTH, BF16, first_block_scale=q_scale, name="in_proj_qkv")

    for l in range(depth):
        lambda_init = 0.8 - 0.6 * math.exp(-0.3 * l)
        lam_params = jnp.stack([lambda_q1[l], lambda_k1[l], lambda_q2[l], lambda_k2[l]])
        attn = attn_call(qkv.reshape(batch, seq, 3 * ATTN_WIDTH), bias_tiles, lam_params, subln_gain[l],
                         lambda_init, batch, seq).reshape(n, ATTN_WIDTH)
        z = merge_call(xn, mixed, attn, w_merge_gate, b_merge_gate.reshape(depth, 1, -1), w_up_pool, w_up_attn, l)

        w_router = jnp.concatenate(
            [w_router_group[l], jnp.transpose(w_router_expert[l], (1, 0, 2)).reshape(d, N_EXPERTS),
             jnp.zeros((d, ROUTER_LANES - N_GROUPS - N_EXPERTS), F32)], axis=1).astype(BF16)
        b_router = jnp.concatenate(
            [b_router_group[l], b_router_expert[l].reshape(-1),
             jnp.zeros((ROUTER_LANES - N_GROUPS - N_EXPERTS,), F32)]).reshape(1, ROUTER_LANES)
        h1, hn, weights, counts = outproj_call(z, w_out, l, h, norm_ffn_gain[l], w_router, b_router)

        pos, meta = plan_call(weights, counts, n_tiles)
        tile_expert = meta[META_TILE_EXPERT, :n_tiles]
        next_expert = meta[META_NEXT_EXPERT, :n_tiles]
        pad_start = meta[META_PAD_START, :N_EXPERTS]
        n_valid = meta[META_N_VALID, :1]
        xs = dispatch_call(hn, pos, pad_start, n_valid, n_tiles)
        ys = expert_call(xs, tile_expert, n_valid, next_expert, w_expert_gate, w_expert_up, w_expert_down, l)
        if l == depth - 1:
            out = combine_call(ys, pos, h1, weights, final_norm_gain)
        else:
            next_proj = (w_in, l + 1, pool_mix[l + 1].astype(BF16), pool_scale[l + 1], seq, q_scale)
            xn, h, mixed, qkv = combine_call(ys, pos, h1, weights, norm_mix_gain[l + 1], next_proj)

    return out.reshape(batch, seq, d)
```

```python
import functools
import math

import numpy as np
import jax
import jax.numpy as jnp
from jax import lax
from jax.experimental import pallas as pl
from jax.experimental.pallas import tpu as pltpu

F32 = jnp.float32
BF16 = jnp.bfloat16

POOL_WIDTH = 1024
POOL_WINDOWS = (2, 4, 8, 16)
POOL_GROUP_DIM = 256
POOL_HALO = 16
DIFF_HEADS = 8
DIFF_HEAD_DIM = 64
DIFF_V_DIM = 128
ATTN_WIDTH = 1024
REL_BUCKETS = 32
REL_MAX_DISTANCE = 128
N_GROUPS = 4
EXPERTS_PER_GROUP = 8
N_EXPERTS = 32
EPS = 1e-6
NEG_INF = -1e30
LOG2E = 1.4426950408889634

ROUTER_LANES = 128
TQ = 256
TQ_BLOCK = 256
TK = 256
KV_SUPER = 2
V_ROWS = DIFF_V_DIM + 16
HEADS_PER_STEP = 8
TM_EXPERT = 256
LANES = 128
ROW_DMA_UNROLL = 8
MIB = 1024 * 1024


def _cparams(sem, vmem_mib):
    return pltpu.CompilerParams(dimension_semantics=sem, vmem_limit_bytes=vmem_mib * MIB)


def _rms(xf, gain):
    ms = jnp.mean(xf * xf, axis=-1, keepdims=True)
    return xf * lax.rsqrt(ms + EPS) * gain


def _proj_kernel(x_ref, w_ref, o_ref, w_s, *, first_block_scale):
    @pl.when(pl.program_id(1) == 0)
    def _():
        w_s[...] = w_ref[...].astype(BF16)

    acc = jnp.dot(x_ref[...], w_s[...], preferred_element_type=F32)
    if first_block_scale is not None:
        acc = acc * jnp.where(pl.program_id(0) == 0, first_block_scale, 1.0).astype(F32)
    o_ref[...] = acc.astype(o_ref.dtype)


def proj_call(x, w, layer, col_block0, n_col_blocks, tn, out_dtype, first_block_scale=None, tm=1024, name="proj"):
    n, k = x.shape
    return pl.pallas_call(
        functools.partial(_proj_kernel, first_block_scale=first_block_scale),
        grid=(n_col_blocks, n // tm),
        in_specs=[pl.BlockSpec((tm, k), lambda j, m: (m, 0)),
                  pl.BlockSpec((None, k, tn), lambda j, m: (layer, 0, col_block0 + j))],
        out_specs=pl.BlockSpec((tm, tn), lambda j, m: (m, j)),
        out_shape=jax.ShapeDtypeStruct((n, n_col_blocks * tn), out_dtype),
        scratch_shapes=[pltpu.VMEM((k, tn), BF16)],
        compiler_params=_cparams(("arbitrary", "arbitrary"), 48),
        name=name,
    )(x, w)


def _first_pool_kernel(x_ref, gain_ref, w_ref, mix_ref, scale_ref, xn_ref, o_ref, w_s, carry, *, tm, seq):
    m = pl.program_id(0)

    @pl.when(m == 0)
    def _():
        w_s[...] = w_ref[...].astype(BF16)

    xn = _rms(x_ref[...], gain_ref[...]).astype(BF16)
    xn_ref[...] = xn
    o_ref[...] = _pool_tile(xn, w_s, carry, (m * tm) % seq, mix_ref, scale_ref).astype(o_ref.dtype)


def _pool_tile(xn, w_s, carry, row0, mix_ref, scale_ref):
    tm = xn.shape[0]
    cur = jnp.dot(xn, w_s[...], preferred_element_type=F32)
    prev = jnp.where(row0 == 0, 0.0, carry[...])
    carry[...] = cur[tm - POOL_HALO:]
    pos = row0 + lax.broadcasted_iota(jnp.int32, (tm, 1), 0)
    outs = []
    for g, w in enumerate(POOL_WINDOWS):
        sl = slice(g * POOL_GROUP_DIM, (g + 1) * POOL_GROUP_DIM)
        x = jnp.concatenate([prev[:, sl], cur[:, sl]], axis=0)
        s, d = x, 1
        while d < w:
            s = s[:-d] + s[d:]
            d *= 2
        start = POOL_HALO - w + 1
        wsum = s[start:start + tm]
        count = jnp.minimum(pos + 1, w).astype(F32)
        pooled = wsum / count - cur[:, sl]
        mixed = jnp.dot(pooled.astype(BF16), mix_ref[g], preferred_element_type=F32)
        outs.append(mixed * scale_ref[:, sl])
    return jnp.concatenate(outs, axis=1)


def first_pool_call(x, gain, w_in, mix_bf16, scale, seq, tm=512):
    n, d = x.shape
    c = POOL_WIDTH
    row_spec = pl.BlockSpec((tm, c), lambda m: (m, 0))
    x_spec = pl.BlockSpec((tm, d), lambda m: (m, 0))
    return pl.pallas_call(
        functools.partial(_first_pool_kernel, tm=tm, seq=seq),
        grid=(n // tm,),
        in_specs=[x_spec,
                  pl.BlockSpec((1, d), lambda m: (0, 0)),
                  pl.BlockSpec((None, d, c), lambda m: (0, 0, 0), pipeline_mode=pl.Buffered(1)),
                  pl.BlockSpec(mix_bf16.shape, lambda m: (0, 0, 0)),
                  pl.BlockSpec((1, c), lambda m: (0, 0))],
        out_specs=[x_spec, row_spec],
        out_shape=[jax.ShapeDtypeStruct((n, d), BF16), jax.ShapeDtypeStruct((n, c), BF16)],
        scratch_shapes=[pltpu.VMEM((d, c), BF16), pltpu.VMEM((POOL_HALO, c), F32)],
        compiler_params=_cparams(("arbitrary",), 48),
        name="first_norm_pool",
    )(x, gain.reshape(1, d), w_in, mix_bf16, scale.reshape(1, c))


N_BIAS_TILES = 4


def _bucket_tiles():
    kk = np.arange(TK)[:, None]
    qq = np.arange(TQ_BLOCK)[None, :]
    tiles = []
    for rel in (0, 1, 2, -1):
        n = rel * TK + qq - kk
        max_exact = REL_BUCKETS // 2
        nf = np.maximum(n, 1).astype(np.float64)
        large = max_exact + (np.log(nf / max_exact) / math.log(REL_MAX_DISTANCE / max_exact)
                             * (REL_BUCKETS - max_exact)).astype(np.int64)
        large = np.minimum(large, REL_BUCKETS - 1)
        bucket = np.where(n < max_exact, n, large)
        tiles.append(np.where(n < 0, -1, bucket))
    return np.stack(tiles).astype(np.int32)


N_NEAR_TILES = 2


def _bias_kernel(table_ref, bucket_ref, o_ref):
    h = pl.program_id(0)
    near = bucket_ref[0:N_NEAR_TILES]
    acc = jnp.full(near.shape, NEG_INF, F32)
    for b in range(REL_BUCKETS):
        acc = jnp.where(near == b, table_ref[b * DIFF_HEADS + h] * LOG2E, acc)
    o_ref[0:N_NEAR_TILES] = acc
    last = REL_BUCKETS - 1
    far_bias = table_ref[last * DIFF_HEADS + h] * LOG2E
    o_ref[N_NEAR_TILES:] = jnp.where(bucket_ref[N_NEAR_TILES:] == last, far_bias, NEG_INF)


def bias_tiles_call(rel_table):
    tiles = _bucket_tiles()
    assert set(np.unique(tiles[N_NEAR_TILES:]).tolist()) <= {-1, REL_BUCKETS - 1}
    bucket = jnp.asarray(tiles)
    return pl.pallas_call(
        _bias_kernel,
        grid=(DIFF_HEADS,),
        in_specs=[pl.BlockSpec(memory_space=pltpu.SMEM),
                  pl.BlockSpec(bucket.shape, lambda h: (0, 0, 0))],
        out_specs=pl.BlockSpec((None,) + bucket.shape, lambda h: (h, 0, 0, 0)),
        out_shape=jax.ShapeDtypeStruct((DIFF_HEADS,) + bucket.shape, F32),
        compiler_params=_cparams(("parallel",), 32),
        name="rel_bias_tiles",
    )(rel_table.reshape(-1), bucket)


def _attn_kernel(q_ref, k_ref, v_ref, bias_ref, lam_ref, gain_ref, o_ref, vt_ref, s_ref, acc_ref,
                 *, lambda_init, n_super):
    qi = pl.program_id(2)
    tks = KV_SUPER * TK
    heads = range(HEADS_PER_STEP)

    def block_diag(ref, hh):
        q = ref[:, hh * DIFF_V_DIM:(hh + 1) * DIFF_V_DIM]
        lane = lax.broadcasted_iota(jnp.int32, q.shape, 1)
        zero = jnp.zeros_like(q)
        return jnp.concatenate([jnp.where(lane < DIFF_HEAD_DIM, q, zero),
                                jnp.where(lane >= DIFF_HEAD_DIM, q, zero)], axis=0)

    def scores(t, hh, qd, tile):
        kb = k_ref[pl.ds(pl.multiple_of(t * tks, tks), tks), hh * DIFF_V_DIM:(hh + 1) * DIFF_V_DIM]
        s = lax.dot_general(kb, qd, (((1,), (1,)), ((), ())), preferred_element_type=F32)
        parts = []
        for u in range(KV_SUPER):
            tiles = []
            for c in range(TQ // TQ_BLOCK):
                rel = tile * (TQ // TQ_BLOCK) + c - (t * KV_SUPER + u)
                tiles.append(bias_ref[hh, jnp.where(rel < 0, N_BIAS_TILES - 1, jnp.minimum(rel, 2))])
            parts.append(s[u * TK:(u + 1) * TK] + jnp.concatenate(tiles + tiles, axis=1))
        s = jnp.concatenate(parts, axis=0)
        s_ref[hh] = s
        return jnp.max(s, axis=0, keepdims=True)

    @pl.when(qi == 0)
    def _():
        extra = (lax.broadcasted_iota(jnp.int32, (V_ROWS - DIFF_V_DIM, tks), 0) == 0).astype(BF16)
        for hh in heads:
            cols = slice(hh * DIFF_V_DIM, (hh + 1) * DIFF_V_DIM)
            for c in range(n_super):
                vt = v_ref[c * tks:(c + 1) * tks, cols].astype(F32).T.astype(BF16)
                vt_ref[hh, c] = jnp.concatenate([vt, extra], axis=0)
        acc_ref[...] = jnp.zeros(acc_ref.shape, F32)

    qds = [block_diag(q_ref, hh) for hh in heads]

    def softmax_step(t, hh, m_prev, m_cur):
        alpha = jnp.exp2(m_prev - m_cur)
        p = jnp.exp2(s_ref[hh] - m_cur)
        pv = jnp.dot(vt_ref[hh, t], p.astype(BF16), preferred_element_type=F32)
        acc_ref[hh] = acc_ref[hh] * alpha + pv

    last = ((qi + 1) * TQ - 1) // tks
    neg = jnp.full((1, 2 * TQ), NEG_INF, F32)
    init = tuple((neg, jnp.maximum(neg, scores(0, hh, qds[hh], qi))) for hh in heads)

    def body(t, carry):
        out = []
        for hh in heads:
            m_prev, m_cur = carry[hh]
            softmax_step(t, hh, m_prev, m_cur)
            m_next = jnp.maximum(m_cur, scores(t + 1, hh, qds[hh], qi))
            out.append((m_cur, m_next))
        return tuple(out)

    carry = lax.fori_loop(0, last, body, init)

    lam_p = lam_ref[...]
    lam = (jnp.exp(jnp.sum(lam_p[0:1] * lam_p[1:2], axis=1, keepdims=True))
           - jnp.exp(jnp.sum(lam_p[2:3] * lam_p[3:4], axis=1, keepdims=True)) + lambda_init)
    for hh in heads:
        m_prev, m_cur = carry[hh]
        softmax_step(last, hh, m_prev, m_cur)
        acc = acc_ref[hh, :DIFF_V_DIM, :]
        l = acc_ref[hh, DIFF_V_DIM:DIFF_V_DIM + 1, :]
        o = acc[:, :TQ] / l[:, :TQ] - lam * (acc[:, TQ:] / l[:, TQ:])
        ms = jnp.mean(o * o, axis=0, keepdims=True)
        y = o * lax.rsqrt(ms + EPS) * gain_ref[...] * (1.0 - lambda_init)
        o_ref[:, hh * DIFF_V_DIM:(hh + 1) * DIFF_V_DIM] = y.T.astype(o_ref.dtype)


def attn_call(qkv, bias_tiles, lam_params, subln_gain, lambda_init, batch, seq):
    tks = KV_SUPER * TK
    n_super = seq // tks
    hps = HEADS_PER_STEP
    width = hps * DIFF_V_DIM
    groups = DIFF_HEADS // hps
    n_q = seq // TQ
    return pl.pallas_call(
        functools.partial(_attn_kernel, lambda_init=lambda_init, n_super=n_super),
        grid=(batch, groups, n_q),
        in_specs=[pl.BlockSpec((None, TQ, width), lambda b, g, i: (b, i, g)),
                  pl.BlockSpec((None, seq, width), lambda b, g, i: (b, 0, groups + g), pipeline_mode=pl.Buffered(1)),
                  pl.BlockSpec((None, seq, width), lambda b, g, i: (b, 0, 2 * groups + g),
                               pipeline_mode=pl.Buffered(1)),
                  pl.BlockSpec((hps, N_BIAS_TILES, TK, TQ_BLOCK), lambda b, g, i: (g, 0, 0, 0),
                               pipeline_mode=pl.Buffered(1)),
                  pl.BlockSpec((4, DIFF_HEAD_DIM), lambda b, g, i: (0, 0)),
                  pl.BlockSpec((DIFF_V_DIM, 1), lambda b, g, i: (0, 0))],
        out_specs=pl.BlockSpec((None, TQ, width), lambda b, g, i: (b, i, g)),
        out_shape=jax.ShapeDtypeStruct((batch, seq, ATTN_WIDTH), BF16),
        scratch_shapes=[pltpu.VMEM((hps, n_super, V_ROWS, tks), BF16),
                        pltpu.VMEM((hps, tks, 2 * TQ), F32),
                        pltpu.VMEM((hps, V_ROWS, 2 * TQ), F32)],
        compiler_params=_cparams(("parallel", "parallel", "arbitrary"), 56),
        name="diff_attention",
    )(qkv, qkv, qkv, bias_tiles, lam_params, subln_gain.reshape(DIFF_V_DIM, 1))


def _sigmoid(x):
    return 1.0 / (1.0 + jnp.exp(-x))


def _merge_kernel(xn_ref, mixed_ref, attn_ref, wgp_ref, wga_ref, bgp_ref, bga_ref, wup_ref, wua_ref, z_ref,
                  wgp_s, wga_s, wup_s, wua_s):
    @pl.when(pl.program_id(1) == 0)
    def _():
        wgp_s[...] = wgp_ref[...].astype(BF16)
        wga_s[...] = wga_ref[...].astype(BF16)
        wup_s[...] = wup_ref[...].astype(BF16)
        wua_s[...] = wua_ref[...].astype(BF16)

    xn = xn_ref[...]
    g_pool = _sigmoid(jnp.dot(xn, wgp_s[...], preferred_element_type=F32) + bgp_ref[...])
    g_attn = _sigmoid(jnp.dot(xn, wga_s[...], preferred_element_type=F32) + bga_ref[...])
    y_pool = jnp.dot(mixed_ref[...], wup_s[...], preferred_element_type=F32)
    y_attn = jnp.dot(attn_ref[...], wua_s[...], preferred_element_type=F32)
    z_ref[...] = (g_pool * y_pool + g_attn * y_attn).astype(z_ref.dtype)


def merge_call(xn, mixed, attn, w_gate, b_gate, w_up_pool, w_up_attn, layer, tm=1024, tn=512):
    n, d = xn.shape
    nb = d // tn
    return pl.pallas_call(
        _merge_kernel,
        grid=(nb, n // tm),
        in_specs=[pl.BlockSpec((tm, d), lambda j, m: (m, 0)),
                  pl.BlockSpec((tm, POOL_WIDTH), lambda j, m: (m, 0)),
                  pl.BlockSpec((tm, ATTN_WIDTH), lambda j, m: (m, 0)),
                  pl.BlockSpec((None, d, tn), lambda j, m: (layer, 0, j)),
                  pl.BlockSpec((None, d, tn), lambda j, m: (layer, 0, nb + j)),
                  pl.BlockSpec((None, 1, tn), lambda j, m: (layer, 0, j)),
                  pl.BlockSpec((None, 1, tn), lambda j, m: (layer, 0, nb + j)),
                  pl.BlockSpec((None, POOL_WIDTH, tn), lambda j, m: (layer, 0, j)),
                  pl.BlockSpec((None, ATTN_WIDTH, tn), lambda j, m: (layer, 0, j))],
        out_specs=pl.BlockSpec((tm, tn), lambda j, m: (m, j)),
        out_shape=jax.ShapeDtypeStruct((n, d), BF16),
        scratch_shapes=[pltpu.VMEM((d, tn), BF16), pltpu.VMEM((d, tn), BF16),
                        pltpu.VMEM((POOL_WIDTH, tn), BF16), pltpu.VMEM((ATTN_WIDTH, tn), BF16)],
        compiler_params=_cparams(("arbitrary", "arbitrary"), 56),
        name="gated_merge",
    )(xn, mixed, attn, w_gate, w_gate, b_gate, b_gate, w_up_pool, w_up_attn)


HIGH_HALF = 0xFFFF0000


def _pack_halves(x):
    c = x.shape[1] // 2
    lo = lax.bitcast_convert_type(x[:, :c].astype(BF16).astype(F32), jnp.uint32)
    hi = lax.bitcast_convert_type(x[:, c:].astype(BF16).astype(F32), jnp.uint32)
    return (lo >> 16) | hi


def _unpack_halves(p):
    lo = lax.bitcast_convert_type(p << 16, F32)
    hi = lax.bitcast_convert_type(p & jnp.uint32(HIGH_HALF), F32)
    return lo, hi


ROW_TILE = 8


def _store_row_tiles(ref, packed):
    m = packed.shape[0]
    for s in range(ROW_TILE):
        ref[pl.ds(s, m, stride=ROW_TILE), :] = packed[:, s * LANES:(s + 1) * LANES]


def _load_row_tiles(ref):
    m = ref.shape[0] // ROW_TILE
    return jnp.concatenate([ref[pl.ds(s, m, stride=ROW_TILE), :] for s in range(ROW_TILE)], axis=1)


def _rows(ref, first, count=1):
    return ref.at[pl.ds(pl.multiple_of(first * ROW_TILE, ROW_TILE), count * ROW_TILE)]


ROUTE_E0, ROUTE_E1, ROUTE_W0, ROUTE_W1 = 0, 1, 2, 3


def _outproj_kernel(z_ref, w_ref, h_ref, gain_ref, wr_ref, br_ref, h1_ref, hn_ref, route_ref, cnt_ref, w_s, logit_s):
    m = pl.program_id(0)
    tiles = pl.num_programs(0) - 1

    def project():
        h1 = h_ref[...] + jnp.dot(z_ref[...], w_s[...], preferred_element_type=F32)
        h1_ref[...] = h1
        hn = _rms(h1, gain_ref[...])
        _store_row_tiles(hn_ref, _pack_halves(hn))
        logit_s[...] = jnp.dot(hn.astype(BF16), wr_ref[...], preferred_element_type=F32) + br_ref[...]

    def route_previous():
        logits = logit_s[...]
        lane = lax.broadcasted_iota(jnp.int32, logits.shape, 1).astype(F32)
        e0, e1, w0, w1 = _route(logits, lane)
        route_ref[...] = jnp.where(lane == ROUTE_E0, e0, jnp.where(lane == ROUTE_E1, e1,
                                   jnp.where(lane == ROUTE_W0, w0, jnp.where(lane == ROUTE_W1, w1, 0.0))))
        onehot = jnp.where(lane == e0, 1.0, 0.0) + jnp.where(lane == e1, 1.0, 0.0)
        cnt_ref[0:1] = cnt_ref[0:1] + jnp.sum(onehot, axis=0, keepdims=True)

    @pl.when(m == 0)
    def _():
        w_s[...] = w_ref[...].astype(BF16)
        cnt_ref[...] = jnp.zeros(cnt_ref.shape, F32)
        project()

    @pl.when(jnp.logical_and(m > 0, m < tiles))
    def _():
        route_previous()
        project()

    @pl.when(m == tiles)
    def _():
        route_previous()


def outproj_call(z, w_out, layer, h, gain, w_router, b_router, tm=256):
    n, d = h.shape
    tiles = n // tm

    def tile(m):
        return (jnp.minimum(m, tiles - 1), 0)

    def routed(m):
        return (jnp.maximum(m - 1, 0), 0)

    return pl.pallas_call(
        _outproj_kernel,
        grid=(tiles + 1,),
        in_specs=[pl.BlockSpec((tm, d), tile),
                  pl.BlockSpec((None, d, d), lambda m: (layer, 0, 0), pipeline_mode=pl.Buffered(1)),
                  pl.BlockSpec((tm, d), tile),
                  pl.BlockSpec((1, d), lambda m: (0, 0)),
                  pl.BlockSpec((d, ROUTER_LANES), lambda m: (0, 0)),
                  pl.BlockSpec((1, ROUTER_LANES), lambda m: (0, 0))],
        out_specs=[pl.BlockSpec((tm, d), tile),
                   pl.BlockSpec((tm * ROW_TILE, LANES), tile),
                   pl.BlockSpec((tm, ROUTER_LANES), routed),
                   pl.BlockSpec((8, ROUTER_LANES), lambda m: (0, 0))],
        out_shape=[jax.ShapeDtypeStruct((n, d), F32),
                   jax.ShapeDtypeStruct((n * ROW_TILE, LANES), jnp.uint32),
                   jax.ShapeDtypeStruct((n, ROUTER_LANES), F32),
                   jax.ShapeDtypeStruct((8, ROUTER_LANES), F32)],
        scratch_shapes=[pltpu.VMEM((d, d), BF16), pltpu.VMEM((tm, ROUTER_LANES), F32)],
        compiler_params=_cparams(("arbitrary",), 56),
        name="out_proj_norm_router",
    )(z, w_out, h, gain.reshape(1, d), w_router, b_router)


def _route(x, lane):
    big = float(ROUTER_LANES)

    def first_argmax(vals):
        top = jnp.max(vals, axis=1, keepdims=True)
        idx = jnp.min(jnp.where(vals == top, lane, big), axis=1, keepdims=True)
        return top, idx

    gmask = lane < N_GROUPS
    g_top, g_sel = first_argmax(jnp.where(gmask, x, -jnp.inf))
    g_weight = 1.0 / jnp.sum(jnp.where(gmask, jnp.exp(x - g_top), 0.0), axis=1, keepdims=True)
    lo = N_GROUPS + EXPERTS_PER_GROUP * g_sel
    e_vals = jnp.where((lane >= lo) & (lane < lo + EXPERTS_PER_GROUP), x, -jnp.inf)
    v1, i1 = first_argmax(e_vals)
    v2, i2 = first_argmax(jnp.where(lane == i1, -jnp.inf, e_vals))
    t = jnp.exp(v2 - v1)
    return i1 - N_GROUPS, i2 - N_GROUPS, g_weight / (1.0 + t), g_weight * t / (1.0 + t)


PLAN_BLOCK = 1024
PLAN_CHUNK = 512
META_TILE_EXPERT, META_NEXT_EXPERT, META_PAD_START, META_N_VALID = 0, 1, 2, 3


def _plan_kernel(route_ref, cnt_ref, pos_ref, meta_ref, tri_s, base_s, off_s):
    blk = pl.program_id(0)
    tb = route_ref.shape[0]
    lane = lax.broadcasted_iota(jnp.int32, (tb, ROUTER_LANES), 1).astype(F32)
    record = route_ref[...]
    e0 = jnp.sum(jnp.where(lane == ROUTE_E0, record, 0.0), axis=1, keepdims=True)
    e1 = jnp.sum(jnp.where(lane == ROUTE_E1, record, 0.0), axis=1, keepdims=True)
    onehot = jnp.where(lane == e0, 1.0, 0.0) + jnp.where(lane == e1, 1.0, 0.0)
    block_counts = jnp.sum(onehot, axis=0, keepdims=True)

    @pl.when(blk == 0)
    def _():
        r = lax.broadcasted_iota(jnp.int32, (tb, tb), 0)
        c = lax.broadcasted_iota(jnp.int32, (tb, tb), 1)
        tri_s[...] = jnp.where(c < r, 1.0, 0.0).astype(BF16)
        lane1 = lane[0:1]
        cnt = cnt_ref[0:1]
        tiles = jnp.floor((cnt + (TM_EXPERT - 1)) * (1.0 / TM_EXPERT))
        ri = lax.broadcasted_iota(jnp.int32, (ROUTER_LANES, ROUTER_LANES), 0)
        ci = lax.broadcasted_iota(jnp.int32, (ROUTER_LANES, ROUTER_LANES), 1)
        upper = jnp.where(ri <= ci, 1.0, 0.0).astype(BF16)
        ends = jnp.dot(jnp.broadcast_to(tiles, (8, ROUTER_LANES)).astype(BF16), upper,
                       preferred_element_type=F32)[0:1]
        off_s[...] = (ends - tiles) * TM_EXPERT
        base_s[...] = jnp.zeros(base_s.shape, F32)

        def pick(vec, e):
            return jnp.sum(jnp.where(lane1 == e, vec, 0.0), axis=1, keepdims=True)

        end_of = [pick(ends, e) for e in range(N_EXPERTS)]
        n_valid = end_of[N_EXPERTS - 1]

        def segment_of(tile):
            return sum(jnp.where(tile >= end_e, 1.0, 0.0) for end_e in end_of)

        tile_expert = jnp.where(lane1 < n_valid, segment_of(lane1), segment_of(n_valid - 1.0))
        following = sum(jnp.where(tile_expert == e, end_of[e], 0.0) for e in range(N_EXPERTS))
        next_expert = jnp.where(following < n_valid, segment_of(following), -1.0)
        rows = [tile_expert, next_expert, off_s[...] + cnt, jnp.broadcast_to(n_valid, (1, ROUTER_LANES))]
        rows += [jnp.zeros((1, ROUTER_LANES), F32)] * (meta_ref.shape[0] - len(rows))
        meta_ref[...] = jnp.concatenate(rows, axis=0).astype(jnp.int32)

    before = jnp.dot(tri_s[...], onehot.astype(BF16), preferred_element_type=F32)
    row = before + base_s[...] + off_s[...]
    base_s[...] = base_s[...] + block_counts
    eye = (lax.broadcasted_iota(jnp.int32, (LANES, LANES), 0)
           == lax.broadcasted_iota(jnp.int32, (LANES, LANES), 1))
    for j, e in enumerate((e0, e1)):
        col = jnp.sum(jnp.where(lane == e, row, 0.0), axis=1, keepdims=True)
        for q in range(tb // PLAN_CHUNK):
            parts = []
            for g in range(PLAN_CHUNK // LANES):
                t0 = q * PLAN_CHUNK + g * LANES
                square = jnp.broadcast_to(col[t0:t0 + LANES], (LANES, LANES))
                parts.append(jnp.sum(jnp.where(eye, square, 0.0), axis=0, keepdims=True))
            pos_ref[j, q] = jnp.concatenate(parts, axis=1).astype(jnp.int32)


def plan_call(route, counts, n_tiles):
    n = route.shape[0]
    tb = PLAN_BLOCK
    chunks = tb // PLAN_CHUNK
    assert n_tiles <= ROUTER_LANES and N_EXPERTS <= ROUTER_LANES
    return pl.pallas_call(
        _plan_kernel,
        grid=(n // tb,),
        in_specs=[pl.BlockSpec((tb, ROUTER_LANES), lambda b: (b, 0)),
                  pl.BlockSpec((8, ROUTER_LANES), lambda b: (0, 0))],
        out_specs=[pl.BlockSpec((2, chunks, 1, PLAN_CHUNK), lambda b: (0, b, 0, 0)),
                   pl.BlockSpec((8, ROUTER_LANES), lambda b: (0, 0))],
        out_shape=[jax.ShapeDtypeStruct((2, n // PLAN_CHUNK, 1, PLAN_CHUNK), jnp.int32),
                   jax.ShapeDtypeStruct((8, ROUTER_LANES), jnp.int32)],
        scratch_shapes=[pltpu.VMEM((tb, tb), BF16), pltpu.VMEM((1, ROUTER_LANES), F32),
                        pltpu.VMEM((1, ROUTER_LANES), F32)],
        compiler_params=_cparams(("arbitrary",), 32),
        name="dispatch_plan",
    )(route, counts)


DISPATCH_SLOTS = 3


def _dispatch_kernel(pad_ref, nv_ref, pos0_ref, pos1_ref, src_hbm, dst_ref, zero_buf, ring, sem, in_sems, row_sems,
                     *, chunk, n_tiles):
    i = pl.program_id(0)
    steps = pl.num_programs(0)
    slot = i % DISPATCH_SLOTS

    def load(block, into):
        return pltpu.make_async_copy(_rows(src_hbm, block * chunk, chunk), ring.at[into], in_sems.at[into])

    def wait_rows(of_slot):
        for j in range(2):
            pltpu.make_async_copy(ring.at[of_slot], _rows(dst_ref, 0, chunk), row_sems.at[of_slot]).wait()

    @pl.when(i == 0)
    def _():
        load(0, 0).start()
        zero_buf[...] = jnp.zeros(zero_buf.shape, zero_buf.dtype)

        def fill(e):
            return pltpu.make_async_copy(zero_buf, _rows(dst_ref, pad_ref[e], TM_EXPERT), sem)

        for e in range(N_EXPERTS):
            fill(e).start()
        for e in range(N_EXPERTS):
            fill(e).wait()

        def fill_tile(i):
            return pltpu.make_async_copy(zero_buf, _rows(dst_ref, i * TM_EXPERT, TM_EXPERT), sem)

        def start_tile(i, c):
            fill_tile(i).start()
            return c

        def wait_tile(i, c):
            fill_tile(i).wait()
            return c

        lax.fori_loop(nv_ref[0], n_tiles, start_tile, 0)
        lax.fori_loop(nv_ref[0], n_tiles, wait_tile, 0)

    nxt = (i + 1) % DISPATCH_SLOTS

    @pl.when(i >= DISPATCH_SLOTS - 1)
    def _():
        wait_rows(nxt)

    @pl.when(i + 1 < steps)
    def _():
        load(i + 1, nxt).start()

    load(i, slot).wait()
    src_ref = ring.at[slot]

    def start(t, c):
        for j, pos_ref in enumerate((pos0_ref, pos1_ref)):
            pltpu.make_async_copy(_rows(src_ref, t), _rows(dst_ref, pos_ref[0, 0, t]),
                                  row_sems.at[slot]).start(priority=j)
        return c

    lax.fori_loop(0, chunk, start, 0, unroll=ROW_DMA_UNROLL)

    @pl.when(i == steps - 1)
    def _():
        for back in range(DISPATCH_SLOTS - 1):
            @pl.when(i - back >= 0)
            def _():
                wait_rows((i - back) % DISPATCH_SLOTS)


def dispatch_call(hn, pos, pad_start, n_valid_tiles, n_tiles):
    n = hn.shape[0] // ROW_TILE
    chunk = PLAN_CHUNK
    grid_spec = pltpu.PrefetchScalarGridSpec(
        num_scalar_prefetch=2,
        grid=(n // chunk,),
        in_specs=[pl.BlockSpec((None, 1, 1, chunk), lambda i, pad, nv: (0, i, 0, 0), memory_space=pltpu.SMEM),
                  pl.BlockSpec((None, 1, 1, chunk), lambda i, pad, nv: (1, i, 0, 0), memory_space=pltpu.SMEM),
                  pl.BlockSpec(memory_space=pl.ANY)],
        out_specs=pl.BlockSpec(memory_space=pl.ANY),
        scratch_shapes=[pltpu.VMEM((TM_EXPERT * ROW_TILE, LANES), hn.dtype),
                        pltpu.VMEM((DISPATCH_SLOTS, chunk * ROW_TILE, LANES), hn.dtype),
                        pltpu.SemaphoreType.DMA(()), pltpu.SemaphoreType.DMA((DISPATCH_SLOTS,)),
                        pltpu.SemaphoreType.DMA((DISPATCH_SLOTS,))],
    )
    return pl.pallas_call(
        functools.partial(_dispatch_kernel, chunk=chunk, n_tiles=n_tiles),
        grid_spec=grid_spec,
        out_shape=jax.ShapeDtypeStruct((n_tiles * TM_EXPERT * ROW_TILE, LANES), hn.dtype),
        compiler_params=_cparams(("arbitrary",), 32),
        name="dispatch_rows",
    )(pad_start, n_valid_tiles, pos, pos, hn)


def _expert_kernel(te_ref, nv_ref, nx_ref, x_ref, wg_hbm, wu_hbm, wd_hbm, y_ref,
                   wg_f, wu_f, wd_f, wg_s, wu_s, wd_s, seg_ref, sems, *, layer):
    i = pl.program_id(0)
    valid = i < nv_ref[0]
    expert = te_ref[i]
    changed = jnp.logical_or(i == 0, expert != te_ref[jnp.maximum(i - 1, 0)])

    def fetch(e, slot):
        return [pltpu.make_async_copy(hbm.at[layer, e], buf.at[slot], sems.at[slot])
                for hbm, buf in ((wg_hbm, wg_f), (wu_hbm, wu_f), (wd_hbm, wd_f))]

    def swiglu_tile():
        lo, hi = _unpack_halves(_load_row_tiles(x_ref))
        x = jnp.concatenate([lo.astype(BF16), hi.astype(BF16)], axis=1)
        a = jnp.dot(x, wg_s[...], preferred_element_type=F32)
        b = jnp.dot(x, wu_s[...], preferred_element_type=F32)
        hmid = (a * _sigmoid(a) * b).astype(BF16)
        _store_row_tiles(y_ref, _pack_halves(jnp.dot(hmid, wd_s[...], preferred_element_type=F32)))

    @pl.when(i == 0)
    def _():
        seg_ref[0] = 0
        for cp in fetch(expert, 0):
            cp.start()

    @pl.when(jnp.logical_and(valid, changed))
    def _():
        slot = seg_ref[0] % 2
        for cp in fetch(expert, slot):
            cp.wait()
        nxt = nx_ref[i]

        @pl.when(nxt >= 0)
        def _():
            for cp in fetch(nxt, 1 - slot):
                cp.start()

        seg_ref[0] = seg_ref[0] + 1
        wg_s[...] = wg_f[slot].astype(BF16)
        wu_s[...] = wu_f[slot].astype(BF16)
        wd_s[...] = wd_f[slot].astype(BF16)
        swiglu_tile()

    @pl.when(jnp.logical_and(valid, jnp.logical_not(changed)))
    def _():
        swiglu_tile()

    @pl.when(jnp.logical_not(valid))
    def _():
        y_ref[...] = jnp.zeros(y_ref.shape, y_ref.dtype)


def expert_call(xs, tile_expert, n_valid_tiles, next_expert, w_gate, w_up, w_down, layer):
    p = xs.shape[0] // ROW_TILE
    d = 2 * ROW_TILE * LANES
    tm = TM_EXPERT
    f = w_gate.shape[-1]
    grid_spec = pltpu.PrefetchScalarGridSpec(
        num_scalar_prefetch=3,
        grid=(p // tm,),
        in_specs=[pl.BlockSpec((tm * ROW_TILE, LANES), lambda i, te, nv, nx: (jnp.minimum(i, nv[0] - 1), 0)),
                  pl.BlockSpec(memory_space=pl.ANY),
                  pl.BlockSpec(memory_space=pl.ANY),
                  pl.BlockSpec(memory_space=pl.ANY)],
        out_specs=pl.BlockSpec((tm * ROW_TILE, LANES), lambda i, te, nv, nx: (i, 0)),
        scratch_shapes=[pltpu.VMEM((2, d, f), F32), pltpu.VMEM((2, d, f), F32), pltpu.VMEM((2, f, d), F32),
                        pltpu.VMEM((d, f), BF16), pltpu.VMEM((d, f), BF16), pltpu.VMEM((f, d), BF16),
                        pltpu.SMEM((1,), jnp.int32), pltpu.SemaphoreType.DMA((2,))],
    )
    return pl.pallas_call(
        functools.partial(_expert_kernel, layer=layer),
        grid_spec=grid_spec,
        out_shape=jax.ShapeDtypeStruct(xs.shape, jnp.uint32),
        compiler_params=_cparams(("arbitrary",), 58),
        name="expert_swiglu",
    )(tile_expert, n_valid_tiles, next_expert, xs, w_gate, w_up, w_down)


def _combine_kernel(pos0_ref, pos1_ref, nxt0_ref, nxt1_ref, ys_ref, h_ref, w_ref, gain_ref, *rest, chunk, seq,
                    layer, q_scale):
    if seq is None:
        xn_ref, buf, sems = rest
        h2_ref = None
    else:
        (w_in_hbm, mix_ref, scale_ref, xn_ref, h2_ref, mixed_ref, qkv_ref, buf, sems, w_all, stage, stage_sems,
         carry) = rest
    i = pl.program_id(0)
    slot = i % 2

    def gather(p0_ref, p1_ref, into):
        def start(t, c):
            for j, pos_ref in enumerate((p0_ref, p1_ref)):
                pltpu.make_async_copy(_rows(ys_ref, pos_ref[0, 0, t]), _rows(buf.at[into, j], t),
                                      sems.at[into]).start(priority=j)
            return c

        lax.fori_loop(0, chunk, start, 0, unroll=ROW_DMA_UNROLL)

    @pl.when(i == 0)
    def _():
        gather(pos0_ref, pos1_ref, 0)

    @pl.when(i + 1 < pl.num_programs(0))
    def _():
        gather(nxt0_ref, nxt1_ref, 1 - slot)

    if seq is not None:
        @pl.when(i == 0)
        def _():
            cols = stage.shape[2]
            n_parts = w_all.shape[1] // cols

            def part(p):
                return pltpu.make_async_copy(w_in_hbm.at[layer, :, pl.ds(p * cols, cols)], stage.at[p % 2],
                                             stage_sems.at[p % 2])

            part(0).start()
            part(1).start()
            for p in range(n_parts):
                part(p).wait()
                w_all[:, p * cols:(p + 1) * cols] = stage[p % 2].astype(BF16)
                if p + 2 < n_parts:
                    part(p + 2).start()

    for j in range(2):
        pltpu.make_async_copy(_rows(ys_ref, 0, chunk), buf.at[slot, j], sems.at[slot]).wait()
    w = w_ref[...]
    lo0, hi0 = _unpack_halves(_load_row_tiles(buf.at[slot, 0]))
    lo1, hi1 = _unpack_halves(_load_row_tiles(buf.at[slot, 1]))
    w0 = w[:, ROUTE_W0:ROUTE_W0 + 1]
    w1 = w[:, ROUTE_W1:ROUTE_W1 + 1]
    y = jnp.concatenate([w0 * lo0 + w1 * lo1, w0 * hi0 + w1 * hi1], axis=1)
    h2 = h_ref[...] + y
    xn = _rms(h2, gain_ref[...]).astype(xn_ref.dtype)
    xn_ref[...] = xn
    if seq is not None:
        h2_ref[...] = h2
        w_pool = w_all.at[:, pl.ds(0, POOL_WIDTH)]
        mixed_ref[...] = _pool_tile(xn, w_pool, carry, (i * chunk) % seq, mix_ref, scale_ref).astype(mixed_ref.dtype)
        qkv = jnp.dot(xn, w_all[:, POOL_WIDTH:], preferred_element_type=F32)
        q_cols = lax.broadcasted_iota(jnp.int32, (1, qkv.shape[1]), 1) < ATTN_WIDTH
        qkv_ref[...] = (qkv * jnp.where(q_cols, q_scale, 1.0)).astype(qkv_ref.dtype)


IN_PROJ_STAGE_COLS = 512


def combine_call(ys, pos, h1, weights, next_gain, next_proj=None, chunk=256):
    n, d = h1.shape
    row_spec = pl.BlockSpec((chunk, d), lambda c: (c, 0))
    per_row = PLAN_CHUNK // chunk
    steps = n // chunk
    in_specs_pool, args_pool, scratch_pool = [], [], []
    layer = q_scale = None
    if next_proj is None:
        seq = None
        out_specs = [row_spec]
        out_shape = [jax.ShapeDtypeStruct((n, d), F32)]
    else:
        w_in, layer, mix_bf16, scale, seq, q_scale = next_proj
        c = POOL_WIDTH
        width = w_in.shape[-1]
        in_specs_pool = [pl.BlockSpec(memory_space=pl.ANY),
                         pl.BlockSpec(mix_bf16.shape, lambda s: (0, 0, 0)),
                         pl.BlockSpec((1, c), lambda s: (0, 0))]
        args_pool = [w_in, mix_bf16, scale.reshape(1, c)]
        scratch_pool = [pltpu.VMEM((d, width), BF16), pltpu.VMEM((2, d, IN_PROJ_STAGE_COLS), F32),
                        pltpu.SemaphoreType.DMA((2,)), pltpu.VMEM((POOL_HALO, c), F32)]
        out_specs = [row_spec, row_spec, pl.BlockSpec((chunk, c), lambda s: (s, 0)),
                     pl.BlockSpec((chunk, width - c), lambda s: (s, 0))]
        out_shape = [jax.ShapeDtypeStruct((n, d), BF16), jax.ShapeDtypeStruct((n, d), F32),
                     jax.ShapeDtypeStruct((n, c), BF16), jax.ShapeDtypeStruct((n, width - c), BF16)]

    def pos_spec(j, ahead):
        def index(c):
            c = jnp.minimum(c + ahead, steps - 1)
            return (j, c // per_row, 0, c % per_row)
        return pl.BlockSpec((None, 1, 1, chunk), index, memory_space=pltpu.SMEM)

    outs = pl.pallas_call(
        functools.partial(_combine_kernel, chunk=chunk, seq=seq, layer=layer, q_scale=q_scale),
        grid=(steps,),
        in_specs=[pos_spec(0, 0), pos_spec(1, 0), pos_spec(0, 1), pos_spec(1, 1),
                  pl.BlockSpec(memory_space=pl.ANY),
                  row_spec,
                  pl.BlockSpec((chunk, ROUTER_LANES), lambda c: (c, 0)),
                  pl.BlockSpec((1, d), lambda c: (0, 0))] + in_specs_pool,
        out_specs=out_specs,
        out_shape=out_shape,
        scratch_shapes=[pltpu.VMEM((2, 2, chunk * ROW_TILE, LANES), jnp.uint32),
                        pltpu.SemaphoreType.DMA((2,))] + scratch_pool,
        compiler_params=_cparams(("arbitrary",), 58),
        name="combine_rows",
    )(pos, pos, pos, pos, ys, h1, weights, next_gain.reshape(1, d), *args_pool)
    return outs[0] if next_proj is None else tuple(outs)


def kernel(x, rel_bias_table, norm_mix_gain, w_in, w_merge_gate, b_merge_gate, pool_mix, pool_scale, w_up_pool, lambda_q1, lambda_k1, lambda_q2, lambda_k2, subln_gain, w_up_attn, w_out, norm_ffn_gain, w_router_group, b_router_group, w_router_expert, b_router_expert, w_expert_gate, w_expert_up, w_expert_down, final_norm_gain):
    batch, seq, d = x.shape
    depth = w_in.shape[0]
    n = batch * seq
    n_tiles = (2 * n + N_EXPERTS * (TM_EXPERT - 1)) // TM_EXPERT + 1

    bias_tiles = bias_tiles_call(rel_bias_table)
    h = x.reshape(n, d)
    q_scale = LOG2E * DIFF_HEAD_DIM ** -0.5
    xn, mixed = first_pool_call(h, norm_mix_gain[0], w_in, pool_mix[0].astype(BF16), pool_scale[0], seq)
    qkv = proj_call(xn, w_in, 0, 1, 3, ATTN_WIDTH, BF16, first_block_scale=q_scale, name="in_proj_qkv")

    for l in range(depth):
        lambda_init = 0.8 - 0.6 * math.exp(-0.3 * l)
        lam_params = jnp.stack([lambda_q1[l], lambda_k1[l], lambda_q2[l], lambda_k2[l]])
        attn = attn_call(qkv.reshape(batch, seq, 3 * ATTN_WIDTH), bias_tiles, lam_params, subln_gain[l],
                         lambda_init, batch, seq).reshape(n, ATTN_WIDTH)
        z = merge_call(xn, mixed, attn, w_merge_gate, b_merge_gate.reshape(depth, 1, -1), w_up_pool, w_up_attn, l)

        w_router = jnp.concatenate(
            [w_router_group[l], jnp.transpose(w_router_expert[l], (1, 0, 2)).reshape(d, N_EXPERTS),
             jnp.zeros((d, ROUTER_LANES - N_GROUPS - N_EXPERTS), F32)], axis=1).astype(BF16)
        b_router = jnp.concatenate(
            [b_router_group[l], b_router_expert[l].reshape(-1),
             jnp.zeros((ROUTER_LANES - N_GROUPS - N_EXPERTS,), F32)]).reshape(1, ROUTER_LANES)
        h1, hn, weights, counts = outproj_call(z, w_out, l, h, norm_ffn_gain[l], w_router, b_router)

        pos, meta = plan_call(weights, counts, n_tiles)
        tile_expert = meta[META_TILE_EXPERT, :n_tiles]
        next_expert = meta[META_NEXT_EXPERT, :n_tiles]
        pad_start = meta[META_PAD_START, :N_EXPERTS]
        n_valid = meta[META_N_VALID, :1]
        xs = dispatch_call(hn, pos, pad_start, n_valid, n_tiles)
        ys = expert_call(xs, tile_expert, n_valid, next_expert, w_expert_gate, w_expert_up, w_expert_down, l)
        if l == depth - 1:
            out = combine_call(ys, pos, h1, weights, final_norm_gain)
        else:
            next_proj = (w_in, l + 1, pool_mix[l + 1].astype(BF16), pool_scale[l + 1], seq, q_scale)
            xn, h, mixed, qkv = combine_call(ys, pos, h1, weights, norm_mix_gain[l + 1], next_proj)

    return out.reshape(batch, seq, d)
```
